```python
import math
import jax, jax.numpy as jnp
from jax import lax
import numpy as np

D_MODEL = 1024
BATCH = 32
SEQ = 256
DEPTH = 1
DEC_BATCH = 4
DEC_SEQ = 1024
PAST_LEN = 256

GRID_W = 64
D_HYENA = 1024
N_HEADS = 16
HEAD_DIM = 64
D_ATTN = N_HEADS * HEAD_DIM
WIN_ROWS_MAX = 8
WIN_COLS = 16
FILTER_EMB = 33
FILTER_BANDS = (FILTER_EMB - 1) // 2
FILTER_HIDDEN = 64
DECAY_TARGET = 1e-2
FAST_DECAY_PCT = 0.3
SLOW_DECAY_PCT = 1.5
MIN_DECAY = math.log(DECAY_TARGET) / SLOW_DECAY_PCT
MAX_DECAY = math.log(DECAY_TARGET) / FAST_DECAY_PCT
DECAY_SHIFT = 0.05
LN_EPS = 1e-5
ALPHA = (2.0 * DEPTH) ** 0.25
BETA = (8.0 * DEPTH) ** -0.25
D_IN = 4 * D_HYENA + 4 * D_ATTN + 2 * D_MODEL
NEG_INF = -1e30

kernel_name = "hyena_natten_gated_deepnorm_step"


def layer_norm(x, g, b):
    xf = x.astype(jnp.float32)
    mu = xf.mean(-1, keepdims=True)
    var = jnp.square(xf - mu).mean(-1, keepdims=True)
    return ((xf - mu) * lax.rsqrt(var + LN_EPS)).astype(x.dtype) * g + b


def modulation(cond, w_ada, b_ada):
    mod = jax.nn.silu(cond) @ w_ada + b_ada
    return jnp.split(mod, 3, axis=-1)


def short_conv(u, w, b):
    L = u.shape[1]
    up = jnp.pad(u, ((0, 0), (1, 1), (0, 0)))
    return up[:, :L] * w[0] + up[:, 1:L + 1] * w[1] + up[:, 2:L + 2] * w[2] + b


def split_projection(h, w_in, conv_w, conv_b):
    B, L, _ = h.shape
    z = h @ w_in
    hy, g_h, qkv, g_a, m = jnp.split(
        z, [3 * D_HYENA, 4 * D_HYENA, 4 * D_HYENA + 3 * D_ATTN, 4 * D_HYENA + 4 * D_ATTN], axis=-1)
    hy = short_conv(hy, conv_w, conv_b)
    v_h, x1, x0 = jnp.split(hy, 3, axis=-1)
    q, k, v = [t.reshape(B, L, N_HEADS, HEAD_DIM) for t in jnp.split(qkv, 3, axis=-1)]
    return (v_h, x1, x0, g_h), (q, k, v, g_a), m


def hyena_filters(L, f_w1, f_b1, f_w2, f_b2, f_w3, f_freq):
    t = jnp.linspace(0.0, 1.0, L, dtype=jnp.float32)[:, None]
    bands = jnp.linspace(1e-4, FILTER_BANDS - 1, FILTER_BANDS, dtype=jnp.float32)[None]
    w = 2.0 * math.pi * jnp.arange(L, dtype=jnp.float32)[:, None] / L
    z = jnp.concatenate([t, jnp.cos(bands * w), -jnp.sin(bands * w)], axis=-1)
    hf = jnp.sin(f_freq[0] * (z @ f_w1 + f_b1))
    hf = jnp.sin(f_freq[1] * (hf @ f_w2 + f_b2))
    hf = (hf @ f_w3).astype(jnp.float32).reshape(L, 2, D_HYENA)
    deltas = jnp.linspace(MIN_DECAY, MAX_DECAY, D_HYENA, dtype=jnp.float32)
    decay = jnp.exp(-t * jnp.abs(deltas)) + DECAY_SHIFT
    hf = hf * decay[:, None, :]
    return hf[:, 0], hf[:, 1]


def hyena_long_conv(v_h, x1, x0, h_fwd, h_bwd, d_bias):
    L = v_h.shape[1]
    u = v_h * x1
    filt_full = jnp.concatenate([h_fwd, jnp.zeros_like(h_fwd[:1]), h_bwd[:0:-1]], axis=0)
    n = 2 * L
    uf = jnp.fft.rfft(u.astype(jnp.float32), n=n, axis=1)
    ff = jnp.fft.rfft(filt_full, n=n, axis=0)
    y = jnp.fft.irfft(uf * ff[None], n=n, axis=1)[:, :L].astype(u.dtype)
    return (y + u * d_bias) * x0


def context_attention(q, k, v):
    B, L, H, Dh = q.shape
    s = jnp.einsum("bqhd,bkhd->bhqk", q, k).astype(jnp.float32) * (HEAD_DIM ** -0.5)
    p = jax.nn.softmax(s, axis=-1).astype(v.dtype)
    return jnp.einsum("bhqk,bkhd->bqhd", p, v).reshape(B, L, H * Dh)


def neighbourhood_attention(q, k, v, ctx_k, ctx_v, rpb):
    B, L, H, Dh = q.shape
    rows = L // GRID_W
    kh = min(WIN_ROWS_MAX, rows)
    kw = WIN_COLS
    r = jnp.arange(rows)
    col = jnp.arange(GRID_W)
    r0 = jnp.clip(r - kh // 2, 0, rows - kh)
    c0 = jnp.clip(col - kw // 2, 0, GRID_W - kw)
    slab_rows = r0[:, None] + jnp.arange(kh)[None]
    n_slab = kh * GRID_W
    kg = k.reshape(B, rows, GRID_W, H, Dh)[:, slab_rows].reshape(B, rows, n_slab, H, Dh)
    vg = v.reshape(B, rows, GRID_W, H, Dh)[:, slab_rows].reshape(B, rows, n_slab, H, Dh)
    qg = q.reshape(B, rows, GRID_W, H, Dh)
    scale = HEAD_DIM ** -0.5
    s_win = jnp.einsum("brqhd,brkhd->bhrqk", qg, kg).astype(jnp.float32) * scale
    key_row = jnp.broadcast_to(slab_rows[:, :, None], (rows, kh, GRID_W)).reshape(rows, n_slab)
    key_col = jnp.tile(col, kh)
    dr = key_row[:, None, :] - r[:, None, None]
    dc = key_col[None, None, :] - col[None, :, None]
    in_win = (key_col[None, :] >= c0[:, None]) & (key_col[None, :] < c0[:, None] + kw)
    bias = rpb[:, dr + WIN_ROWS_MAX - 1, jnp.clip(dc + WIN_COLS - 1, 0, 2 * WIN_COLS - 2)]
    s_win = jnp.where(in_win[None, None, None], s_win + bias[None].astype(jnp.float32), NEG_INF)
    s_ctx = jnp.einsum("brqhd,bkhd->bhrqk", qg, ctx_k).astype(jnp.float32) * scale
    p = jax.nn.softmax(jnp.concatenate([s_win, s_ctx], axis=-1), axis=-1).astype(v.dtype)
    o = (jnp.einsum("bhrqk,brkhd->brqhd", p[..., :n_slab], vg)
         + jnp.einsum("bhrqk,bkhd->brqhd", p[..., n_slab:], ctx_v))
    return o.reshape(B, L, H * Dh)


def merge_branches(y_h, g_h, y_a, g_a, m, w_bh, w_ba, w_out):
    p_h = (y_h * jax.nn.silu(g_h)) @ w_bh
    p_a = (y_a * jax.nn.silu(g_a)) @ w_ba
    m_h, m_a = jnp.split(jax.nn.sigmoid(m), 2, axis=-1)
    return (m_h * p_h + m_a * p_a) @ w_out


def layer_forward(x, cond, ctx_kv, p):
    shift, scale, gate = modulation(cond, p["w_ada"], p["b_ada"])
    h = x * (1 + scale) + shift
    (v_h, x1, x0, g_h), (q, k, v, g_a), m = split_projection(h, p["w_in"], p["conv_w"], p["conv_b"])
    h_fwd, h_bwd = hyena_filters(x.shape[1], p["filt_w1"], p["filt_b1"], p["filt_w2"],
                                 p["filt_b2"], p["filt_w3"], p["filt_freq"])
    y_h = hyena_long_conv(v_h, x1, x0, h_fwd, h_bwd, p["hyena_d"])
    if ctx_kv is None:
        y_a = context_attention(q, k, v)
        kv_out = (k, v)
    else:
        y_a = neighbourhood_attention(q, k, v, ctx_kv[0], ctx_kv[1], p["rpb"])
        kv_out = None
    out = merge_branches(y_h, g_h, y_a.reshape(g_a.shape), g_a, m, p["w_bh"], p["w_ba"], p["w_out"])
    x_new = layer_norm(ALPHA * x + gate * out, p["ln_g"], p["ln_b"])
    return x_new, kv_out


def setup_inputs(seed: int = 0) -> dict:
    key = jax.random.key(seed)
    ks = jax.random.split(key, 24)

    def nrm(k, shape, s):
        return jax.random.normal(k, shape, jnp.float32) * s

    return {
        "x_prompt": nrm(ks[0], (BATCH, SEQ, D_MODEL), 1.0),
        "x_sample": nrm(ks[1], (DEC_BATCH, DEC_SEQ, D_MODEL), 1.0),
        "c": nrm(ks[2], (DEC_BATCH, D_MODEL), 1.0),
        "cache_k": nrm(ks[3], (DEC_BATCH, DEPTH, PAST_LEN, N_HEADS, HEAD_DIM), 1.0),
        "cache_v": nrm(ks[4], (DEC_BATCH, DEPTH, PAST_LEN, N_HEADS, HEAD_DIM), 1.0),
        "c_ctx": nrm(ks[5], (D_MODEL,), 1.0),
        "w_ada": nrm(ks[6], (DEPTH, D_MODEL, 3 * D_MODEL), 0.5 * D_MODEL ** -0.5),
        "b_ada": nrm(ks[7], (DEPTH, 3 * D_MODEL), 0.02),
        "w_in": nrm(ks[8], (DEPTH, D_MODEL, D_IN), D_MODEL ** -0.5),
        "conv_w": nrm(ks[9], (DEPTH, 3, 3 * D_HYENA), 0.5),
        "conv_b": nrm(ks[10], (DEPTH, 3 * D_HYENA), 0.02),
        "filt_w1": nrm(ks[11], (DEPTH, FILTER_EMB, FILTER_HIDDEN), FILTER_EMB ** -0.5),
        "filt_b1": nrm(ks[12], (DEPTH, FILTER_HIDDEN), 0.1),
        "filt_w2": nrm(ks[13], (DEPTH, FILTER_HIDDEN, FILTER_HIDDEN), FILTER_HIDDEN ** -0.5),
        "filt_b2": nrm(ks[14], (DEPTH, FILTER_HIDDEN), 0.1),
        "filt_w3": nrm(ks[15], (DEPTH, FILTER_HIDDEN, 2 * D_HYENA), 0.1 * FILTER_HIDDEN ** -0.5),
        "filt_freq": 1.0 + nrm(ks[16], (DEPTH, 2, FILTER_HIDDEN), 0.1),
        "hyena_d": nrm(ks[17], (DEPTH, D_HYENA), 0.5),
        "rpb": nrm(ks[18], (DEPTH, N_HEADS, 2 * WIN_ROWS_MAX - 1, 2 * WIN_COLS - 1), 0.1),
        "w_bh": nrm(ks[19], (DEPTH, D_HYENA, D_MODEL), BETA * D_HYENA ** -0.5),
        "w_ba": nrm(ks[20], (DEPTH, D_ATTN, D_MODEL), BETA * D_ATTN ** -0.5),
        "w_out": nrm(ks[21], (DEPTH, D_MODEL, D_MODEL), BETA * D_MODEL ** -0.5),
        "ln_g": 1.0 + nrm(ks[22], (DEPTH, D_MODEL), 0.01),
        "ln_b": nrm(ks[23], (DEPTH, D_MODEL), 0.01),
    }


def reference(x_prompt, x_sample, c, cache_k, cache_v, c_ctx, w_ada, b_ada, w_in, conv_w, conv_b,
              filt_w1, filt_b1, filt_w2, filt_b2, filt_w3, filt_freq, hyena_d, rpb,
              w_bh, w_ba, w_out, ln_g, ln_b):
    xp = x_prompt
    xs = x_sample
    cond_lat = c[:, None, :]
    new_k = []
    new_v = []
    for l in range(DEPTH):
        p = {
            "w_ada": w_ada[l], "b_ada": b_ada[l], "w_in": w_in[l],
            "conv_w": conv_w[l], "conv_b": conv_b[l],
            "filt_w1": filt_w1[l], "filt_b1": filt_b1[l], "filt_w2": filt_w2[l],
            "filt_b2": filt_b2[l], "filt_w3": filt_w3[l], "filt_freq": filt_freq[l],
            "hyena_d": hyena_d[l], "rpb": rpb[l],
            "w_bh": w_bh[l], "w_ba": w_ba[l], "w_out": w_out[l],
            "ln_g": ln_g[l], "ln_b": ln_b[l],
        }
        xp, (k_ctx, v_ctx) = layer_forward(xp, c_ctx, None, p)
        new_k.append(k_ctx)
        new_v.append(v_ctx)
        xs, _ = layer_forward(xs, cond_lat, (cache_k[:, l], cache_v[:, l]), p)
    return (xp, xs, jnp.stack(new_k, axis=1), jnp.stack(new_v, axis=1))
```

```python
import functools
import math

import jax
import jax.numpy as jnp
import numpy as np
from jax.experimental import pallas as pl
from jax.experimental.pallas import tpu as pltpu

F32 = jnp.float32
BF16 = jnp.bfloat16

D_MODEL = 1024
D_HYENA = 1024
N_HEADS = 16
HEAD_DIM = 64
D_ATTN = N_HEADS * HEAD_DIM
D_IN = 4 * D_HYENA + 4 * D_ATTN + 2 * D_MODEL
GRID_W = 64
WIN_ROWS = 8
WIN_COLS = 16
FILTER_EMB = 33
FILTER_BANDS = (FILTER_EMB - 1) // 2
FILTER_HIDDEN = 64
DECAY_TARGET = 1e-2
MIN_DECAY = math.log(DECAY_TARGET) / 1.5
MAX_DECAY = math.log(DECAY_TARGET) / 0.3
DECAY_SHIFT = 0.05
LN_EPS = 1e-5
NEG_INF = -1e30

COL_VH, COL_X1, COL_X0, COL_GH, COL_Q, COL_K, COL_V, COL_GA, COL_MH, COL_MA = range(10)

LANES = 128
HEADS_PER_TILE = LANES // HEAD_DIM
N_HEAD_TILES = N_HEADS // HEADS_PER_TILE
V7X_VMEM_BYTES = 64 * 1024 * 1024
ATTN_SCALE = HEAD_DIM ** -0.5

Q_BLOCK = 256
Q_BLOCK_ROWS = Q_BLOCK // GRID_W
KEY_WIN_ROWS = 12
KEY_WIN = KEY_WIN_ROWS * GRID_W


def _vmem_limit(block_bytes, temp_bytes):
    need = int(block_bytes + temp_bytes)
    return min(max(need, 16 * 1024 * 1024), V7X_VMEM_BYTES - 8 * 1024 * 1024)


def _nbytes(shape, dtype):
    return int(np.prod(shape)) * jnp.dtype(dtype).itemsize


def _silu(x):
    return x * jax.nn.sigmoid(x)


def _split_bf16(a):
    hi = a.astype(BF16)
    lo = (a - hi.astype(F32)).astype(BF16)
    return hi, lo


def _dot(a, b):
    return jnp.dot(a, b, preferred_element_type=F32)


def _dot_split(a_hi, a_lo, b):
    b_hi, b_lo = _split_bf16(b)
    return _dot(a_hi, b_hi) + (_dot(a_hi, b_lo) + _dot(a_lo, b_hi))


@functools.lru_cache(maxsize=None)
def _dft_constants(L):
    n = 2 * L
    k = np.arange(L, dtype=np.float64)[:, None]
    s = np.arange(L, dtype=np.float64)[None, :]
    ang = 2.0 * np.pi * k * s / n
    nyq = np.cos(np.pi * np.arange(L, dtype=np.float64))
    C = np.cos(ang)
    S = np.sin(ang)
    S[0, :] = nyq
    F = np.concatenate([C, S], axis=0)
    w = np.full((L,), 2.0)
    w[0] = 1.0
    Gc = (C * w[:, None]).T / n
    Ss = 2.0 * np.sin(ang)
    Ss[0, :] = nyq
    Gs = Ss.T / n
    G = np.concatenate([Gc, Gs], axis=1)

    def split(a):
        a32 = a.astype(np.float32)
        hi = a32.astype(BF16)
        lo = (a32 - hi.astype(np.float32)).astype(BF16)
        return hi, lo

    c_hi, c_lo = split(C)
    s_hi, s_lo = split(S)
    return dict(F=F.astype(np.float32).astype(BF16), G=G.astype(np.float32).astype(BF16),
                c_hi=c_hi, c_lo=c_lo, s_hi=s_hi, s_lo=s_lo)


@functools.lru_cache(maxsize=None)
def _filter_constants(L):
    t = np.linspace(0.0, 1.0, L, dtype=np.float32)[:, None]
    bands = np.linspace(1e-4, FILTER_BANDS - 1, FILTER_BANDS, dtype=np.float32)[None]
    w = (2.0 * math.pi * np.arange(L, dtype=np.float32)[:, None] / L).astype(np.float32)
    z = np.concatenate([t, np.cos(bands * w), -np.sin(bands * w)], axis=-1).astype(np.float32)
    z_pad = np.zeros((L, LANES), np.float32)
    z_pad[:, :FILTER_EMB] = z
    z_hi = z_pad.astype(BF16)
    z_lo = (z_pad - z_hi.astype(np.float32)).astype(BF16)
    deltas = np.linspace(MIN_DECAY, MAX_DECAY, D_HYENA, dtype=np.float32)
    decay = (np.exp(-t * np.abs(deltas)) + np.float32(DECAY_SHIFT)).astype(np.float32)
    return z_hi, z_lo, decay


@functools.lru_cache(maxsize=None)
def _rpb_constants():
    c = np.arange(GRID_W)[:, None]
    kc = np.arange(GRID_W)[None, :]
    c0 = np.clip(c - WIN_COLS // 2, 0, GRID_W - WIN_COLS)
    in_win = (kc >= c0) & (kc < c0 + WIN_COLS)
    idx = np.clip(kc - c + WIN_COLS - 1, 0, 2 * WIN_COLS - 2)
    onehot = np.zeros((LANES, GRID_W, LANES), np.float32)
    for half in range(HEADS_PER_TILE):
        lanes = half * GRID_W + np.arange(GRID_W)
        onehot[idx, c, lanes[None, :]] = 1.0
    mask = np.concatenate([in_win, in_win], axis=1).astype(np.float32)
    return onehot.reshape(LANES, GRID_W * LANES).astype(BF16), mask.reshape(1, GRID_W * LANES)


def _mod_kernel(cond_ref, w_ref, b_ref, o_ref):
    s = _silu(cond_ref[...]).astype(BF16)
    o_ref[...] = _dot(s, w_ref[...].astype(BF16)) + b_ref[...]


def _modulation(cond, w_ada, b_ada):
    rows = cond.shape[0]
    return pl.pallas_call(
        _mod_kernel,
        out_shape=jax.ShapeDtypeStruct((rows, 3 * D_MODEL), F32),
        grid=(3,),
        in_specs=[pl.BlockSpec((rows, D_MODEL), lambda j: (0, 0)),
                  pl.BlockSpec((D_MODEL, D_MODEL), lambda j: (0, j)),
                  pl.BlockSpec((1, D_MODEL), lambda j: (0, j))],
        out_specs=pl.BlockSpec((rows, D_MODEL), lambda j: (0, j)),
        name="mod",
    )(cond, w_ada, b_ada.reshape(1, -1))


def _spectra_kernel(zh_ref, zl_ref, w1_ref, b1_ref, w2_ref, b2_ref, w3f_ref, w3b_ref, freq_ref, decay_ref,
                    ch_ref, cl_ref, sh_ref, sl_ref, fp_ref, fqa_ref, fpb_ref):
    L = decay_ref.shape[0]
    f0 = freq_ref[0:1, :]
    f1 = freq_ref[1:2, :]
    hf = jnp.sin(f0 * (_dot_split(zh_ref[...], zl_ref[...], w1_ref[...]) + b1_ref[...]))
    h_hi, h_lo = _split_bf16(hf)
    hf = jnp.sin(f1 * (_dot_split(h_hi, h_lo, w2_ref[...]) + b2_ref[...]))
    h_hi, h_lo = _split_bf16(hf)
    decay = decay_ref[...]
    h_fwd = _dot_split(h_hi, h_lo, w3f_ref[...]) * decay
    h_bwd = _dot_split(h_hi, h_lo, w3b_ref[...]) * decay
    row0 = jax.lax.broadcasted_iota(jnp.int32, (L, 1), 0) == 0
    h_bwd = jnp.where(row0, 0.0, h_bwd)
    fsum = h_fwd + h_bwd
    fdif = h_fwd - h_bwd
    fp = _dot_split(ch_ref[...], cl_ref[...], fsum)
    fq = _dot_split(sh_ref[...], sl_ref[...], fdif)
    nyq = _dot_split(sh_ref[0:8, :], sl_ref[0:8, :], fsum)[0:1, :]
    fp_ref[...] = fp
    fqa_ref[...] = jnp.where(row0, 0.0, fq)
    fpb_ref[...] = jnp.where(row0, nyq, fp)


def _filter_spectra(L, w1, b1, w2, b2, w3, freq):
    tc = 256
    z_hi, z_lo, decay = _filter_constants(L)
    dft = _dft_constants(L)
    w1p = jnp.zeros((LANES, FILTER_HIDDEN), F32).at[:FILTER_EMB].set(w1)
    nblk = D_HYENA // tc
    const = lambda shape: pl.BlockSpec(shape, lambda c: (0, 0), pipeline_mode=pl.Buffered(1))
    out_spec = pl.BlockSpec((L, tc), lambda c: (0, c))
    blocks = (4 * _nbytes((L, L), BF16) + 2 * _nbytes((L, LANES), BF16) + 2 * 4 * _nbytes((L, tc), F32))
    return pl.pallas_call(
        _spectra_kernel,
        out_shape=[jax.ShapeDtypeStruct((L, D_HYENA), F32)] * 3,
        grid=(nblk,),
        in_specs=[const((L, LANES)), const((L, LANES)),
                  const((LANES, FILTER_HIDDEN)), const((1, FILTER_HIDDEN)),
                  const((FILTER_HIDDEN, FILTER_HIDDEN)), const((1, FILTER_HIDDEN)),
                  pl.BlockSpec((FILTER_HIDDEN, tc), lambda c: (0, c)),
                  pl.BlockSpec((FILTER_HIDDEN, tc), lambda c: (0, nblk + c)),
                  const((2, FILTER_HIDDEN)),
                  pl.BlockSpec((L, tc), lambda c: (0, c)),
                  const((L, L)), const((L, L)), const((L, L)), const((L, L))],
        out_specs=[out_spec] * 3,
        compiler_params=pltpu.CompilerParams(
            dimension_semantics=("arbitrary",),
            vmem_limit_bytes=_vmem_limit(blocks, 12 * _nbytes((L, tc), F32))),
        name=f"spectra_{L}",
    )(z_hi, z_lo, w1p, b1.reshape(1, -1), w2, b2.reshape(1, -1), w3, w3, freq, decay,
      dft["c_hi"], dft["c_lo"], dft["s_hi"], dft["s_lo"])


def _rpb_kernel(r_ref, onehot_ref, mask_ref, o_ref):
    r = r_ref[...]
    r_hi = r.astype(BF16)
    rem = r - r_hi.astype(F32)
    r_mid = rem.astype(BF16)
    r_lo = (rem - r_mid.astype(F32)).astype(BF16)
    oh = onehot_ref[...]
    t = _dot(r_hi, oh) + (_dot(r_mid, oh) + _dot(r_lo, oh))
    o_ref[...] = jnp.where(mask_ref[...] > 0.0, t, NEG_INF)


def _rpb_tiles(rpb):
    n_dr = 2 * WIN_ROWS - 1
    rows = N_HEADS * n_dr
    onehot, mask = _rpb_constants()
    r = jnp.zeros((rows, LANES), F32).at[:, :2 * WIN_COLS - 1].set(rpb.reshape(rows, 2 * WIN_COLS - 1))
    tn = 2048
    out = pl.pallas_call(
        _rpb_kernel,
        out_shape=jax.ShapeDtypeStruct((rows, GRID_W * LANES), F32),
        grid=(GRID_W * LANES // tn,),
        in_specs=[pl.BlockSpec((rows, LANES), lambda j: (0, 0)),
                  pl.BlockSpec((LANES, tn), lambda j: (0, j)),
                  pl.BlockSpec((1, tn), lambda j: (0, j))],
        out_specs=pl.BlockSpec((rows, tn), lambda j: (0, j)),
        name="rpb_tiles",
    )(r, onehot, mask)
    return out.reshape(N_HEADS, n_dr, GRID_W, LANES)


def _inproj_kernel(x_ref, mod_ref, w_ref, z_ref, *kv_refs):
    shift = mod_ref[:, 0:D_MODEL]
    scale = mod_ref[:, D_MODEL:2 * D_MODEL]
    h = (x_ref[...] * (1.0 + scale) + shift).astype(BF16)
    for j in range(D_IN // D_MODEL):
        cols = slice(j * D_MODEL, (j + 1) * D_MODEL)
        acc = _dot(h, w_ref[:, cols])
        z_ref[:, cols] = acc.astype(BF16)
        if kv_refs and j == COL_K:
            kv_refs[0][...] = acc
        if kv_refs and j == COL_V:
            kv_refs[1][...] = acc


def _in_projection(x2d, mod3, mod_index, w_in_bf16, *, emit_kv, tm=256):
    M = x2d.shape[0]
    out_shape = [jax.ShapeDtypeStruct((M, D_IN), BF16)]
    out_specs = [pl.BlockSpec((tm, D_IN), lambda i: (i, 0))]
    if emit_kv:
        out_shape += [jax.ShapeDtypeStruct((M, D_ATTN), F32)] * 2
        out_specs += [pl.BlockSpec((tm, D_ATTN), lambda i: (i, 0))] * 2
    blocks = (_nbytes((D_MODEL, D_IN), BF16) + 2 * _nbytes((tm, D_MODEL), F32) + 2 * _nbytes((tm, D_IN), BF16)
              + (4 * _nbytes((tm, D_ATTN), F32) if emit_kv else 0))
    return pl.pallas_call(
        _inproj_kernel,
        out_shape=out_shape,
        grid=(M // tm,),
        in_specs=[pl.BlockSpec((tm, D_MODEL), lambda i: (i, 0)),
                  pl.BlockSpec((None, 1, 3 * D_MODEL), lambda i: (mod_index(i), 0, 0)),
                  pl.BlockSpec((D_MODEL, D_IN), lambda i: (0, 0), pipeline_mode=pl.Buffered(1))],
        out_specs=out_specs,
        compiler_params=pltpu.CompilerParams(
            dimension_semantics=("arbitrary",),
            vmem_limit_bytes=_vmem_limit(blocks, 4 * _nbytes((tm, D_MODEL), F32))),
        name="inproj",
    )(x2d, mod3, w_in_bf16)


def _hyena_kernel(vh_ref, x1_ref, x0_ref, gh_ref, cwv_ref, cw1_ref, cw0_ref, cbv_ref, cb1_ref, cb0_ref, d_ref,
                  f_ref, g_ref, fp_ref, fqa_ref, fpb_ref, o_ref):
    L = vh_ref.shape[0]
    row = jax.lax.broadcasted_iota(jnp.int32, (L, 1), 0)
    first = row == 0
    last = row == L - 1

    def short_conv(x_ref, w_ref, b_ref):
        x = x_ref[...].astype(F32)
        prev = jnp.where(first, 0.0, pltpu.roll(x, 1, 0))
        nxt = jnp.where(last, 0.0, pltpu.roll(x, L - 1, 0))
        return prev * w_ref[0:1, :] + x * w_ref[1:2, :] + nxt * w_ref[2:3, :] + b_ref[...]

    u = short_conv(vh_ref, cwv_ref, cbv_ref) * short_conv(x1_ref, cw1_ref, cb1_ref)
    t = _dot(f_ref[...], u.astype(BF16))
    p = t[:L]
    q = t[L:]
    yp = p * fp_ref[...] - q * fqa_ref[...]
    yq = p * fqa_ref[...] + q * fpb_ref[...]
    y = _dot(g_ref[...], jnp.concatenate([yp.astype(BF16), yq.astype(BF16)], axis=0))
    y_h = (y + u * d_ref[...]) * short_conv(x0_ref, cw0_ref, cb0_ref)
    o_ref[...] = (y_h * _silu(gh_ref[...].astype(F32))).astype(BF16)


def _hyena_branch(z, B, L, conv_w, conv_b, hyena_d, spectra, *, tc=256):
    dft = _dft_constants(L)
    nblk = D_HYENA // tc
    zcol = lambda piece: pl.BlockSpec((L, tc), lambda c, b: (b, piece * nblk + c))
    wcol = lambda rows, piece: pl.BlockSpec((rows, tc), lambda c, b: (0, piece * nblk + c))
    chan = lambda rows: pl.BlockSpec((rows, tc), lambda c, b: (0, c))
    const = lambda shape: pl.BlockSpec(shape, lambda c, b: (0, 0), pipeline_mode=pl.Buffered(1))
    blocks = (2 * _nbytes((2 * L, L), BF16) + 2 * 5 * _nbytes((L, tc), BF16) + 2 * 3 * _nbytes((L, tc), F32))
    return pl.pallas_call(
        _hyena_kernel,
        out_shape=jax.ShapeDtypeStruct((B * L, D_HYENA), BF16),
        grid=(nblk, B),
        in_specs=[zcol(COL_VH), zcol(COL_X1), zcol(COL_X0), zcol(COL_GH),
                  wcol(3, 0), wcol(3, 1), wcol(3, 2), wcol(1, 0), wcol(1, 1), wcol(1, 2), chan(1),
                  const((2 * L, L)), const((L, 2 * L)), chan(L), chan(L), chan(L)],
        out_specs=pl.BlockSpec((L, tc), lambda c, b: (b, c)),
        compiler_params=pltpu.CompilerParams(
            dimension_semantics=("arbitrary", "arbitrary"),
            vmem_limit_bytes=_vmem_limit(blocks, 16 * _nbytes((L, tc), F32))),
        name=f"hyena_{L}",
    )(z, z, z, z, conv_w, conv_w, conv_w, conv_b.reshape(1, -1), conv_b.reshape(1, -1), conv_b.reshape(1, -1),
      hyena_d.reshape(1, -1), dft["F"], dft["G"], *spectra)


def _head_masks():
    lane = jax.lax.broadcasted_iota(jnp.int32, (1, LANES), 1)
    return [(lane >= h * HEAD_DIM) & (lane < (h + 1) * HEAD_DIM) for h in range(HEADS_PER_TILE)]


def _qk(q, k):
    return jax.lax.dot_general(q, k, (((1,), (1,)), ((), ())), preferred_element_type=F32)


def _ctx_attn_kernel(q_ref, k_ref, v_ref, g_ref, o_ref):
    masks = _head_masks()

    def tile(t, carry):
        cols = pl.ds(pl.multiple_of(t * LANES, LANES), LANES)
        q = q_ref[:, cols] * jnp.asarray(ATTN_SCALE, BF16)
        k = k_ref[:, cols]
        v = v_ref[:, cols]
        out = jnp.zeros(q.shape, F32)
        for msk in masks:
            s = _qk(jnp.where(msk, q, jnp.zeros_like(q)), k)
            m = jnp.max(s, axis=-1, keepdims=True)
            p = jnp.exp(s - m)
            l = jnp.sum(p, axis=-1, keepdims=True)
            o = _dot(p.astype(BF16), jnp.where(msk, v, jnp.zeros_like(v)))
            out = out + o / l
        o_ref[:, cols] = (out * _silu(g_ref[:, cols].astype(F32))).astype(BF16)
        return carry

    jax.lax.fori_loop(0, N_HEAD_TILES, tile, 0)


def _context_attention(z, B, L):
    zcol = lambda piece: pl.BlockSpec((L, D_ATTN), lambda b: (b, piece))
    return pl.pallas_call(
        _ctx_attn_kernel,
        out_shape=jax.ShapeDtypeStruct((B * L, D_ATTN), BF16),
        grid=(B,),
        in_specs=[zcol(COL_Q), zcol(COL_K), zcol(COL_V), zcol(COL_GA)],
        out_specs=pl.BlockSpec((L, D_ATTN), lambda b: (b, 0)),
        compiler_params=pltpu.CompilerParams(dimension_semantics=("arbitrary",)),
        name="ctx_attn",
    )(z, z, z, z)


def _key_window_start(qb):
    rows = 1024 // GRID_W
    r_first = qb * Q_BLOCK_ROWS
    r_last = r_first + Q_BLOCK_ROWS - 1
    lo = min(max(r_first - WIN_ROWS // 2, 0), rows - WIN_ROWS)
    hi = min(max(r_last - WIN_ROWS // 2, 0), rows - WIN_ROWS) + WIN_ROWS
    start = min(lo, rows - KEY_WIN_ROWS)
    start -= start % Q_BLOCK_ROWS
    assert start <= lo and hi <= start + KEY_WIN_ROWS
    return start


def _nbr_attn_kernel(q_ref, k0_ref, k1_ref, k2_ref, v0_ref, v1_ref, v2_ref, kc_ref, vc_ref, g_ref, tiles_ref,
                     o_ref, bias_ref, *, n_rows):
    qb = pl.program_id(1)
    masks = _head_masks()
    n_qb = n_rows // Q_BLOCK_ROWS

    for qb_static in range(n_qb):
        @pl.when(qb == qb_static)
        def _(qb_static=qb_static):
            k_start = _key_window_start(qb_static)
            for i in range(Q_BLOCK_ROWS):
                r = qb_static * Q_BLOCK_ROWS + i
                r0 = min(max(r - WIN_ROWS // 2, 0), n_rows - WIN_ROWS)
                for j in range(KEY_WIN_ROWS):
                    kr = k_start + j
                    rows = slice(i * GRID_W, (i + 1) * GRID_W)
                    cols = slice(j * GRID_W, (j + 1) * GRID_W)
                    lanes = slice((j % HEADS_PER_TILE) * GRID_W, (j % HEADS_PER_TILE + 1) * GRID_W)
                    for h in range(HEADS_PER_TILE):
                        if r0 <= kr < r0 + WIN_ROWS:
                            bias_ref[h, rows, cols] = tiles_ref[h, kr - r + WIN_ROWS - 1, :, lanes]
                        else:
                            bias_ref[h, rows, cols] = jnp.full((GRID_W, GRID_W), NEG_INF, F32)

    k_refs = (k0_ref, k1_ref, k2_ref)
    v_refs = (v0_ref, v1_ref, v2_ref)
    n_batch = q_ref.shape[0]
    for b in range(n_batch):
        q = q_ref[b] * jnp.asarray(ATTN_SCALE, BF16)
        kc = kc_ref[b].astype(BF16)
        vc = vc_ref[b].astype(BF16)
        out = jnp.zeros(q.shape, F32)
        for h, msk in enumerate(masks):
            qh = jnp.where(msk, q, jnp.zeros_like(q))
            s = [_qk(qh, k_refs[j][b]) + bias_ref[h, :, j * Q_BLOCK:(j + 1) * Q_BLOCK] for j in range(3)]
            s.append(_qk(qh, kc))
            m = functools.reduce(jnp.maximum, [jnp.max(x, axis=-1, keepdims=True) for x in s])
            p = [jnp.exp(x - m) for x in s]
            l = functools.reduce(jnp.add, [jnp.sum(x, axis=-1, keepdims=True) for x in p])
            vals = [v_refs[j][b] for j in range(3)] + [vc]
            o = functools.reduce(jnp.add, [_dot(pj.astype(BF16), jnp.where(msk, vj, jnp.zeros_like(vj)))
                                           for pj, vj in zip(p, vals)])
            out = out + o / l
        o_ref[b] = (out * _silu(g_ref[b].astype(F32))).astype(BF16)


def _neighbourhood_attention(z, B, L, cache_k, cache_v, tiles):
    n_rows = L // GRID_W
    n_qb = L // Q_BLOCK
    assert KEY_WIN == 3 * Q_BLOCK and all(_key_window_start(qb) % Q_BLOCK_ROWS == 0 for qb in range(n_qb))
    z3 = z.reshape(B, L, D_IN)
    kc = cache_k.reshape(B, -1, D_ATTN)
    vc = cache_v.reshape(B, -1, D_ATTN)
    n_ctx = kc.shape[1]
    tiles_per_col = D_ATTN // LANES

    def win_block(qb):
        return jnp.where(qb < n_qb // 2, 0, (n_rows - KEY_WIN_ROWS) // Q_BLOCK_ROWS)

    assert [_key_window_start(qb) // Q_BLOCK_ROWS for qb in range(n_qb)] == \
        [0 if qb < n_qb // 2 else (n_rows - KEY_WIN_ROWS) // Q_BLOCK_ROWS for qb in range(n_qb)]
    qspec = lambda piece: pl.BlockSpec((B, Q_BLOCK, LANES), lambda t, qb: (0, qb, piece * tiles_per_col + t))
    kspec = lambda piece, j: pl.BlockSpec((B, Q_BLOCK, LANES),
                                          lambda t, qb: (0, win_block(qb) + j, piece * tiles_per_col + t))
    cspec = pl.BlockSpec((B, n_ctx, LANES), lambda t, qb: (0, 0, t))
    out = pl.pallas_call(
        functools.partial(_nbr_attn_kernel, n_rows=n_rows),
        out_shape=jax.ShapeDtypeStruct((B, L, D_ATTN), BF16),
        grid=(N_HEAD_TILES, n_qb),
        in_specs=[qspec(COL_Q), kspec(COL_K, 0), kspec(COL_K, 1), kspec(COL_K, 2),
                  kspec(COL_V, 0), kspec(COL_V, 1), kspec(COL_V, 2), cspec, cspec, qspec(COL_GA),
                  pl.BlockSpec((HEADS_PER_TILE, 2 * WIN_ROWS - 1, GRID_W, LANES), lambda t, qb: (t, 0, 0, 0))],
        out_specs=pl.BlockSpec((B, Q_BLOCK, LANES), lambda t, qb: (0, qb, t)),
        scratch_shapes=[pltpu.VMEM((HEADS_PER_TILE, Q_BLOCK, KEY_WIN), F32)],
        compiler_params=pltpu.CompilerParams(dimension_semantics=("arbitrary", "arbitrary")),
        name="nbr_attn",
    )(z3, z3, z3, z3, z3, z3, z3, kc, vc, z3, tiles)
    return out.reshape(B * L, D_ATTN)


def _merge_kernel(yh_ref, ya_ref, mh_ref, ma_ref, x_ref, mod_ref, wbh_ref, wba_ref, wout_ref, lng_ref, lnb_ref,
                  o_ref, *, alpha):
    p_h = _dot(yh_ref[...], wbh_ref[...])
    p_a = _dot(ya_ref[...], wba_ref[...])
    m_h = jax.nn.sigmoid(mh_ref[...].astype(F32))
    m_a = jax.nn.sigmoid(ma_ref[...].astype(F32))
    out = _dot((m_h * p_h + m_a * p_a).astype(BF16), wout_ref[...])
    gate = mod_ref[:, 2 * D_MODEL:3 * D_MODEL]
    v = alpha * x_ref[...] + gate * out
    mu = jnp.mean(v, axis=-1, keepdims=True)
    d = v - mu
    var = jnp.mean(d * d, axis=-1, keepdims=True)
    o_ref[...] = d * jax.lax.rsqrt(var + LN_EPS) * lng_ref[...] + lnb_ref[...]


def _merge(yh, ya, z, x2d, mod3, mod_index, w_bh, w_ba, w_out, ln_g, ln_b, *, alpha, tm=512):
    M = x2d.shape[0]
    tok = lambda width, col: pl.BlockSpec((tm, width), lambda i: (i, col))
    const = lambda shape: pl.BlockSpec(shape, lambda i: (0, 0))
    return pl.pallas_call(
        functools.partial(_merge_kernel, alpha=alpha),
        out_shape=jax.ShapeDtypeStruct((M, D_MODEL), F32),
        grid=(M // tm,),
        in_specs=[tok(D_HYENA, 0), tok(D_ATTN, 0), tok(D_MODEL, COL_MH), tok(D_MODEL, COL_MA), tok(D_MODEL, 0),
                  pl.BlockSpec((None, 1, 3 * D_MODEL), lambda i: (mod_index(i), 0, 0)),
                  const((D_HYENA, D_MODEL)), const((D_ATTN, D_MODEL)), const((D_MODEL, D_MODEL)),
                  const((1, D_MODEL)), const((1, D_MODEL))],
        out_specs=pl.BlockSpec((tm, D_MODEL), lambda i: (i, 0)),
        compiler_params=pltpu.CompilerParams(
            dimension_semantics=("arbitrary",),
            vmem_limit_bytes=_vmem_limit(
                2 * (4 * _nbytes((tm, D_MODEL), BF16) + 2 * _nbytes((tm, D_MODEL), F32)
                     + 3 * _nbytes((D_MODEL, D_MODEL), BF16)),
                8 * _nbytes((tm, D_MODEL), F32))),
        name="merge",
    )(yh, ya, z, z, x2d, mod3, w_bh, w_ba, w_out, ln_g.reshape(1, -1), ln_b.reshape(1, -1))


def _layer(x, mod3, mod_index_in, mod_index_merge, ctx_kv, p, spectra, tiles, *, alpha):
    B, L, _ = x.shape
    x2d = x.reshape(B * L, D_MODEL)
    outs = _in_projection(x2d, mod3, mod_index_in, p["w_in"], emit_kv=ctx_kv is None)
    z = outs[0]
    yh = _hyena_branch(z, B, L, p["conv_w"], p["conv_b"], p["hyena_d"], spectra)
    if ctx_kv is None:
        ya = _context_attention(z, B, L)
    else:
        ya = _neighbourhood_attention(z, B, L, ctx_kv[0], ctx_kv[1], tiles)
    y = _merge(yh, ya, z, x2d, mod3, mod_index_merge, p["w_bh"], p["w_ba"], p["w_out"], p["ln_g"], p["ln_b"],
               alpha=alpha)
    return y.reshape(B, L, D_MODEL), outs[1:]


def kernel(x_prompt, x_sample, c, cache_k, cache_v, c_ctx, w_ada, b_ada, w_in, conv_w, conv_b, filt_w1, filt_b1,
           filt_w2, filt_b2, filt_w3, filt_freq, hyena_d, rpb, w_bh, w_ba, w_out, ln_g, ln_b):
    depth = w_in.shape[0]
    alpha = (2.0 * depth) ** 0.25
    n_lat, lat_len = x_sample.shape[0], x_sample.shape[1]
    n_ctx, ctx_len = x_prompt.shape[0], x_prompt.shape[1]
    ctx_row = n_lat
    cond = jnp.zeros((8, D_MODEL), F32).at[:n_lat].set(c).at[ctx_row].set(c_ctx)
    in_tm, merge_tm = 256, 512

    xp, xs = x_prompt, x_sample
    new_k, new_v = [], []
    for l in range(depth):
        p = {"w_in": w_in[l].astype(BF16), "conv_w": conv_w[l], "conv_b": conv_b[l], "hyena_d": hyena_d[l],
             "w_bh": w_bh[l].astype(BF16), "w_ba": w_ba[l].astype(BF16), "w_out": w_out[l].astype(BF16),
             "ln_g": ln_g[l], "ln_b": ln_b[l]}
        mod3 = _modulation(cond, w_ada[l], b_ada[l]).reshape(8, 1, 3 * D_MODEL)
        filt = (filt_w1[l], filt_b1[l], filt_w2[l], filt_b2[l], filt_w3[l], filt_freq[l])
        spectra_ctx = _filter_spectra(ctx_len, *filt)
        spectra_lat = _filter_spectra(lat_len, *filt)
        tiles = _rpb_tiles(rpb[l])
        xp, (k_ctx, v_ctx) = _layer(xp, mod3, lambda i: ctx_row, lambda i: ctx_row, None, p, spectra_ctx, None,
                                    alpha=alpha)
        new_k.append(k_ctx.reshape(n_ctx, ctx_len, N_HEADS, HEAD_DIM))
        new_v.append(v_ctx.reshape(n_ctx, ctx_len, N_HEADS, HEAD_DIM))
        xs, _ = _layer(xs, mod3, lambda i: i // (lat_len // in_tm), lambda i: i // (lat_len // merge_tm),
                       (cache_k[:, l], cache_v[:, l]), p, spectra_lat, tiles, alpha=alpha)
    return xp, xs, jnp.stack(new_k, axis=1), jnp.stack(new_v, axis=1)
```

```python
import functools
import math

import jax
import jax.numpy as jnp
import numpy as np
from jax.experimental import pallas as pl
from jax.experimental.pallas import tpu as pltpu

F32 = jnp.float32
BF16 = jnp.bfloat16

D_MODEL = 1024
D_HYENA = 1024
N_HEADS = 16
HEAD_DIM = 64
D_ATTN = N_HEADS * HEAD_DIM
D_IN = 4 * D_HYENA + 4 * D_ATTN + 2 * D_MODEL
GRID_W = 64
WIN_ROWS = 8
WIN_COLS = 16
FILTER_EMB = 33
FILTER_BANDS = (FILTER_EMB - 1) // 2
FILTER_HIDDEN = 64
DECAY_TARGET = 1e-2
MIN_DECAY = math.log(DECAY_TARGET) / 1.5
MAX_DECAY = math.log(DECAY_TARGET) / 0.3
DECAY_SHIFT = 0.05
LN_EPS = 1e-5
NEG_INF = -1e30

COL_VH, COL_X1, COL_X0, COL_GH, COL_Q, COL_K, COL_V, COL_GA, COL_MH, COL_MA = range(10)

LANES = 128
HEADS_PER_TILE = LANES // HEAD_DIM
N_HEAD_TILES = N_HEADS // HEADS_PER_TILE
V7X_VMEM_BYTES = 64 * 1024 * 1024
ATTN_SCALE = HEAD_DIM ** -0.5

Q_BLOCK = 256
Q_BLOCK_ROWS = Q_BLOCK // GRID_W
KEY_WIN_ROWS = 12
KEY_WIN = KEY_WIN_ROWS * GRID_W

IN_TM = 256
MERGE_TM = 512
HYENA_ROWS_PER_STEP = 2048


def _vmem_limit(block_bytes, temp_bytes):
    need = int(block_bytes + temp_bytes)
    return min(max(need, 16 * 1024 * 1024), V7X_VMEM_BYTES - 8 * 1024 * 1024)


def _nbytes(shape, dtype):
    return int(np.prod(shape)) * jnp.dtype(dtype).itemsize


def _silu(x):
    return x * jax.nn.sigmoid(x)


def _split_bf16(a):
    hi = a.astype(BF16)
    lo = (a - hi.astype(F32)).astype(BF16)
    return hi, lo


def _dot(a, b):
    return jnp.dot(a, b, preferred_element_type=F32)


def _dot_split(a_hi, a_lo, b):
    b_hi, b_lo = _split_bf16(b)
    return _dot(a_hi, b_hi) + (_dot(a_hi, b_lo) + _dot(a_lo, b_hi))


@functools.lru_cache(maxsize=None)
def _dft_constants(L):
    n = 2 * L
    k = np.arange(L, dtype=np.float64)[:, None]
    s = np.arange(L, dtype=np.float64)[None, :]
    ang = 2.0 * np.pi * k * s / n
    nyq = np.cos(np.pi * np.arange(L, dtype=np.float64))
    C = np.cos(ang)
    S = np.sin(ang)
    S[0, :] = nyq
    F = np.concatenate([C, S], axis=0)
    w = np.full((L,), 2.0)
    w[0] = 1.0
    Gc = (C * w[:, None]).T / n
    Ss = 2.0 * np.sin(ang)
    Ss[0, :] = nyq
    Gs = Ss.T / n
    G = np.concatenate([Gc, Gs], axis=1)

    def split(a):
        a32 = a.astype(np.float32)
        hi = a32.astype(BF16)
        lo = (a32 - hi.astype(np.float32)).astype(BF16)
        return hi, lo

    c_hi, c_lo = split(C)
    s_hi, s_lo = split(S)
    return dict(F=F.astype(np.float32).astype(BF16), G=G.astype(np.float32).astype(BF16),
                c_hi=c_hi, c_lo=c_lo, s_hi=s_hi, s_lo=s_lo)


@functools.lru_cache(maxsize=None)
def _filter_constants(L):
    t = np.linspace(0.0, 1.0, L, dtype=np.float32)[:, None]
    bands = np.linspace(1e-4, FILTER_BANDS - 1, FILTER_BANDS, dtype=np.float32)[None]
    w = (2.0 * math.pi * np.arange(L, dtype=np.float32)[:, None] / L).astype(np.float32)
    z = np.concatenate([t, np.cos(bands * w), -np.sin(bands * w)], axis=-1).astype(np.float32)
    z_pad = np.zeros((L, LANES), np.float32)
    z_pad[:, :FILTER_EMB] = z
    z_hi = z_pad.astype(BF16)
    z_lo = (z_pad - z_hi.astype(np.float32)).astype(BF16)
    deltas = np.linspace(MIN_DECAY, MAX_DECAY, D_HYENA, dtype=np.float32)
    decay = (np.exp(-t * np.abs(deltas)) + np.float32(DECAY_SHIFT)).astype(np.float32)
    return z_hi, z_lo, decay


@functools.lru_cache(maxsize=None)
def _rpb_constants():
    c = np.arange(GRID_W)[:, None]
    kc = np.arange(GRID_W)[None, :]
    c0 = np.clip(c - WIN_COLS // 2, 0, GRID_W - WIN_COLS)
    in_win = (kc >= c0) & (kc < c0 + WIN_COLS)
    idx = np.clip(kc - c + WIN_COLS - 1, 0, 2 * WIN_COLS - 2)
    onehot = np.zeros((LANES, GRID_W, LANES), np.float32)
    for half in range(HEADS_PER_TILE):
        lanes = half * GRID_W + np.arange(GRID_W)
        onehot[idx, c, lanes[None, :]] = 1.0
    mask = np.concatenate([in_win, in_win], axis=1).astype(np.float32)
    return onehot.reshape(LANES, GRID_W * LANES).astype(BF16), mask.reshape(1, GRID_W * LANES)


def _mod_kernel(cond_ref, w_ref, b_ref, o_ref):
    s = _silu(cond_ref[...]).astype(BF16)
    o_ref[...] = _dot(s, w_ref[...].astype(BF16)) + b_ref[...]


def _modulation(cond, w_ada, b_ada):
    rows = cond.shape[0]
    return pl.pallas_call(
        _mod_kernel,
        out_shape=jax.ShapeDtypeStruct((rows, 3 * D_MODEL), F32),
        grid=(3,),
        in_specs=[pl.BlockSpec((rows, D_MODEL), lambda j: (0, 0)),
                  pl.BlockSpec((D_MODEL, D_MODEL), lambda j: (0, j)),
                  pl.BlockSpec((1, D_MODEL), lambda j: (0, j))],
        out_specs=pl.BlockSpec((rows, D_MODEL), lambda j: (0, j)),
        name="mod",
    )(cond, w_ada, b_ada.reshape(1, -1))


def _spectra_kernel(zh_ref, zl_ref, w1_ref, b1_ref, w2_ref, b2_ref, w3f_ref, w3b_ref, freq_ref, decay_ref,
                    ch_ref, cl_ref, sh_ref, sl_ref, fp_ref, fqa_ref, fpb_ref):
    L = decay_ref.shape[0]
    f0 = freq_ref[0:1, :]
    f1 = freq_ref[1:2, :]
    hf = jnp.sin(f0 * (_dot_split(zh_ref[...], zl_ref[...], w1_ref[...]) + b1_ref[...]))
    h_hi, h_lo = _split_bf16(hf)
    hf = jnp.sin(f1 * (_dot_split(h_hi, h_lo, w2_ref[...]) + b2_ref[...]))
    h_hi, h_lo = _split_bf16(hf)
    decay = decay_ref[...]
    h_fwd = _dot_split(h_hi, h_lo, w3f_ref[...]) * decay
    h_bwd = _dot_split(h_hi, h_lo, w3b_ref[...]) * decay
    row0 = jax.lax.broadcasted_iota(jnp.int32, (L, 1), 0) == 0
    h_bwd = jnp.where(row0, 0.0, h_bwd)
    fsum = h_fwd + h_bwd
    fdif = h_fwd - h_bwd
    fp = _dot_split(ch_ref[...], cl_ref[...], fsum)
    fq = _dot_split(sh_ref[...], sl_ref[...], fdif)
    nyq = _dot_split(sh_ref[0:8, :], sl_ref[0:8, :], fsum)[0:1, :]
    fp_ref[...] = fp
    fqa_ref[...] = jnp.where(row0, 0.0, fq)
    fpb_ref[...] = jnp.where(row0, nyq, fp)


def _filter_spectra(L, w1, b1, w2, b2, w3, freq):
    tc = 256
    z_hi, z_lo, decay = _filter_constants(L)
    dft = _dft_constants(L)
    w1p = jnp.zeros((LANES, FILTER_HIDDEN), F32).at[:FILTER_EMB].set(w1)
    nblk = D_HYENA // tc
    const = lambda shape: pl.BlockSpec(shape, lambda c: (0, 0), pipeline_mode=pl.Buffered(1))
    out_spec = pl.BlockSpec((L, tc), lambda c: (0, c))
    blocks = (4 * _nbytes((L, L), BF16) + 2 * _nbytes((L, LANES), BF16) + 2 * 4 * _nbytes((L, tc), F32))
    return pl.pallas_call(
        _spectra_kernel,
        out_shape=[jax.ShapeDtypeStruct((L, D_HYENA), F32)] * 3,
        grid=(nblk,),
        in_specs=[const((L, LANES)), const((L, LANES)),
                  const((LANES, FILTER_HIDDEN)), const((1, FILTER_HIDDEN)),
                  const((FILTER_HIDDEN, FILTER_HIDDEN)), const((1, FILTER_HIDDEN)),
                  pl.BlockSpec((FILTER_HIDDEN, tc), lambda c: (0, c)),
                  pl.BlockSpec((FILTER_HIDDEN, tc), lambda c: (0, nblk + c)),
                  const((2, FILTER_HIDDEN)),
                  pl.BlockSpec((L, tc), lambda c: (0, c)),
                  const((L, L)), const((L, L)), const((L, L)), const((L, L))],
        out_specs=[out_spec] * 3,
        compiler_params=pltpu.CompilerParams(
            dimension_semantics=("arbitrary",),
            vmem_limit_bytes=_vmem_limit(blocks, 12 * _nbytes((L, tc), F32))),
        name=f"spectra_{L}",
    )(z_hi, z_lo, w1p, b1.reshape(1, -1), w2, b2.reshape(1, -1), w3, w3, freq, decay,
      dft["c_hi"], dft["c_lo"], dft["s_hi"], dft["s_lo"])


def _rpb_kernel(r_ref, onehot_ref, mask_ref, o_ref):
    r = r_ref[...]
    r_hi = r.astype(BF16)
    rem = r - r_hi.astype(F32)
    r_mid = rem.astype(BF16)
    r_lo = (rem - r_mid.astype(F32)).astype(BF16)
    oh = onehot_ref[...]
    t = _dot(r_hi, oh) + (_dot(r_mid, oh) + _dot(r_lo, oh))
    t = jnp.where(mask_ref[...] > 0.0, t, NEG_INF)
    for c in range(o_ref.shape[1]):
        o_ref[:, c, :] = t[:, c * LANES:(c + 1) * LANES]


def _rpb_tiles(rpb):
    n_dr = 2 * WIN_ROWS - 1
    rows = N_HEADS * n_dr
    onehot, mask = _rpb_constants()
    r = jnp.zeros((rows, LANES), F32).at[:, :2 * WIN_COLS - 1].set(rpb.reshape(rows, 2 * WIN_COLS - 1))
    cols_per_step = 16
    tn = cols_per_step * LANES
    out = pl.pallas_call(
        _rpb_kernel,
        out_shape=jax.ShapeDtypeStruct((rows, GRID_W, LANES), F32),
        grid=(GRID_W // cols_per_step,),
        in_specs=[pl.BlockSpec((rows, LANES), lambda j: (0, 0)),
                  pl.BlockSpec((LANES, tn), lambda j: (0, j)),
                  pl.BlockSpec((1, tn), lambda j: (0, j))],
        out_specs=pl.BlockSpec((rows, cols_per_step, LANES), lambda j: (0, j, 0)),
        name="rpb_tiles",
    )(r, onehot, mask)
    return out.reshape(N_HEADS, n_dr, GRID_W, LANES)


def _inproj_kernel(x_ref, mod_ref, w_ref, z_ref, *kv_refs):
    shift = mod_ref[:, 0:D_MODEL]
    scale = mod_ref[:, D_MODEL:2 * D_MODEL]
    h = (x_ref[...] * (1.0 + scale) + shift).astype(BF16)
    for j in range(D_IN // D_MODEL):
        cols = slice(j * D_MODEL, (j + 1) * D_MODEL)
        acc = _dot(h, w_ref[:, cols])
        z_ref[:, cols] = acc.astype(BF16)
        if kv_refs and j == COL_K:
            kv_refs[0][...] = acc.T
        if kv_refs and j == COL_V:
            kv_refs[1][...] = acc.T


def _in_projection(x2d, mod3, mod_index, w_in_bf16, *, emit_kv, tm):
    M = x2d.shape[0]
    out_shape = [jax.ShapeDtypeStruct((M, D_IN), BF16)]
    out_specs = [pl.BlockSpec((tm, D_IN), lambda i: (i, 0))]
    if emit_kv:
        out_shape += [jax.ShapeDtypeStruct((M // tm, D_ATTN, tm), F32)] * 2
        out_specs += [pl.BlockSpec((None, D_ATTN, tm), lambda i: (i, 0, 0))] * 2
    blocks = (_nbytes((D_MODEL, D_IN), BF16) + 2 * _nbytes((tm, D_MODEL), F32) + 2 * _nbytes((tm, D_IN), BF16)
              + (4 * _nbytes((tm, D_ATTN), F32) if emit_kv else 0))
    return pl.pallas_call(
        _inproj_kernel,
        out_shape=out_shape,
        grid=(M // tm,),
        in_specs=[pl.BlockSpec((tm, D_MODEL), lambda i: (i, 0)),
                  pl.BlockSpec((None, 1, 3 * D_MODEL), lambda i: (mod_index(i), 0, 0)),
                  pl.BlockSpec((D_MODEL, D_IN), lambda i: (0, 0), pipeline_mode=pl.Buffered(1))],
        out_specs=out_specs,
        compiler_params=pltpu.CompilerParams(
            dimension_semantics=("arbitrary",),
            vmem_limit_bytes=_vmem_limit(blocks, 4 * _nbytes((tm, D_MODEL), F32))),
        name="inproj",
    )(x2d, mod3, w_in_bf16)


def _hyena_kernel(vh_ref, x1_ref, x0_ref, gh_ref, cwv_ref, cw1_ref, cw0_ref, cbv_ref, cb1_ref, cb0_ref, d_ref,
                  f_ref, g_ref, fp_ref, fqa_ref, fpb_ref, o_ref):
    L = fp_ref.shape[0]
    row = jax.lax.broadcasted_iota(jnp.int32, (L, 1), 0)
    first = row == 0
    last = row == L - 1

    def short_conv(x_ref, rows, w_ref, b_ref):
        x = x_ref[rows, :].astype(F32)
        prev = jnp.where(first, 0.0, pltpu.roll(x, 1, 0))
        nxt = jnp.where(last, 0.0, pltpu.roll(x, L - 1, 0))
        return prev * w_ref[0:1, :] + x * w_ref[1:2, :] + nxt * w_ref[2:3, :] + b_ref[...]

    for s in range(vh_ref.shape[0] // L):
        rows = slice(s * L, (s + 1) * L)
        u = short_conv(vh_ref, rows, cwv_ref, cbv_ref) * short_conv(x1_ref, rows, cw1_ref, cb1_ref)
        t = _dot(f_ref[...], u.astype(BF16))
        p = t[:L]
        q = t[L:]
        yp = p * fp_ref[...] - q * fqa_ref[...]
        yq = p * fqa_ref[...] + q * fpb_ref[...]
        y = _dot(g_ref[...], jnp.concatenate([yp.astype(BF16), yq.astype(BF16)], axis=0))
        y_h = (y + u * d_ref[...]) * short_conv(x0_ref, rows, cw0_ref, cb0_ref)
        o_ref[rows, :] = (y_h * _silu(gh_ref[rows, :].astype(F32))).astype(BF16)


def _hyena_branch(z, B, L, conv_w, conv_b, hyena_d, spectra, *, tc=256, seqs_per_step):
    dft = _dft_constants(L)
    nblk = D_HYENA // tc
    tm = seqs_per_step * L
    zcol = lambda piece: pl.BlockSpec((tm, tc), lambda c, b: (b, piece * nblk + c))
    wcol = lambda rows, piece: pl.BlockSpec((rows, tc), lambda c, b: (0, piece * nblk + c))
    chan = lambda rows: pl.BlockSpec((rows, tc), lambda c, b: (0, c))
    const = lambda shape: pl.BlockSpec(shape, lambda c, b: (0, 0), pipeline_mode=pl.Buffered(1))
    blocks = (2 * _nbytes((2 * L, L), BF16) + 2 * 5 * _nbytes((tm, tc), BF16) + 2 * 3 * _nbytes((L, tc), F32))
    return pl.pallas_call(
        _hyena_kernel,
        out_shape=jax.ShapeDtypeStruct((B * L, D_HYENA), BF16),
        grid=(nblk, B // seqs_per_step),
        in_specs=[zcol(COL_VH), zcol(COL_X1), zcol(COL_X0), zcol(COL_GH),
                  wcol(3, 0), wcol(3, 1), wcol(3, 2), wcol(1, 0), wcol(1, 1), wcol(1, 2), chan(1),
                  const((2 * L, L)), const((L, 2 * L)), chan(L), chan(L), chan(L)],
        out_specs=pl.BlockSpec((tm, tc), lambda c, b: (b, c)),
        compiler_params=pltpu.CompilerParams(
            dimension_semantics=("arbitrary", "arbitrary"),
            vmem_limit_bytes=_vmem_limit(blocks, 16 * _nbytes((tm, tc), F32))),
        name=f"hyena_{L}",
    )(z, z, z, z, conv_w, conv_w, conv_w, conv_b.reshape(1, -1), conv_b.reshape(1, -1), conv_b.reshape(1, -1),
      hyena_d.reshape(1, -1), dft["F"], dft["G"], *spectra)


def _head_masks():
    lane = jax.lax.broadcasted_iota(jnp.int32, (1, LANES), 1)
    return [(lane >= h * HEAD_DIM) & (lane < (h + 1) * HEAD_DIM) for h in range(HEADS_PER_TILE)]


def _qk(q, k):
    return jax.lax.dot_general(q, k, (((1,), (1,)), ((), ())), preferred_element_type=F32)


def _ctx_attn_kernel(q_ref, k_ref, v_ref, g_ref, o_ref):
    masks = _head_masks()
    for t in range(N_HEAD_TILES):
        cols = slice(t * LANES, (t + 1) * LANES)
        q = q_ref[:, cols] * jnp.asarray(ATTN_SCALE, BF16)
        k = k_ref[:, cols]
        v = v_ref[:, cols]
        out = jnp.zeros(q.shape, F32)
        for msk in masks:
            s = _qk(jnp.where(msk, q, jnp.zeros_like(q)), k)
            m = jnp.max(s, axis=-1, keepdims=True)
            p = jnp.exp(s - m)
            l = jnp.sum(p, axis=-1, keepdims=True)
            o = _dot(p.astype(BF16), jnp.where(msk, v, jnp.zeros_like(v)))
            out = out + o / l
        o_ref[:, cols] = (out * _silu(g_ref[:, cols].astype(F32))).astype(BF16)


def _context_attention(z, B, L):
    zcol = lambda piece: pl.BlockSpec((L, D_ATTN), lambda b: (b, piece))
    return pl.pallas_call(
        _ctx_attn_kernel,
        out_shape=jax.ShapeDtypeStruct((B * L, D_ATTN), BF16),
        grid=(B,),
        in_specs=[zcol(COL_Q), zcol(COL_K), zcol(COL_V), zcol(COL_GA)],
        out_specs=pl.BlockSpec((L, D_ATTN), lambda b: (b, 0)),
        compiler_params=pltpu.CompilerParams(dimension_semantics=("arbitrary",)),
        name="ctx_attn",
    )(z, z, z, z)


def _key_window_start(qb):
    rows = 1024 // GRID_W
    r_first = qb * Q_BLOCK_ROWS
    r_last = r_first + Q_BLOCK_ROWS - 1
    lo = min(max(r_first - WIN_ROWS // 2, 0), rows - WIN_ROWS)
    hi = min(max(r_last - WIN_ROWS // 2, 0), rows - WIN_ROWS) + WIN_ROWS
    start = min(lo, rows - KEY_WIN_ROWS)
    start -= start % Q_BLOCK_ROWS
    assert start <= lo and hi <= start + KEY_WIN_ROWS
    return start


def _nbr_attn_kernel(q_ref, k0_ref, k1_ref, k2_ref, v0_ref, v1_ref, v2_ref, kc_ref, vc_ref, g_ref, tiles_ref,
                     o_ref, bias_ref, *, n_rows):
    qb = pl.program_id(1)
    masks = _head_masks()
    n_qb = n_rows // Q_BLOCK_ROWS

    for qb_static in range(n_qb):
        @pl.when(qb == qb_static)
        def _(qb_static=qb_static):
            k_start = _key_window_start(qb_static)
            for i in range(Q_BLOCK_ROWS):
                r = qb_static * Q_BLOCK_ROWS + i
                r0 = min(max(r - WIN_ROWS // 2, 0), n_rows - WIN_ROWS)
                for j in range(KEY_WIN_ROWS):
                    kr = k_start + j
                    rows = slice(i * GRID_W, (i + 1) * GRID_W)
                    cols = slice(j * GRID_W, (j + 1) * GRID_W)
                    lanes = slice((j % HEADS_PER_TILE) * GRID_W, (j % HEADS_PER_TILE + 1) * GRID_W)
                    for h in range(HEADS_PER_TILE):
                        if r0 <= kr < r0 + WIN_ROWS:
                            bias_ref[h, rows, cols] = tiles_ref[h, kr - r + WIN_ROWS - 1, :, lanes]
                        else:
                            bias_ref[h, rows, cols] = jnp.full((GRID_W, GRID_W), NEG_INF, F32)

    k_refs = (k0_ref, k1_ref, k2_ref)
    v_refs = (v0_ref, v1_ref, v2_ref)
    n_batch = q_ref.shape[0]
    for b in range(n_batch):
        q = q_ref[b] * jnp.asarray(ATTN_SCALE, BF16)
        kc_t = kc_ref[b].astype(BF16)
        vc_t = vc_ref[b].astype(BF16)
        out = jnp.zeros(q.shape, F32)
        for h, msk in enumerate(masks):
            qh = jnp.where(msk, q, jnp.zeros_like(q))
            s = [_qk(qh, k_refs[j][b]) + bias_ref[h, :, j * Q_BLOCK:(j + 1) * Q_BLOCK] for j in range(3)]
            s.append(_dot(qh, kc_t))
            m = functools.reduce(jnp.maximum, [jnp.max(x, axis=-1, keepdims=True) for x in s])
            p = [jnp.exp(x - m) for x in s]
            l = functools.reduce(jnp.add, [jnp.sum(x, axis=-1, keepdims=True) for x in p])
            o = functools.reduce(jnp.add, [_dot(p[j].astype(BF16),
                                                jnp.where(msk, v_refs[j][b], jnp.zeros((Q_BLOCK, LANES), BF16)))
                                           for j in range(3)])
            head_rows = slice(h * HEAD_DIM, (h + 1) * HEAD_DIM)
            vch = jnp.concatenate([vc_t[head_rows] if g == h else jnp.zeros((HEAD_DIM, vc_t.shape[1]), BF16)
                                   for g in range(HEADS_PER_TILE)], axis=0)
            o = o + _qk(p[3].astype(BF16), vch)
            out = out + o / l
        o_ref[b] = (out * _silu(g_ref[b].astype(F32))).astype(BF16)


def _neighbourhood_attention(z, B, L, cache_k, cache_v, tiles):
    n_rows = L // GRID_W
    n_qb = L // Q_BLOCK
    assert KEY_WIN == 3 * Q_BLOCK and all(_key_window_start(qb) % Q_BLOCK_ROWS == 0 for qb in range(n_qb))
    z3 = z.reshape(B, L, D_IN)
    n_ctx = cache_k.shape[1]
    kc = jnp.transpose(cache_k, (0, 2, 3, 1)).reshape(B, D_ATTN, n_ctx)
    vc = jnp.transpose(cache_v, (0, 2, 3, 1)).reshape(B, D_ATTN, n_ctx)
    tiles_per_col = D_ATTN // LANES

    def win_block(qb):
        return jnp.where(qb < n_qb // 2, 0, (n_rows - KEY_WIN_ROWS) // Q_BLOCK_ROWS)

    assert [_key_window_start(qb) // Q_BLOCK_ROWS for qb in range(n_qb)] == \
        [0 if qb < n_qb // 2 else (n_rows - KEY_WIN_ROWS) // Q_BLOCK_ROWS for qb in range(n_qb)]
    qspec = lambda piece: pl.BlockSpec((B, Q_BLOCK, LANES), lambda t, qb: (0, qb, piece * tiles_per_col + t))
    kspec = lambda piece, j: pl.BlockSpec((B, Q_BLOCK, LANES),
                                          lambda t, qb: (0, win_block(qb) + j, piece * tiles_per_col + t))
    cspec = pl.BlockSpec((B, LANES, n_ctx), lambda t, qb: (0, t, 0))
    out = pl.pallas_call(
        functools.partial(_nbr_attn_kernel, n_rows=n_rows),
        out_shape=jax.ShapeDtypeStruct((B, L, D_ATTN), BF16),
        grid=(N_HEAD_TILES, n_qb),
        in_specs=[qspec(COL_Q), kspec(COL_K, 0), kspec(COL_K, 1), kspec(COL_K, 2),
                  kspec(COL_V, 0), kspec(COL_V, 1), kspec(COL_V, 2), cspec, cspec, qspec(COL_GA),
                  pl.BlockSpec((HEADS_PER_TILE, 2 * WIN_ROWS - 1, GRID_W, LANES), lambda t, qb: (t, 0, 0, 0))],
        out_specs=pl.BlockSpec((B, Q_BLOCK, LANES), lambda t, qb: (0, qb, t)),
        scratch_shapes=[pltpu.VMEM((HEADS_PER_TILE, Q_BLOCK, KEY_WIN), F32)],
        compiler_params=pltpu.CompilerParams(dimension_semantics=("arbitrary", "arbitrary")),
        name="nbr_attn",
    )(z3, z3, z3, z3, z3, z3, z3, kc, vc, z3, tiles)
    return out.reshape(B * L, D_ATTN)


def _merge_kernel(yh_ref, ya_ref, mh_ref, ma_ref, x_ref, mod_ref, wbh_ref, wba_ref, wout_ref, lng_ref, lnb_ref,
                  o_ref, *, alpha):
    p_h = _dot(yh_ref[...], wbh_ref[...])
    p_a = _dot(ya_ref[...], wba_ref[...])
    m_h = jax.nn.sigmoid(mh_ref[...].astype(F32))
    m_a = jax.nn.sigmoid(ma_ref[...].astype(F32))
    out = _dot((m_h * p_h + m_a * p_a).astype(BF16), wout_ref[...])
    gate = mod_ref[:, 2 * D_MODEL:3 * D_MODEL]
    v = alpha * x_ref[...] + gate * out
    mu = jnp.mean(v, axis=-1, keepdims=True)
    d = v - mu
    var = jnp.mean(d * d, axis=-1, keepdims=True)
    o_ref[...] = d * jax.lax.rsqrt(var + LN_EPS) * lng_ref[...] + lnb_ref[...]


def _merge(yh, ya, z, x2d, mod3, mod_index, w_bh, w_ba, w_out, ln_g, ln_b, *, alpha, tm):
    M = x2d.shape[0]
    tok = lambda width, col: pl.BlockSpec((tm, width), lambda i: (i, col))
    const = lambda shape: pl.BlockSpec(shape, lambda i: (0, 0))
    return pl.pallas_call(
        functools.partial(_merge_kernel, alpha=alpha),
        out_shape=jax.ShapeDtypeStruct((M, D_MODEL), F32),
        grid=(M // tm,),
        in_specs=[tok(D_HYENA, 0), tok(D_ATTN, 0), tok(D_MODEL, COL_MH), tok(D_MODEL, COL_MA), tok(D_MODEL, 0),
                  pl.BlockSpec((None, 1, 3 * D_MODEL), lambda i: (mod_index(i), 0, 0)),
                  const((D_HYENA, D_MODEL)), const((D_ATTN, D_MODEL)), const((D_MODEL, D_MODEL)),
                  const((1, D_MODEL)), const((1, D_MODEL))],
        out_specs=pl.BlockSpec((tm, D_MODEL), lambda i: (i, 0)),
        compiler_params=pltpu.CompilerParams(
            dimension_semantics=("arbitrary",),
            vmem_limit_bytes=_vmem_limit(
                2 * (4 * _nbytes((tm, D_MODEL), BF16) + 2 * _nbytes((tm, D_MODEL), F32)
                     + 3 * _nbytes((D_MODEL, D_MODEL), BF16)),
                8 * _nbytes((tm, D_MODEL), F32))),
        name="merge",
    )(yh, ya, z, z, x2d, mod3, w_bh, w_ba, w_out, ln_g.reshape(1, -1), ln_b.reshape(1, -1))


def _layer(x, mod3, cond_row, ctx_kv, p, spectra, tiles, *, alpha):
    B, L, _ = x.shape
    x2d = x.reshape(B * L, D_MODEL)
    emit_kv = ctx_kv is None
    assert not emit_kv or IN_TM == L
    outs = _in_projection(x2d, mod3, lambda i: cond_row(i * IN_TM // L), p["w_in"], emit_kv=emit_kv, tm=IN_TM)
    z = outs[0]
    yh = _hyena_branch(z, B, L, p["conv_w"], p["conv_b"], p["hyena_d"], spectra,
                       seqs_per_step=max(1, HYENA_ROWS_PER_STEP // L))
    if ctx_kv is None:
        ya = _context_attention(z, B, L)
    else:
        ya = _neighbourhood_attention(z, B, L, ctx_kv[0], ctx_kv[1], tiles)
    y = _merge(yh, ya, z, x2d, mod3, lambda i: cond_row(i * MERGE_TM // L), p["w_bh"], p["w_ba"], p["w_out"],
               p["ln_g"], p["ln_b"], alpha=alpha, tm=MERGE_TM)
    return y.reshape(B, L, D_MODEL), outs[1:]


def kernel(x_prompt, x_sample, c, cache_k, cache_v, c_ctx, w_ada, b_ada, w_in, conv_w, conv_b, filt_w1, filt_b1,
           filt_w2, filt_b2, filt_w3, filt_freq, hyena_d, rpb, w_bh, w_ba, w_out, ln_g, ln_b):
    depth = w_in.shape[0]
    alpha = (2.0 * depth) ** 0.25
    n_lat, lat_len = x_sample.shape[0], x_sample.shape[1]
    n_ctx, ctx_len = x_prompt.shape[0], x_prompt.shape[1]
    ctx_row = n_lat
    cond = jnp.zeros((8, D_MODEL), F32).at[:n_lat].set(c).at[ctx_row].set(c_ctx)

    xp, xs = x_prompt, x_sample
    new_k, new_v = [], []
    for l in range(depth):
        p = {"w_in": w_in[l].astype(BF16), "conv_w": conv_w[l], "conv_b": conv_b[l], "hyena_d": hyena_d[l],
             "w_bh": w_bh[l].astype(BF16), "w_ba": w_ba[l].astype(BF16), "w_out": w_out[l].astype(BF16),
             "ln_g": ln_g[l], "ln_b": ln_b[l]}
        mod3 = _modulation(cond, w_ada[l], b_ada[l]).reshape(8, 1, 3 * D_MODEL)
        filt = (filt_w1[l], filt_b1[l], filt_w2[l], filt_b2[l], filt_w3[l], filt_freq[l])
        spectra_ctx = _filter_spectra(ctx_len, *filt)
        spectra_lat = _filter_spectra(lat_len, *filt)
        tiles = _rpb_tiles(rpb[l])
        xp, (k_ctx, v_ctx) = _layer(xp, mod3, lambda b: ctx_row, None, p, spectra_ctx, None, alpha=alpha)
        new_k.append(jnp.transpose(k_ctx.reshape(n_ctx, N_HEADS, HEAD_DIM, ctx_len), (0, 3, 1, 2)))
        new_v.append(jnp.transpose(v_ctx.reshape(n_ctx, N_HEADS, HEAD_DIM, ctx_len), (0, 3, 1, 2)))
        xs, _ = _layer(xs, mod3, lambda b: b, (cache_k[:, l], cache_v[:, l]), p, spectra_lat, tiles, alpha=alpha)
    return xp, xs, jnp.stack(new_k, axis=1), jnp.stack(new_v, axis=1)
```

```python
import functools
import math

import jax
import jax.numpy as jnp
import numpy as np
from jax.experimental import pallas as pl
from jax.experimental.pallas import tpu as pltpu

F32 = jnp.float32
BF16 = jnp.bfloat16

D_MODEL = 1024
D_HYENA = 1024
N_HEADS = 16
HEAD_DIM = 64
D_ATTN = N_HEADS * HEAD_DIM
D_IN = 4 * D_HYENA + 4 * D_ATTN + 2 * D_MODEL
GRID_W = 64
WIN_ROWS = 8
WIN_COLS = 16
FILTER_EMB = 33
FILTER_BANDS = (FILTER_EMB - 1) // 2
FILTER_HIDDEN = 64
DECAY_TARGET = 1e-2
MIN_DECAY = math.log(DECAY_TARGET) / 1.5
MAX_DECAY = math.log(DECAY_TARGET) / 0.3
DECAY_SHIFT = 0.05
LN_EPS = 1e-5
NEG_INF = -1e30

IN_VH, IN_X1, IN_X0, IN_GH, IN_Q, IN_K, IN_V, IN_GA, IN_MH, IN_MA = range(10)
COL_U, COL_X0, COL_GH, COL_Q, COL_K, COL_V, COL_GA, COL_MH, COL_MA = range(9)
D_Z = 9 * D_MODEL

LANES = 128
SUBLANES = 8
HEADS_PER_TILE = LANES // HEAD_DIM
N_HEAD_TILES = N_HEADS // HEADS_PER_TILE
V7X_VMEM_BYTES = 64 * 1024 * 1024
ATTN_SCALE = HEAD_DIM ** -0.5

Q_BLOCK = 256
Q_BLOCK_ROWS = Q_BLOCK // GRID_W
KEY_WIN_ROWS = 12
KEY_WIN = KEY_WIN_ROWS * GRID_W

IN_TM = 256
HALO = 16
MERGE_TM = 512
HYENA_ROWS_PER_STEP = 2048


def _vmem_limit(block_bytes, temp_bytes):
    need = int(block_bytes + temp_bytes)
    return min(max(need, 16 * 1024 * 1024), V7X_VMEM_BYTES - 8 * 1024 * 1024)


def _nbytes(shape, dtype):
    return int(np.prod(shape)) * jnp.dtype(dtype).itemsize


def _silu(x):
    return x * jax.nn.sigmoid(x)


def _split_bf16(a):
    hi = a.astype(BF16)
    lo = (a - hi.astype(F32)).astype(BF16)
    return hi, lo


def _dot(a, b):
    return jnp.dot(a, b, preferred_element_type=F32)


def _dot_split(a_hi, a_lo, b):
    b_hi, b_lo = _split_bf16(b)
    return _dot(a_hi, b_hi) + (_dot(a_hi, b_lo) + _dot(a_lo, b_hi))


@functools.lru_cache(maxsize=None)
def _dft_constants(L):
    n = 2 * L
    k = np.arange(L, dtype=np.float64)[:, None]
    s = np.arange(L, dtype=np.float64)[None, :]
    ang = 2.0 * np.pi * k * s / n
    nyq = np.cos(np.pi * np.arange(L, dtype=np.float64))
    C = np.cos(ang)
    S = np.sin(ang)
    S[0, :] = nyq
    F = np.concatenate([C, S], axis=0)
    w = np.full((L,), 2.0)
    w[0] = 1.0
    Gc = (C * w[:, None]).T / n
    Ss = 2.0 * np.sin(ang)
    Ss[0, :] = nyq
    Gs = Ss.T / n
    G = np.concatenate([Gc, Gs], axis=1)

    def split(a):
        a32 = a.astype(np.float32)
        hi = a32.astype(BF16)
        lo = (a32 - hi.astype(np.float32)).astype(BF16)
        return hi, lo

    c_hi, c_lo = split(C)
    s_hi, s_lo = split(S)
    return dict(F=F.astype(np.float32).astype(BF16), G=G.astype(np.float32).astype(BF16),
                c_hi=c_hi, c_lo=c_lo, s_hi=s_hi, s_lo=s_lo)


@functools.lru_cache(maxsize=None)
def _filter_constants(L):
    t = np.linspace(0.0, 1.0, L, dtype=np.float32)[:, None]
    bands = np.linspace(1e-4, FILTER_BANDS - 1, FILTER_BANDS, dtype=np.float32)[None]
    w = (2.0 * math.pi * np.arange(L, dtype=np.float32)[:, None] / L).astype(np.float32)
    z = np.concatenate([t, np.cos(bands * w), -np.sin(bands * w)], axis=-1).astype(np.float32)
    z_pad = np.zeros((L, LANES), np.float32)
    z_pad[:, :FILTER_EMB] = z
    z_hi = z_pad.astype(BF16)
    z_lo = (z_pad - z_hi.astype(np.float32)).astype(BF16)
    deltas = np.linspace(MIN_DECAY, MAX_DECAY, D_HYENA, dtype=np.float32)
    decay = (np.exp(-t * np.abs(deltas)) + np.float32(DECAY_SHIFT)).astype(np.float32)
    return z_hi, z_lo, decay


@functools.lru_cache(maxsize=None)
def _rpb_constants():
    c = np.arange(GRID_W)[:, None]
    kc = np.arange(GRID_W)[None, :]
    c0 = np.clip(c - WIN_COLS // 2, 0, GRID_W - WIN_COLS)
    in_win = (kc >= c0) & (kc < c0 + WIN_COLS)
    idx = np.clip(kc - c + WIN_COLS - 1, 0, 2 * WIN_COLS - 2)
    onehot = np.zeros((LANES, GRID_W, LANES), np.float32)
    for half in range(HEADS_PER_TILE):
        lanes = half * GRID_W + np.arange(GRID_W)
        onehot[idx, c, lanes[None, :]] = 1.0
    mask = np.concatenate([in_win, in_win], axis=1).astype(np.float32)
    return onehot.reshape(LANES, GRID_W * LANES).astype(BF16), mask.reshape(1, GRID_W * LANES)


def _mod_kernel(cond_ref, w_ref, b_ref, o_ref):
    s = _silu(cond_ref[...]).astype(BF16)
    o_ref[...] = _dot(s, w_ref[...].astype(BF16)) + b_ref[...]


def _modulation(cond, w_ada, b_ada):
    rows = cond.shape[0]
    return pl.pallas_call(
        _mod_kernel,
        out_shape=jax.ShapeDtypeStruct((rows, 3 * D_MODEL), F32),
        grid=(3,),
        in_specs=[pl.BlockSpec((rows, D_MODEL), lambda j: (0, 0)),
                  pl.BlockSpec((D_MODEL, D_MODEL), lambda j: (0, j)),
                  pl.BlockSpec((1, D_MODEL), lambda j: (0, j))],
        out_specs=pl.BlockSpec((rows, D_MODEL), lambda j: (0, j)),
        name="mod",
    )(cond, w_ada, b_ada.reshape(1, -1))


def _spectra_kernel(zh_ref, zl_ref, w1_ref, b1_ref, w2_ref, b2_ref, w3f_ref, w3b_ref, freq_ref, decay_ref,
                    ch_ref, cl_ref, sh_ref, sl_ref, fp_ref, fqa_ref, fpb_ref):
    L = decay_ref.shape[0]
    f0 = freq_ref[0:1, :]
    f1 = freq_ref[1:2, :]
    hf = jnp.sin(f0 * (_dot_split(zh_ref[...], zl_ref[...], w1_ref[...]) + b1_ref[...]))
    h_hi, h_lo = _split_bf16(hf)
    hf = jnp.sin(f1 * (_dot_split(h_hi, h_lo, w2_ref[...]) + b2_ref[...]))
    h_hi, h_lo = _split_bf16(hf)
    decay = decay_ref[...]
    h_fwd = _dot_split(h_hi, h_lo, w3f_ref[...]) * decay
    h_bwd = _dot_split(h_hi, h_lo, w3b_ref[...]) * decay
    row0 = jax.lax.broadcasted_iota(jnp.int32, (L, 1), 0) == 0
    h_bwd = jnp.where(row0, 0.0, h_bwd)
    fsum = h_fwd + h_bwd
    fdif = h_fwd - h_bwd
    fp = _dot_split(ch_ref[...], cl_ref[...], fsum)
    fq = _dot_split(sh_ref[...], sl_ref[...], fdif)
    nyq = _dot_split(sh_ref[0:8, :], sl_ref[0:8, :], fsum)[0:1, :]
    fp_ref[...] = fp
    fqa_ref[...] = jnp.where(row0, 0.0, fq)
    fpb_ref[...] = jnp.where(row0, nyq, fp)


def _filter_spectra(L, w1, b1, w2, b2, w3, freq):
    tc = 256
    z_hi, z_lo, decay = _filter_constants(L)
    dft = _dft_constants(L)
    w1p = jnp.zeros((LANES, FILTER_HIDDEN), F32).at[:FILTER_EMB].set(w1)
    nblk = D_HYENA // tc
    const = lambda shape: pl.BlockSpec(shape, lambda c: (0, 0), pipeline_mode=pl.Buffered(1))
    out_spec = pl.BlockSpec((L, tc), lambda c: (0, c))
    blocks = (4 * _nbytes((L, L), BF16) + 2 * _nbytes((L, LANES), BF16) + 2 * 4 * _nbytes((L, tc), F32))
    return pl.pallas_call(
        _spectra_kernel,
        out_shape=[jax.ShapeDtypeStruct((L, D_HYENA), F32)] * 3,
        grid=(nblk,),
        in_specs=[const((L, LANES)), const((L, LANES)),
                  const((LANES, FILTER_HIDDEN)), const((1, FILTER_HIDDEN)),
                  const((FILTER_HIDDEN, FILTER_HIDDEN)), const((1, FILTER_HIDDEN)),
                  pl.BlockSpec((FILTER_HIDDEN, tc), lambda c: (0, c)),
                  pl.BlockSpec((FILTER_HIDDEN, tc), lambda c: (0, nblk + c)),
                  const((2, FILTER_HIDDEN)),
                  pl.BlockSpec((L, tc), lambda c: (0, c)),
                  const((L, L)), const((L, L)), const((L, L)), const((L, L))],
        out_specs=[out_spec] * 3,
        compiler_params=pltpu.CompilerParams(
            dimension_semantics=("arbitrary",),
            vmem_limit_bytes=_vmem_limit(blocks, 12 * _nbytes((L, tc), F32))),
        name=f"spectra_{L}",
    )(z_hi, z_lo, w1p, b1.reshape(1, -1), w2, b2.reshape(1, -1), w3, w3, freq, decay,
      dft["c_hi"], dft["c_lo"], dft["s_hi"], dft["s_lo"])


def _rpb_kernel(r_ref, onehot_ref, mask_ref, o_ref):
    r = r_ref[...]
    r_hi = r.astype(BF16)
    rem = r - r_hi.astype(F32)
    r_mid = rem.astype(BF16)
    r_lo = (rem - r_mid.astype(F32)).astype(BF16)
    oh = onehot_ref[...]
    t = _dot(r_hi, oh) + (_dot(r_mid, oh) + _dot(r_lo, oh))
    t = jnp.where(mask_ref[...] > 0.0, t, NEG_INF)
    for c in range(o_ref.shape[1]):
        o_ref[:, c, :] = t[:, c * LANES:(c + 1) * LANES]


def _rpb_tiles(rpb):
    n_dr = 2 * WIN_ROWS - 1
    rows = N_HEADS * n_dr
    onehot, mask = _rpb_constants()
    r = jnp.zeros((rows, LANES), F32).at[:, :2 * WIN_COLS - 1].set(rpb.reshape(rows, 2 * WIN_COLS - 1))
    cols_per_step = 16
    tn = cols_per_step * LANES
    out = pl.pallas_call(
        _rpb_kernel,
        out_shape=jax.ShapeDtypeStruct((rows, GRID_W, LANES), F32),
        grid=(GRID_W // cols_per_step,),
        in_specs=[pl.BlockSpec((rows, LANES), lambda j: (0, 0)),
                  pl.BlockSpec((LANES, tn), lambda j: (0, j)),
                  pl.BlockSpec((1, tn), lambda j: (0, j))],
        out_specs=pl.BlockSpec((rows, cols_per_step, LANES), lambda j: (0, j, 0)),
        name="rpb_tiles",
    )(r, onehot, mask)
    return out.reshape(N_HEADS, n_dr, GRID_W, LANES)


def _inproj_kernel(*refs, tiles_per_seq, emit_kv):
    refs = list(refs)
    x_ref = refs.pop(0)
    halo_refs = [refs.pop(0), refs.pop(0)] if tiles_per_seq > 1 else []
    mod_ref, w_ref, cw_ref, cb_ref, z_ref = refs[:5]
    kv_refs = refs[5:]
    tm = x_ref.shape[0]
    shift = mod_ref[:, 0:D_MODEL]
    scale = mod_ref[:, D_MODEL:2 * D_MODEL]

    def modulate(ref):
        return (ref[...] * (1.0 + scale) + shift).astype(BF16)

    h = modulate(x_ref)
    if halo_refs:
        tile_in_seq = pl.program_id(0) % tiles_per_seq
        keep_prev = (tile_in_seq != 0).astype(BF16)
        keep_next = (tile_in_seq != tiles_per_seq - 1).astype(BF16)
        h_conv = jnp.concatenate([modulate(halo_refs[0]) * keep_prev, h, modulate(halo_refs[1]) * keep_next], axis=0)
    else:
        h_conv = h
    n_conv_rows = h_conv.shape[0]
    slab = jax.lax.broadcasted_iota(jnp.int32, (SUBLANES, 1), 0)

    def proj(piece):
        return _dot(h, w_ref[:, piece * D_MODEL:(piece + 1) * D_MODEL])

    def zcols(col):
        return slice(col * D_MODEL, (col + 1) * D_MODEL)

    def short_conv(piece):
        cols = slice(piece * D_HYENA, (piece + 1) * D_HYENA)
        acc = _dot(h_conv, w_ref[:, cols])
        prev = pltpu.roll(acc, 1, 0)
        nxt = pltpu.roll(acc, n_conv_rows - 1, 0)
        if halo_refs:
            body = slice(HALO, HALO + tm)
            prev, acc, nxt = prev[body], acc[body], nxt[body]
        else:
            prev = jnp.concatenate([jnp.where(slab == 0, 0.0, prev[:SUBLANES]), prev[SUBLANES:]], axis=0)
            nxt = jnp.concatenate([nxt[:-SUBLANES], jnp.where(slab == SUBLANES - 1, 0.0, nxt[-SUBLANES:])], axis=0)
        return prev * cw_ref[0:1, cols] + acc * cw_ref[1:2, cols] + nxt * cw_ref[2:3, cols] + cb_ref[:, cols]

    z_ref[:, zcols(COL_U)] = (short_conv(IN_VH) * short_conv(IN_X1)).astype(BF16)
    z_ref[:, zcols(COL_X0)] = short_conv(IN_X0).astype(BF16)
    z_ref[:, zcols(COL_GH)] = _silu(proj(IN_GH)).astype(BF16)
    z_ref[:, zcols(COL_Q)] = (proj(IN_Q) * ATTN_SCALE).astype(BF16)
    for piece, col, t_ref in zip((IN_K, IN_V), (COL_K, COL_V), kv_refs or (None, None)):
        acc = proj(piece)
        z_ref[:, zcols(col)] = acc.astype(BF16)
        if emit_kv:
            t_ref[...] = acc.T
    z_ref[:, zcols(COL_GA)] = _silu(proj(IN_GA)).astype(BF16)
    z_ref[:, zcols(COL_MH)] = jax.nn.sigmoid(proj(IN_MH)).astype(BF16)
    z_ref[:, zcols(COL_MA)] = jax.nn.sigmoid(proj(IN_MA)).astype(BF16)


def _in_projection(x2d, mod3, mod_index, w_in_bf16, conv_w, conv_b, *, seq_len, emit_kv, tm):
    M = x2d.shape[0]
    assert seq_len % tm == 0 and (not emit_kv or seq_len == tm)
    tiles_per_seq = seq_len // tm
    halo_blocks_per_tile = tm // HALO
    n_halo_blocks = M // HALO
    in_specs = [pl.BlockSpec((tm, D_MODEL), lambda i: (i, 0))]
    operands = [x2d]
    if tiles_per_seq > 1:
        in_specs += [pl.BlockSpec((HALO, D_MODEL), lambda i: (jnp.maximum(i * halo_blocks_per_tile - 1, 0), 0)),
                     pl.BlockSpec((HALO, D_MODEL),
                                  lambda i: (jnp.minimum((i + 1) * halo_blocks_per_tile, n_halo_blocks - 1), 0))]
        operands += [x2d, x2d]
    in_specs += [pl.BlockSpec((None, 1, 3 * D_MODEL), lambda i: (mod_index(i), 0, 0)),
                 pl.BlockSpec((D_MODEL, D_IN), lambda i: (0, 0), pipeline_mode=pl.Buffered(1)),
                 pl.BlockSpec((3, 3 * D_HYENA), lambda i: (0, 0)),
                 pl.BlockSpec((1, 3 * D_HYENA), lambda i: (0, 0))]
    operands += [mod3, w_in_bf16, conv_w, conv_b.reshape(1, -1)]
    out_shape = [jax.ShapeDtypeStruct((M, D_Z), BF16)]
    out_specs = [pl.BlockSpec((tm, D_Z), lambda i: (i, 0))]
    if emit_kv:
        out_shape += [jax.ShapeDtypeStruct((M // seq_len, D_ATTN, seq_len), F32)] * 2
        out_specs += [pl.BlockSpec((None, D_ATTN, seq_len), lambda i: (i, 0, 0))] * 2
    blocks = (_nbytes((D_MODEL, D_IN), BF16) + 2 * _nbytes((tm + 2 * HALO, D_MODEL), F32)
              + 2 * _nbytes((tm, D_Z), BF16) + (4 * _nbytes((tm, D_ATTN), F32) if emit_kv else 0))
    return pl.pallas_call(
        functools.partial(_inproj_kernel, tiles_per_seq=tiles_per_seq, emit_kv=emit_kv),
        out_shape=out_shape,
        grid=(M // tm,),
        in_specs=in_specs,
        out_specs=out_specs,
        compiler_params=pltpu.CompilerParams(
            dimension_semantics=("arbitrary",),
            vmem_limit_bytes=_vmem_limit(blocks, 8 * _nbytes((tm, D_MODEL), F32))),
        name="inproj",
    )(*operands)


def _hyena_kernel(u_ref, x0_ref, gh_ref, d_ref, f_ref, g_ref, fp_ref, fqa_ref, fpb_ref, o_ref):
    L = fp_ref.shape[0]
    for s in range(u_ref.shape[0] // L):
        rows = slice(s * L, (s + 1) * L)
        u = u_ref[rows, :]
        t = _dot(f_ref[...], u)
        p = t[:L]
        q = t[L:]
        yp = p * fp_ref[...] - q * fqa_ref[...]
        yq = p * fqa_ref[...] + q * fpb_ref[...]
        y = _dot(g_ref[...], jnp.concatenate([yp.astype(BF16), yq.astype(BF16)], axis=0))
        y_h = (y + u.astype(F32) * d_ref[...]) * x0_ref[rows, :].astype(F32)
        o_ref[rows, :] = (y_h * gh_ref[rows, :].astype(F32)).astype(BF16)


def _hyena_branch(z, B, L, hyena_d, spectra, *, tc=256, seqs_per_step):
    dft = _dft_constants(L)
    nblk = D_HYENA // tc
    tm = seqs_per_step * L
    zcol = lambda piece: pl.BlockSpec((tm, tc), lambda c, b: (b, piece * nblk + c))
    chan = lambda rows: pl.BlockSpec((rows, tc), lambda c, b: (0, c))
    const = lambda shape: pl.BlockSpec(shape, lambda c, b: (0, 0), pipeline_mode=pl.Buffered(1))
    blocks = (2 * _nbytes((2 * L, L), BF16) + 2 * 4 * _nbytes((tm, tc), BF16) + 2 * 3 * _nbytes((L, tc), F32))
    return pl.pallas_call(
        _hyena_kernel,
        out_shape=jax.ShapeDtypeStruct((B * L, D_HYENA), BF16),
        grid=(nblk, B // seqs_per_step),
        in_specs=[zcol(COL_U), zcol(COL_X0), zcol(COL_GH), chan(1),
                  const((2 * L, L)), const((L, 2 * L)), chan(L), chan(L), chan(L)],
        out_specs=pl.BlockSpec((tm, tc), lambda c, b: (b, c)),
        compiler_params=pltpu.CompilerParams(
            dimension_semantics=("arbitrary", "arbitrary"),
            vmem_limit_bytes=_vmem_limit(blocks, 12 * _nbytes((tm, tc), F32))),
        name=f"hyena_{L}",
    )(z, z, z, hyena_d.reshape(1, -1), dft["F"], dft["G"], *spectra)


def _head_masks():
    lane = jax.lax.broadcasted_iota(jnp.int32, (1, LANES), 1)
    return [(lane >= h * HEAD_DIM) & (lane < (h + 1) * HEAD_DIM) for h in range(HEADS_PER_TILE)]


def _qk(q, k):
    return jax.lax.dot_general(q, k, (((1,), (1,)), ((), ())), preferred_element_type=F32)


def _ctx_attn_kernel(q_ref, k_ref, v_ref, g_ref, o_ref):
    masks = _head_masks()
    for t in range(N_HEAD_TILES):
        cols = slice(t * LANES, (t + 1) * LANES)
        q = q_ref[:, cols]
        k = k_ref[:, cols]
        v = v_ref[:, cols]
        out = jnp.zeros(q.shape, F32)
        for msk in masks:
            s = _qk(jnp.where(msk, q, jnp.zeros_like(q)), k)
            m = jnp.max(s, axis=-1, keepdims=True)
            p = jnp.exp(s - m)
            l = jnp.sum(p, axis=-1, keepdims=True)
            o = _dot(p.astype(BF16), jnp.where(msk, v, jnp.zeros_like(v)))
            out = out + o / l
        o_ref[:, cols] = (out * g_ref[:, cols].astype(F32)).astype(BF16)


def _context_attention(z, B, L):
    zcol = lambda piece: pl.BlockSpec((L, D_ATTN), lambda b: (b, piece))
    return pl.pallas_call(
        _ctx_attn_kernel,
        out_shape=jax.ShapeDtypeStruct((B * L, D_ATTN), BF16),
        grid=(B,),
        in_specs=[zcol(COL_Q), zcol(COL_K), zcol(COL_V), zcol(COL_GA)],
        out_specs=pl.BlockSpec((L, D_ATTN), lambda b: (b, 0)),
        compiler_params=pltpu.CompilerParams(dimension_semantics=("arbitrary",)),
        name="ctx_attn",
    )(z, z, z, z)


def _key_window_start(qb):
    rows = 1024 // GRID_W
    r_first = qb * Q_BLOCK_ROWS
    r_last = r_first + Q_BLOCK_ROWS - 1
    lo = min(max(r_first - WIN_ROWS // 2, 0), rows - WIN_ROWS)
    hi = min(max(r_last - WIN_ROWS // 2, 0), rows - WIN_ROWS) + WIN_ROWS
    start = min(lo, rows - KEY_WIN_ROWS)
    start -= start % Q_BLOCK_ROWS
    assert start <= lo and hi <= start + KEY_WIN_ROWS
    return start


def _nbr_attn_kernel(q_ref, k0_ref, k1_ref, k2_ref, v0_ref, v1_ref, v2_ref, kc_ref, vc_ref, g_ref, tiles_ref,
                     o_ref, bias_ref, *, n_rows):
    qb = pl.program_id(1)
    masks = _head_masks()
    n_qb = n_rows // Q_BLOCK_ROWS

    for qb_static in range(n_qb):
        @pl.when(qb == qb_static)
        def _(qb_static=qb_static):
            k_start = _key_window_start(qb_static)
            for i in range(Q_BLOCK_ROWS):
                r = qb_static * Q_BLOCK_ROWS + i
                r0 = min(max(r - WIN_ROWS // 2, 0), n_rows - WIN_ROWS)
                for j in range(KEY_WIN_ROWS):
                    kr = k_start + j
                    rows = slice(i * GRID_W, (i + 1) * GRID_W)
                    cols = slice(j * GRID_W, (j + 1) * GRID_W)
                    lanes = slice((j % HEADS_PER_TILE) * GRID_W, (j % HEADS_PER_TILE + 1) * GRID_W)
                    for h in range(HEADS_PER_TILE):
                        if r0 <= kr < r0 + WIN_ROWS:
                            bias_ref[h, rows, cols] = tiles_ref[h, kr - r + WIN_ROWS - 1, :, lanes]
                        else:
                            bias_ref[h, rows, cols] = jnp.full((GRID_W, GRID_W), NEG_INF, F32)

    k_refs = (k0_ref, k1_ref, k2_ref)
    v_refs = (v0_ref, v1_ref, v2_ref)
    n_batch = q_ref.shape[0]
    for b in range(n_batch):
        q = q_ref[b]
        kc_t = kc_ref[b].astype(BF16)
        vc_t = vc_ref[b].astype(BF16)
        out = jnp.zeros(q.shape, F32)
        for h, msk in enumerate(masks):
            qh = jnp.where(msk, q, jnp.zeros_like(q))
            s = [_qk(qh, k_refs[j][b]) + bias_ref[h, :, j * Q_BLOCK:(j + 1) * Q_BLOCK] for j in range(3)]
            s.append(_dot(qh, kc_t))
            m = functools.reduce(jnp.maximum, [jnp.max(x, axis=-1, keepdims=True) for x in s])
            p = [jnp.exp(x - m) for x in s]
            l = functools.reduce(jnp.add, [jnp.sum(x, axis=-1, keepdims=True) for x in p])
            o = functools.reduce(jnp.add, [_dot(p[j].astype(BF16),
                                                jnp.where(msk, v_refs[j][b], jnp.zeros((Q_BLOCK, LANES), BF16)))
                                           for j in range(3)])
            head_rows = slice(h * HEAD_DIM, (h + 1) * HEAD_DIM)
            vch = jnp.concatenate([vc_t[head_rows] if g == h else jnp.zeros((HEAD_DIM, vc_t.shape[1]), BF16)
                                   for g in range(HEADS_PER_TILE)], axis=0)
            o = o + _qk(p[3].astype(BF16), vch)
            out = out + o / l
        o_ref[b] = (out * g_ref[b].astype(F32)).astype(BF16)


def _neighbourhood_attention(z, B, L, cache_k, cache_v, tiles):
    n_rows = L // GRID_W
    n_qb = L // Q_BLOCK
    assert KEY_WIN == 3 * Q_BLOCK and all(_key_window_start(qb) % Q_BLOCK_ROWS == 0 for qb in range(n_qb))
    z3 = z.reshape(B, L, D_Z)
    n_ctx = cache_k.shape[1]
    kc = jnp.transpose(cache_k, (0, 2, 3, 1)).reshape(B, D_ATTN, n_ctx)
    vc = jnp.transpose(cache_v, (0, 2, 3, 1)).reshape(B, D_ATTN, n_ctx)
    tiles_per_col = D_ATTN // LANES

    def win_block(qb):
        return jnp.where(qb < n_qb // 2, 0, (n_rows - KEY_WIN_ROWS) // Q_BLOCK_ROWS)

    assert [_key_window_start(qb) // Q_BLOCK_ROWS for qb in range(n_qb)] == \
        [0 if qb < n_qb // 2 else (n_rows - KEY_WIN_ROWS) // Q_BLOCK_ROWS for qb in range(n_qb)]
    qspec = lambda piece: pl.BlockSpec((B, Q_BLOCK, LANES), lambda t, qb: (0, qb, piece * tiles_per_col + t))
    kspec = lambda piece, j: pl.BlockSpec((B, Q_BLOCK, LANES),
                                          lambda t, qb: (0, win_block(qb) + j, piece * tiles_per_col + t))
    cspec = pl.BlockSpec((B, LANES, n_ctx), lambda t, qb: (0, t, 0))
    out = pl.pallas_call(
        functools.partial(_nbr_attn_kernel, n_rows=n_rows),
        out_shape=jax.ShapeDtypeStruct((B, L, D_ATTN), BF16),
        grid=(N_HEAD_TILES, n_qb),
        in_specs=[qspec(COL_Q), kspec(COL_K, 0), kspec(COL_K, 1), kspec(COL_K, 2),
                  kspec(COL_V, 0), kspec(COL_V, 1), kspec(COL_V, 2), cspec, cspec, qspec(COL_GA),
                  pl.BlockSpec((HEADS_PER_TILE, 2 * WIN_ROWS - 1, GRID_W, LANES), lambda t, qb: (t, 0, 0, 0))],
        out_specs=pl.BlockSpec((B, Q_BLOCK, LANES), lambda t, qb: (0, qb, t)),
        scratch_shapes=[pltpu.VMEM((HEADS_PER_TILE, Q_BLOCK, KEY_WIN), F32)],
        compiler_params=pltpu.CompilerParams(dimension_semantics=("arbitrary", "arbitrary")),
        name="nbr_attn",
    )(z3, z3, z3, z3, z3, z3, z3, kc, vc, z3, tiles)
    return out.reshape(B * L, D_ATTN)


def _merge_kernel(yh_ref, ya_ref, mh_ref, ma_ref, x_ref, mod_ref, wbh_ref, wba_ref, wout_ref, lng_ref, lnb_ref,
                  o_ref, *, alpha):
    p_h = _dot(yh_ref[...], wbh_ref[...])
    p_a = _dot(ya_ref[...], wba_ref[...])
    m_h = mh_ref[...].astype(F32)
    m_a = ma_ref[...].astype(F32)
    out = _dot((m_h * p_h + m_a * p_a).astype(BF16), wout_ref[...])
    gate = mod_ref[:, 2 * D_MODEL:3 * D_MODEL]
    v = alpha * x_ref[...] + gate * out
    mu = jnp.mean(v, axis=-1, keepdims=True)
    d = v - mu
    var = jnp.mean(d * d, axis=-1, keepdims=True)
    o_ref[...] = d * jax.lax.rsqrt(var + LN_EPS) * lng_ref[...] + lnb_ref[...]


def _merge(yh, ya, z, x2d, mod3, mod_index, w_bh, w_ba, w_out, ln_g, ln_b, *, alpha, tm):
    M = x2d.shape[0]
    tok = lambda width, col: pl.BlockSpec((tm, width), lambda i: (i, col))
    const = lambda shape: pl.BlockSpec(shape, lambda i: (0, 0))
    return pl.pallas_call(
        functools.partial(_merge_kernel, alpha=alpha),
        out_shape=jax.ShapeDtypeStruct((M, D_MODEL), F32),
        grid=(M // tm,),
        in_specs=[tok(D_HYENA, 0), tok(D_ATTN, 0), tok(D_MODEL, COL_MH), tok(D_MODEL, COL_MA), tok(D_MODEL, 0),
                  pl.BlockSpec((None, 1, 3 * D_MODEL), lambda i: (mod_index(i), 0, 0)),
                  const((D_HYENA, D_MODEL)), const((D_ATTN, D_MODEL)), const((D_MODEL, D_MODEL)),
                  const((1, D_MODEL)), const((1, D_MODEL))],
        out_specs=pl.BlockSpec((tm, D_MODEL), lambda i: (i, 0)),
        compiler_params=pltpu.CompilerParams(
            dimension_semantics=("arbitrary",),
            vmem_limit_bytes=_vmem_limit(
                2 * (4 * _nbytes((tm, D_MODEL), BF16) + 2 * _nbytes((tm, D_MODEL), F32)
                     + 3 * _nbytes((D_MODEL, D_MODEL), BF16)),
                8 * _nbytes((tm, D_MODEL), F32))),
        name="merge",
    )(yh, ya, z, z, x2d, mod3, w_bh, w_ba, w_out, ln_g.reshape(1, -1), ln_b.reshape(1, -1))


def _layer(x, mod3, cond_row, ctx_kv, p, spectra, tiles, *, alpha):
    B, L, _ = x.shape
    x2d = x.reshape(B * L, D_MODEL)
    emit_kv = ctx_kv is None
    outs = _in_projection(x2d, mod3, lambda i: cond_row(i * IN_TM // L), p["w_in"], p["conv_w"], p["conv_b"],
                          seq_len=L, emit_kv=emit_kv, tm=IN_TM)
    z = outs[0]
    yh = _hyena_branch(z, B, L, p["hyena_d"], spectra, seqs_per_step=max(1, HYENA_ROWS_PER_STEP // L))
    if ctx_kv is None:
        ya = _context_attention(z, B, L)
    else:
        ya = _neighbourhood_attention(z, B, L, ctx_kv[0], ctx_kv[1], tiles)
    y = _merge(yh, ya, z, x2d, mod3, lambda i: cond_row(i * MERGE_TM // L), p["w_bh"], p["w_ba"], p["w_out"],
               p["ln_g"], p["ln_b"], alpha=alpha, tm=MERGE_TM)
    return y.reshape(B, L, D_MODEL), outs[1:]


def kernel(x_prompt, x_sample, c, cache_k, cache_v, c_ctx, w_ada, b_ada, w_in, conv_w, conv_b, filt_w1, filt_b1,
           filt_w2, filt_b2, filt_w3, filt_freq, hyena_d, rpb, w_bh, w_ba, w_out, ln_g, ln_b):
    depth = w_in.shape[0]
    alpha = (2.0 * depth) ** 0.25
    n_lat, lat_len = x_sample.shape[0], x_sample.shape[1]
    n_ctx, ctx_len = x_prompt.shape[0], x_prompt.shape[1]
    ctx_row = n_lat
    cond = jnp.zeros((8, D_MODEL), F32).at[:n_lat].set(c).at[ctx_row].set(c_ctx)

    xp, xs = x_prompt, x_sample
    new_k, new_v = [], []
    for l in range(depth):
        p = {"w_in": w_in[l].astype(BF16), "conv_w": conv_w[l], "conv_b": conv_b[l], "hyena_d": hyena_d[l],
             "w_bh": w_bh[l].astype(BF16), "w_ba": w_ba[l].astype(BF16), "w_out": w_out[l].astype(BF16),
             "ln_g": ln_g[l], "ln_b": ln_b[l]}
        mod3 = _modulation(cond, w_ada[l], b_ada[l]).reshape(8, 1, 3 * D_MODEL)
        filt = (filt_w1[l], filt_b1[l], filt_w2[l], filt_b2[l], filt_w3[l], filt_freq[l])
        spectra_ctx = _filter_spectra(ctx_len, *filt)
        spectra_lat = _filter_spectra(lat_len, *filt)
        tiles = _rpb_tiles(rpb[l])
        xp, (k_ctx, v_ctx) = _layer(xp, mod3, lambda b: ctx_row, None, p, spectra_ctx, None, alpha=alpha)
        new_k.append(jnp.transpose(k_ctx.reshape(n_ctx, N_HEADS, HEAD_DIM, ctx_len), (0, 3, 1, 2)))
        new_v.append(jnp.transpose(v_ctx.reshape(n_ctx, N_HEADS, HEAD_DIM, ctx_len), (0, 3, 1, 2)))
        xs, _ = _layer(xs, mod3, lambda b: b, (cache_k[:, l], cache_v[:, l]), p, spectra_lat, tiles, alpha=alpha)
    return xp, xs, jnp.stack(new_k, axis=1), jnp.stack(new_v, axis=1)
```

```python
import functools
import math

import jax
import jax.numpy as jnp
import numpy as np
from jax.experimental import pallas as pl
from jax.experimental.pallas import tpu as pltpu

F32 = jnp.float32
BF16 = jnp.bfloat16

D_MODEL = 1024
D_HYENA = 1024
N_HEADS = 16
HEAD_DIM = 64
D_ATTN = N_HEADS * HEAD_DIM
D_IN = 4 * D_HYENA + 4 * D_ATTN + 2 * D_MODEL
GRID_W = 64
WIN_ROWS = 8
WIN_COLS = 16
FILTER_EMB = 33
FILTER_BANDS = (FILTER_EMB - 1) // 2
FILTER_HIDDEN = 64
DECAY_TARGET = 1e-2
MIN_DECAY = math.log(DECAY_TARGET) / 1.5
MAX_DECAY = math.log(DECAY_TARGET) / 0.3
DECAY_SHIFT = 0.05
LN_EPS = 1e-5
NEG_INF = -1e30

COL_VH, COL_X1, COL_X0, COL_GH, COL_Q, COL_K, COL_V, COL_GA, COL_MH, COL_MA = range(10)

LANES = 128
SUBLANES = 8
HEADS_PER_TILE = LANES // HEAD_DIM
N_HEAD_TILES = N_HEADS // HEADS_PER_TILE
V7X_VMEM_BYTES = 64 * 1024 * 1024
ATTN_SCALE = HEAD_DIM ** -0.5

Q_BLOCK = 256
Q_BLOCK_ROWS = Q_BLOCK // GRID_W
KEY_WIN_ROWS = 12
KEY_WIN = KEY_WIN_ROWS * GRID_W

IN_TM = 256
MERGE_TM = 512
HYENA_ROWS_PER_STEP = 4096


def _vmem_limit(block_bytes, temp_bytes):
    need = int(block_bytes + temp_bytes)
    return min(max(need, 16 * 1024 * 1024), V7X_VMEM_BYTES - 8 * 1024 * 1024)


def _nbytes(shape, dtype):
    return int(np.prod(shape)) * jnp.dtype(dtype).itemsize


def _silu(x):
    return x * jax.nn.sigmoid(x)


def _split_bf16(a):
    hi = a.astype(BF16)
    lo = (a - hi.astype(F32)).astype(BF16)
    return hi, lo


def _dot(a, b):
    return jnp.dot(a, b, preferred_element_type=F32)


def _dot_split(a_hi, a_lo, b):
    b_hi, b_lo = _split_bf16(b)
    return _dot(a_hi, b_hi) + (_dot(a_hi, b_lo) + _dot(a_lo, b_hi))


@functools.lru_cache(maxsize=None)
def _dft_constants(L):
    n = 2 * L
    k = np.arange(L, dtype=np.float64)[:, None]
    s = np.arange(L, dtype=np.float64)[None, :]
    ang = 2.0 * np.pi * k * s / n
    nyq = np.cos(np.pi * np.arange(L, dtype=np.float64))
    C = np.cos(ang)
    S = np.sin(ang)
    S[0, :] = nyq
    F = np.concatenate([C, S], axis=0)
    w = np.full((L,), 2.0)
    w[0] = 1.0
    Gc = (C * w[:, None]).T / n
    Ss = 2.0 * np.sin(ang)
    Ss[0, :] = nyq
    Gs = Ss.T / n
    G = np.concatenate([Gc, Gs], axis=1)
    return F.astype(np.float32), G.astype(np.float32)


def _dft_matrices(L):
    F, G = _dft_constants(L)
    return jnp.asarray(F).astype(BF16), jnp.asarray(G).astype(BF16)


@functools.lru_cache(maxsize=None)
def _filter_constants(L):
    t = np.linspace(0.0, 1.0, L, dtype=np.float32)[:, None]
    bands = np.linspace(1e-4, FILTER_BANDS - 1, FILTER_BANDS, dtype=np.float32)[None]
    w = (2.0 * math.pi * np.arange(L, dtype=np.float32)[:, None] / L).astype(np.float32)
    z = np.concatenate([t, np.cos(bands * w), -np.sin(bands * w)], axis=-1).astype(np.float32)
    z_pad = np.zeros((L, LANES), np.float32)
    z_pad[:, :FILTER_EMB] = z
    deltas = np.linspace(MIN_DECAY, MAX_DECAY, D_HYENA, dtype=np.float32)
    decay = (np.exp(-t * np.abs(deltas)) + np.float32(DECAY_SHIFT)).astype(np.float32)
    return z_pad, decay


@functools.lru_cache(maxsize=None)
def _rpb_constants():
    c = np.arange(GRID_W)[:, None]
    kc = np.arange(GRID_W)[None, :]
    c0 = np.clip(c - WIN_COLS // 2, 0, GRID_W - WIN_COLS)
    in_win = (kc >= c0) & (kc < c0 + WIN_COLS)
    idx = np.clip(kc - c + WIN_COLS - 1, 0, 2 * WIN_COLS - 2)
    onehot = np.zeros((LANES, GRID_W, LANES), np.float32)
    for half in range(HEADS_PER_TILE):
        lanes = half * GRID_W + np.arange(GRID_W)
        onehot[idx, c, lanes[None, :]] = 1.0
    mask = np.concatenate([in_win, in_win], axis=1).astype(np.float32)
    return onehot.reshape(LANES, GRID_W * LANES).astype(BF16), mask.reshape(1, GRID_W * LANES)


def _mod_kernel(cond_ref, w_ref, b_ref, o_ref):
    s = _silu(cond_ref[...]).astype(BF16)
    o_ref[...] = _dot(s, w_ref[...].astype(BF16)) + b_ref[...]


def _modulation(cond, w_ada, b_ada):
    rows = cond.shape[0]
    return pl.pallas_call(
        _mod_kernel,
        out_shape=jax.ShapeDtypeStruct((rows, 3 * D_MODEL), F32),
        grid=(3,),
        in_specs=[pl.BlockSpec((rows, D_MODEL), lambda j: (0, 0)),
                  pl.BlockSpec((D_MODEL, D_MODEL), lambda j: (0, j)),
                  pl.BlockSpec((1, D_MODEL), lambda j: (0, j))],
        out_specs=pl.BlockSpec((rows, D_MODEL), lambda j: (0, j)),
        name="mod",
    )(cond, w_ada, b_ada.reshape(1, -1))


def _spectra_kernel(z_ref, w1_ref, b1_ref, w2_ref, b2_ref, w3f_ref, w3b_ref, freq_ref, decay_ref, f_ref,
                    fp_ref, fqa_ref, fpb_ref, hf_ref):
    L = decay_ref.shape[0]

    @pl.when(pl.program_id(0) == 0)
    def _():
        z_hi, z_lo = _split_bf16(z_ref[...])
        hf = jnp.sin(freq_ref[0:1, :] * (_dot_split(z_hi, z_lo, w1_ref[...]) + b1_ref[...]))
        h_hi, h_lo = _split_bf16(hf)
        hf_ref[...] = jnp.sin(freq_ref[1:2, :] * (_dot_split(h_hi, h_lo, w2_ref[...]) + b2_ref[...]))

    h_hi, h_lo = _split_bf16(hf_ref[...])
    decay = decay_ref[...]
    h_fwd = _dot_split(h_hi, h_lo, w3f_ref[...]) * decay
    h_bwd = _dot_split(h_hi, h_lo, w3b_ref[...]) * decay
    row0 = jax.lax.broadcasted_iota(jnp.int32, (L, 1), 0) == 0
    h_bwd = jnp.where(row0, 0.0, h_bwd)
    fsum = (h_fwd + h_bwd).astype(BF16)
    fdif = (h_fwd - h_bwd).astype(BF16)
    fp = _dot(f_ref[0:L, :], fsum)
    fq = _dot(f_ref[L:2 * L, :], fdif)
    nyq = _dot(f_ref[L:L + 2 * SUBLANES, :], fsum)[0:1, :]
    fp_ref[...] = fp
    fqa_ref[...] = jnp.where(row0, 0.0, fq)
    fpb_ref[...] = jnp.where(row0, nyq, fp)


def _filter_spectra(L, f_mat, w1, b1, w2, b2, w3, freq):
    tc = 256
    z_emb, decay = _filter_constants(L)
    w1p = jnp.zeros((LANES, FILTER_HIDDEN), F32).at[:FILTER_EMB].set(w1)
    nblk = D_HYENA // tc
    const = lambda shape: pl.BlockSpec(shape, lambda c: (0, 0), pipeline_mode=pl.Buffered(1))
    out_spec = pl.BlockSpec((L, tc), lambda c: (0, c))
    blocks = (_nbytes((2 * L, L), BF16) + _nbytes((L, LANES), F32) + 2 * 4 * _nbytes((L, tc), F32))
    return pl.pallas_call(
        _spectra_kernel,
        out_shape=[jax.ShapeDtypeStruct((L, D_HYENA), F32)] * 3,
        grid=(nblk,),
        in_specs=[const((L, LANES)),
                  const((LANES, FILTER_HIDDEN)), const((1, FILTER_HIDDEN)),
                  const((FILTER_HIDDEN, FILTER_HIDDEN)), const((1, FILTER_HIDDEN)),
                  pl.BlockSpec((FILTER_HIDDEN, tc), lambda c: (0, c)),
                  pl.BlockSpec((FILTER_HIDDEN, tc), lambda c: (0, nblk + c)),
                  const((2, FILTER_HIDDEN)),
                  pl.BlockSpec((L, tc), lambda c: (0, c)),
                  const((2 * L, L))],
        out_specs=[out_spec] * 3,
        scratch_shapes=[pltpu.VMEM((L, FILTER_HIDDEN), F32)],
        compiler_params=pltpu.CompilerParams(
            dimension_semantics=("arbitrary",),
            vmem_limit_bytes=_vmem_limit(blocks, 12 * _nbytes((L, tc), F32))),
        name=f"spectra_{L}",
    )(z_emb, w1p, b1.reshape(1, -1), w2, b2.reshape(1, -1), w3, w3, freq, decay, f_mat)


def _rpb_kernel(r_ref, onehot_ref, mask_ref, o_ref):
    r = r_ref[...]
    r_hi = r.astype(BF16)
    rem = r - r_hi.astype(F32)
    r_mid = rem.astype(BF16)
    r_lo = (rem - r_mid.astype(F32)).astype(BF16)
    oh = onehot_ref[...]
    t = _dot(r_hi, oh) + (_dot(r_mid, oh) + _dot(r_lo, oh))
    t = jnp.where(mask_ref[...] > 0.0, t, NEG_INF)
    for c in range(o_ref.shape[1]):
        o_ref[:, c, :] = t[:, c * LANES:(c + 1) * LANES]


def _rpb_tiles(rpb):
    n_dr = 2 * WIN_ROWS - 1
    rows = N_HEADS * n_dr
    onehot, mask = _rpb_constants()
    r = jnp.zeros((rows, LANES), F32).at[:, :2 * WIN_COLS - 1].set(rpb.reshape(rows, 2 * WIN_COLS - 1))
    cols_per_step = 16
    tn = cols_per_step * LANES
    out = pl.pallas_call(
        _rpb_kernel,
        out_shape=jax.ShapeDtypeStruct((rows, GRID_W, LANES), F32),
        grid=(GRID_W // cols_per_step,),
        in_specs=[pl.BlockSpec((rows, LANES), lambda j: (0, 0)),
                  pl.BlockSpec((LANES, tn), lambda j: (0, j)),
                  pl.BlockSpec((1, tn), lambda j: (0, j))],
        out_specs=pl.BlockSpec((rows, cols_per_step, LANES), lambda j: (0, j, 0)),
        name="rpb_tiles",
    )(r, onehot, mask)
    return out.reshape(N_HEADS, n_dr, GRID_W, LANES)


def _inproj_kernel(x_ref, mod_ref, w_ref, z_ref, *kv_refs):
    shift = mod_ref[:, 0:D_MODEL]
    scale = mod_ref[:, D_MODEL:2 * D_MODEL]
    h = (x_ref[...] * (1.0 + scale) + shift).astype(BF16)
    epilogue = {COL_GH: _silu, COL_GA: _silu, COL_MH: jax.nn.sigmoid, COL_MA: jax.nn.sigmoid,
                COL_Q: lambda acc: acc * ATTN_SCALE}
    for col in range(D_IN // D_MODEL):
        cols = slice(col * D_MODEL, (col + 1) * D_MODEL)
        acc = _dot(h, w_ref[:, cols])
        z_ref[:, cols] = epilogue.get(col, lambda acc: acc)(acc).astype(BF16)
        if kv_refs and col == COL_K:
            kv_refs[0][...] = acc.T
        if kv_refs and col == COL_V:
            kv_refs[1][...] = acc.T


def _in_projection(x2d, mod3, mod_index, w_in_bf16, *, emit_kv, tm):
    M = x2d.shape[0]
    out_shape = [jax.ShapeDtypeStruct((M, D_IN), BF16)]
    out_specs = [pl.BlockSpec((tm, D_IN), lambda i: (i, 0))]
    if emit_kv:
        out_shape += [jax.ShapeDtypeStruct((M // tm, D_ATTN, tm), F32)] * 2
        out_specs += [pl.BlockSpec((None, D_ATTN, tm), lambda i: (i, 0, 0))] * 2
    blocks = (_nbytes((D_MODEL, D_IN), BF16) + 2 * _nbytes((tm, D_MODEL), F32) + 2 * _nbytes((tm, D_IN), BF16)
              + (4 * _nbytes((tm, D_ATTN), F32) if emit_kv else 0))
    return pl.pallas_call(
        _inproj_kernel,
        out_shape=out_shape,
        grid=(M // tm,),
        in_specs=[pl.BlockSpec((tm, D_MODEL), lambda i: (i, 0)),
                  pl.BlockSpec((None, 1, 3 * D_MODEL), lambda i: (mod_index(i), 0, 0)),
                  pl.BlockSpec((D_MODEL, D_IN), lambda i: (0, 0), pipeline_mode=pl.Buffered(1))],
        out_specs=out_specs,
        compiler_params=pltpu.CompilerParams(
            dimension_semantics=("arbitrary",),
            vmem_limit_bytes=_vmem_limit(blocks, 4 * _nbytes((tm, D_MODEL), F32))),
        name="inproj",
    )(x2d, mod3, w_in_bf16)


def _hyena_kernel(vh_ref, x1_ref, x0_ref, gh_ref, cwv_ref, cw1_ref, cw0_ref, cbv_ref, cb1_ref, cb0_ref, d_ref,
                  f_ref, g_ref, fp_ref, fqa_ref, fpb_ref, o_ref):
    L = fp_ref.shape[0]
    slab = jax.lax.broadcasted_iota(jnp.int32, (SUBLANES, 1), 0)

    def short_conv(x_ref, rows, w_ref, b_ref):
        x = x_ref[rows, :].astype(F32)
        prev = pltpu.roll(x, 1, 0)
        nxt = pltpu.roll(x, L - 1, 0)
        prev = jnp.concatenate([jnp.where(slab == 0, 0.0, prev[:SUBLANES]), prev[SUBLANES:]], axis=0)
        nxt = jnp.concatenate([nxt[:-SUBLANES], jnp.where(slab == SUBLANES - 1, 0.0, nxt[-SUBLANES:])], axis=0)
        return prev * w_ref[0:1, :] + x * w_ref[1:2, :] + nxt * w_ref[2:3, :] + b_ref[...]

    for s in range(vh_ref.shape[0] // L):
        rows = slice(s * L, (s + 1) * L)
        u = short_conv(vh_ref, rows, cwv_ref, cbv_ref) * short_conv(x1_ref, rows, cw1_ref, cb1_ref)
        t = _dot(f_ref[...], u.astype(BF16))
        p = t[:L]
        q = t[L:]
        yp = p * fp_ref[...] - q * fqa_ref[...]
        yq = p * fqa_ref[...] + q * fpb_ref[...]
        y = _dot(g_ref[...], jnp.concatenate([yp.astype(BF16), yq.astype(BF16)], axis=0))
        y_h = (y + u * d_ref[...]) * short_conv(x0_ref, rows, cw0_ref, cb0_ref)
        o_ref[rows, :] = (y_h * gh_ref[rows, :].astype(F32)).astype(BF16)


def _hyena_branch(z, B, L, conv_w, conv_b, hyena_d, dft, spectra, *, tc=256, seqs_per_step):
    nblk = D_HYENA // tc
    tm = seqs_per_step * L
    zcol = lambda piece: pl.BlockSpec((tm, tc), lambda c, b: (b, piece * nblk + c))
    wcol = lambda rows, piece: pl.BlockSpec((rows, tc), lambda c, b: (0, piece * nblk + c))
    chan = lambda rows: pl.BlockSpec((rows, tc), lambda c, b: (0, c))
    const = lambda shape: pl.BlockSpec(shape, lambda c, b: (0, 0), pipeline_mode=pl.Buffered(1))
    blocks = (2 * _nbytes((2 * L, L), BF16) + 2 * 5 * _nbytes((tm, tc), BF16) + 2 * 3 * _nbytes((L, tc), F32))
    return pl.pallas_call(
        _hyena_kernel,
        out_shape=jax.ShapeDtypeStruct((B * L, D_HYENA), BF16),
        grid=(nblk, B // seqs_per_step),
        in_specs=[zcol(COL_VH), zcol(COL_X1), zcol(COL_X0), zcol(COL_GH),
                  wcol(3, 0), wcol(3, 1), wcol(3, 2), wcol(1, 0), wcol(1, 1), wcol(1, 2), chan(1),
                  const((2 * L, L)), const((L, 2 * L)), chan(L), chan(L), chan(L)],
        out_specs=pl.BlockSpec((tm, tc), lambda c, b: (b, c)),
        compiler_params=pltpu.CompilerParams(
            dimension_semantics=("arbitrary", "arbitrary"),
            vmem_limit_bytes=_vmem_limit(blocks, 16 * _nbytes((tm, tc), F32))),
        name=f"hyena_{L}",
    )(z, z, z, z, conv_w, conv_w, conv_w, conv_b.reshape(1, -1), conv_b.reshape(1, -1), conv_b.reshape(1, -1),
      hyena_d.reshape(1, -1), *dft, *spectra)


def _head_masks():
    lane = jax.lax.broadcasted_iota(jnp.int32, (1, LANES), 1)
    return [(lane >= h * HEAD_DIM) & (lane < (h + 1) * HEAD_DIM) for h in range(HEADS_PER_TILE)]


def _qk(q, k):
    return jax.lax.dot_general(q, k, (((1,), (1,)), ((), ())), preferred_element_type=F32)


def _ctx_attn_kernel(q_ref, k_ref, v_ref, g_ref, o_ref):
    masks = _head_masks()
    for t in range(N_HEAD_TILES):
        cols = slice(t * LANES, (t + 1) * LANES)
        q = q_ref[:, cols]
        k = k_ref[:, cols]
        v = v_ref[:, cols]
        out = jnp.zeros(q.shape, F32)
        for msk in masks:
            s = _qk(jnp.where(msk, q, jnp.zeros_like(q)), k)
            m = jnp.max(s, axis=-1, keepdims=True)
            p = jnp.exp(s - m)
            l = jnp.sum(p, axis=-1, keepdims=True)
            o = _dot(p.astype(BF16), jnp.where(msk, v, jnp.zeros_like(v)))
            out = out + o / l
        o_ref[:, cols] = (out * g_ref[:, cols].astype(F32)).astype(BF16)


def _context_attention(z, B, L):
    zcol = lambda piece: pl.BlockSpec((L, D_ATTN), lambda b: (b, piece))
    return pl.pallas_call(
        _ctx_attn_kernel,
        out_shape=jax.ShapeDtypeStruct((B * L, D_ATTN), BF16),
        grid=(B,),
        in_specs=[zcol(COL_Q), zcol(COL_K), zcol(COL_V), zcol(COL_GA)],
        out_specs=pl.BlockSpec((L, D_ATTN), lambda b: (b, 0)),
        compiler_params=pltpu.CompilerParams(dimension_semantics=("arbitrary",)),
        name="ctx_attn",
    )(z, z, z, z)


def _key_window_start(qb):
    rows = 1024 // GRID_W
    r_first = qb * Q_BLOCK_ROWS
    r_last = r_first + Q_BLOCK_ROWS - 1
    lo = min(max(r_first - WIN_ROWS // 2, 0), rows - WIN_ROWS)
    hi = min(max(r_last - WIN_ROWS // 2, 0), rows - WIN_ROWS) + WIN_ROWS
    start = min(lo, rows - KEY_WIN_ROWS)
    start -= start % Q_BLOCK_ROWS
    assert start <= lo and hi <= start + KEY_WIN_ROWS
    return start


def _nbr_attn_kernel(q_ref, k0_ref, k1_ref, k2_ref, v0_ref, v1_ref, v2_ref, kc_ref, vc_ref, g_ref, tiles_ref,
                     o_ref, bias_ref, *, n_rows):
    qb = pl.program_id(1)
    masks = _head_masks()
    n_qb = n_rows // Q_BLOCK_ROWS

    for qb_static in range(n_qb):
        @pl.when(qb == qb_static)
        def _(qb_static=qb_static):
            k_start = _key_window_start(qb_static)
            for i in range(Q_BLOCK_ROWS):
                r = qb_static * Q_BLOCK_ROWS + i
                r0 = min(max(r - WIN_ROWS // 2, 0), n_rows - WIN_ROWS)
                for j in range(KEY_WIN_ROWS):
                    kr = k_start + j
                    rows = slice(i * GRID_W, (i + 1) * GRID_W)
                    cols = slice(j * GRID_W, (j + 1) * GRID_W)
                    lanes = slice((j % HEADS_PER_TILE) * GRID_W, (j % HEADS_PER_TILE + 1) * GRID_W)
                    for h in range(HEADS_PER_TILE):
                        if r0 <= kr < r0 + WIN_ROWS:
                            bias_ref[h, rows, cols] = tiles_ref[h, kr - r + WIN_ROWS - 1, :, lanes]
                        else:
                            bias_ref[h, rows, cols] = jnp.full((GRID_W, GRID_W), NEG_INF, F32)

    k_refs = (k0_ref, k1_ref, k2_ref)
    v_refs = (v0_ref, v1_ref, v2_ref)
    n_batch = q_ref.shape[0]
    for b in range(n_batch):
        q = q_ref[b]
        kc_t = kc_ref[b].astype(BF16)
        vc_t = vc_ref[b].astype(BF16)
        out = jnp.zeros(q.shape, F32)
        for h, msk in enumerate(masks):
            qh = jnp.where(msk, q, jnp.zeros_like(q))
            s = [_qk(qh, k_refs[j][b]) + bias_ref[h, :, j * Q_BLOCK:(j + 1) * Q_BLOCK] for j in range(3)]
            s.append(_dot(qh, kc_t))
            m = jnp.max(functools.reduce(jnp.maximum, s), axis=-1, keepdims=True)
            p = [jnp.exp(x - m) for x in s]
            l = jnp.sum(functools.reduce(jnp.add, p), axis=-1, keepdims=True)
            o = functools.reduce(jnp.add, [_dot(p[j].astype(BF16),
                                                jnp.where(msk, v_refs[j][b], jnp.zeros((Q_BLOCK, LANES), BF16)))
                                           for j in range(3)])
            head_rows = slice(h * HEAD_DIM, (h + 1) * HEAD_DIM)
            vch = jnp.concatenate([vc_t[head_rows] if g == h else jnp.zeros((HEAD_DIM, vc_t.shape[1]), BF16)
                                   for g in range(HEADS_PER_TILE)], axis=0)
            o = o + _qk(p[3].astype(BF16), vch)
            out = out + o / l
        o_ref[b] = (out * g_ref[b].astype(F32)).astype(BF16)


def _neighbourhood_attention(z, B, L, cache_k, cache_v, tiles):
    n_rows = L // GRID_W
    n_qb = L // Q_BLOCK
    assert KEY_WIN == 3 * Q_BLOCK and all(_key_window_start(qb) % Q_BLOCK_ROWS == 0 for qb in range(n_qb))
    z3 = z.reshape(B, L, D_IN)
    n_ctx = cache_k.shape[1]
    kc = jnp.transpose(cache_k, (0, 2, 3, 1)).reshape(B, D_ATTN, n_ctx)
    vc = jnp.transpose(cache_v, (0, 2, 3, 1)).reshape(B, D_ATTN, n_ctx)
    tiles_per_col = D_ATTN // LANES

    def win_block(qb):
        return jnp.where(qb < n_qb // 2, 0, (n_rows - KEY_WIN_ROWS) // Q_BLOCK_ROWS)

    assert [_key_window_start(qb) // Q_BLOCK_ROWS for qb in range(n_qb)] == \
        [0 if qb < n_qb // 2 else (n_rows - KEY_WIN_ROWS) // Q_BLOCK_ROWS for qb in range(n_qb)]
    qspec = lambda piece: pl.BlockSpec((B, Q_BLOCK, LANES), lambda t, qb: (0, qb, piece * tiles_per_col + t))
    kspec = lambda piece, j: pl.BlockSpec((B, Q_BLOCK, LANES),
                                          lambda t, qb: (0, win_block(qb) + j, piece * tiles_per_col + t))
    cspec = pl.BlockSpec((B, LANES, n_ctx), lambda t, qb: (0, t, 0))
    out = pl.pallas_call(
        functools.partial(_nbr_attn_kernel, n_rows=n_rows),
        out_shape=jax.ShapeDtypeStruct((B, L, D_ATTN), BF16),
        grid=(N_HEAD_TILES, n_qb),
        in_specs=[qspec(COL_Q), kspec(COL_K, 0), kspec(COL_K, 1), kspec(COL_K, 2),
                  kspec(COL_V, 0), kspec(COL_V, 1), kspec(COL_V, 2), cspec, cspec, qspec(COL_GA),
                  pl.BlockSpec((HEADS_PER_TILE, 2 * WIN_ROWS - 1, GRID_W, LANES), lambda t, qb: (t, 0, 0, 0))],
        out_specs=pl.BlockSpec((B, Q_BLOCK, LANES), lambda t, qb: (0, qb, t)),
        scratch_shapes=[pltpu.VMEM((HEADS_PER_TILE, Q_BLOCK, KEY_WIN), F32)],
        compiler_params=pltpu.CompilerParams(dimension_semantics=("arbitrary", "arbitrary")),
        name="nbr_attn",
    )(z3, z3, z3, z3, z3, z3, z3, kc, vc, z3, tiles)
    return out.reshape(B * L, D_ATTN)


def _merge_kernel(yh_ref, ya_ref, mh_ref, ma_ref, x_ref, mod_ref, wbh_ref, wba_ref, wout_ref, lng_ref, lnb_ref,
                  o_ref, *, alpha):
    p_h = _dot(yh_ref[...], wbh_ref[...])
    p_a = _dot(ya_ref[...], wba_ref[...])
    m_h = mh_ref[...].astype(F32)
    m_a = ma_ref[...].astype(F32)
    out = _dot((m_h * p_h + m_a * p_a).astype(BF16), wout_ref[...])
    gate = mod_ref[:, 2 * D_MODEL:3 * D_MODEL]
    v = alpha * x_ref[...] + gate * out
    mu = jnp.mean(v, axis=-1, keepdims=True)
    d = v - mu
    var = jnp.mean(d * d, axis=-1, keepdims=True)
    o_ref[...] = d * jax.lax.rsqrt(var + LN_EPS) * lng_ref[...] + lnb_ref[...]


def _merge(yh, ya, z, x2d, mod3, mod_index, w_bh, w_ba, w_out, ln_g, ln_b, *, alpha, tm):
    M = x2d.shape[0]
    tok = lambda width, col: pl.BlockSpec((tm, width), lambda i: (i, col))
    const = lambda shape: pl.BlockSpec(shape, lambda i: (0, 0))
    return pl.pallas_call(
        functools.partial(_merge_kernel, alpha=alpha),
        out_shape=jax.ShapeDtypeStruct((M, D_MODEL), F32),
        grid=(M // tm,),
        in_specs=[tok(D_HYENA, 0), tok(D_ATTN, 0), tok(D_MODEL, COL_MH), tok(D_MODEL, COL_MA), tok(D_MODEL, 0),
                  pl.BlockSpec((None, 1, 3 * D_MODEL), lambda i: (mod_index(i), 0, 0)),
                  const((D_HYENA, D_MODEL)), const((D_ATTN, D_MODEL)), const((D_MODEL, D_MODEL)),
                  const((1, D_MODEL)), const((1, D_MODEL))],
        out_specs=pl.BlockSpec((tm, D_MODEL), lambda i: (i, 0)),
        compiler_params=pltpu.CompilerParams(
            dimension_semantics=("arbitrary",),
            vmem_limit_bytes=_vmem_limit(
                2 * (4 * _nbytes((tm, D_MODEL), BF16) + 2 * _nbytes((tm, D_MODEL), F32)
                     + 3 * _nbytes((D_MODEL, D_MODEL), BF16)),
                8 * _nbytes((tm, D_MODEL), F32))),
        name="merge",
    )(yh, ya, z, z, x2d, mod3, w_bh, w_ba, w_out, ln_g.reshape(1, -1), ln_b.reshape(1, -1))


def _layer(x, mod3, cond_row, ctx_kv, p, filt, tiles, *, alpha):
    B, L, _ = x.shape
    x2d = x.reshape(B * L, D_MODEL)
    emit_kv = ctx_kv is None
    assert not emit_kv or IN_TM == L
    outs = _in_projection(x2d, mod3, lambda i: cond_row(i * IN_TM // L), p["w_in"], emit_kv=emit_kv, tm=IN_TM)
    z = outs[0]
    dft = _dft_matrices(L)
    spectra = _filter_spectra(L, dft[0], *filt)
    yh = _hyena_branch(z, B, L, p["conv_w"], p["conv_b"], p["hyena_d"], dft, spectra,
                       seqs_per_step=max(1, HYENA_ROWS_PER_STEP // L))
    if ctx_kv is None:
        ya = _context_attention(z, B, L)
    else:
        ya = _neighbourhood_attention(z, B, L, ctx_kv[0], ctx_kv[1], tiles)
    y = _merge(yh, ya, z, x2d, mod3, lambda i: cond_row(i * MERGE_TM // L), p["w_bh"], p["w_ba"], p["w_out"],
               p["ln_g"], p["ln_b"], alpha=alpha, tm=MERGE_TM)
    return y.reshape(B, L, D_MODEL), outs[1:]


def kernel(x_prompt, x_sample, c, cache_k, cache_v, c_ctx, w_ada, b_ada, w_in, conv_w, conv_b, filt_w1, filt_b1,
           filt_w2, filt_b2, filt_w3, filt_freq, hyena_d, rpb, w_bh, w_ba, w_out, ln_g, ln_b):
    depth = w_in.shape[0]
    alpha = (2.0 * depth) ** 0.25
    n_lat, lat_len = x_sample.shape[0], x_sample.shape[1]
    n_ctx, ctx_len = x_prompt.shape[0], x_prompt.shape[1]
    ctx_row = n_lat
    cond = jnp.zeros((8, D_MODEL), F32).at[:n_lat].set(c).at[ctx_row].set(c_ctx)

    xp, xs = x_prompt, x_sample
    new_k, new_v = [], []
    for l in range(depth):
        p = {"w_in": w_in[l].astype(BF16), "conv_w": conv_w[l], "conv_b": conv_b[l], "hyena_d": hyena_d[l],
             "w_bh": w_bh[l].astype(BF16), "w_ba": w_ba[l].astype(BF16), "w_out": w_out[l].astype(BF16),
             "ln_g": ln_g[l], "ln_b": ln_b[l]}
        mod3 = _modulation(cond, w_ada[l], b_ada[l]).reshape(8, 1, 3 * D_MODEL)
        filt = (filt_w1[l], filt_b1[l], filt_w2[l], filt_b2[l], filt_w3[l], filt_freq[l])
        tiles = _rpb_tiles(rpb[l])
        xp, (k_ctx, v_ctx) = _layer(xp, mod3, lambda b: ctx_row, None, p, filt, None, alpha=alpha)
        new_k.append(jnp.transpose(k_ctx.reshape(n_ctx, N_HEADS, HEAD_DIM, ctx_len), (0, 3, 1, 2)))
        new_v.append(jnp.transpose(v_ctx.reshape(n_ctx, N_HEADS, HEAD_DIM, ctx_len), (0, 3, 1, 2)))
        xs, _ = _layer(xs, mod3, lambda b: b, (cache_k[:, l], cache_v[:, l]), p, filt, tiles, alpha=alpha)
    return xp, xs, jnp.stack(new_k, axis=1), jnp.stack(new_v, axis=1)
```

```python
import functools
import math

import jax
import jax.numpy as jnp
import numpy as np
from jax.experimental import pallas as pl
from jax.experimental.pallas import tpu as pltpu

F32 = jnp.float32
BF16 = jnp.bfloat16

D_MODEL = 1024
D_HYENA = 1024
N_HEADS = 16
HEAD_DIM = 64
D_ATTN = N_HEADS * HEAD_DIM
D_IN = 4 * D_HYENA + 4 * D_ATTN + 2 * D_MODEL
GRID_W = 64
WIN_ROWS = 8
WIN_COLS = 16
FILTER_EMB = 33
FILTER_BANDS = (FILTER_EMB - 1) // 2
FILTER_HIDDEN = 64
DECAY_TARGET = 1e-2
MIN_DECAY = math.log(DECAY_TARGET) / 1.5
MAX_DECAY = math.log(DECAY_TARGET) / 0.3
DECAY_SHIFT = 0.05
LN_EPS = 1e-5
NEG_INF = -1e30

COL_VH, COL_X1, COL_X0, COL_GH, COL_Q, COL_K, COL_V, COL_GA, COL_MH, COL_MA = range(10)

LANES = 128
SUBLANES = 8
HEADS_PER_TILE = LANES // HEAD_DIM
N_HEAD_TILES = N_HEADS // HEADS_PER_TILE
V7X_VMEM_BYTES = 64 * 1024 * 1024
ATTN_SCALE = HEAD_DIM ** -0.5
LOG2E = math.log2(math.e)

Q_BLOCK = 256
Q_BLOCK_ROWS = Q_BLOCK // GRID_W
KEY_WIN_ROWS = 12
KEY_WIN = KEY_WIN_ROWS * GRID_W

IN_TM = 256
MERGE_TM = 1024
MERGE_CHUNK = 256
HYENA_ROWS_PER_STEP = 4096


def _vmem_limit(block_bytes, temp_bytes):
    need = int(block_bytes + temp_bytes)
    return min(max(need, 16 * 1024 * 1024), V7X_VMEM_BYTES - 8 * 1024 * 1024)


def _nbytes(shape, dtype):
    return int(np.prod(shape)) * jnp.dtype(dtype).itemsize


def _silu(x):
    return x * jax.nn.sigmoid(x)


def _split_bf16(a):
    hi = a.astype(BF16)
    lo = (a - hi.astype(F32)).astype(BF16)
    return hi, lo


def _dot(a, b):
    return jnp.dot(a, b, preferred_element_type=F32)


def _dot_split(a_hi, a_lo, b):
    b_hi, b_lo = _split_bf16(b)
    return _dot(a_hi, b_hi) + (_dot(a_hi, b_lo) + _dot(a_lo, b_hi))


@functools.lru_cache(maxsize=None)
def _dft_constants(L):
    n = 2 * L
    k = np.arange(L, dtype=np.float64)[:, None]
    s = np.arange(L, dtype=np.float64)[None, :]
    ang = 2.0 * np.pi * k * s / n
    nyq = np.cos(np.pi * np.arange(L, dtype=np.float64))
    C = np.cos(ang)
    S = np.sin(ang)
    S[0, :] = nyq
    F = np.concatenate([C, S], axis=0)
    w = np.full((L,), 2.0)
    w[0] = 1.0
    Gc = (C * w[:, None]).T / n
    Ss = 2.0 * np.sin(ang)
    Ss[0, :] = nyq
    Gs = Ss.T / n
    G = np.concatenate([Gc, Gs], axis=1)
    return F.astype(np.float32), G.astype(np.float32)


def _dft_matrices(L):
    F, G = _dft_constants(L)
    return jnp.asarray(F).astype(BF16), jnp.asarray(G).astype(BF16)


@functools.lru_cache(maxsize=None)
def _filter_constants(L):
    t = np.linspace(0.0, 1.0, L, dtype=np.float32)[:, None]
    bands = np.linspace(1e-4, FILTER_BANDS - 1, FILTER_BANDS, dtype=np.float32)[None]
    w = (2.0 * math.pi * np.arange(L, dtype=np.float32)[:, None] / L).astype(np.float32)
    z = np.concatenate([t, np.cos(bands * w), -np.sin(bands * w)], axis=-1).astype(np.float32)
    z_pad = np.zeros((L, LANES), np.float32)
    z_pad[:, :FILTER_EMB] = z
    deltas = np.linspace(MIN_DECAY, MAX_DECAY, D_HYENA, dtype=np.float32)
    decay = (np.exp(-t * np.abs(deltas)) + np.float32(DECAY_SHIFT)).astype(np.float32)
    return z_pad, decay


@functools.lru_cache(maxsize=None)
def _rpb_constants():
    c = np.arange(GRID_W)[:, None]
    kc = np.arange(GRID_W)[None, :]
    c0 = np.clip(c - WIN_COLS // 2, 0, GRID_W - WIN_COLS)
    in_win = (kc >= c0) & (kc < c0 + WIN_COLS)
    idx = np.clip(kc - c + WIN_COLS - 1, 0, 2 * WIN_COLS - 2)
    onehot = np.zeros((LANES, GRID_W, LANES), np.float32)
    for half in range(HEADS_PER_TILE):
        lanes = half * GRID_W + np.arange(GRID_W)
        onehot[idx, c, lanes[None, :]] = 1.0
    mask = np.concatenate([in_win, in_win], axis=1).astype(np.float32)
    return onehot.reshape(LANES, GRID_W * LANES).astype(BF16), mask.reshape(1, GRID_W * LANES)


def _mod_kernel(cond_ref, w_ref, b_ref, o_ref):
    s = _silu(cond_ref[...]).astype(BF16)
    o_ref[...] = _dot(s, w_ref[...].astype(BF16)) + b_ref[...]


def _modulation(cond, w_ada, b_ada):
    rows = cond.shape[0]
    return pl.pallas_call(
        _mod_kernel,
        out_shape=jax.ShapeDtypeStruct((rows, 3 * D_MODEL), F32),
        grid=(3,),
        in_specs=[pl.BlockSpec((rows, D_MODEL), lambda j: (0, 0)),
                  pl.BlockSpec((D_MODEL, D_MODEL), lambda j: (0, j)),
                  pl.BlockSpec((1, D_MODEL), lambda j: (0, j))],
        out_specs=pl.BlockSpec((rows, D_MODEL), lambda j: (0, j)),
        name="mod",
    )(cond, w_ada, b_ada.reshape(1, -1))


def _spectra_kernel(z_ref, w1_ref, b1_ref, w2_ref, b2_ref, w3f_ref, w3b_ref, freq_ref, decay_ref, d_ref, f_ref,
                    fp_ref, fqa_ref, fpb_ref, hf_ref):
    L = decay_ref.shape[0]

    @pl.when(pl.program_id(0) == 0)
    def _():
        z_hi, z_lo = _split_bf16(z_ref[...])
        hf = jnp.sin(freq_ref[0:1, :] * (_dot_split(z_hi, z_lo, w1_ref[...]) + b1_ref[...]))
        h_hi, h_lo = _split_bf16(hf)
        hf_ref[...] = jnp.sin(freq_ref[1:2, :] * (_dot_split(h_hi, h_lo, w2_ref[...]) + b2_ref[...]))

    h_hi, h_lo = _split_bf16(hf_ref[...])
    decay = decay_ref[...]
    h_fwd = _dot_split(h_hi, h_lo, w3f_ref[...]) * decay
    h_bwd = _dot_split(h_hi, h_lo, w3b_ref[...]) * decay
    row0 = jax.lax.broadcasted_iota(jnp.int32, (L, 1), 0) == 0
    h_bwd = jnp.where(row0, 0.0, h_bwd)
    fsum = (h_fwd + h_bwd).astype(BF16)
    fdif = (h_fwd - h_bwd).astype(BF16)
    skip = d_ref[...]
    fp = _dot(f_ref[0:L, :], fsum) + skip
    fq = _dot(f_ref[L:2 * L, :], fdif)
    nyq = _dot(f_ref[L:L + 2 * SUBLANES, :], fsum)[0:1, :] + skip
    fp_ref[...] = fp
    fqa_ref[...] = jnp.where(row0, 0.0, fq)
    fpb_ref[...] = jnp.where(row0, nyq, fp)


def _filter_spectra(L, f_mat, hyena_d, w1, b1, w2, b2, w3, freq):
    tc = 256
    z_emb, decay = _filter_constants(L)
    w1p = jnp.zeros((LANES, FILTER_HIDDEN), F32).at[:FILTER_EMB].set(w1)
    nblk = D_HYENA // tc
    const = lambda shape: pl.BlockSpec(shape, lambda c: (0, 0), pipeline_mode=pl.Buffered(1))
    out_spec = pl.BlockSpec((L, tc), lambda c: (0, c))
    blocks = (_nbytes((2 * L, L), BF16) + _nbytes((L, LANES), F32) + 2 * 4 * _nbytes((L, tc), F32))
    return pl.pallas_call(
        _spectra_kernel,
        out_shape=[jax.ShapeDtypeStruct((L, D_HYENA), F32)] * 3,
        grid=(nblk,),
        in_specs=[const((L, LANES)),
                  const((LANES, FILTER_HIDDEN)), const((1, FILTER_HIDDEN)),
                  const((FILTER_HIDDEN, FILTER_HIDDEN)), const((1, FILTER_HIDDEN)),
                  pl.BlockSpec((FILTER_HIDDEN, tc), lambda c: (0, c)),
                  pl.BlockSpec((FILTER_HIDDEN, tc), lambda c: (0, nblk + c)),
                  const((2, FILTER_HIDDEN)),
                  pl.BlockSpec((L, tc), lambda c: (0, c)),
                  pl.BlockSpec((1, tc), lambda c: (0, c)),
                  const((2 * L, L))],
        out_specs=[out_spec] * 3,
        scratch_shapes=[pltpu.VMEM((L, FILTER_HIDDEN), F32)],
        compiler_params=pltpu.CompilerParams(
            dimension_semantics=("arbitrary",),
            vmem_limit_bytes=_vmem_limit(blocks, 12 * _nbytes((L, tc), F32))),
        name=f"spectra_{L}",
    )(z_emb, w1p, b1.reshape(1, -1), w2, b2.reshape(1, -1), w3, w3, freq, decay, hyena_d.reshape(1, -1), f_mat)


def _rpb_kernel(r_ref, onehot_ref, mask_ref, o_ref):
    r = r_ref[...]
    r_hi = r.astype(BF16)
    rem = r - r_hi.astype(F32)
    r_mid = rem.astype(BF16)
    r_lo = (rem - r_mid.astype(F32)).astype(BF16)
    oh = onehot_ref[...]
    t = _dot(r_hi, oh) + (_dot(r_mid, oh) + _dot(r_lo, oh))
    t = jnp.where(mask_ref[...] > 0.0, t * LOG2E, NEG_INF)
    for c in range(o_ref.shape[1]):
        o_ref[:, c, :] = t[:, c * LANES:(c + 1) * LANES]


def _rpb_tiles(rpb):
    n_dr = 2 * WIN_ROWS - 1
    rows = N_HEADS * n_dr
    onehot, mask = _rpb_constants()
    r = jnp.zeros((rows, LANES), F32).at[:, :2 * WIN_COLS - 1].set(rpb.reshape(rows, 2 * WIN_COLS - 1))
    cols_per_step = 16
    tn = cols_per_step * LANES
    out = pl.pallas_call(
        _rpb_kernel,
        out_shape=jax.ShapeDtypeStruct((rows, GRID_W, LANES), F32),
        grid=(GRID_W // cols_per_step,),
        in_specs=[pl.BlockSpec((rows, LANES), lambda j: (0, 0)),
                  pl.BlockSpec((LANES, tn), lambda j: (0, j)),
                  pl.BlockSpec((1, tn), lambda j: (0, j))],
        out_specs=pl.BlockSpec((rows, cols_per_step, LANES), lambda j: (0, j, 0)),
        name="rpb_tiles",
    )(r, onehot, mask)
    return out.reshape(N_HEADS, n_dr, GRID_W, LANES)


def _inproj_kernel(x_ref, mod_ref, w_ref, z_ref, *kv_refs):
    shift = mod_ref[:, 0:D_MODEL]
    scale = mod_ref[:, D_MODEL:2 * D_MODEL]
    h = (x_ref[...] * (1.0 + scale) + shift).astype(BF16)
    epilogue = {COL_GH: _silu, COL_GA: _silu, COL_MH: jax.nn.sigmoid, COL_MA: jax.nn.sigmoid,
                COL_Q: lambda acc: acc * (ATTN_SCALE * LOG2E)}
    for col in range(D_IN // D_MODEL):
        cols = slice(col * D_MODEL, (col + 1) * D_MODEL)
        acc = _dot(h, w_ref[:, cols])
        z_ref[:, cols] = epilogue.get(col, lambda acc: acc)(acc).astype(BF16)
        if kv_refs and col == COL_K:
            kv_refs[0][...] = acc.T
        if kv_refs and col == COL_V:
            kv_refs[1][...] = acc.T


def _in_projection(x2d, mod3, mod_index, w_in_bf16, *, emit_kv, tm):
    M = x2d.shape[0]
    out_shape = [jax.ShapeDtypeStruct((M, D_IN), BF16)]
    out_specs = [pl.BlockSpec((tm, D_IN), lambda i: (i, 0))]
    if emit_kv:
        out_shape += [jax.ShapeDtypeStruct((M // tm, D_ATTN, tm), F32)] * 2
        out_specs += [pl.BlockSpec((None, D_ATTN, tm), lambda i: (i, 0, 0))] * 2
    blocks = (_nbytes((D_MODEL, D_IN), BF16) + 2 * _nbytes((tm, D_MODEL), F32) + 2 * _nbytes((tm, D_IN), BF16)
              + (4 * _nbytes((tm, D_ATTN), F32) if emit_kv else 0))
    return pl.pallas_call(
        _inproj_kernel,
        out_shape=out_shape,
        grid=(M // tm,),
        in_specs=[pl.BlockSpec((tm, D_MODEL), lambda i: (i, 0)),
                  pl.BlockSpec((None, 1, 3 * D_MODEL), lambda i: (mod_index(i), 0, 0)),
                  pl.BlockSpec((D_MODEL, D_IN), lambda i: (0, 0), pipeline_mode=pl.Buffered(1))],
        out_specs=out_specs,
        compiler_params=pltpu.CompilerParams(
            dimension_semantics=("arbitrary",),
            vmem_limit_bytes=_vmem_limit(blocks, 4 * _nbytes((tm, D_MODEL), F32))),
        name="inproj",
    )(x2d, mod3, w_in_bf16)


def _hyena_kernel(vh_ref, x1_ref, x0_ref, gh_ref, cwv_ref, cw1_ref, cw0_ref, cbv_ref, cb1_ref, cb0_ref,
                  f_ref, g_ref, fp_ref, fqa_ref, fpb_ref, o_ref):
    L = fp_ref.shape[0]
    slab = jax.lax.broadcasted_iota(jnp.int32, (SUBLANES, 1), 0)

    def short_conv(x_ref, rows, w_ref, b_ref):
        x = x_ref[rows, :].astype(F32)
        prev = pltpu.roll(x, 1, 0)
        nxt = pltpu.roll(x, L - 1, 0)
        prev = jnp.concatenate([jnp.where(slab == 0, 0.0, prev[:SUBLANES]), prev[SUBLANES:]], axis=0)
        nxt = jnp.concatenate([nxt[:-SUBLANES], jnp.where(slab == SUBLANES - 1, 0.0, nxt[-SUBLANES:])], axis=0)
        return prev * w_ref[0:1, :] + x * w_ref[1:2, :] + nxt * w_ref[2:3, :] + b_ref[...]

    for s in range(vh_ref.shape[0] // L):
        rows = slice(s * L, (s + 1) * L)
        u = short_conv(vh_ref, rows, cwv_ref, cbv_ref) * short_conv(x1_ref, rows, cw1_ref, cb1_ref)
        t = _dot(f_ref[...], u.astype(BF16))
        p = t[:L]
        q = t[L:]
        yp = p * fp_ref[...] - q * fqa_ref[...]
        yq = p * fqa_ref[...] + q * fpb_ref[...]
        y = _dot(g_ref[...], jnp.concatenate([yp.astype(BF16), yq.astype(BF16)], axis=0))
        y_h = y * short_conv(x0_ref, rows, cw0_ref, cb0_ref)
        o_ref[rows, :] = (y_h * gh_ref[rows, :].astype(F32)).astype(BF16)


def _hyena_branch(z, B, L, conv_w, conv_b, dft, spectra, *, tc=256, seqs_per_step):
    nblk = D_HYENA // tc
    tm = seqs_per_step * L
    zcol = lambda piece: pl.BlockSpec((tm, tc), lambda c, b: (b, piece * nblk + c))
    wcol = lambda rows, piece: pl.BlockSpec((rows, tc), lambda c, b: (0, piece * nblk + c))
    chan = lambda rows: pl.BlockSpec((rows, tc), lambda c, b: (0, c))
    const = lambda shape: pl.BlockSpec(shape, lambda c, b: (0, 0), pipeline_mode=pl.Buffered(1))
    blocks = (2 * _nbytes((2 * L, L), BF16) + 2 * 5 * _nbytes((tm, tc), BF16) + 2 * 3 * _nbytes((L, tc), F32))
    return pl.pallas_call(
        _hyena_kernel,
        out_shape=jax.ShapeDtypeStruct((B * L, D_HYENA), BF16),
        grid=(nblk, B // seqs_per_step),
        in_specs=[zcol(COL_VH), zcol(COL_X1), zcol(COL_X0), zcol(COL_GH),
                  wcol(3, 0), wcol(3, 1), wcol(3, 2), wcol(1, 0), wcol(1, 1), wcol(1, 2),
                  const((2 * L, L)), const((L, 2 * L)), chan(L), chan(L), chan(L)],
        out_specs=pl.BlockSpec((tm, tc), lambda c, b: (b, c)),
        compiler_params=pltpu.CompilerParams(
            dimension_semantics=("arbitrary", "arbitrary"),
            vmem_limit_bytes=_vmem_limit(blocks, 16 * _nbytes((tm, tc), F32))),
        name=f"hyena_{L}",
    )(z, z, z, z, conv_w, conv_w, conv_w, conv_b.reshape(1, -1), conv_b.reshape(1, -1), conv_b.reshape(1, -1),
      *dft, *spectra)


def _head_masks():
    lane = jax.lax.broadcasted_iota(jnp.int32, (1, LANES), 1)
    return [(lane >= h * HEAD_DIM) & (lane < (h + 1) * HEAD_DIM) for h in range(HEADS_PER_TILE)]


def _qk(q, k):
    return jax.lax.dot_general(q, k, (((1,), (1,)), ((), ())), preferred_element_type=F32)


def _ctx_attn_kernel(q_ref, k_ref, v_ref, g_ref, o_ref):
    masks = _head_masks()
    for t in range(N_HEAD_TILES):
        cols = slice(t * LANES, (t + 1) * LANES)
        q = q_ref[:, cols]
        k = k_ref[:, cols]
        v = v_ref[:, cols]
        out = jnp.zeros(q.shape, F32)
        for msk in masks:
            s = _qk(jnp.where(msk, q, jnp.zeros_like(q)), k)
            m = jnp.max(s, axis=-1, keepdims=True)
            p = jnp.exp2(s - m)
            l = jnp.sum(p, axis=-1, keepdims=True)
            o = _dot(p.astype(BF16), jnp.where(msk, v, jnp.zeros_like(v)))
            out = out + o / l
        o_ref[:, cols] = (out * g_ref[:, cols].astype(F32)).astype(BF16)


def _context_attention(z, B, L):
    zcol = lambda piece: pl.BlockSpec((L, D_ATTN), lambda b: (b, piece))
    return pl.pallas_call(
        _ctx_attn_kernel,
        out_shape=jax.ShapeDtypeStruct((B * L, D_ATTN), BF16),
        grid=(B,),
        in_specs=[zcol(COL_Q), zcol(COL_K), zcol(COL_V), zcol(COL_GA)],
        out_specs=pl.BlockSpec((L, D_ATTN), lambda b: (b, 0)),
        compiler_params=pltpu.CompilerParams(dimension_semantics=("arbitrary",)),
        name="ctx_attn",
    )(z, z, z, z)


def _key_window_start(qb):
    rows = 1024 // GRID_W
    r_first = qb * Q_BLOCK_ROWS
    r_last = r_first + Q_BLOCK_ROWS - 1
    lo = min(max(r_first - WIN_ROWS // 2, 0), rows - WIN_ROWS)
    hi = min(max(r_last - WIN_ROWS // 2, 0), rows - WIN_ROWS) + WIN_ROWS
    start = min(lo, rows - KEY_WIN_ROWS)
    start -= start % Q_BLOCK_ROWS
    assert start <= lo and hi <= start + KEY_WIN_ROWS
    return start


def _nbr_attn_kernel(q_ref, k0_ref, k1_ref, k2_ref, v0_ref, v1_ref, v2_ref, kc_ref, vc_ref, g_ref, tiles_ref,
                     o_ref, bias_ref, *, n_rows):
    masks = _head_masks()
    feat = jax.lax.broadcasted_iota(jnp.int32, (LANES, 1), 0)
    k_refs = (k0_ref, k1_ref, k2_ref)
    v_refs = (v0_ref, v1_ref, v2_ref)
    key_blocks = KEY_WIN // Q_BLOCK

    def window_row_start(r):
        return min(max(r - WIN_ROWS // 2, 0), n_rows - WIN_ROWS)

    def build_bias(qb):
        k_start = _key_window_start(qb)
        used = set()
        for i in range(Q_BLOCK_ROWS):
            r = qb * Q_BLOCK_ROWS + i
            r0 = window_row_start(r)
            for j in range(KEY_WIN_ROWS):
                kr = k_start + j
                rows = slice(i * GRID_W, (i + 1) * GRID_W)
                cols = slice(j * GRID_W, (j + 1) * GRID_W)
                lanes = slice((j % HEADS_PER_TILE) * GRID_W, (j % HEADS_PER_TILE + 1) * GRID_W)
                inside = r0 <= kr < r0 + WIN_ROWS
                if inside:
                    used.add(j // Q_BLOCK_ROWS)
                for h in range(HEADS_PER_TILE):
                    if inside:
                        bias_ref[h, rows, cols] = tiles_ref[h, kr - r + WIN_ROWS - 1, :, lanes]
                    else:
                        bias_ref[h, rows, cols] = jnp.full((GRID_W, GRID_W), NEG_INF, F32)
        return sorted(used)

    def attend(blocks):
        for b in range(q_ref.shape[0]):
            q = q_ref[b]
            kc_t = kc_ref[b].astype(BF16)
            vc_t = vc_ref[b].astype(BF16)
            out = jnp.zeros(q.shape, F32)
            for h, msk in enumerate(masks):
                qh = jnp.where(msk, q, jnp.zeros_like(q))
                s = [_qk(qh, k_refs[j][b]) + bias_ref[h, :, j * Q_BLOCK:(j + 1) * Q_BLOCK] for j in blocks]
                s.append(_dot(qh, kc_t))
                m = jnp.max(functools.reduce(jnp.maximum, s), axis=-1, keepdims=True)
                p = [jnp.exp2(x - m) for x in s]
                l = jnp.sum(functools.reduce(jnp.add, p), axis=-1, keepdims=True)
                p = [x.astype(BF16) for x in p]
                zeros = jnp.zeros((), BF16)
                o = functools.reduce(jnp.add, [_dot(pj, jnp.where(msk, v_refs[j][b], zeros))
                                               for pj, j in zip(p, blocks)])
                head_feat = (feat >= h * HEAD_DIM) & (feat < (h + 1) * HEAD_DIM)
                o = o + _qk(p[-1], jnp.where(head_feat, vc_t, zeros))
                out = out + o / l
            o_ref[b] = (out * g_ref[b].astype(F32)).astype(BF16)

    for qb in range(n_rows // Q_BLOCK_ROWS):
        @pl.when(pl.program_id(1) == qb)
        def _(qb=qb):
            blocks = build_bias(qb)
            assert blocks and all(0 <= j < key_blocks for j in blocks)
            attend(blocks)


def _neighbourhood_attention(z, B, L, cache_k, cache_v, tiles):
    n_rows = L // GRID_W
    n_qb = L // Q_BLOCK
    assert KEY_WIN == 3 * Q_BLOCK and all(_key_window_start(qb) % Q_BLOCK_ROWS == 0 for qb in range(n_qb))
    z3 = z.reshape(B, L, D_IN)
    n_ctx = cache_k.shape[1]
    kc = jnp.transpose(cache_k, (0, 2, 3, 1)).reshape(B, D_ATTN, n_ctx)
    vc = jnp.transpose(cache_v, (0, 2, 3, 1)).reshape(B, D_ATTN, n_ctx)
    tiles_per_col = D_ATTN // LANES

    def win_block(qb):
        return jnp.where(qb < n_qb // 2, 0, (n_rows - KEY_WIN_ROWS) // Q_BLOCK_ROWS)

    assert [_key_window_start(qb) // Q_BLOCK_ROWS for qb in range(n_qb)] == \
        [0 if qb < n_qb // 2 else (n_rows - KEY_WIN_ROWS) // Q_BLOCK_ROWS for qb in range(n_qb)]
    qspec = lambda piece: pl.BlockSpec((B, Q_BLOCK, LANES), lambda t, qb: (0, qb, piece * tiles_per_col + t))
    kspec = lambda piece, j: pl.BlockSpec((B, Q_BLOCK, LANES),
                                          lambda t, qb: (0, win_block(qb) + j, piece * tiles_per_col + t))
    cspec = pl.BlockSpec((B, LANES, n_ctx), lambda t, qb: (0, t, 0))
    out = pl.pallas_call(
        functools.partial(_nbr_attn_kernel, n_rows=n_rows),
        out_shape=jax.ShapeDtypeStruct((B, L, D_ATTN), BF16),
        grid=(N_HEAD_TILES, n_qb),
        in_specs=[qspec(COL_Q), kspec(COL_K, 0), kspec(COL_K, 1), kspec(COL_K, 2),
                  kspec(COL_V, 0), kspec(COL_V, 1), kspec(COL_V, 2), cspec, cspec, qspec(COL_GA),
                  pl.BlockSpec((HEADS_PER_TILE, 2 * WIN_ROWS - 1, GRID_W, LANES), lambda t, qb: (t, 0, 0, 0))],
        out_specs=pl.BlockSpec((B, Q_BLOCK, LANES), lambda t, qb: (0, qb, t)),
        scratch_shapes=[pltpu.VMEM((HEADS_PER_TILE, Q_BLOCK, KEY_WIN), F32)],
        compiler_params=pltpu.CompilerParams(dimension_semantics=("arbitrary", "arbitrary")),
        name="nbr_attn",
    )(z3, z3, z3, z3, z3, z3, z3, kc, vc, z3, tiles)
    return out.reshape(B * L, D_ATTN)


def _merge_kernel(yh_ref, ya_ref, mh_ref, ma_ref, x_ref, mod_ref, wbh_ref, wba_ref, wout_ref, lng_ref, lnb_ref,
                  o_ref, *, alpha):
    gate = mod_ref[:, 2 * D_MODEL:3 * D_MODEL]
    tm = x_ref.shape[0]
    for r in range(tm // MERGE_CHUNK):
        rows = slice(r * MERGE_CHUNK, (r + 1) * MERGE_CHUNK)
        p_h = _dot(yh_ref[rows, :], wbh_ref[...])
        p_a = _dot(ya_ref[rows, :], wba_ref[...])
        m_h = mh_ref[rows, :].astype(F32)
        m_a = ma_ref[rows, :].astype(F32)
        out = _dot((m_h * p_h + m_a * p_a).astype(BF16), wout_ref[...])
        v = alpha * x_ref[rows, :] + gate * out
        mu = jnp.mean(v, axis=-1, keepdims=True)
        d = v - mu
        var = jnp.mean(d * d, axis=-1, keepdims=True)
        o_ref[rows, :] = d * jax.lax.rsqrt(var + LN_EPS) * lng_ref[...] + lnb_ref[...]


def _merge(yh, ya, z, x2d, mod3, mod_index, w_bh, w_ba, w_out, ln_g, ln_b, *, alpha, tm):
    M = x2d.shape[0]
    tok = lambda width, col: pl.BlockSpec((tm, width), lambda i: (i, col))
    const = lambda shape: pl.BlockSpec(shape, lambda i: (0, 0))
    return pl.pallas_call(
        functools.partial(_merge_kernel, alpha=alpha),
        out_shape=jax.ShapeDtypeStruct((M, D_MODEL), F32),
        grid=(M // tm,),
        in_specs=[tok(D_HYENA, 0), tok(D_ATTN, 0), tok(D_MODEL, COL_MH), tok(D_MODEL, COL_MA), tok(D_MODEL, 0),
                  pl.BlockSpec((None, 1, 3 * D_MODEL), lambda i: (mod_index(i), 0, 0)),
                  const((D_HYENA, D_MODEL)), const((D_ATTN, D_MODEL)), const((D_MODEL, D_MODEL)),
                  const((1, D_MODEL)), const((1, D_MODEL))],
        out_specs=pl.BlockSpec((tm, D_MODEL), lambda i: (i, 0)),
        compiler_params=pltpu.CompilerParams(
            dimension_semantics=("arbitrary",),
            vmem_limit_bytes=_vmem_limit(
                2 * (4 * _nbytes((tm, D_MODEL), BF16) + 2 * _nbytes((tm, D_MODEL), F32)
                     + 3 * _nbytes((D_MODEL, D_MODEL), BF16)),
                8 * _nbytes((tm, D_MODEL), F32))),
        name="merge",
    )(yh, ya, z, z, x2d, mod3, w_bh, w_ba, w_out, ln_g.reshape(1, -1), ln_b.reshape(1, -1))


def _layer(x, mod3, cond_row, ctx_kv, p, filt, tiles, *, alpha):
    B, L, _ = x.shape
    x2d = x.reshape(B * L, D_MODEL)
    emit_kv = ctx_kv is None
    assert not emit_kv or IN_TM == L
    outs = _in_projection(x2d, mod3, lambda i: cond_row(i * IN_TM // L), p["w_in"], emit_kv=emit_kv, tm=IN_TM)
    z = outs[0]
    dft = _dft_matrices(L)
    spectra = _filter_spectra(L, dft[0], p["hyena_d"], *filt)
    yh = _hyena_branch(z, B, L, p["conv_w"], p["conv_b"], dft, spectra,
                       seqs_per_step=max(1, HYENA_ROWS_PER_STEP // L))
    if ctx_kv is None:
        ya = _context_attention(z, B, L)
    else:
        ya = _neighbourhood_attention(z, B, L, ctx_kv[0], ctx_kv[1], tiles)
    y = _merge(yh, ya, z, x2d, mod3, lambda i: cond_row(i * MERGE_TM // L), p["w_bh"], p["w_ba"], p["w_out"],
               p["ln_g"], p["ln_b"], alpha=alpha, tm=MERGE_TM)
    return y.reshape(B, L, D_MODEL), outs[1:]


def kernel(x_prompt, x_sample, c, cache_k, cache_v, c_ctx, w_ada, b_ada, w_in, conv_w, conv_b, filt_w1, filt_b1,
           filt_w2, filt_b2, filt_w3, filt_freq, hyena_d, rpb, w_bh, w_ba, w_out, ln_g, ln_b):
    depth = w_in.shape[0]
    alpha = (2.0 * depth) ** 0.25
    n_lat, lat_len = x_sample.shape[0], x_sample.shape[1]
    n_ctx, ctx_len = x_prompt.shape[0], x_prompt.shape[1]
    ctx_row = n_lat
    cond = jnp.zeros((8, D_MODEL), F32).at[:n_lat].set(c).at[ctx_row].set(c_ctx)

    xp, xs = x_prompt, x_sample
    new_k, new_v = [], []
    for l in range(depth):
        p = {"w_in": w_in[l].astype(BF16), "conv_w": conv_w[l], "conv_b": conv_b[l], "hyena_d": hyena_d[l],
             "w_bh": w_bh[l].astype(BF16), "w_ba": w_ba[l].astype(BF16), "w_out": w_out[l].astype(BF16),
             "ln_g": ln_g[l], "ln_b": ln_b[l]}
        mod3 = _modulation(cond, w_ada[l], b_ada[l]).reshape(8, 1, 3 * D_MODEL)
        filt = (filt_w1[l], filt_b1[l], filt_w2[l], filt_b2[l], filt_w3[l], filt_freq[l])
        tiles = _rpb_tiles(rpb[l])
        xp, (k_ctx, v_ctx) = _layer(xp, mod3, lambda b: ctx_row, None, p, filt, None, alpha=alpha)
        new_k.append(jnp.transpose(k_ctx.reshape(n_ctx, N_HEADS, HEAD_DIM, ctx_len), (0, 3, 1, 2)))
        new_v.append(jnp.transpose(v_ctx.reshape(n_ctx, N_HEADS, HEAD_DIM, ctx_len), (0, 3, 1, 2)))
        xs, _ = _layer(xs, mod3, lambda b: b, (cache_k[:, l], cache_v[:, l]), p, filt, tiles, alpha=alpha)
    return xp, xs, jnp.stack(new_k, axis=1), jnp.stack(new_v, axis=1)
```

```python
import functools
import math

import jax
import jax.numpy as jnp
import numpy as np
from jax.experimental import pallas as pl
from jax.experimental.pallas import tpu as pltpu

F32 = jnp.float32
BF16 = jnp.bfloat16

D_MODEL = 1024
D_HYENA = 1024
N_HEADS = 16
HEAD_DIM = 64
D_ATTN = N_HEADS * HEAD_DIM
D_IN = 4 * D_HYENA + 4 * D_ATTN + 2 * D_MODEL
GRID_W = 64
WIN_ROWS = 8
WIN_COLS = 16
FILTER_EMB = 33
FILTER_BANDS = (FILTER_EMB - 1) // 2
FILTER_HIDDEN = 64
DECAY_TARGET = 1e-2
MIN_DECAY = math.log(DECAY_TARGET) / 1.5
MAX_DECAY = math.log(DECAY_TARGET) / 0.3
DECAY_SHIFT = 0.05
LN_EPS = 1e-5
NEG_INF = -1e30

COL_VH, COL_X1, COL_X0, COL_GH, COL_Q, COL_K, COL_V, COL_GA, COL_MH, COL_MA = range(10)

LANES = 128
SUBLANES = 8
HEADS_PER_TILE = LANES // HEAD_DIM
N_HEAD_TILES = N_HEADS // HEADS_PER_TILE
V7X_VMEM_BYTES = 64 * 1024 * 1024
ATTN_SCALE = HEAD_DIM ** -0.5
LOG2E = math.log2(math.e)

Q_BLOCK = 256
Q_BLOCK_ROWS = Q_BLOCK // GRID_W
KEY_WIN_ROWS = 12
KEY_WIN = KEY_WIN_ROWS * GRID_W

IN_TM = 256
MERGE_TM = 512
MERGE_CHUNK = 256
HYENA_ROWS_PER_STEP = 4096


def _vmem_limit(block_bytes, temp_bytes):
    need = int(block_bytes + temp_bytes)
    return min(max(need, 16 * 1024 * 1024), V7X_VMEM_BYTES - 8 * 1024 * 1024)


def _nbytes(shape, dtype):
    return int(np.prod(shape)) * jnp.dtype(dtype).itemsize


def _silu(x):
    return x * jax.nn.sigmoid(x)


def _split_bf16(a):
    hi = a.astype(BF16)
    lo = (a - hi.astype(F32)).astype(BF16)
    return hi, lo


def _dot(a, b):
    return jnp.dot(a, b, preferred_element_type=F32)


def _dot_split(a_hi, a_lo, b):
    b_hi, b_lo = _split_bf16(b)
    return _dot(a_hi, b_hi) + (_dot(a_hi, b_lo) + _dot(a_lo, b_hi))


@functools.lru_cache(maxsize=None)
def _dft_constants(L):
    n = 2 * L
    k = np.arange(L, dtype=np.float64)[:, None]
    s = np.arange(L, dtype=np.float64)[None, :]
    ang = 2.0 * np.pi * k * s / n
    nyq = np.cos(np.pi * np.arange(L, dtype=np.float64))
    C = np.cos(ang)
    S = np.sin(ang)
    S[0, :] = nyq
    F = np.concatenate([C, S], axis=0)
    w = np.full((L,), 2.0)
    w[0] = 1.0
    Gc = (C * w[:, None]).T / n
    Ss = 2.0 * np.sin(ang)
    Ss[0, :] = nyq
    Gs = Ss.T / n
    G = np.concatenate([Gc, Gs], axis=1)
    return F.astype(np.float32), G.astype(np.float32)


def _dft_matrices(L):
    F, G = _dft_constants(L)
    return jnp.asarray(F).astype(BF16), jnp.asarray(G).astype(BF16)


@functools.lru_cache(maxsize=None)
def _filter_constants(L):
    t = np.linspace(0.0, 1.0, L, dtype=np.float32)[:, None]
    bands = np.linspace(1e-4, FILTER_BANDS - 1, FILTER_BANDS, dtype=np.float32)[None]
    w = (2.0 * math.pi * np.arange(L, dtype=np.float32)[:, None] / L).astype(np.float32)
    z = np.concatenate([t, np.cos(bands * w), -np.sin(bands * w)], axis=-1).astype(np.float32)
    z_pad = np.zeros((L, LANES), np.float32)
    z_pad[:, :FILTER_EMB] = z
    deltas = np.linspace(MIN_DECAY, MAX_DECAY, D_HYENA, dtype=np.float32)
    decay = (np.exp(-t * np.abs(deltas)) + np.float32(DECAY_SHIFT)).astype(np.float32)
    return z_pad, decay


@functools.lru_cache(maxsize=None)
def _rpb_constants():
    c = np.arange(GRID_W)[:, None]
    kc = np.arange(GRID_W)[None, :]
    c0 = np.clip(c - WIN_COLS // 2, 0, GRID_W - WIN_COLS)
    in_win = (kc >= c0) & (kc < c0 + WIN_COLS)
    idx = np.clip(kc - c + WIN_COLS - 1, 0, 2 * WIN_COLS - 2)
    onehot = np.zeros((LANES, GRID_W, LANES), np.float32)
    for half in range(HEADS_PER_TILE):
        lanes = half * GRID_W + np.arange(GRID_W)
        onehot[idx, c, lanes[None, :]] = 1.0
    mask = np.concatenate([in_win, in_win], axis=1).astype(np.float32)
    return onehot.reshape(LANES, GRID_W * LANES).astype(BF16), mask.reshape(1, GRID_W * LANES)


def _mod_kernel(cond_ref, w_ref, b_ref, o_ref):
    s = _silu(cond_ref[...]).astype(BF16)
    o_ref[...] = _dot(s, w_ref[...].astype(BF16)) + b_ref[...]


def _modulation(cond, w_ada, b_ada):
    rows = cond.shape[0]
    return pl.pallas_call(
        _mod_kernel,
        out_shape=jax.ShapeDtypeStruct((rows, 3 * D_MODEL), F32),
        grid=(3,),
        in_specs=[pl.BlockSpec((rows, D_MODEL), lambda j: (0, 0)),
                  pl.BlockSpec((D_MODEL, D_MODEL), lambda j: (0, j)),
                  pl.BlockSpec((1, D_MODEL), lambda j: (0, j))],
        out_specs=pl.BlockSpec((rows, D_MODEL), lambda j: (0, j)),
        name="mod",
    )(cond, w_ada, b_ada.reshape(1, -1))


def _spectra_kernel(z_ref, w1_ref, b1_ref, w2_ref, b2_ref, w3f_ref, w3b_ref, freq_ref, decay_ref, d_ref, f_ref,
                    fp_ref, fqa_ref, fpb_ref, hf_ref):
    L = decay_ref.shape[0]

    @pl.when(pl.program_id(0) == 0)
    def _():
        z_hi, z_lo = _split_bf16(z_ref[...])
        hf = jnp.sin(freq_ref[0:1, :] * (_dot_split(z_hi, z_lo, w1_ref[...]) + b1_ref[...]))
        h_hi, h_lo = _split_bf16(hf)
        hf_ref[...] = jnp.sin(freq_ref[1:2, :] * (_dot_split(h_hi, h_lo, w2_ref[...]) + b2_ref[...]))

    h_hi, h_lo = _split_bf16(hf_ref[...])
    decay = decay_ref[...]
    h_fwd = _dot_split(h_hi, h_lo, w3f_ref[...]) * decay
    h_bwd = _dot_split(h_hi, h_lo, w3b_ref[...]) * decay
    row0 = jax.lax.broadcasted_iota(jnp.int32, (L, 1), 0) == 0
    h_bwd = jnp.where(row0, 0.0, h_bwd)
    fsum = (h_fwd + h_bwd).astype(BF16)
    fdif = (h_fwd - h_bwd).astype(BF16)
    skip = d_ref[...]
    fp = _dot(f_ref[0:L, :], fsum) + skip
    fq = _dot(f_ref[L:2 * L, :], fdif)
    nyq = _dot(f_ref[L:L + 2 * SUBLANES, :], fsum)[0:1, :] + skip
    fp_ref[...] = fp
    fqa_ref[...] = jnp.where(row0, 0.0, fq)
    fpb_ref[...] = jnp.where(row0, nyq, fp)


def _filter_spectra(L, f_mat, hyena_d, w1, b1, w2, b2, w3, freq):
    tc = 256
    z_emb, decay = _filter_constants(L)
    w1p = jnp.zeros((LANES, FILTER_HIDDEN), F32).at[:FILTER_EMB].set(w1)
    nblk = D_HYENA // tc
    const = lambda shape: pl.BlockSpec(shape, lambda c: (0, 0), pipeline_mode=pl.Buffered(1))
    out_spec = pl.BlockSpec((L, tc), lambda c: (0, c))
    blocks = (_nbytes((2 * L, L), BF16) + _nbytes((L, LANES), F32) + 2 * 4 * _nbytes((L, tc), F32))
    return pl.pallas_call(
        _spectra_kernel,
        out_shape=[jax.ShapeDtypeStruct((L, D_HYENA), F32)] * 3,
        grid=(nblk,),
        in_specs=[const((L, LANES)),
                  const((LANES, FILTER_HIDDEN)), const((1, FILTER_HIDDEN)),
                  const((FILTER_HIDDEN, FILTER_HIDDEN)), const((1, FILTER_HIDDEN)),
                  pl.BlockSpec((FILTER_HIDDEN, tc), lambda c: (0, c)),
                  pl.BlockSpec((FILTER_HIDDEN, tc), lambda c: (0, nblk + c)),
                  const((2, FILTER_HIDDEN)),
                  pl.BlockSpec((L, tc), lambda c: (0, c)),
                  pl.BlockSpec((1, tc), lambda c: (0, c)),
                  const((2 * L, L))],
        out_specs=[out_spec] * 3,
        scratch_shapes=[pltpu.VMEM((L, FILTER_HIDDEN), F32)],
        compiler_params=pltpu.CompilerParams(
            dimension_semantics=("arbitrary",),
            vmem_limit_bytes=_vmem_limit(blocks, 12 * _nbytes((L, tc), F32))),
        name=f"spectra_{L}",
    )(z_emb, w1p, b1.reshape(1, -1), w2, b2.reshape(1, -1), w3, w3, freq, decay, hyena_d.reshape(1, -1), f_mat)


def _rpb_kernel(r_ref, onehot_ref, mask_ref, o_ref):
    r = r_ref[...]
    r_hi = r.astype(BF16)
    rem = r - r_hi.astype(F32)
    r_mid = rem.astype(BF16)
    r_lo = (rem - r_mid.astype(F32)).astype(BF16)
    oh = onehot_ref[...]
    t = _dot(r_hi, oh) + (_dot(r_mid, oh) + _dot(r_lo, oh))
    t = jnp.where(mask_ref[...] > 0.0, t * LOG2E, NEG_INF)
    for c in range(o_ref.shape[1]):
        o_ref[:, c, :] = t[:, c * LANES:(c + 1) * LANES]


def _rpb_tiles(rpb):
    n_dr = 2 * WIN_ROWS - 1
    rows = N_HEADS * n_dr
    onehot, mask = _rpb_constants()
    r = jnp.zeros((rows, LANES), F32).at[:, :2 * WIN_COLS - 1].set(rpb.reshape(rows, 2 * WIN_COLS - 1))
    cols_per_step = 16
    tn = cols_per_step * LANES
    out = pl.pallas_call(
        _rpb_kernel,
        out_shape=jax.ShapeDtypeStruct((rows, GRID_W, LANES), F32),
        grid=(GRID_W // cols_per_step,),
        in_specs=[pl.BlockSpec((rows, LANES), lambda j: (0, 0)),
                  pl.BlockSpec((LANES, tn), lambda j: (0, j)),
                  pl.BlockSpec((1, tn), lambda j: (0, j))],
        out_specs=pl.BlockSpec((rows, cols_per_step, LANES), lambda j: (0, j, 0)),
        name="rpb_tiles",
    )(r, onehot, mask)
    return out.reshape(N_HEADS, n_dr, GRID_W, LANES)


def _inproj_kernel(x_ref, mod_ref, w_ref, z_ref, *kv_refs):
    shift = mod_ref[:, 0:D_MODEL]
    scale = mod_ref[:, D_MODEL:2 * D_MODEL]
    h = (x_ref[...] * (1.0 + scale) + shift).astype(BF16)
    epilogue = {COL_GH: _silu, COL_GA: _silu, COL_MH: jax.nn.sigmoid, COL_MA: jax.nn.sigmoid,
                COL_Q: lambda acc: acc * (ATTN_SCALE * LOG2E)}
    for col in range(D_IN // D_MODEL):
        cols = slice(col * D_MODEL, (col + 1) * D_MODEL)
        acc = _dot(h, w_ref[:, cols])
        z_ref[:, cols] = epilogue.get(col, lambda acc: acc)(acc).astype(BF16)
        if kv_refs and col == COL_K:
            kv_refs[0][...] = acc.T
        if kv_refs and col == COL_V:
            kv_refs[1][...] = acc.T


def _in_projection(x2d, mod3, mod_index, w_in_bf16, *, emit_kv, tm):
    M = x2d.shape[0]
    out_shape = [jax.ShapeDtypeStruct((M, D_IN), BF16)]
    out_specs = [pl.BlockSpec((tm, D_IN), lambda i: (i, 0))]
    if emit_kv:
        out_shape += [jax.ShapeDtypeStruct((M // tm, D_ATTN, tm), F32)] * 2
        out_specs += [pl.BlockSpec((None, D_ATTN, tm), lambda i: (i, 0, 0))] * 2
    blocks = (_nbytes((D_MODEL, D_IN), BF16) + 2 * _nbytes((tm, D_MODEL), F32) + 2 * _nbytes((tm, D_IN), BF16)
              + (4 * _nbytes((tm, D_ATTN), F32) if emit_kv else 0))
    return pl.pallas_call(
        _inproj_kernel,
        out_shape=out_shape,
        grid=(M // tm,),
        in_specs=[pl.BlockSpec((tm, D_MODEL), lambda i: (i, 0)),
                  pl.BlockSpec((None, 1, 3 * D_MODEL), lambda i: (mod_index(i), 0, 0)),
                  pl.BlockSpec((D_MODEL, D_IN), lambda i: (0, 0), pipeline_mode=pl.Buffered(1))],
        out_specs=out_specs,
        compiler_params=pltpu.CompilerParams(
            dimension_semantics=("arbitrary",),
            vmem_limit_bytes=_vmem_limit(blocks, 4 * _nbytes((tm, D_MODEL), F32))),
        name="inproj",
    )(x2d, mod3, w_in_bf16)


def _hyena_kernel(vh_ref, x1_ref, x0_ref, gh_ref, cwv_ref, cw1_ref, cw0_ref, cbv_ref, cb1_ref, cb0_ref,
                  f_ref, g_ref, fp_ref, fqa_ref, fpb_ref, o_ref):
    L = fp_ref.shape[0]
    slab = jax.lax.broadcasted_iota(jnp.int32, (SUBLANES, 1), 0)

    def short_conv(x_ref, rows, w_ref, b_ref):
        x = x_ref[rows, :].astype(F32)
        prev = pltpu.roll(x, 1, 0)
        nxt = pltpu.roll(x, L - 1, 0)
        prev = jnp.concatenate([jnp.where(slab == 0, 0.0, prev[:SUBLANES]), prev[SUBLANES:]], axis=0)
        nxt = jnp.concatenate([nxt[:-SUBLANES], jnp.where(slab == SUBLANES - 1, 0.0, nxt[-SUBLANES:])], axis=0)
        return prev * w_ref[0:1, :] + x * w_ref[1:2, :] + nxt * w_ref[2:3, :] + b_ref[...]

    for s in range(vh_ref.shape[0] // L):
        rows = slice(s * L, (s + 1) * L)
        u = short_conv(vh_ref, rows, cwv_ref, cbv_ref) * short_conv(x1_ref, rows, cw1_ref, cb1_ref)
        t = _dot(f_ref[...], u.astype(BF16))
        p = t[:L]
        q = t[L:]
        yp = p * fp_ref[...] - q * fqa_ref[...]
        yq = p * fqa_ref[...] + q * fpb_ref[...]
        y = _dot(g_ref[...], jnp.concatenate([yp.astype(BF16), yq.astype(BF16)], axis=0))
        y_h = y * short_conv(x0_ref, rows, cw0_ref, cb0_ref)
        o_ref[rows, :] = (y_h * gh_ref[rows, :].astype(F32)).astype(BF16)


def _hyena_branch(z, B, L, conv_w, conv_b, dft, spectra, *, tc=256, seqs_per_step):
    nblk = D_HYENA // tc
    tm = seqs_per_step * L
    zcol = lambda piece: pl.BlockSpec((tm, tc), lambda c, b: (b, piece * nblk + c))
    wcol = lambda rows, piece: pl.BlockSpec((rows, tc), lambda c, b: (0, piece * nblk + c))
    chan = lambda rows: pl.BlockSpec((rows, tc), lambda c, b: (0, c))
    const = lambda shape: pl.BlockSpec(shape, lambda c, b: (0, 0), pipeline_mode=pl.Buffered(1))
    blocks = (2 * _nbytes((2 * L, L), BF16) + 2 * 5 * _nbytes((tm, tc), BF16) + 2 * 3 * _nbytes((L, tc), F32))
    return pl.pallas_call(
        _hyena_kernel,
        out_shape=jax.ShapeDtypeStruct((B * L, D_HYENA), BF16),
        grid=(nblk, B // seqs_per_step),
        in_specs=[zcol(COL_VH), zcol(COL_X1), zcol(COL_X0), zcol(COL_GH),
                  wcol(3, 0), wcol(3, 1), wcol(3, 2), wcol(1, 0), wcol(1, 1), wcol(1, 2),
                  const((2 * L, L)), const((L, 2 * L)), chan(L), chan(L), chan(L)],
        out_specs=pl.BlockSpec((tm, tc), lambda c, b: (b, c)),
        compiler_params=pltpu.CompilerParams(
            dimension_semantics=("arbitrary", "arbitrary"),
            vmem_limit_bytes=_vmem_limit(blocks, 16 * _nbytes((tm, tc), F32))),
        name=f"hyena_{L}",
    )(z, z, z, z, conv_w, conv_w, conv_w, conv_b.reshape(1, -1), conv_b.reshape(1, -1), conv_b.reshape(1, -1),
      *dft, *spectra)


def _head_masks():
    lane = jax.lax.broadcasted_iota(jnp.int32, (1, LANES), 1)
    return [(lane >= h * HEAD_DIM) & (lane < (h + 1) * HEAD_DIM) for h in range(HEADS_PER_TILE)]


def _qk(q, k):
    return jax.lax.dot_general(q, k, (((1,), (1,)), ((), ())), preferred_element_type=F32)


def _ctx_attn_kernel(q_ref, k_ref, v_ref, g_ref, o_ref):
    masks = _head_masks()
    for t in range(N_HEAD_TILES):
        cols = slice(t * LANES, (t + 1) * LANES)
        q = q_ref[:, cols]
        k = k_ref[:, cols]
        v = v_ref[:, cols]
        out = jnp.zeros(q.shape, F32)
        for msk in masks:
            s = _qk(jnp.where(msk, q, jnp.zeros_like(q)), k)
            m = jnp.max(s, axis=-1, keepdims=True)
            p = jnp.exp2(s - m)
            l = jnp.sum(p, axis=-1, keepdims=True)
            o = _dot(p.astype(BF16), jnp.where(msk, v, jnp.zeros_like(v)))
            out = out + o / l
        o_ref[:, cols] = (out * g_ref[:, cols].astype(F32)).astype(BF16)


def _context_attention(z, B, L):
    zcol = lambda piece: pl.BlockSpec((L, D_ATTN), lambda b: (b, piece))
    return pl.pallas_call(
        _ctx_attn_kernel,
        out_shape=jax.ShapeDtypeStruct((B * L, D_ATTN), BF16),
        grid=(B,),
        in_specs=[zcol(COL_Q), zcol(COL_K), zcol(COL_V), zcol(COL_GA)],
        out_specs=pl.BlockSpec((L, D_ATTN), lambda b: (b, 0)),
        compiler_params=pltpu.CompilerParams(dimension_semantics=("arbitrary",)),
        name="ctx_attn",
    )(z, z, z, z)


def _key_window_start(qb):
    rows = 1024 // GRID_W
    r_first = qb * Q_BLOCK_ROWS
    r_last = r_first + Q_BLOCK_ROWS - 1
    lo = min(max(r_first - WIN_ROWS // 2, 0), rows - WIN_ROWS)
    hi = min(max(r_last - WIN_ROWS // 2, 0), rows - WIN_ROWS) + WIN_ROWS
    start = min(lo, rows - KEY_WIN_ROWS)
    start -= start % Q_BLOCK_ROWS
    assert start <= lo and hi <= start + KEY_WIN_ROWS
    return start


def _nbr_attn_kernel(q_ref, k0_ref, k1_ref, k2_ref, v0_ref, v1_ref, v2_ref, kc_ref, vc_ref, g_ref, tiles_ref,
                     o_ref, bias_ref, *, n_rows):
    masks = _head_masks()
    feat = jax.lax.broadcasted_iota(jnp.int32, (LANES, 1), 0)
    k_refs = (k0_ref, k1_ref, k2_ref)
    v_refs = (v0_ref, v1_ref, v2_ref)
    key_blocks = KEY_WIN // Q_BLOCK

    def window_row_start(r):
        return min(max(r - WIN_ROWS // 2, 0), n_rows - WIN_ROWS)

    def build_bias(qb):
        k_start = _key_window_start(qb)
        used = set()
        for i in range(Q_BLOCK_ROWS):
            r = qb * Q_BLOCK_ROWS + i
            r0 = window_row_start(r)
            for j in range(KEY_WIN_ROWS):
                kr = k_start + j
                rows = slice(i * GRID_W, (i + 1) * GRID_W)
                cols = slice(j * GRID_W, (j + 1) * GRID_W)
                lanes = slice((j % HEADS_PER_TILE) * GRID_W, (j % HEADS_PER_TILE + 1) * GRID_W)
                inside = r0 <= kr < r0 + WIN_ROWS
                if inside:
                    used.add(j // Q_BLOCK_ROWS)
                for h in range(HEADS_PER_TILE):
                    if inside:
                        bias_ref[h, rows, cols] = tiles_ref[h, kr - r + WIN_ROWS - 1, :, lanes]
                    else:
                        bias_ref[h, rows, cols] = jnp.full((GRID_W, GRID_W), NEG_INF, F32)
        return sorted(used)

    def attend(blocks):
        for b in range(q_ref.shape[0]):
            q = q_ref[b]
            kc_t = kc_ref[b].astype(BF16)
            vc_t = vc_ref[b].astype(BF16)
            out = jnp.zeros(q.shape, F32)
            for h, msk in enumerate(masks):
                qh = jnp.where(msk, q, jnp.zeros_like(q))
                s = [_qk(qh, k_refs[j][b]) + bias_ref[h, :, j * Q_BLOCK:(j + 1) * Q_BLOCK] for j in blocks]
                s.append(_dot(qh, kc_t))
                m = jnp.max(functools.reduce(jnp.maximum, s), axis=-1, keepdims=True)
                p = [jnp.exp2(x - m) for x in s]
                l = jnp.sum(functools.reduce(jnp.add, p), axis=-1, keepdims=True)
                p = [x.astype(BF16) for x in p]
                zeros = jnp.zeros((), BF16)
                v_win = jnp.concatenate([v_refs[j][b] for j in blocks], axis=0)
                o = _dot(jnp.concatenate(p[:-1], axis=1), jnp.where(msk, v_win, zeros))
                head_feat = (feat >= h * HEAD_DIM) & (feat < (h + 1) * HEAD_DIM)
                o = o + _qk(p[-1], jnp.where(head_feat, vc_t, zeros))
                out = out + o / l
            o_ref[b] = (out * g_ref[b].astype(F32)).astype(BF16)

    for qb in range(n_rows // Q_BLOCK_ROWS):
        @pl.when(pl.program_id(1) == qb)
        def _(qb=qb):
            blocks = build_bias(qb)
            assert blocks and all(0 <= j < key_blocks for j in blocks)
            attend(blocks)


def _neighbourhood_attention(z, B, L, cache_k, cache_v, tiles):
    n_rows = L // GRID_W
    n_qb = L // Q_BLOCK
    assert KEY_WIN == 3 * Q_BLOCK and all(_key_window_start(qb) % Q_BLOCK_ROWS == 0 for qb in range(n_qb))
    z3 = z.reshape(B, L, D_IN)
    n_ctx = cache_k.shape[1]
    kc = jnp.transpose(cache_k, (0, 2, 3, 1)).reshape(B, D_ATTN, n_ctx)
    vc = jnp.transpose(cache_v, (0, 2, 3, 1)).reshape(B, D_ATTN, n_ctx)
    tiles_per_col = D_ATTN // LANES

    def win_block(qb):
        return jnp.where(qb < n_qb // 2, 0, (n_rows - KEY_WIN_ROWS) // Q_BLOCK_ROWS)

    assert [_key_window_start(qb) // Q_BLOCK_ROWS for qb in range(n_qb)] == \
        [0 if qb < n_qb // 2 else (n_rows - KEY_WIN_ROWS) // Q_BLOCK_ROWS for qb in range(n_qb)]
    qspec = lambda piece: pl.BlockSpec((B, Q_BLOCK, LANES), lambda t, qb: (0, qb, piece * tiles_per_col + t))
    kspec = lambda piece, j: pl.BlockSpec((B, Q_BLOCK, LANES),
                                          lambda t, qb: (0, win_block(qb) + j, piece * tiles_per_col + t))
    cspec = pl.BlockSpec((B, LANES, n_ctx), lambda t, qb: (0, t, 0))
    out = pl.pallas_call(
        functools.partial(_nbr_attn_kernel, n_rows=n_rows),
        out_shape=jax.ShapeDtypeStruct((B, L, D_ATTN), BF16),
        grid=(N_HEAD_TILES, n_qb),
        in_specs=[qspec(COL_Q), kspec(COL_K, 0), kspec(COL_K, 1), kspec(COL_K, 2),
                  kspec(COL_V, 0), kspec(COL_V, 1), kspec(COL_V, 2), cspec, cspec, qspec(COL_GA),
                  pl.BlockSpec((HEADS_PER_TILE, 2 * WIN_ROWS - 1, GRID_W, LANES), lambda t, qb: (t, 0, 0, 0))],
        out_specs=pl.BlockSpec((B, Q_BLOCK, LANES), lambda t, qb: (0, qb, t)),
        scratch_shapes=[pltpu.VMEM((HEADS_PER_TILE, Q_BLOCK, KEY_WIN), F32)],
        compiler_params=pltpu.CompilerParams(dimension_semantics=("arbitrary", "arbitrary")),
        name="nbr_attn",
    )(z3, z3, z3, z3, z3, z3, z3, kc, vc, z3, tiles)
    return out.reshape(B * L, D_ATTN)


def _merge_kernel(yh_ref, ya_ref, mh_ref, ma_ref, x_ref, mod_ref, wbh32_ref, wba32_ref, wout32_ref, lng_ref, lnb_ref,
                  o_ref, wbh_ref, wba_ref, wout_ref, *, alpha):
    @pl.when(pl.program_id(0) == 0)
    def _():
        wbh_ref[...] = wbh32_ref[...].astype(BF16)
        wba_ref[...] = wba32_ref[...].astype(BF16)
        wout_ref[...] = wout32_ref[...].astype(BF16)

    gate = mod_ref[:, 2 * D_MODEL:3 * D_MODEL]
    tm = x_ref.shape[0]
    for r in range(tm // MERGE_CHUNK):
        rows = slice(r * MERGE_CHUNK, (r + 1) * MERGE_CHUNK)
        p_h = _dot(yh_ref[rows, :], wbh_ref[...])
        p_a = _dot(ya_ref[rows, :], wba_ref[...])
        m_h = mh_ref[rows, :].astype(F32)
        m_a = ma_ref[rows, :].astype(F32)
        out = _dot((m_h * p_h + m_a * p_a).astype(BF16), wout_ref[...])
        v = alpha * x_ref[rows, :] + gate * out
        mu = jnp.mean(v, axis=-1, keepdims=True)
        d = v - mu
        var = jnp.mean(d * d, axis=-1, keepdims=True)
        o_ref[rows, :] = d * jax.lax.rsqrt(var + LN_EPS) * lng_ref[...] + lnb_ref[...]


def _merge(yh, ya, z, x2d, mod3, mod_index, w_bh, w_ba, w_out, ln_g, ln_b, *, alpha, tm):
    M = x2d.shape[0]
    tok = lambda width, col: pl.BlockSpec((tm, width), lambda i: (i, col))
    const = lambda shape: pl.BlockSpec(shape, lambda i: (0, 0))
    weight = pl.BlockSpec((D_MODEL, D_MODEL), lambda i: (0, 0), pipeline_mode=pl.Buffered(1))
    return pl.pallas_call(
        functools.partial(_merge_kernel, alpha=alpha),
        out_shape=jax.ShapeDtypeStruct((M, D_MODEL), F32),
        grid=(M // tm,),
        in_specs=[tok(D_HYENA, 0), tok(D_ATTN, 0), tok(D_MODEL, COL_MH), tok(D_MODEL, COL_MA), tok(D_MODEL, 0),
                  pl.BlockSpec((None, 1, 3 * D_MODEL), lambda i: (mod_index(i), 0, 0)),
                  weight, weight, weight, const((1, D_MODEL)), const((1, D_MODEL))],
        out_specs=pl.BlockSpec((tm, D_MODEL), lambda i: (i, 0)),
        scratch_shapes=[pltpu.VMEM((D_MODEL, D_MODEL), BF16)] * 3,
        compiler_params=pltpu.CompilerParams(
            dimension_semantics=("arbitrary",),
            vmem_limit_bytes=_vmem_limit(
                2 * (4 * _nbytes((tm, D_MODEL), BF16) + 2 * _nbytes((tm, D_MODEL), F32))
                + 3 * (_nbytes((D_MODEL, D_MODEL), F32) + _nbytes((D_MODEL, D_MODEL), BF16)),
                8 * _nbytes((MERGE_CHUNK, D_MODEL), F32))),
        name="merge",
    )(yh, ya, z, z, x2d, mod3, w_bh, w_ba, w_out, ln_g.reshape(1, -1), ln_b.reshape(1, -1))


def _layer(x, mod3, cond_row, ctx_kv, p, filt, tiles, *, alpha):
    B, L, _ = x.shape
    x2d = x.reshape(B * L, D_MODEL)
    emit_kv = ctx_kv is None
    assert not emit_kv or IN_TM == L
    outs = _in_projection(x2d, mod3, lambda i: cond_row(i * IN_TM // L), p["w_in"], emit_kv=emit_kv, tm=IN_TM)
    z = outs[0]
    dft = _dft_matrices(L)
    spectra = _filter_spectra(L, dft[0], p["hyena_d"], *filt)
    yh = _hyena_branch(z, B, L, p["conv_w"], p["conv_b"], dft, spectra,
                       seqs_per_step=max(1, HYENA_ROWS_PER_STEP // L))
    if ctx_kv is None:
        ya = _context_attention(z, B, L)
    else:
        ya = _neighbourhood_attention(z, B, L, ctx_kv[0], ctx_kv[1], tiles)
    y = _merge(yh, ya, z, x2d, mod3, lambda i: cond_row(i * MERGE_TM // L), p["w_bh"], p["w_ba"], p["w_out"],
               p["ln_g"], p["ln_b"], alpha=alpha, tm=MERGE_TM)
    return y.reshape(B, L, D_MODEL), outs[1:]


def kernel(x_prompt, x_sample, c, cache_k, cache_v, c_ctx, w_ada, b_ada, w_in, conv_w, conv_b, filt_w1, filt_b1,
           filt_w2, filt_b2, filt_w3, filt_freq, hyena_d, rpb, w_bh, w_ba, w_out, ln_g, ln_b):
    depth = w_in.shape[0]
    alpha = (2.0 * depth) ** 0.25
    n_lat, lat_len = x_sample.shape[0], x_sample.shape[1]
    n_ctx, ctx_len = x_prompt.shape[0], x_prompt.shape[1]
    ctx_row = n_lat
    cond = jnp.zeros((8, D_MODEL), F32).at[:n_lat].set(c).at[ctx_row].set(c_ctx)

    xp, xs = x_prompt, x_sample
    new_k, new_v = [], []
    for l in range(depth):
        p = {"w_in": w_in[l].astype(BF16), "conv_w": conv_w[l], "conv_b": conv_b[l], "hyena_d": hyena_d[l],
             "w_bh": w_bh[l], "w_ba": w_ba[l], "w_out": w_out[l],
             "ln_g": ln_g[l], "ln_b": ln_b[l]}
        mod3 = _modulation(cond, w_ada[l], b_ada[l]).reshape(8, 1, 3 * D_MODEL)
        filt = (filt_w1[l], filt_b1[l], filt_w2[l], filt_b2[l], filt_w3[l], filt_freq[l])
        tiles = _rpb_tiles(rpb[l])
        xp, (k_ctx, v_ctx) = _layer(xp, mod3, lambda b: ctx_row, None, p, filt, None, alpha=alpha)
        new_k.append(jnp.transpose(k_ctx.reshape(n_ctx, N_HEADS, HEAD_DIM, ctx_len), (0, 3, 1, 2)))
        new_v.append(jnp.transpose(v_ctx.reshape(n_ctx, N_HEADS, HEAD_DIM, ctx_len), (0, 3, 1, 2)))
        xs, _ = _layer(xs, mod3, lambda b: b, (cache_k[:, l], cache_v[:, l]), p, filt, tiles, alpha=alpha)
    return xp, xs, jnp.stack(new_k, axis=1), jnp.stack(new_v, axis=1)
```

```python
import functools
import math

import jax
import jax.numpy as jnp
import numpy as np
from jax.experimental import pallas as pl
from jax.experimental.pallas import tpu as pltpu

F32 = jnp.float32
BF16 = jnp.bfloat16

D_MODEL = 1024
D_HYENA = 1024
N_HEADS = 16
HEAD_DIM = 64
D_ATTN = N_HEADS * HEAD_DIM
D_IN = 4 * D_HYENA + 4 * D_ATTN + 2 * D_MODEL
GRID_W = 64
WIN_ROWS = 8
WIN_COLS = 16
FILTER_EMB = 33
FILTER_BANDS = (FILTER_EMB - 1) // 2
FILTER_HIDDEN = 64
DECAY_TARGET = 1e-2
MIN_DECAY = math.log(DECAY_TARGET) / 1.5
MAX_DECAY = math.log(DECAY_TARGET) / 0.3
DECAY_SHIFT = 0.05
LN_EPS = 1e-5
NEG_INF = -1e30

COL_VH, COL_X1, COL_X0, COL_GH, COL_Q, COL_K, COL_V, COL_GA, COL_MH, COL_MA = range(10)

LANES = 128
SUBLANES = 8
HEADS_PER_TILE = LANES // HEAD_DIM
N_HEAD_TILES = N_HEADS // HEADS_PER_TILE
V7X_VMEM_BYTES = 64 * 1024 * 1024
ATTN_SCALE = HEAD_DIM ** -0.5
LOG2E = math.log2(math.e)

Q_BLOCK = 256
Q_BLOCK_ROWS = Q_BLOCK // GRID_W
KEY_WIN_ROWS = 12
KEY_WIN = KEY_WIN_ROWS * GRID_W

IN_TM = 256
MERGE_TM = 512
HYENA_ROWS_PER_STEP = 4096


def _vmem_limit(block_bytes, temp_bytes):
    need = int(block_bytes + temp_bytes)
    return min(max(need, 16 * 1024 * 1024), V7X_VMEM_BYTES - 8 * 1024 * 1024)


def _nbytes(shape, dtype):
    return int(np.prod(shape)) * jnp.dtype(dtype).itemsize


def _silu(x):
    return x * jax.nn.sigmoid(x)


def _split_bf16(a):
    hi = a.astype(BF16)
    lo = (a - hi.astype(F32)).astype(BF16)
    return hi, lo


def _dot(a, b):
    return jnp.dot(a, b, preferred_element_type=F32)


def _dot_split(a_hi, a_lo, b):
    b_hi, b_lo = _split_bf16(b)
    return _dot(a_hi, b_hi) + (_dot(a_hi, b_lo) + _dot(a_lo, b_hi))


@functools.lru_cache(maxsize=None)
def _dft_constants(L):
    n = 2 * L
    k = np.arange(L, dtype=np.float64)[:, None]
    s = np.arange(L, dtype=np.float64)[None, :]
    ang = 2.0 * np.pi * k * s / n
    nyq = np.cos(np.pi * np.arange(L, dtype=np.float64))
    C = np.cos(ang)
    S = np.sin(ang)
    S[0, :] = nyq
    F = np.concatenate([C, S], axis=0)
    w = np.full((L,), 2.0)
    w[0] = 1.0
    Gc = (C * w[:, None]).T / n
    Ss = 2.0 * np.sin(ang)
    Ss[0, :] = nyq
    Gs = Ss.T / n
    G = np.concatenate([Gc, Gs], axis=1)
    return F.astype(np.float32), G.astype(np.float32)


def _dft_matrices(L):
    F, G = _dft_constants(L)
    return jnp.asarray(F).astype(BF16), jnp.asarray(G).astype(BF16)


@functools.lru_cache(maxsize=None)
def _filter_constants(L):
    t = np.linspace(0.0, 1.0, L, dtype=np.float32)[:, None]
    bands = np.linspace(1e-4, FILTER_BANDS - 1, FILTER_BANDS, dtype=np.float32)[None]
    w = (2.0 * math.pi * np.arange(L, dtype=np.float32)[:, None] / L).astype(np.float32)
    z = np.concatenate([t, np.cos(bands * w), -np.sin(bands * w)], axis=-1).astype(np.float32)
    z_pad = np.zeros((L, LANES), np.float32)
    z_pad[:, :FILTER_EMB] = z
    deltas = np.linspace(MIN_DECAY, MAX_DECAY, D_HYENA, dtype=np.float32)
    decay = (np.exp(-t * np.abs(deltas)) + np.float32(DECAY_SHIFT)).astype(np.float32)
    return z_pad, decay


@functools.lru_cache(maxsize=None)
def _rpb_constants():
    c = np.arange(GRID_W)[:, None]
    kc = np.arange(GRID_W)[None, :]
    c0 = np.clip(c - WIN_COLS // 2, 0, GRID_W - WIN_COLS)
    in_win = (kc >= c0) & (kc < c0 + WIN_COLS)
    idx = np.clip(kc - c + WIN_COLS - 1, 0, 2 * WIN_COLS - 2)
    onehot = np.zeros((LANES, GRID_W, LANES), np.float32)
    for half in range(HEADS_PER_TILE):
        lanes = half * GRID_W + np.arange(GRID_W)
        onehot[idx, c, lanes[None, :]] = 1.0
    mask = np.concatenate([in_win, in_win], axis=1).astype(np.float32)
    return onehot.reshape(LANES, GRID_W * LANES).astype(BF16), mask.reshape(1, GRID_W * LANES)


def _mod_kernel(cond_ref, w_ref, b_ref, o_ref):
    s = _silu(cond_ref[...]).astype(BF16)
    o_ref[...] = _dot(s, w_ref[...].astype(BF16)) + b_ref[...]


def _modulation(cond, w_ada, b_ada):
    rows = cond.shape[0]
    return pl.pallas_call(
        _mod_kernel,
        out_shape=jax.ShapeDtypeStruct((rows, 3 * D_MODEL), F32),
        grid=(3,),
        in_specs=[pl.BlockSpec((rows, D_MODEL), lambda j: (0, 0)),
                  pl.BlockSpec((D_MODEL, D_MODEL), lambda j: (0, j)),
                  pl.BlockSpec((1, D_MODEL), lambda j: (0, j))],
        out_specs=pl.BlockSpec((rows, D_MODEL), lambda j: (0, j)),
        name="mod",
    )(cond, w_ada, b_ada.reshape(1, -1))


def _spectra_kernel(z_ref, w1_ref, b1_ref, w2_ref, b2_ref, w3f_ref, w3b_ref, freq_ref, decay_ref, d_ref, f_ref,
                    fp_ref, fqa_ref, fpb_ref, hf_ref):
    L = decay_ref.shape[0]

    @pl.when(pl.program_id(0) == 0)
    def _():
        z_hi, z_lo = _split_bf16(z_ref[...])
        hf = jnp.sin(freq_ref[0:1, :] * (_dot_split(z_hi, z_lo, w1_ref[...]) + b1_ref[...]))
        h_hi, h_lo = _split_bf16(hf)
        hf_ref[...] = jnp.sin(freq_ref[1:2, :] * (_dot_split(h_hi, h_lo, w2_ref[...]) + b2_ref[...]))

    h_hi, h_lo = _split_bf16(hf_ref[...])
    decay = decay_ref[...]
    h_fwd = _dot_split(h_hi, h_lo, w3f_ref[...]) * decay
    h_bwd = _dot_split(h_hi, h_lo, w3b_ref[...]) * decay
    row0 = jax.lax.broadcasted_iota(jnp.int32, (L, 1), 0) == 0
    h_bwd = jnp.where(row0, 0.0, h_bwd)
    fsum = (h_fwd + h_bwd).astype(BF16)
    fdif = (h_fwd - h_bwd).astype(BF16)
    skip = d_ref[...]
    fp = _dot(f_ref[0:L, :], fsum) + skip
    fq = _dot(f_ref[L:2 * L, :], fdif)
    nyq = _dot(f_ref[L:L + 2 * SUBLANES, :], fsum)[0:1, :] + skip
    fp_ref[...] = fp
    fqa_ref[...] = jnp.where(row0, 0.0, fq)
    fpb_ref[...] = jnp.where(row0, nyq, fp)


def _filter_spectra(L, f_mat, hyena_d, w1, b1, w2, b2, w3, freq):
    tc = 256
    z_emb, decay = _filter_constants(L)
    w1p = jnp.zeros((LANES, FILTER_HIDDEN), F32).at[:FILTER_EMB].set(w1)
    nblk = D_HYENA // tc
    const = lambda shape: pl.BlockSpec(shape, lambda c: (0, 0), pipeline_mode=pl.Buffered(1))
    out_spec = pl.BlockSpec((L, tc), lambda c: (0, c))
    blocks = (_nbytes((2 * L, L), BF16) + _nbytes((L, LANES), F32) + 2 * 4 * _nbytes((L, tc), F32))
    return pl.pallas_call(
        _spectra_kernel,
        out_shape=[jax.ShapeDtypeStruct((L, D_HYENA), F32)] * 3,
        grid=(nblk,),
        in_specs=[const((L, LANES)),
                  const((LANES, FILTER_HIDDEN)), const((1, FILTER_HIDDEN)),
                  const((FILTER_HIDDEN, FILTER_HIDDEN)), const((1, FILTER_HIDDEN)),
                  pl.BlockSpec((FILTER_HIDDEN, tc), lambda c: (0, c)),
                  pl.BlockSpec((FILTER_HIDDEN, tc), lambda c: (0, nblk + c)),
                  const((2, FILTER_HIDDEN)),
                  pl.BlockSpec((L, tc), lambda c: (0, c)),
                  pl.BlockSpec((1, tc), lambda c: (0, c)),
                  const((2 * L, L))],
        out_specs=[out_spec] * 3,
        scratch_shapes=[pltpu.VMEM((L, FILTER_HIDDEN), F32)],
        compiler_params=pltpu.CompilerParams(
            dimension_semantics=("arbitrary",),
            vmem_limit_bytes=_vmem_limit(blocks, 12 * _nbytes((L, tc), F32))),
        name=f"spectra_{L}",
    )(z_emb, w1p, b1.reshape(1, -1), w2, b2.reshape(1, -1), w3, w3, freq, decay, hyena_d.reshape(1, -1), f_mat)


def _rpb_kernel(r_ref, onehot_ref, mask_ref, o_ref):
    r = r_ref[...]
    r_hi = r.astype(BF16)
    rem = r - r_hi.astype(F32)
    r_mid = rem.astype(BF16)
    r_lo = (rem - r_mid.astype(F32)).astype(BF16)
    oh = onehot_ref[...]
    t = _dot(r_hi, oh) + (_dot(r_mid, oh) + _dot(r_lo, oh))
    t = jnp.where(mask_ref[...] > 0.0, t * LOG2E, NEG_INF)
    for c in range(o_ref.shape[1]):
        o_ref[:, c, :] = t[:, c * LANES:(c + 1) * LANES]


def _rpb_tiles(rpb):
    n_dr = 2 * WIN_ROWS - 1
    rows = N_HEADS * n_dr
    onehot, mask = _rpb_constants()
    r = jnp.zeros((rows, LANES), F32).at[:, :2 * WIN_COLS - 1].set(rpb.reshape(rows, 2 * WIN_COLS - 1))
    cols_per_step = 16
    tn = cols_per_step * LANES
    out = pl.pallas_call(
        _rpb_kernel,
        out_shape=jax.ShapeDtypeStruct((rows, GRID_W, LANES), F32),
        grid=(GRID_W // cols_per_step,),
        in_specs=[pl.BlockSpec((rows, LANES), lambda j: (0, 0)),
                  pl.BlockSpec((LANES, tn), lambda j: (0, j)),
                  pl.BlockSpec((1, tn), lambda j: (0, j))],
        out_specs=pl.BlockSpec((rows, cols_per_step, LANES), lambda j: (0, j, 0)),
        name="rpb_tiles",
    )(r, onehot, mask)
    return out.reshape(N_HEADS, n_dr, GRID_W, LANES)


def _inproj_kernel(x_ref, mod_ref, w_ref, z_ref, *kv_refs):
    shift = mod_ref[:, 0:D_MODEL]
    scale = mod_ref[:, D_MODEL:2 * D_MODEL]
    h = (x_ref[...] * (1.0 + scale) + shift).astype(BF16)
    epilogue = {COL_GH: _silu, COL_GA: _silu, COL_MH: jax.nn.sigmoid, COL_MA: jax.nn.sigmoid,
                COL_Q: lambda acc: acc * (ATTN_SCALE * LOG2E)}
    for col in range(D_IN // D_MODEL):
        cols = slice(col * D_MODEL, (col + 1) * D_MODEL)
        acc = _dot(h, w_ref[:, cols])
        z_ref[:, cols] = epilogue.get(col, lambda acc: acc)(acc).astype(BF16)
        if kv_refs and col == COL_K:
            kv_refs[0][...] = acc.T
        if kv_refs and col == COL_V:
            kv_refs[1][...] = acc.T


def _in_projection(x2d, mod3, mod_index, w_in_bf16, *, emit_kv, tm):
    M = x2d.shape[0]
    out_shape = [jax.ShapeDtypeStruct((M, D_IN), BF16)]
    out_specs = [pl.BlockSpec((tm, D_IN), lambda i: (i, 0))]
    if emit_kv:
        out_shape += [jax.ShapeDtypeStruct((M // tm, D_ATTN, tm), F32)] * 2
        out_specs += [pl.BlockSpec((None, D_ATTN, tm), lambda i: (i, 0, 0))] * 2
    blocks = (_nbytes((D_MODEL, D_IN), BF16) + 2 * _nbytes((tm, D_MODEL), F32) + 2 * _nbytes((tm, D_IN), BF16)
              + (4 * _nbytes((tm, D_ATTN), F32) if emit_kv else 0))
    return pl.pallas_call(
        _inproj_kernel,
        out_shape=out_shape,
        grid=(M // tm,),
        in_specs=[pl.BlockSpec((tm, D_MODEL), lambda i: (i, 0)),
                  pl.BlockSpec((None, 1, 3 * D_MODEL), lambda i: (mod_index(i), 0, 0)),
                  pl.BlockSpec((D_MODEL, D_IN), lambda i: (0, 0), pipeline_mode=pl.Buffered(1))],
        out_specs=out_specs,
        compiler_params=pltpu.CompilerParams(
            dimension_semantics=("arbitrary",),
            vmem_limit_bytes=_vmem_limit(blocks, 4 * _nbytes((tm, D_MODEL), F32))),
        name="inproj",
    )(x2d, mod3, w_in_bf16)


def _hyena_kernel(vh_ref, x1_ref, x0_ref, gh_ref, cwv_ref, cw1_ref, cw0_ref, cbv_ref, cb1_ref, cb0_ref,
                  f_ref, g_ref, fp_ref, fqa_ref, fpb_ref, o_ref):
    L = fp_ref.shape[0]
    slab = jax.lax.broadcasted_iota(jnp.int32, (SUBLANES, 1), 0)

    def short_conv(x_ref, rows, w_ref, b_ref):
        x = x_ref[rows, :].astype(F32)
        prev = pltpu.roll(x, 1, 0)
        nxt = pltpu.roll(x, L - 1, 0)
        prev = jnp.concatenate([jnp.where(slab == 0, 0.0, prev[:SUBLANES]), prev[SUBLANES:]], axis=0)
        nxt = jnp.concatenate([nxt[:-SUBLANES], jnp.where(slab == SUBLANES - 1, 0.0, nxt[-SUBLANES:])], axis=0)
        return prev * w_ref[0:1, :] + x * w_ref[1:2, :] + nxt * w_ref[2:3, :] + b_ref[...]

    for s in range(vh_ref.shape[0] // L):
        rows = slice(s * L, (s + 1) * L)
        u = short_conv(vh_ref, rows, cwv_ref, cbv_ref) * short_conv(x1_ref, rows, cw1_ref, cb1_ref)
        t = _dot(f_ref[...], u.astype(BF16))
        p = t[:L]
        q = t[L:]
        yp = p * fp_ref[...] - q * fqa_ref[...]
        yq = p * fqa_ref[...] + q * fpb_ref[...]
        y = _dot(g_ref[...], jnp.concatenate([yp.astype(BF16), yq.astype(BF16)], axis=0))
        y_h = y * short_conv(x0_ref, rows, cw0_ref, cb0_ref)
        o_ref[rows, :] = (y_h * gh_ref[rows, :].astype(F32)).astype(BF16)


def _hyena_branch(z, B, L, conv_w, conv_b, dft, spectra, *, tc=256, seqs_per_step):
    nblk = D_HYENA // tc
    tm = seqs_per_step * L
    zcol = lambda piece: pl.BlockSpec((tm, tc), lambda c, b: (b, piece * nblk + c))
    wcol = lambda rows, piece: pl.BlockSpec((rows, tc), lambda c, b: (0, piece * nblk + c))
    chan = lambda rows: pl.BlockSpec((rows, tc), lambda c, b: (0, c))
    const = lambda shape: pl.BlockSpec(shape, lambda c, b: (0, 0), pipeline_mode=pl.Buffered(1))
    blocks = (2 * _nbytes((2 * L, L), BF16) + 2 * 5 * _nbytes((tm, tc), BF16) + 2 * 3 * _nbytes((L, tc), F32))
    return pl.pallas_call(
        _hyena_kernel,
        out_shape=jax.ShapeDtypeStruct((B * L, D_HYENA), BF16),
        grid=(nblk, B // seqs_per_step),
        in_specs=[zcol(COL_VH), zcol(COL_X1), zcol(COL_X0), zcol(COL_GH),
                  wcol(3, 0), wcol(3, 1), wcol(3, 2), wcol(1, 0), wcol(1, 1), wcol(1, 2),
                  const((2 * L, L)), const((L, 2 * L)), chan(L), chan(L), chan(L)],
        out_specs=pl.BlockSpec((tm, tc), lambda c, b: (b, c)),
        compiler_params=pltpu.CompilerParams(
            dimension_semantics=("arbitrary", "arbitrary"),
            vmem_limit_bytes=_vmem_limit(blocks, 16 * _nbytes((tm, tc), F32))),
        name=f"hyena_{L}",
    )(z, z, z, z, conv_w, conv_w, conv_w, conv_b.reshape(1, -1), conv_b.reshape(1, -1), conv_b.reshape(1, -1),
      *dft, *spectra)


def _head_masks():
    lane = jax.lax.broadcasted_iota(jnp.int32, (1, LANES), 1)
    return [(lane >= h * HEAD_DIM) & (lane < (h + 1) * HEAD_DIM) for h in range(HEADS_PER_TILE)]


def _qk(q, k):
    return jax.lax.dot_general(q, k, (((1,), (1,)), ((), ())), preferred_element_type=F32)


def _stack_heads(q, masks):
    return jnp.concatenate([jnp.where(msk, q, jnp.zeros_like(q)) for msk in masks], axis=0)


def _unstack_heads(o, masks):
    n = o.shape[0] // len(masks)
    out = o[:n]
    for h in range(1, len(masks)):
        out = jnp.where(masks[h], o[h * n:(h + 1) * n], out)
    return out


def _ctx_attn_kernel(q_ref, k_ref, v_ref, g_ref, o_ref):
    masks = _head_masks()
    for t in range(N_HEAD_TILES):
        cols = slice(t * LANES, (t + 1) * LANES)
        s = _qk(_stack_heads(q_ref[:, cols], masks), k_ref[:, cols])
        m = jnp.max(s, axis=-1, keepdims=True)
        p = jnp.exp2(s - m)
        l = jnp.sum(p, axis=-1, keepdims=True)
        out = _unstack_heads(_dot(p.astype(BF16), v_ref[:, cols]) / l, masks)
        o_ref[:, cols] = (out * g_ref[:, cols].astype(F32)).astype(BF16)


def _context_attention(z, B, L):
    zcol = lambda piece: pl.BlockSpec((L, D_ATTN), lambda b: (b, piece))
    return pl.pallas_call(
        _ctx_attn_kernel,
        out_shape=jax.ShapeDtypeStruct((B * L, D_ATTN), BF16),
        grid=(B,),
        in_specs=[zcol(COL_Q), zcol(COL_K), zcol(COL_V), zcol(COL_GA)],
        out_specs=pl.BlockSpec((L, D_ATTN), lambda b: (b, 0)),
        compiler_params=pltpu.CompilerParams(dimension_semantics=("arbitrary",)),
        name="ctx_attn",
    )(z, z, z, z)


def _key_window_start(qb):
    rows = 1024 // GRID_W
    r_first = qb * Q_BLOCK_ROWS
    r_last = r_first + Q_BLOCK_ROWS - 1
    lo = min(max(r_first - WIN_ROWS // 2, 0), rows - WIN_ROWS)
    hi = min(max(r_last - WIN_ROWS // 2, 0), rows - WIN_ROWS) + WIN_ROWS
    start = min(lo, rows - KEY_WIN_ROWS)
    start -= start % Q_BLOCK_ROWS
    assert start <= lo and hi <= start + KEY_WIN_ROWS
    return start


def _nbr_attn_kernel(q_ref, k0_ref, k1_ref, k2_ref, v0_ref, v1_ref, v2_ref, kc_ref, vc_ref, g_ref, tiles_ref,
                     o_ref, bias_ref, *, n_rows):
    masks = _head_masks()
    k_refs = (k0_ref, k1_ref, k2_ref)
    v_refs = (v0_ref, v1_ref, v2_ref)
    key_blocks = KEY_WIN // Q_BLOCK

    def window_row_start(r):
        return min(max(r - WIN_ROWS // 2, 0), n_rows - WIN_ROWS)

    def build_bias(qb):
        k_start = _key_window_start(qb)
        used = set()
        for i in range(Q_BLOCK_ROWS):
            r = qb * Q_BLOCK_ROWS + i
            r0 = window_row_start(r)
            for j in range(KEY_WIN_ROWS):
                kr = k_start + j
                rows = slice(i * GRID_W, (i + 1) * GRID_W)
                cols = slice(j * GRID_W, (j + 1) * GRID_W)
                lanes = slice((j % HEADS_PER_TILE) * GRID_W, (j % HEADS_PER_TILE + 1) * GRID_W)
                inside = r0 <= kr < r0 + WIN_ROWS
                if inside:
                    used.add(j // Q_BLOCK_ROWS)
                for h in range(HEADS_PER_TILE):
                    if inside:
                        bias_ref[h, rows, cols] = tiles_ref[h, kr - r + WIN_ROWS - 1, :, lanes]
                    else:
                        bias_ref[h, rows, cols] = jnp.full((GRID_W, GRID_W), NEG_INF, F32)
        return sorted(used)

    def attend(blocks):
        bias = [jnp.concatenate([bias_ref[h, :, j * Q_BLOCK:(j + 1) * Q_BLOCK] for h in range(HEADS_PER_TILE)],
                                axis=0) for j in blocks]
        for b in range(q_ref.shape[0]):
            qs = _stack_heads(q_ref[b], masks)
            s = [_qk(qs, k_refs[j][b]) + bias_j for j, bias_j in zip(blocks, bias)]
            s.append(_dot(qs, kc_ref[b].astype(BF16)))
            m = jnp.max(functools.reduce(jnp.maximum, s), axis=-1, keepdims=True)
            p = [jnp.exp2(x - m) for x in s]
            l = jnp.sum(functools.reduce(jnp.add, p), axis=-1, keepdims=True)
            p = [x.astype(BF16) for x in p]
            v_win = jnp.concatenate([v_refs[j][b] for j in blocks], axis=0)
            o = _dot(jnp.concatenate(p[:-1], axis=1), v_win) + _qk(p[-1], vc_ref[b].astype(BF16))
            o_ref[b] = (_unstack_heads(o / l, masks) * g_ref[b].astype(F32)).astype(BF16)

    for qb in range(n_rows // Q_BLOCK_ROWS):
        @pl.when(pl.program_id(1) == qb)
        def _(qb=qb):
            blocks = build_bias(qb)
            assert blocks and all(0 <= j < key_blocks for j in blocks)
            attend(blocks)


def _neighbourhood_attention(z, B, L, cache_k, cache_v, tiles):
    n_rows = L // GRID_W
    n_qb = L // Q_BLOCK
    assert KEY_WIN == 3 * Q_BLOCK and all(_key_window_start(qb) % Q_BLOCK_ROWS == 0 for qb in range(n_qb))
    z3 = z.reshape(B, L, D_IN)
    n_ctx = cache_k.shape[1]
    kc = jnp.transpose(cache_k, (0, 2, 3, 1)).reshape(B, D_ATTN, n_ctx)
    vc = jnp.transpose(cache_v, (0, 2, 3, 1)).reshape(B, D_ATTN, n_ctx)
    tiles_per_col = D_ATTN // LANES

    def win_block(qb):
        return jnp.where(qb < n_qb // 2, 0, (n_rows - KEY_WIN_ROWS) // Q_BLOCK_ROWS)

    assert [_key_window_start(qb) // Q_BLOCK_ROWS for qb in range(n_qb)] == \
        [0 if qb < n_qb // 2 else (n_rows - KEY_WIN_ROWS) // Q_BLOCK_ROWS for qb in range(n_qb)]
    qspec = lambda piece: pl.BlockSpec((B, Q_BLOCK, LANES), lambda t, qb: (0, qb, piece * tiles_per_col + t))
    kspec = lambda piece, j: pl.BlockSpec((B, Q_BLOCK, LANES),
                                          lambda t, qb: (0, win_block(qb) + j, piece * tiles_per_col + t))
    cspec = pl.BlockSpec((B, LANES, n_ctx), lambda t, qb: (0, t, 0))
    out = pl.pallas_call(
        functools.partial(_nbr_attn_kernel, n_rows=n_rows),
        out_shape=jax.ShapeDtypeStruct((B, L, D_ATTN), BF16),
        grid=(N_HEAD_TILES, n_qb),
        in_specs=[qspec(COL_Q), kspec(COL_K, 0), kspec(COL_K, 1), kspec(COL_K, 2),
                  kspec(COL_V, 0), kspec(COL_V, 1), kspec(COL_V, 2), cspec, cspec, qspec(COL_GA),
                  pl.BlockSpec((HEADS_PER_TILE, 2 * WIN_ROWS - 1, GRID_W, LANES), lambda t, qb: (t, 0, 0, 0))],
        out_specs=pl.BlockSpec((B, Q_BLOCK, LANES), lambda t, qb: (0, qb, t)),
        scratch_shapes=[pltpu.VMEM((HEADS_PER_TILE, Q_BLOCK, KEY_WIN), F32)],
        compiler_params=pltpu.CompilerParams(dimension_semantics=("arbitrary", "arbitrary")),
        name="nbr_attn",
    )(z3, z3, z3, z3, z3, z3, z3, kc, vc, z3, tiles)
    return out.reshape(B * L, D_ATTN)


def _merge_kernel(yh_ref, ya_ref, mh_ref, ma_ref, x_ref, mod_ref, wbh_ref, wba_ref, wout_ref, lng_ref, lnb_ref,
                  o_ref, *, alpha):
    p_h = _dot(yh_ref[...], wbh_ref[...])
    p_a = _dot(ya_ref[...], wba_ref[...])
    m_h = mh_ref[...].astype(F32)
    m_a = ma_ref[...].astype(F32)
    out = _dot((m_h * p_h + m_a * p_a).astype(BF16), wout_ref[...])
    gate = mod_ref[:, 2 * D_MODEL:3 * D_MODEL]
    v = alpha * x_ref[...] + gate * out
    mu = jnp.mean(v, axis=-1, keepdims=True)
    d = v - mu
    var = jnp.mean(d * d, axis=-1, keepdims=True)
    o_ref[...] = d * jax.lax.rsqrt(var + LN_EPS) * lng_ref[...] + lnb_ref[...]


def _merge(yh, ya, z, x2d, mod3, mod_index, w_bh, w_ba, w_out, ln_g, ln_b, *, alpha, tm):
    M = x2d.shape[0]
    tok = lambda width, col: pl.BlockSpec((tm, width), lambda i: (i, col))
    const = lambda shape: pl.BlockSpec(shape, lambda i: (0, 0))
    return pl.pallas_call(
        functools.partial(_merge_kernel, alpha=alpha),
        out_shape=jax.ShapeDtypeStruct((M, D_MODEL), F32),
        grid=(M // tm,),
        in_specs=[tok(D_HYENA, 0), tok(D_ATTN, 0), tok(D_MODEL, COL_MH), tok(D_MODEL, COL_MA), tok(D_MODEL, 0),
                  pl.BlockSpec((None, 1, 3 * D_MODEL), lambda i: (mod_index(i), 0, 0)),
                  const((D_HYENA, D_MODEL)), const((D_ATTN, D_MODEL)), const((D_MODEL, D_MODEL)),
                  const((1, D_MODEL)), const((1, D_MODEL))],
        out_specs=pl.BlockSpec((tm, D_MODEL), lambda i: (i, 0)),
        compiler_params=pltpu.CompilerParams(
            dimension_semantics=("arbitrary",),
            vmem_limit_bytes=_vmem_limit(
                2 * (4 * _nbytes((tm, D_MODEL), BF16) + 2 * _nbytes((tm, D_MODEL), F32)
                     + 3 * _nbytes((D_MODEL, D_MODEL), BF16)),
                8 * _nbytes((tm, D_MODEL), F32))),
        name="merge",
    )(yh, ya, z, z, x2d, mod3, w_bh, w_ba, w_out, ln_g.reshape(1, -1), ln_b.reshape(1, -1))


def _layer(x, mod3, cond_row, ctx_kv, p, filt, tiles, *, alpha):
    B, L, _ = x.shape
    x2d = x.reshape(B * L, D_MODEL)
    emit_kv = ctx_kv is None
    assert not emit_kv or IN_TM == L
    outs = _in_projection(x2d, mod3, lambda i: cond_row(i * IN_TM // L), p["w_in"], emit_kv=emit_kv, tm=IN_TM)
    z = outs[0]
    dft = _dft_matrices(L)
    spectra = _filter_spectra(L, dft[0], p["hyena_d"], *filt)
    yh = _hyena_branch(z, B, L, p["conv_w"], p["conv_b"], dft, spectra,
                       seqs_per_step=max(1, HYENA_ROWS_PER_STEP // L))
    if ctx_kv is None:
        ya = _context_attention(z, B, L)
    else:
        ya = _neighbourhood_attention(z, B, L, ctx_kv[0], ctx_kv[1], tiles)
    y = _merge(yh, ya, z, x2d, mod3, lambda i: cond_row(i * MERGE_TM // L), p["w_bh"], p["w_ba"], p["w_out"],
               p["ln_g"], p["ln_b"], alpha=alpha, tm=MERGE_TM)
    return y.reshape(B, L, D_MODEL), outs[1:]


def kernel(x_prompt, x_sample, c, cache_k, cache_v, c_ctx, w_ada, b_ada, w_in, conv_w, conv_b, filt_w1, filt_b1,
           filt_w2, filt_b2, filt_w3, filt_freq, hyena_d, rpb, w_bh, w_ba, w_out, ln_g, ln_b):
    depth = w_in.shape[0]
    alpha = (2.0 * depth) ** 0.25
    n_lat, lat_len = x_sample.shape[0], x_sample.shape[1]
    n_ctx, ctx_len = x_prompt.shape[0], x_prompt.shape[1]
    ctx_row = n_lat
    cond = jnp.zeros((8, D_MODEL), F32).at[:n_lat].set(c).at[ctx_row].set(c_ctx)

    xp, xs = x_prompt, x_sample
    new_k, new_v = [], []
    for l in range(depth):
        p = {"w_in": w_in[l].astype(BF16), "conv_w": conv_w[l], "conv_b": conv_b[l], "hyena_d": hyena_d[l],
             "w_bh": w_bh[l].astype(BF16), "w_ba": w_ba[l].astype(BF16), "w_out": w_out[l].astype(BF16),
             "ln_g": ln_g[l], "ln_b": ln_b[l]}
        mod3 = _modulation(cond, w_ada[l], b_ada[l]).reshape(8, 1, 3 * D_MODEL)
        filt = (filt_w1[l], filt_b1[l], filt_w2[l], filt_b2[l], filt_w3[l], filt_freq[l])
        tiles = _rpb_tiles(rpb[l])
        xp, (k_ctx, v_ctx) = _layer(xp, mod3, lambda b: ctx_row, None, p, filt, None, alpha=alpha)
        new_k.append(jnp.transpose(k_ctx.reshape(n_ctx, N_HEADS, HEAD_DIM, ctx_len), (0, 3, 1, 2)))
        new_v.append(jnp.transpose(v_ctx.reshape(n_ctx, N_HEADS, HEAD_DIM, ctx_len), (0, 3, 1, 2)))
        xs, _ = _layer(xs, mod3, lambda b: b, (cache_k[:, l], cache_v[:, l]), p, filt, tiles, alpha=alpha)
    return xp, xs, jnp.stack(new_k, axis=1), jnp.stack(new_v, axis=1)
```

```python
import functools
import math

import jax
import jax.numpy as jnp
import numpy as np
from jax.experimental import pallas as pl
from jax.experimental.pallas import tpu as pltpu

F32 = jnp.float32
BF16 = jnp.bfloat16

D_MODEL = 1024
D_HYENA = 1024
N_HEADS = 16
HEAD_DIM = 64
D_ATTN = N_HEADS * HEAD_DIM
D_IN = 4 * D_HYENA + 4 * D_ATTN + 2 * D_MODEL
GRID_W = 64
WIN_ROWS = 8
WIN_COLS = 16
FILTER_EMB = 33
FILTER_BANDS = (FILTER_EMB - 1) // 2
FILTER_HIDDEN = 64
DECAY_TARGET = 1e-2
MIN_DECAY = math.log(DECAY_TARGET) / 1.5
MAX_DECAY = math.log(DECAY_TARGET) / 0.3
DECAY_SHIFT = 0.05
LN_EPS = 1e-5
NEG_INF = -1e30

COL_VH, COL_X1, COL_X0, COL_GH, COL_Q, COL_K, COL_V, COL_GA, COL_MH, COL_MA = range(10)

LANES = 128
SUBLANES = 8
HEADS_PER_TILE = LANES // HEAD_DIM
N_HEAD_TILES = N_HEADS // HEADS_PER_TILE
V7X_VMEM_BYTES = 64 * 1024 * 1024
ATTN_SCALE = HEAD_DIM ** -0.5
LOG2E = math.log2(math.e)

Q_BLOCK = 256
Q_BLOCK_ROWS = Q_BLOCK // GRID_W
KEY_WIN_ROWS = 12
KEY_WIN = KEY_WIN_ROWS * GRID_W

IN_TM = 256
W_STAGES = 2
MERGE_TM = 512
HYENA_ROWS_PER_STEP = 4096


def _vmem_limit(block_bytes, temp_bytes):
    need = int(block_bytes + temp_bytes)
    return min(max(need, 16 * 1024 * 1024), V7X_VMEM_BYTES - 8 * 1024 * 1024)


def _nbytes(shape, dtype):
    return int(np.prod(shape)) * jnp.dtype(dtype).itemsize


def _silu(x):
    return x * jax.nn.sigmoid(x)


def _split_bf16(a):
    hi = a.astype(BF16)
    lo = (a - hi.astype(F32)).astype(BF16)
    return hi, lo


def _dot(a, b):
    return jnp.dot(a, b, preferred_element_type=F32)


def _dot_split(a_hi, a_lo, b):
    b_hi, b_lo = _split_bf16(b)
    return _dot(a_hi, b_hi) + (_dot(a_hi, b_lo) + _dot(a_lo, b_hi))


@functools.lru_cache(maxsize=None)
def _dft_constants(L):
    n = 2 * L
    k = np.arange(L, dtype=np.float64)[:, None]
    s = np.arange(L, dtype=np.float64)[None, :]
    ang = 2.0 * np.pi * k * s / n
    nyq = np.cos(np.pi * np.arange(L, dtype=np.float64))
    C = np.cos(ang)
    S = np.sin(ang)
    S[0, :] = nyq
    F = np.concatenate([C, S], axis=0)
    w = np.full((L,), 2.0)
    w[0] = 1.0
    Gc = (C * w[:, None]).T / n
    Ss = 2.0 * np.sin(ang)
    Ss[0, :] = nyq
    Gs = Ss.T / n
    G = np.concatenate([Gc, Gs], axis=1)
    return F.astype(np.float32), G.astype(np.float32)


def _dft_matrices(L):
    F, G = _dft_constants(L)
    return jnp.asarray(F).astype(BF16), jnp.asarray(G).astype(BF16)


@functools.lru_cache(maxsize=None)
def _filter_constants(L):
    t = np.linspace(0.0, 1.0, L, dtype=np.float32)[:, None]
    bands = np.linspace(1e-4, FILTER_BANDS - 1, FILTER_BANDS, dtype=np.float32)[None]
    w = (2.0 * math.pi * np.arange(L, dtype=np.float32)[:, None] / L).astype(np.float32)
    z = np.concatenate([t, np.cos(bands * w), -np.sin(bands * w)], axis=-1).astype(np.float32)
    z_pad = np.zeros((L, LANES), np.float32)
    z_pad[:, :FILTER_EMB] = z
    deltas = np.linspace(MIN_DECAY, MAX_DECAY, D_HYENA, dtype=np.float32)
    decay = (np.exp(-t * np.abs(deltas)) + np.float32(DECAY_SHIFT)).astype(np.float32)
    return z_pad, decay


@functools.lru_cache(maxsize=None)
def _rpb_constants():
    c = np.arange(GRID_W)[:, None]
    kc = np.arange(GRID_W)[None, :]
    c0 = np.clip(c - WIN_COLS // 2, 0, GRID_W - WIN_COLS)
    in_win = (kc >= c0) & (kc < c0 + WIN_COLS)
    idx = np.clip(kc - c + WIN_COLS - 1, 0, 2 * WIN_COLS - 2)
    onehot = np.zeros((LANES, GRID_W, LANES), np.float32)
    for half in range(HEADS_PER_TILE):
        lanes = half * GRID_W + np.arange(GRID_W)
        onehot[idx, c, lanes[None, :]] = 1.0
    mask = np.concatenate([in_win, in_win], axis=1).astype(np.float32)
    return onehot.reshape(LANES, GRID_W * LANES).astype(BF16), mask.reshape(1, GRID_W * LANES)


def _mod_kernel(cond_ref, w_ref, b_ref, o_ref):
    s = _silu(cond_ref[...]).astype(BF16)
    o_ref[...] = _dot(s, w_ref[...].astype(BF16)) + b_ref[...]


def _modulation(cond, w_ada, b_ada):
    rows = cond.shape[0]
    return pl.pallas_call(
        _mod_kernel,
        out_shape=jax.ShapeDtypeStruct((rows, 3 * D_MODEL), F32),
        grid=(3,),
        in_specs=[pl.BlockSpec((rows, D_MODEL), lambda j: (0, 0)),
                  pl.BlockSpec((D_MODEL, D_MODEL), lambda j: (0, j)),
                  pl.BlockSpec((1, D_MODEL), lambda j: (0, j))],
        out_specs=pl.BlockSpec((rows, D_MODEL), lambda j: (0, j)),
        name="mod",
    )(cond, w_ada, b_ada.reshape(1, -1))


def _spectra_kernel(z_ref, w1_ref, b1_ref, w2_ref, b2_ref, w3f_ref, w3b_ref, freq_ref, decay_ref, d_ref, f_ref,
                    fp_ref, fqa_ref, fpb_ref, hf_ref):
    L = decay_ref.shape[0]

    @pl.when(pl.program_id(0) == 0)
    def _():
        z_hi, z_lo = _split_bf16(z_ref[...])
        hf = jnp.sin(freq_ref[0:1, :] * (_dot_split(z_hi, z_lo, w1_ref[...]) + b1_ref[...]))
        h_hi, h_lo = _split_bf16(hf)
        hf_ref[...] = jnp.sin(freq_ref[1:2, :] * (_dot_split(h_hi, h_lo, w2_ref[...]) + b2_ref[...]))

    h_hi, h_lo = _split_bf16(hf_ref[...])
    decay = decay_ref[...]
    h_fwd = _dot_split(h_hi, h_lo, w3f_ref[...]) * decay
    h_bwd = _dot_split(h_hi, h_lo, w3b_ref[...]) * decay
    row0 = jax.lax.broadcasted_iota(jnp.int32, (L, 1), 0) == 0
    h_bwd = jnp.where(row0, 0.0, h_bwd)
    fsum = (h_fwd + h_bwd).astype(BF16)
    fdif = (h_fwd - h_bwd).astype(BF16)
    skip = d_ref[...]
    fp = _dot(f_ref[0:L, :], fsum) + skip
    fq = _dot(f_ref[L:2 * L, :], fdif)
    nyq = _dot(f_ref[L:L + 2 * SUBLANES, :], fsum)[0:1, :] + skip
    fp_ref[...] = fp
    fqa_ref[...] = jnp.where(row0, 0.0, fq)
    fpb_ref[...] = jnp.where(row0, nyq, fp)


def _filter_spectra(L, f_mat, hyena_d, w1, b1, w2, b2, w3, freq):
    tc = 256
    z_emb, decay = _filter_constants(L)
    w1p = jnp.zeros((LANES, FILTER_HIDDEN), F32).at[:FILTER_EMB].set(w1)
    nblk = D_HYENA // tc
    const = lambda shape: pl.BlockSpec(shape, lambda c: (0, 0), pipeline_mode=pl.Buffered(1))
    out_spec = pl.BlockSpec((L, tc), lambda c: (0, c))
    blocks = (_nbytes((2 * L, L), BF16) + _nbytes((L, LANES), F32) + 2 * 4 * _nbytes((L, tc), F32))
    return pl.pallas_call(
        _spectra_kernel,
        out_shape=[jax.ShapeDtypeStruct((L, D_HYENA), F32)] * 3,
        grid=(nblk,),
        in_specs=[const((L, LANES)),
                  const((LANES, FILTER_HIDDEN)), const((1, FILTER_HIDDEN)),
                  const((FILTER_HIDDEN, FILTER_HIDDEN)), const((1, FILTER_HIDDEN)),
                  pl.BlockSpec((FILTER_HIDDEN, tc), lambda c: (0, c)),
                  pl.BlockSpec((FILTER_HIDDEN, tc), lambda c: (0, nblk + c)),
                  const((2, FILTER_HIDDEN)),
                  pl.BlockSpec((L, tc), lambda c: (0, c)),
                  pl.BlockSpec((1, tc), lambda c: (0, c)),
                  const((2 * L, L))],
        out_specs=[out_spec] * 3,
        scratch_shapes=[pltpu.VMEM((L, FILTER_HIDDEN), F32)],
        compiler_params=pltpu.CompilerParams(
            dimension_semantics=("arbitrary",),
            vmem_limit_bytes=_vmem_limit(blocks, 12 * _nbytes((L, tc), F32))),
        name=f"spectra_{L}",
    )(z_emb, w1p, b1.reshape(1, -1), w2, b2.reshape(1, -1), w3, w3, freq, decay, hyena_d.reshape(1, -1), f_mat)


def _rpb_kernel(r_ref, onehot_ref, mask_ref, o_ref):
    r = r_ref[...]
    r_hi = r.astype(BF16)
    rem = r - r_hi.astype(F32)
    r_mid = rem.astype(BF16)
    r_lo = (rem - r_mid.astype(F32)).astype(BF16)
    oh = onehot_ref[...]
    t = _dot(r_hi, oh) + (_dot(r_mid, oh) + _dot(r_lo, oh))
    t = jnp.where(mask_ref[...] > 0.0, t * LOG2E, NEG_INF)
    for c in range(o_ref.shape[1]):
        o_ref[:, c, :] = t[:, c * LANES:(c + 1) * LANES]


def _rpb_tiles(rpb):
    n_dr = 2 * WIN_ROWS - 1
    rows = N_HEADS * n_dr
    onehot, mask = _rpb_constants()
    r = jnp.zeros((rows, LANES), F32).at[:, :2 * WIN_COLS - 1].set(rpb.reshape(rows, 2 * WIN_COLS - 1))
    cols_per_step = 16
    tn = cols_per_step * LANES
    out = pl.pallas_call(
        _rpb_kernel,
        out_shape=jax.ShapeDtypeStruct((rows, GRID_W, LANES), F32),
        grid=(GRID_W // cols_per_step,),
        in_specs=[pl.BlockSpec((rows, LANES), lambda j: (0, 0)),
                  pl.BlockSpec((LANES, tn), lambda j: (0, j)),
                  pl.BlockSpec((1, tn), lambda j: (0, j))],
        out_specs=pl.BlockSpec((rows, cols_per_step, LANES), lambda j: (0, j, 0)),
        name="rpb_tiles",
    )(r, onehot, mask)
    return out.reshape(N_HEADS, n_dr, GRID_W, LANES)


def _inproj_kernel(x_ref, mod_ref, w_hbm, z_ref, *rest):
    *kv_refs, w_ref, stage_ref, sem = rest
    n_col = D_IN // D_MODEL
    n_stage = stage_ref.shape[0]
    epilogue = {COL_GH: _silu, COL_GA: _silu, COL_MH: jax.nn.sigmoid, COL_MA: jax.nn.sigmoid,
                COL_Q: lambda acc: acc * (ATTN_SCALE * LOG2E)}

    def columns(col):
        return slice(col * D_MODEL, (col + 1) * D_MODEL)

    def fetch(col):
        return pltpu.make_async_copy(w_hbm.at[:, columns(col)], stage_ref.at[col % n_stage], sem.at[col % n_stage])

    def tile(stream_weights):
        if stream_weights:
            for col in range(min(n_stage, n_col)):
                fetch(col).start()
        shift = mod_ref[:, 0:D_MODEL]
        scale = mod_ref[:, D_MODEL:2 * D_MODEL]
        h = (x_ref[...] * (1.0 + scale) + shift).astype(BF16)
        for col in range(n_col):
            if stream_weights:
                fetch(col).wait()
                w_ref[:, columns(col)] = stage_ref[col % n_stage].astype(BF16)
                if col + n_stage < n_col:
                    fetch(col + n_stage).start()
            acc = _dot(h, w_ref[:, columns(col)])
            z_ref[:, columns(col)] = epilogue.get(col, lambda acc: acc)(acc).astype(BF16)
            if kv_refs and col == COL_K:
                kv_refs[0][...] = acc.T
            if kv_refs and col == COL_V:
                kv_refs[1][...] = acc.T

    pl.when(pl.program_id(0) == 0)(functools.partial(tile, True))
    pl.when(pl.program_id(0) != 0)(functools.partial(tile, False))


def _in_projection(x2d, mod3, mod_index, w_in, *, emit_kv, tm):
    M = x2d.shape[0]
    out_shape = [jax.ShapeDtypeStruct((M, D_IN), BF16)]
    out_specs = [pl.BlockSpec((tm, D_IN), lambda i: (i, 0))]
    if emit_kv:
        out_shape += [jax.ShapeDtypeStruct((M // tm, D_ATTN, tm), F32)] * 2
        out_specs += [pl.BlockSpec((None, D_ATTN, tm), lambda i: (i, 0, 0))] * 2
    stage = (W_STAGES, D_MODEL, D_MODEL)
    blocks = (_nbytes((D_MODEL, D_IN), BF16) + _nbytes(stage, F32) + 2 * _nbytes((tm, D_MODEL), F32)
              + 2 * _nbytes((tm, D_IN), BF16) + (4 * _nbytes((tm, D_ATTN), F32) if emit_kv else 0))
    return pl.pallas_call(
        _inproj_kernel,
        out_shape=out_shape,
        grid=(M // tm,),
        in_specs=[pl.BlockSpec((tm, D_MODEL), lambda i: (i, 0)),
                  pl.BlockSpec((None, 1, 3 * D_MODEL), lambda i: (mod_index(i), 0, 0)),
                  pl.BlockSpec(memory_space=pl.ANY)],
        out_specs=out_specs,
        scratch_shapes=[pltpu.VMEM((D_MODEL, D_IN), BF16), pltpu.VMEM(stage, F32),
                        pltpu.SemaphoreType.DMA((W_STAGES,))],
        compiler_params=pltpu.CompilerParams(
            dimension_semantics=("arbitrary",),
            vmem_limit_bytes=_vmem_limit(blocks, 4 * _nbytes((tm, D_MODEL), F32))),
        name="inproj",
    )(x2d, mod3, w_in)


def _hyena_kernel(vh_ref, x1_ref, x0_ref, gh_ref, cwv_ref, cw1_ref, cw0_ref, cbv_ref, cb1_ref, cb0_ref,
                  f_ref, g_ref, fp_ref, fqa_ref, fpb_ref, o_ref):
    L = fp_ref.shape[0]
    slab = jax.lax.broadcasted_iota(jnp.int32, (SUBLANES, 1), 0)

    def short_conv(x_ref, rows, w_ref, b_ref):
        x = x_ref[rows, :].astype(F32)
        prev = pltpu.roll(x, 1, 0)
        nxt = pltpu.roll(x, L - 1, 0)
        prev = jnp.concatenate([jnp.where(slab == 0, 0.0, prev[:SUBLANES]), prev[SUBLANES:]], axis=0)
        nxt = jnp.concatenate([nxt[:-SUBLANES], jnp.where(slab == SUBLANES - 1, 0.0, nxt[-SUBLANES:])], axis=0)
        return prev * w_ref[0:1, :] + x * w_ref[1:2, :] + nxt * w_ref[2:3, :] + b_ref[...]

    for s in range(vh_ref.shape[0] // L):
        rows = slice(s * L, (s + 1) * L)
        u = short_conv(vh_ref, rows, cwv_ref, cbv_ref) * short_conv(x1_ref, rows, cw1_ref, cb1_ref)
        t = _dot(f_ref[...], u.astype(BF16))
        p = t[:L]
        q = t[L:]
        yp = p * fp_ref[...] - q * fqa_ref[...]
        yq = p * fqa_ref[...] + q * fpb_ref[...]
        y = _dot(g_ref[...], jnp.concatenate([yp.astype(BF16), yq.astype(BF16)], axis=0))
        y_h = y * short_conv(x0_ref, rows, cw0_ref, cb0_ref)
        o_ref[rows, :] = (y_h * gh_ref[rows, :].astype(F32)).astype(BF16)


def _hyena_branch(z, B, L, conv_w, conv_b, dft, spectra, *, tc=256, seqs_per_step):
    nblk = D_HYENA // tc
    tm = seqs_per_step * L
    zcol = lambda piece: pl.BlockSpec((tm, tc), lambda c, b: (b, piece * nblk + c))
    wcol = lambda rows, piece: pl.BlockSpec((rows, tc), lambda c, b: (0, piece * nblk + c))
    chan = lambda rows: pl.BlockSpec((rows, tc), lambda c, b: (0, c))
    const = lambda shape: pl.BlockSpec(shape, lambda c, b: (0, 0), pipeline_mode=pl.Buffered(1))
    blocks = (2 * _nbytes((2 * L, L), BF16) + 2 * 5 * _nbytes((tm, tc), BF16) + 2 * 3 * _nbytes((L, tc), F32))
    return pl.pallas_call(
        _hyena_kernel,
        out_shape=jax.ShapeDtypeStruct((B * L, D_HYENA), BF16),
        grid=(nblk, B // seqs_per_step),
        in_specs=[zcol(COL_VH), zcol(COL_X1), zcol(COL_X0), zcol(COL_GH),
                  wcol(3, 0), wcol(3, 1), wcol(3, 2), wcol(1, 0), wcol(1, 1), wcol(1, 2),
                  const((2 * L, L)), const((L, 2 * L)), chan(L), chan(L), chan(L)],
        out_specs=pl.BlockSpec((tm, tc), lambda c, b: (b, c)),
        compiler_params=pltpu.CompilerParams(
            dimension_semantics=("arbitrary", "arbitrary"),
            vmem_limit_bytes=_vmem_limit(blocks, 16 * _nbytes((tm, tc), F32))),
        name=f"hyena_{L}",
    )(z, z, z, z, conv_w, conv_w, conv_w, conv_b.reshape(1, -1), conv_b.reshape(1, -1), conv_b.reshape(1, -1),
      *dft, *spectra)


def _head_masks():
    lane = jax.lax.broadcasted_iota(jnp.int32, (1, LANES), 1)
    return [(lane >= h * HEAD_DIM) & (lane < (h + 1) * HEAD_DIM) for h in range(HEADS_PER_TILE)]


def _qk(q, k):
    return jax.lax.dot_general(q, k, (((1,), (1,)), ((), ())), preferred_element_type=F32)


def _stack_heads(q, masks):
    return jnp.concatenate([jnp.where(msk, q, jnp.zeros_like(q)) for msk in masks], axis=0)


def _unstack_heads(o, masks):
    n = o.shape[0] // len(masks)
    out = o[:n]
    for h in range(1, len(masks)):
        out = jnp.where(masks[h], o[h * n:(h + 1) * n], out)
    return out


def _ctx_attn_kernel(q_ref, k_ref, v_ref, g_ref, o_ref):
    masks = _head_masks()
    for t in range(N_HEAD_TILES):
        cols = slice(t * LANES, (t + 1) * LANES)
        s = _qk(_stack_heads(q_ref[:, cols], masks), k_ref[:, cols])
        m = jnp.max(s, axis=-1, keepdims=True)
        p = jnp.exp2(s - m)
        l = jnp.sum(p, axis=-1, keepdims=True)
        out = _unstack_heads(_dot(p.astype(BF16), v_ref[:, cols]) / l, masks)
        o_ref[:, cols] = (out * g_ref[:, cols].astype(F32)).astype(BF16)


def _context_attention(z, B, L):
    zcol = lambda piece: pl.BlockSpec((L, D_ATTN), lambda b: (b, piece))
    return pl.pallas_call(
        _ctx_attn_kernel,
        out_shape=jax.ShapeDtypeStruct((B * L, D_ATTN), BF16),
        grid=(B,),
        in_specs=[zcol(COL_Q), zcol(COL_K), zcol(COL_V), zcol(COL_GA)],
        out_specs=pl.BlockSpec((L, D_ATTN), lambda b: (b, 0)),
        compiler_params=pltpu.CompilerParams(dimension_semantics=("arbitrary",)),
        name="ctx_attn",
    )(z, z, z, z)


def _key_window_start(qb):
    rows = 1024 // GRID_W
    r_first = qb * Q_BLOCK_ROWS
    r_last = r_first + Q_BLOCK_ROWS - 1
    lo = min(max(r_first - WIN_ROWS // 2, 0), rows - WIN_ROWS)
    hi = min(max(r_last - WIN_ROWS // 2, 0), rows - WIN_ROWS) + WIN_ROWS
    start = min(lo, rows - KEY_WIN_ROWS)
    start -= start % Q_BLOCK_ROWS
    assert start <= lo and hi <= start + KEY_WIN_ROWS
    return start


def _nbr_attn_kernel(q_ref, k0_ref, k1_ref, k2_ref, v0_ref, v1_ref, v2_ref, kc_ref, vc_ref, g_ref, tiles_ref,
                     o_ref, bias_ref, *, n_rows):
    masks = _head_masks()
    k_refs = (k0_ref, k1_ref, k2_ref)
    v_refs = (v0_ref, v1_ref, v2_ref)
    key_blocks = KEY_WIN // Q_BLOCK

    def window_row_start(r):
        return min(max(r - WIN_ROWS // 2, 0), n_rows - WIN_ROWS)

    def build_bias(qb):
        k_start = _key_window_start(qb)
        used = set()
        for i in range(Q_BLOCK_ROWS):
            r = qb * Q_BLOCK_ROWS + i
            r0 = window_row_start(r)
            for j in range(KEY_WIN_ROWS):
                kr = k_start + j
                rows = slice(i * GRID_W, (i + 1) * GRID_W)
                cols = slice(j * GRID_W, (j + 1) * GRID_W)
                lanes = slice((j % HEADS_PER_TILE) * GRID_W, (j % HEADS_PER_TILE + 1) * GRID_W)
                inside = r0 <= kr < r0 + WIN_ROWS
                if inside:
                    used.add(j // Q_BLOCK_ROWS)
                for h in range(HEADS_PER_TILE):
                    if inside:
                        bias_ref[h, rows, cols] = tiles_ref[h, kr - r + WIN_ROWS - 1, :, lanes]
                    else:
                        bias_ref[h, rows, cols] = jnp.full((GRID_W, GRID_W), NEG_INF, F32)
        return sorted(used)

    def attend(blocks):
        bias = [jnp.concatenate([bias_ref[h, :, j * Q_BLOCK:(j + 1) * Q_BLOCK] for h in range(HEADS_PER_TILE)],
                                axis=0) for j in blocks]
        for b in range(q_ref.shape[0]):
            qs = _stack_heads(q_ref[b], masks)
            s = [_qk(qs, k_refs[j][b]) + bias_j for j, bias_j in zip(blocks, bias)]
            s.append(_dot(qs, kc_ref[b].astype(BF16)))
            m = jnp.max(functools.reduce(jnp.maximum, s), axis=-1, keepdims=True)
            p = [jnp.exp2(x - m) for x in s]
            l = jnp.sum(functools.reduce(jnp.add, p), axis=-1, keepdims=True)
            p = [x.astype(BF16) for x in p]
            v_win = jnp.concatenate([v_refs[j][b] for j in blocks], axis=0)
            o = _dot(jnp.concatenate(p[:-1], axis=1), v_win) + _qk(p[-1], vc_ref[b].astype(BF16))
            o_ref[b] = (_unstack_heads(o / l, masks) * g_ref[b].astype(F32)).astype(BF16)

    for qb in range(n_rows // Q_BLOCK_ROWS):
        @pl.when(pl.program_id(1) == qb)
        def _(qb=qb):
            blocks = build_bias(qb)
            assert blocks and all(0 <= j < key_blocks for j in blocks)
            attend(blocks)


def _neighbourhood_attention(z, B, L, cache_k, cache_v, tiles):
    n_rows = L // GRID_W
    n_qb = L // Q_BLOCK
    assert KEY_WIN == 3 * Q_BLOCK and all(_key_window_start(qb) % Q_BLOCK_ROWS == 0 for qb in range(n_qb))
    z3 = z.reshape(B, L, D_IN)
    n_ctx = cache_k.shape[1]
    kc = jnp.transpose(cache_k, (0, 2, 3, 1)).reshape(B, D_ATTN, n_ctx)
    vc = jnp.transpose(cache_v, (0, 2, 3, 1)).reshape(B, D_ATTN, n_ctx)
    tiles_per_col = D_ATTN // LANES

    def win_block(qb):
        return jnp.where(qb < n_qb // 2, 0, (n_rows - KEY_WIN_ROWS) // Q_BLOCK_ROWS)

    assert [_key_window_start(qb) // Q_BLOCK_ROWS for qb in range(n_qb)] == \
        [0 if qb < n_qb // 2 else (n_rows - KEY_WIN_ROWS) // Q_BLOCK_ROWS for qb in range(n_qb)]
    qspec = lambda piece: pl.BlockSpec((B, Q_BLOCK, LANES), lambda t, qb: (0, qb, piece * tiles_per_col + t))
    kspec = lambda piece, j: pl.BlockSpec((B, Q_BLOCK, LANES),
                                          lambda t, qb: (0, win_block(qb) + j, piece * tiles_per_col + t))
    cspec = pl.BlockSpec((B, LANES, n_ctx), lambda t, qb: (0, t, 0))
    out = pl.pallas_call(
        functools.partial(_nbr_attn_kernel, n_rows=n_rows),
        out_shape=jax.ShapeDtypeStruct((B, L, D_ATTN), BF16),
        grid=(N_HEAD_TILES, n_qb),
        in_specs=[qspec(COL_Q), kspec(COL_K, 0), kspec(COL_K, 1), kspec(COL_K, 2),
                  kspec(COL_V, 0), kspec(COL_V, 1), kspec(COL_V, 2), cspec, cspec, qspec(COL_GA),
                  pl.BlockSpec((HEADS_PER_TILE, 2 * WIN_ROWS - 1, GRID_W, LANES), lambda t, qb: (t, 0, 0, 0))],
        out_specs=pl.BlockSpec((B, Q_BLOCK, LANES), lambda t, qb: (0, qb, t)),
        scratch_shapes=[pltpu.VMEM((HEADS_PER_TILE, Q_BLOCK, KEY_WIN), F32)],
        compiler_params=pltpu.CompilerParams(dimension_semantics=("arbitrary", "arbitrary")),
        name="nbr_attn",
    )(z3, z3, z3, z3, z3, z3, z3, kc, vc, z3, tiles)
    return out.reshape(B * L, D_ATTN)


def _merge_kernel(yh_ref, ya_ref, mh_ref, ma_ref, x_ref, mod_ref, wbh_ref, wba_ref, wout_ref, lng_ref, lnb_ref,
                  o_ref, *, alpha):
    p_h = _dot(yh_ref[...], wbh_ref[...])
    p_a = _dot(ya_ref[...], wba_ref[...])
    m_h = mh_ref[...].astype(F32)
    m_a = ma_ref[...].astype(F32)
    out = _dot((m_h * p_h + m_a * p_a).astype(BF16), wout_ref[...])
    gate = mod_ref[:, 2 * D_MODEL:3 * D_MODEL]
    v = alpha * x_ref[...] + gate * out
    mu = jnp.mean(v, axis=-1, keepdims=True)
    d = v - mu
    var = jnp.mean(d * d, axis=-1, keepdims=True)
    o_ref[...] = d * jax.lax.rsqrt(var + LN_EPS) * lng_ref[...] + lnb_ref[...]


def _merge(yh, ya, z, x2d, mod3, mod_index, w_bh, w_ba, w_out, ln_g, ln_b, *, alpha, tm):
    M = x2d.shape[0]
    tok = lambda width, col: pl.BlockSpec((tm, width), lambda i: (i, col))
    const = lambda shape: pl.BlockSpec(shape, lambda i: (0, 0))
    return pl.pallas_call(
        functools.partial(_merge_kernel, alpha=alpha),
        out_shape=jax.ShapeDtypeStruct((M, D_MODEL), F32),
        grid=(M // tm,),
        in_specs=[tok(D_HYENA, 0), tok(D_ATTN, 0), tok(D_MODEL, COL_MH), tok(D_MODEL, COL_MA), tok(D_MODEL, 0),
                  pl.BlockSpec((None, 1, 3 * D_MODEL), lambda i: (mod_index(i), 0, 0)),
                  const((D_HYENA, D_MODEL)), const((D_ATTN, D_MODEL)), const((D_MODEL, D_MODEL)),
                  const((1, D_MODEL)), const((1, D_MODEL))],
        out_specs=pl.BlockSpec((tm, D_MODEL), lambda i: (i, 0)),
        compiler_params=pltpu.CompilerParams(
            dimension_semantics=("arbitrary",),
            vmem_limit_bytes=_vmem_limit(
                2 * (4 * _nbytes((tm, D_MODEL), BF16) + 2 * _nbytes((tm, D_MODEL), F32)
                     + 3 * _nbytes((D_MODEL, D_MODEL), BF16)),
                8 * _nbytes((tm, D_MODEL), F32))),
        name="merge",
    )(yh, ya, z, z, x2d, mod3, w_bh, w_ba, w_out, ln_g.reshape(1, -1), ln_b.reshape(1, -1))


def _layer(x, mod3, cond_row, ctx_kv, p, filt, tiles, *, alpha):
    B, L, _ = x.shape
    x2d = x.reshape(B * L, D_MODEL)
    emit_kv = ctx_kv is None
    assert not emit_kv or IN_TM == L
    outs = _in_projection(x2d, mod3, lambda i: cond_row(i * IN_TM // L), p["w_in"], emit_kv=emit_kv, tm=IN_TM)
    z = outs[0]
    dft = _dft_matrices(L)
    spectra = _filter_spectra(L, dft[0], p["hyena_d"], *filt)
    yh = _hyena_branch(z, B, L, p["conv_w"], p["conv_b"], dft, spectra,
                       seqs_per_step=max(1, HYENA_ROWS_PER_STEP // L))
    if ctx_kv is None:
        ya = _context_attention(z, B, L)
    else:
        ya = _neighbourhood_attention(z, B, L, ctx_kv[0], ctx_kv[1], tiles)
    y = _merge(yh, ya, z, x2d, mod3, lambda i: cond_row(i * MERGE_TM // L), p["w_bh"], p["w_ba"], p["w_out"],
               p["ln_g"], p["ln_b"], alpha=alpha, tm=MERGE_TM)
    return y.reshape(B, L, D_MODEL), outs[1:]


def kernel(x_prompt, x_sample, c, cache_k, cache_v, c_ctx, w_ada, b_ada, w_in, conv_w, conv_b, filt_w1, filt_b1,
           filt_w2, filt_b2, filt_w3, filt_freq, hyena_d, rpb, w_bh, w_ba, w_out, ln_g, ln_b):
    depth = w_in.shape[0]
    alpha = (2.0 * depth) ** 0.25
    n_lat, lat_len = x_sample.shape[0], x_sample.shape[1]
    n_ctx, ctx_len = x_prompt.shape[0], x_prompt.shape[1]
    ctx_row = n_lat
    cond = jnp.zeros((8, D_MODEL), F32).at[:n_lat].set(c).at[ctx_row].set(c_ctx)

    xp, xs = x_prompt, x_sample
    new_k, new_v = [], []
    for l in range(depth):
        p = {"w_in": w_in[l], "conv_w": conv_w[l], "conv_b": conv_b[l], "hyena_d": hyena_d[l],
             "w_bh": w_bh[l].astype(BF16), "w_ba": w_ba[l].astype(BF16), "w_out": w_out[l].astype(BF16),
             "ln_g": ln_g[l], "ln_b": ln_b[l]}
        mod3 = _modulation(cond, w_ada[l], b_ada[l]).reshape(8, 1, 3 * D_MODEL)
        filt = (filt_w1[l], filt_b1[l], filt_w2[l], filt_b2[l], filt_w3[l], filt_freq[l])
        tiles = _rpb_tiles(rpb[l])
        xp, (k_ctx, v_ctx) = _layer(xp, mod3, lambda b: ctx_row, None, p, filt, None, alpha=alpha)
        new_k.append(jnp.transpose(k_ctx.reshape(n_ctx, N_HEADS, HEAD_DIM, ctx_len), (0, 3, 1, 2)))
        new_v.append(jnp.transpose(v_ctx.reshape(n_ctx, N_HEADS, HEAD_DIM, ctx_len), (0, 3, 1, 2)))
        xs, _ = _layer(xs, mod3, lambda b: b, (cache_k[:, l], cache_v[:, l]), p, filt, tiles, alpha=alpha)
    return xp, xs, jnp.stack(new_k, axis=1), jnp.stack(new_v, axis=1)
```

```python
import functools
import math

import jax
import jax.numpy as jnp
import numpy as np
from jax.experimental import pallas as pl
from jax.experimental.pallas import tpu as pltpu

F32 = jnp.float32
BF16 = jnp.bfloat16

D_MODEL = 1024
D_HYENA = 1024
N_HEADS = 16
HEAD_DIM = 64
D_ATTN = N_HEADS * HEAD_DIM
D_IN = 4 * D_HYENA + 4 * D_ATTN + 2 * D_MODEL
GRID_W = 64
WIN_ROWS = 8
WIN_COLS = 16
FILTER_EMB = 33
FILTER_BANDS = (FILTER_EMB - 1) // 2
FILTER_HIDDEN = 64
DECAY_TARGET = 1e-2
MIN_DECAY = math.log(DECAY_TARGET) / 1.5
MAX_DECAY = math.log(DECAY_TARGET) / 0.3
DECAY_SHIFT = 0.05
LN_EPS = 1e-5
NEG_INF = -1e30

COL_VH, COL_X1, COL_X0, COL_GH, COL_Q, COL_K, COL_V, COL_GA, COL_MH, COL_MA = range(10)

LANES = 128
SUBLANES = 8
CTX_HEAD_TILE = 128
NBR_HEAD_TILE = 128
NBR_BLOCK_LANES = 256
RPB_COPIES = LANES // GRID_W
V7X_VMEM_BYTES = 64 * 1024 * 1024
ATTN_SCALE = HEAD_DIM ** -0.5
LOG2E = math.log2(math.e)

Q_BLOCK = 256
Q_BLOCK_ROWS = Q_BLOCK // GRID_W
KEY_WIN_ROWS = 12
KEY_WIN = KEY_WIN_ROWS * GRID_W

IN_TM = 256
W_STAGES = 2
MERGE_TM = 512
HYENA_ROWS_PER_STEP = 4096


def _vmem_limit(block_bytes, temp_bytes):
    need = int(block_bytes + temp_bytes)
    return min(max(need, 16 * 1024 * 1024), V7X_VMEM_BYTES - 8 * 1024 * 1024)


def _nbytes(shape, dtype):
    return int(np.prod(shape)) * jnp.dtype(dtype).itemsize


def _silu(x):
    return x * jax.nn.sigmoid(x)


def _split_bf16(a):
    hi = a.astype(BF16)
    lo = (a - hi.astype(F32)).astype(BF16)
    return hi, lo


def _dot(a, b):
    return jnp.dot(a, b, preferred_element_type=F32)


def _dot_split(a_hi, a_lo, b):
    b_hi, b_lo = _split_bf16(b)
    return _dot(a_hi, b_hi) + (_dot(a_hi, b_lo) + _dot(a_lo, b_hi))


@functools.lru_cache(maxsize=None)
def _dft_constants(L):
    n = 2 * L
    k = np.arange(L, dtype=np.float64)[:, None]
    s = np.arange(L, dtype=np.float64)[None, :]
    ang = 2.0 * np.pi * k * s / n
    nyq = np.cos(np.pi * np.arange(L, dtype=np.float64))
    C = np.cos(ang)
    S = np.sin(ang)
    S[0, :] = nyq
    F = np.concatenate([C, S], axis=0)
    w = np.full((L,), 2.0)
    w[0] = 1.0
    Gc = (C * w[:, None]).T / n
    Ss = 2.0 * np.sin(ang)
    Ss[0, :] = nyq
    Gs = Ss.T / n
    G = np.concatenate([Gc, Gs], axis=1)
    return F.astype(np.float32), G.astype(np.float32)


def _dft_matrices(L):
    F, G = _dft_constants(L)
    return jnp.asarray(F).astype(BF16), jnp.asarray(G).astype(BF16)


@functools.lru_cache(maxsize=None)
def _filter_constants(L):
    t = np.linspace(0.0, 1.0, L, dtype=np.float32)[:, None]
    bands = np.linspace(1e-4, FILTER_BANDS - 1, FILTER_BANDS, dtype=np.float32)[None]
    w = (2.0 * math.pi * np.arange(L, dtype=np.float32)[:, None] / L).astype(np.float32)
    z = np.concatenate([t, np.cos(bands * w), -np.sin(bands * w)], axis=-1).astype(np.float32)
    z_pad = np.zeros((L, LANES), np.float32)
    z_pad[:, :FILTER_EMB] = z
    deltas = np.linspace(MIN_DECAY, MAX_DECAY, D_HYENA, dtype=np.float32)
    decay = (np.exp(-t * np.abs(deltas)) + np.float32(DECAY_SHIFT)).astype(np.float32)
    return z_pad, decay


@functools.lru_cache(maxsize=None)
def _rpb_constants():
    c = np.arange(GRID_W)[:, None]
    kc = np.arange(GRID_W)[None, :]
    c0 = np.clip(c - WIN_COLS // 2, 0, GRID_W - WIN_COLS)
    in_win = (kc >= c0) & (kc < c0 + WIN_COLS)
    idx = np.clip(kc - c + WIN_COLS - 1, 0, 2 * WIN_COLS - 2)
    onehot = np.zeros((LANES, GRID_W, LANES), np.float32)
    for half in range(RPB_COPIES):
        lanes = half * GRID_W + np.arange(GRID_W)
        onehot[idx, c, lanes[None, :]] = 1.0
    mask = np.concatenate([in_win, in_win], axis=1).astype(np.float32)
    return onehot.reshape(LANES, GRID_W * LANES).astype(BF16), mask.reshape(1, GRID_W * LANES)


def _mod_kernel(cond_ref, w_ref, b_ref, o_ref):
    s = _silu(cond_ref[...]).astype(BF16)
    o_ref[...] = _dot(s, w_ref[...].astype(BF16)) + b_ref[...]


def _modulation(cond, w_ada, b_ada):
    rows = cond.shape[0]
    return pl.pallas_call(
        _mod_kernel,
        out_shape=jax.ShapeDtypeStruct((rows, 3 * D_MODEL), F32),
        grid=(3,),
        in_specs=[pl.BlockSpec((rows, D_MODEL), lambda j: (0, 0)),
                  pl.BlockSpec((D_MODEL, D_MODEL), lambda j: (0, j)),
                  pl.BlockSpec((1, D_MODEL), lambda j: (0, j))],
        out_specs=pl.BlockSpec((rows, D_MODEL), lambda j: (0, j)),
        name="mod",
    )(cond, w_ada, b_ada.reshape(1, -1))


def _spectra_kernel(z_ref, w1_ref, b1_ref, w2_ref, b2_ref, w3f_ref, w3b_ref, freq_ref, decay_ref, d_ref, f_ref,
                    fp_ref, fqa_ref, fpb_ref, hf_ref):
    L = decay_ref.shape[0]

    @pl.when(pl.program_id(0) == 0)
    def _():
        z_hi, z_lo = _split_bf16(z_ref[...])
        hf = jnp.sin(freq_ref[0:1, :] * (_dot_split(z_hi, z_lo, w1_ref[...]) + b1_ref[...]))
        h_hi, h_lo = _split_bf16(hf)
        hf_ref[...] = jnp.sin(freq_ref[1:2, :] * (_dot_split(h_hi, h_lo, w2_ref[...]) + b2_ref[...]))

    h_hi, h_lo = _split_bf16(hf_ref[...])
    decay = decay_ref[...]
    h_fwd = _dot_split(h_hi, h_lo, w3f_ref[...]) * decay
    h_bwd = _dot_split(h_hi, h_lo, w3b_ref[...]) * decay
    row0 = jax.lax.broadcasted_iota(jnp.int32, (L, 1), 0) == 0
    h_bwd = jnp.where(row0, 0.0, h_bwd)
    fsum = (h_fwd + h_bwd).astype(BF16)
    fdif = (h_fwd - h_bwd).astype(BF16)
    skip = d_ref[...]
    fp = _dot(f_ref[0:L, :], fsum) + skip
    fq = _dot(f_ref[L:2 * L, :], fdif)
    nyq = _dot(f_ref[L:L + 2 * SUBLANES, :], fsum)[0:1, :] + skip
    fp_ref[...] = fp
    fqa_ref[...] = jnp.where(row0, 0.0, fq)
    fpb_ref[...] = jnp.where(row0, nyq, fp)


def _filter_spectra(L, f_mat, hyena_d, w1, b1, w2, b2, w3, freq):
    tc = 256
    z_emb, decay = _filter_constants(L)
    w1p = jnp.zeros((LANES, FILTER_HIDDEN), F32).at[:FILTER_EMB].set(w1)
    nblk = D_HYENA // tc
    const = lambda shape: pl.BlockSpec(shape, lambda c: (0, 0), pipeline_mode=pl.Buffered(1))
    out_spec = pl.BlockSpec((L, tc), lambda c: (0, c))
    blocks = (_nbytes((2 * L, L), BF16) + _nbytes((L, LANES), F32) + 2 * 4 * _nbytes((L, tc), F32))
    return pl.pallas_call(
        _spectra_kernel,
        out_shape=[jax.ShapeDtypeStruct((L, D_HYENA), F32)] * 3,
        grid=(nblk,),
        in_specs=[const((L, LANES)),
                  const((LANES, FILTER_HIDDEN)), const((1, FILTER_HIDDEN)),
                  const((FILTER_HIDDEN, FILTER_HIDDEN)), const((1, FILTER_HIDDEN)),
                  pl.BlockSpec((FILTER_HIDDEN, tc), lambda c: (0, c)),
                  pl.BlockSpec((FILTER_HIDDEN, tc), lambda c: (0, nblk + c)),
                  const((2, FILTER_HIDDEN)),
                  pl.BlockSpec((L, tc), lambda c: (0, c)),
                  pl.BlockSpec((1, tc), lambda c: (0, c)),
                  const((2 * L, L))],
        out_specs=[out_spec] * 3,
        scratch_shapes=[pltpu.VMEM((L, FILTER_HIDDEN), F32)],
        compiler_params=pltpu.CompilerParams(
            dimension_semantics=("arbitrary",),
            vmem_limit_bytes=_vmem_limit(blocks, 12 * _nbytes((L, tc), F32))),
        name=f"spectra_{L}",
    )(z_emb, w1p, b1.reshape(1, -1), w2, b2.reshape(1, -1), w3, w3, freq, decay, hyena_d.reshape(1, -1), f_mat)


def _rpb_kernel(r_ref, onehot_ref, mask_ref, o_ref):
    r = r_ref[...]
    r_hi = r.astype(BF16)
    rem = r - r_hi.astype(F32)
    r_mid = rem.astype(BF16)
    r_lo = (rem - r_mid.astype(F32)).astype(BF16)
    oh = onehot_ref[...]
    t = _dot(r_hi, oh) + (_dot(r_mid, oh) + _dot(r_lo, oh))
    t = jnp.where(mask_ref[...] > 0.0, t * LOG2E, NEG_INF)
    for c in range(o_ref.shape[1]):
        o_ref[:, c, :] = t[:, c * LANES:(c + 1) * LANES]


def _rpb_tiles(rpb):
    n_dr = 2 * WIN_ROWS - 1
    rows = N_HEADS * n_dr
    onehot, mask = _rpb_constants()
    r = jnp.zeros((rows, LANES), F32).at[:, :2 * WIN_COLS - 1].set(rpb.reshape(rows, 2 * WIN_COLS - 1))
    cols_per_step = 16
    tn = cols_per_step * LANES
    out = pl.pallas_call(
        _rpb_kernel,
        out_shape=jax.ShapeDtypeStruct((rows, GRID_W, LANES), F32),
        grid=(GRID_W // cols_per_step,),
        in_specs=[pl.BlockSpec((rows, LANES), lambda j: (0, 0)),
                  pl.BlockSpec((LANES, tn), lambda j: (0, j)),
                  pl.BlockSpec((1, tn), lambda j: (0, j))],
        out_specs=pl.BlockSpec((rows, cols_per_step, LANES), lambda j: (0, j, 0)),
        name="rpb_tiles",
    )(r, onehot, mask)
    return out.reshape(N_HEADS, n_dr, GRID_W, LANES)


def _inproj_kernel(x_ref, mod_ref, w_hbm, z_ref, *rest):
    *kv_refs, w_ref, stage_ref, sem = rest
    n_col = D_IN // D_MODEL
    n_stage = stage_ref.shape[0]
    epilogue = {COL_GH: _silu, COL_GA: _silu, COL_MH: jax.nn.sigmoid, COL_MA: jax.nn.sigmoid,
                COL_Q: lambda acc: acc * (ATTN_SCALE * LOG2E)}

    def columns(col):
        return slice(col * D_MODEL, (col + 1) * D_MODEL)

    def fetch(col):
        return pltpu.make_async_copy(w_hbm.at[:, columns(col)], stage_ref.at[col % n_stage], sem.at[col % n_stage])

    def tile(stream_weights):
        if stream_weights:
            for col in range(min(n_stage, n_col)):
                fetch(col).start()
        shift = mod_ref[:, 0:D_MODEL]
        scale = mod_ref[:, D_MODEL:2 * D_MODEL]
        h = (x_ref[...] * (1.0 + scale) + shift).astype(BF16)
        for col in range(n_col):
            if stream_weights:
                fetch(col).wait()
                w_ref[:, columns(col)] = stage_ref[col % n_stage].astype(BF16)
                if col + n_stage < n_col:
                    fetch(col + n_stage).start()
            acc = _dot(h, w_ref[:, columns(col)])
            z_ref[:, columns(col)] = epilogue.get(col, lambda acc: acc)(acc).astype(BF16)
            if kv_refs and col == COL_K:
                kv_refs[0][...] = acc.T
            if kv_refs and col == COL_V:
                kv_refs[1][...] = acc.T

    pl.when(pl.program_id(0) == 0)(functools.partial(tile, True))
    pl.when(pl.program_id(0) != 0)(functools.partial(tile, False))


def _in_projection(x2d, mod3, mod_index, w_in, *, emit_kv, tm):
    M = x2d.shape[0]
    out_shape = [jax.ShapeDtypeStruct((M, D_IN), BF16)]
    out_specs = [pl.BlockSpec((tm, D_IN), lambda i: (i, 0))]
    if emit_kv:
        out_shape += [jax.ShapeDtypeStruct((M // tm, D_ATTN, tm), F32)] * 2
        out_specs += [pl.BlockSpec((None, D_ATTN, tm), lambda i: (i, 0, 0))] * 2
    stage = (W_STAGES, D_MODEL, D_MODEL)
    blocks = (_nbytes((D_MODEL, D_IN), BF16) + _nbytes(stage, F32) + 2 * _nbytes((tm, D_MODEL), F32)
              + 2 * _nbytes((tm, D_IN), BF16) + (4 * _nbytes((tm, D_ATTN), F32) if emit_kv else 0))
    return pl.pallas_call(
        _inproj_kernel,
        out_shape=out_shape,
        grid=(M // tm,),
        in_specs=[pl.BlockSpec((tm, D_MODEL), lambda i: (i, 0)),
                  pl.BlockSpec((None, 1, 3 * D_MODEL), lambda i: (mod_index(i), 0, 0)),
                  pl.BlockSpec(memory_space=pl.ANY)],
        out_specs=out_specs,
        scratch_shapes=[pltpu.VMEM((D_MODEL, D_IN), BF16), pltpu.VMEM(stage, F32),
                        pltpu.SemaphoreType.DMA((W_STAGES,))],
        compiler_params=pltpu.CompilerParams(
            dimension_semantics=("arbitrary",),
            vmem_limit_bytes=_vmem_limit(blocks, 4 * _nbytes((tm, D_MODEL), F32))),
        name="inproj",
    )(x2d, mod3, w_in)


def _hyena_kernel(vh_ref, x1_ref, x0_ref, gh_ref, cwv_ref, cw1_ref, cw0_ref, cbv_ref, cb1_ref, cb0_ref,
                  f_ref, g_ref, fp_ref, fqa_ref, fpb_ref, o_ref):
    L = fp_ref.shape[0]
    slab = jax.lax.broadcasted_iota(jnp.int32, (SUBLANES, 1), 0)

    def short_conv(x_ref, rows, w_ref, b_ref):
        x = x_ref[rows, :].astype(F32)
        prev = pltpu.roll(x, 1, 0)
        nxt = pltpu.roll(x, L - 1, 0)
        prev = jnp.concatenate([jnp.where(slab == 0, 0.0, prev[:SUBLANES]), prev[SUBLANES:]], axis=0)
        nxt = jnp.concatenate([nxt[:-SUBLANES], jnp.where(slab == SUBLANES - 1, 0.0, nxt[-SUBLANES:])], axis=0)
        return prev * w_ref[0:1, :] + x * w_ref[1:2, :] + nxt * w_ref[2:3, :] + b_ref[...]

    for s in range(vh_ref.shape[0] // L):
        rows = slice(s * L, (s + 1) * L)
        u = short_conv(vh_ref, rows, cwv_ref, cbv_ref) * short_conv(x1_ref, rows, cw1_ref, cb1_ref)
        t = _dot(f_ref[...], u.astype(BF16))
        p = t[:L]
        q = t[L:]
        yp = p * fp_ref[...] - q * fqa_ref[...]
        yq = p * fqa_ref[...] + q * fpb_ref[...]
        y = _dot(g_ref[...], jnp.concatenate([yp.astype(BF16), yq.astype(BF16)], axis=0))
        y_h = y * short_conv(x0_ref, rows, cw0_ref, cb0_ref)
        o_ref[rows, :] = (y_h * gh_ref[rows, :].astype(F32)).astype(BF16)


def _hyena_branch(z, B, L, conv_w, conv_b, dft, spectra, *, tc=256, seqs_per_step):
    nblk = D_HYENA // tc
    tm = seqs_per_step * L
    zcol = lambda piece: pl.BlockSpec((tm, tc), lambda c, b: (b, piece * nblk + c))
    wcol = lambda rows, piece: pl.BlockSpec((rows, tc), lambda c, b: (0, piece * nblk + c))
    chan = lambda rows: pl.BlockSpec((rows, tc), lambda c, b: (0, c))
    const = lambda shape: pl.BlockSpec(shape, lambda c, b: (0, 0), pipeline_mode=pl.Buffered(1))
    blocks = (2 * _nbytes((2 * L, L), BF16) + 2 * 5 * _nbytes((tm, tc), BF16) + 2 * 3 * _nbytes((L, tc), F32))
    return pl.pallas_call(
        _hyena_kernel,
        out_shape=jax.ShapeDtypeStruct((B * L, D_HYENA), BF16),
        grid=(nblk, B // seqs_per_step),
        in_specs=[zcol(COL_VH), zcol(COL_X1), zcol(COL_X0), zcol(COL_GH),
                  wcol(3, 0), wcol(3, 1), wcol(3, 2), wcol(1, 0), wcol(1, 1), wcol(1, 2),
                  const((2 * L, L)), const((L, 2 * L)), chan(L), chan(L), chan(L)],
        out_specs=pl.BlockSpec((tm, tc), lambda c, b: (b, c)),
        compiler_params=pltpu.CompilerParams(
            dimension_semantics=("arbitrary", "arbitrary"),
            vmem_limit_bytes=_vmem_limit(blocks, 16 * _nbytes((tm, tc), F32))),
        name=f"hyena_{L}",
    )(z, z, z, z, conv_w, conv_w, conv_w, conv_b.reshape(1, -1), conv_b.reshape(1, -1), conv_b.reshape(1, -1),
      *dft, *spectra)


def _head_masks(width):
    lane = jax.lax.broadcasted_iota(jnp.int32, (1, width), 1)
    return [(lane >= h * HEAD_DIM) & (lane < (h + 1) * HEAD_DIM) for h in range(width // HEAD_DIM)]


def _qk(q, k):
    return jax.lax.dot_general(q, k, (((1,), (1,)), ((), ())), preferred_element_type=F32)


def _stack_heads(q, masks):
    return jnp.concatenate([jnp.where(msk, q, jnp.zeros_like(q)) for msk in masks], axis=0)


def _unstack_heads(o, masks):
    n = o.shape[0] // len(masks)
    out = o[:n]
    for h in range(1, len(masks)):
        out = jnp.where(masks[h], o[h * n:(h + 1) * n], out)
    return out


def _ctx_attn_kernel(q_ref, k_ref, v_ref, g_ref, o_ref):
    masks = _head_masks(CTX_HEAD_TILE)
    for t in range(D_ATTN // CTX_HEAD_TILE):
        cols = slice(t * CTX_HEAD_TILE, (t + 1) * CTX_HEAD_TILE)
        s = _qk(_stack_heads(q_ref[:, cols], masks), k_ref[:, cols])
        m = jnp.max(s, axis=-1, keepdims=True)
        p = jnp.exp2(s - m)
        l = jnp.sum(p, axis=-1, keepdims=True)
        out = _unstack_heads(_dot(p.astype(BF16), v_ref[:, cols]) / l, masks)
        o_ref[:, cols] = (out * g_ref[:, cols].astype(F32)).astype(BF16)


def _context_attention(z, B, L):
    zcol = lambda piece: pl.BlockSpec((L, D_ATTN), lambda b: (b, piece))
    return pl.pallas_call(
        _ctx_attn_kernel,
        out_shape=jax.ShapeDtypeStruct((B * L, D_ATTN), BF16),
        grid=(B,),
        in_specs=[zcol(COL_Q), zcol(COL_K), zcol(COL_V), zcol(COL_GA)],
        out_specs=pl.BlockSpec((L, D_ATTN), lambda b: (b, 0)),
        compiler_params=pltpu.CompilerParams(dimension_semantics=("arbitrary",)),
        name="ctx_attn",
    )(z, z, z, z)


def _key_window_start(qb):
    rows = 1024 // GRID_W
    r_first = qb * Q_BLOCK_ROWS
    r_last = r_first + Q_BLOCK_ROWS - 1
    lo = min(max(r_first - WIN_ROWS // 2, 0), rows - WIN_ROWS)
    hi = min(max(r_last - WIN_ROWS // 2, 0), rows - WIN_ROWS) + WIN_ROWS
    start = min(lo, rows - KEY_WIN_ROWS)
    start -= start % Q_BLOCK_ROWS
    assert start <= lo and hi <= start + KEY_WIN_ROWS
    return start


def _nbr_attn_kernel(q_ref, k0_ref, k1_ref, k2_ref, v0_ref, v1_ref, v2_ref, kc_ref, vc_ref, g_ref, tiles_ref,
                     o_ref, bias_ref, *, n_rows):
    masks = _head_masks(NBR_HEAD_TILE)
    n_heads = q_ref.shape[-1] // HEAD_DIM
    chain_tiles = [slice(c * NBR_HEAD_TILE, (c + 1) * NBR_HEAD_TILE) for c in range(q_ref.shape[-1] // NBR_HEAD_TILE)]
    k_refs = (k0_ref, k1_ref, k2_ref)
    v_refs = (v0_ref, v1_ref, v2_ref)
    key_blocks = KEY_WIN // Q_BLOCK

    def window_row_start(r):
        return min(max(r - WIN_ROWS // 2, 0), n_rows - WIN_ROWS)

    def build_bias(qb):
        k_start = _key_window_start(qb)
        used = set()
        for i in range(Q_BLOCK_ROWS):
            r = qb * Q_BLOCK_ROWS + i
            r0 = window_row_start(r)
            for j in range(KEY_WIN_ROWS):
                kr = k_start + j
                rows = slice(i * GRID_W, (i + 1) * GRID_W)
                cols = slice(j * GRID_W, (j + 1) * GRID_W)
                lanes = slice((j % RPB_COPIES) * GRID_W, (j % RPB_COPIES + 1) * GRID_W)
                inside = r0 <= kr < r0 + WIN_ROWS
                if inside:
                    used.add(j // Q_BLOCK_ROWS)
                for h in range(n_heads):
                    if inside:
                        bias_ref[h, rows, cols] = tiles_ref[h, kr - r + WIN_ROWS - 1, :, lanes]
                    else:
                        bias_ref[h, rows, cols] = jnp.full((GRID_W, GRID_W), NEG_INF, F32)
        return sorted(used)

    def attend(blocks):
        for c, lanes in enumerate(chain_tiles):
            heads = range(c * len(masks), (c + 1) * len(masks))
            bias = [jnp.concatenate([bias_ref[h, :, j * Q_BLOCK:(j + 1) * Q_BLOCK] for h in heads], axis=0)
                    for j in blocks]
            for b in range(q_ref.shape[0]):
                qs = _stack_heads(q_ref[b, :, lanes], masks)
                s = [_qk(qs, k_refs[j][b, :, lanes]) + bias_j for j, bias_j in zip(blocks, bias)]
                s.append(_dot(qs, kc_ref[b, lanes, :].astype(BF16)))
                m = jnp.max(functools.reduce(jnp.maximum, s), axis=-1, keepdims=True)
                p = [jnp.exp2(x - m) for x in s]
                l = jnp.sum(functools.reduce(jnp.add, p), axis=-1, keepdims=True)
                p = [x.astype(BF16) for x in p]
                v_win = jnp.concatenate([v_refs[j][b, :, lanes] for j in blocks], axis=0)
                o = _dot(jnp.concatenate(p[:-1], axis=1), v_win) + _qk(p[-1], vc_ref[b, lanes, :].astype(BF16))
                o_ref[b, :, lanes] = (_unstack_heads(o / l, masks) * g_ref[b, :, lanes].astype(F32)).astype(BF16)

    for qb in range(n_rows // Q_BLOCK_ROWS):
        @pl.when(pl.program_id(1) == qb)
        def _(qb=qb):
            blocks = build_bias(qb)
            assert blocks and all(0 <= j < key_blocks for j in blocks)
            attend(blocks)


def _neighbourhood_attention(z, B, L, cache_k, cache_v, tiles):
    n_rows = L // GRID_W
    n_qb = L // Q_BLOCK
    assert KEY_WIN == 3 * Q_BLOCK and all(_key_window_start(qb) % Q_BLOCK_ROWS == 0 for qb in range(n_qb))
    z3 = z.reshape(B, L, D_IN)
    n_ctx = cache_k.shape[1]
    kc = jnp.transpose(cache_k, (0, 2, 3, 1)).reshape(B, D_ATTN, n_ctx)
    vc = jnp.transpose(cache_v, (0, 2, 3, 1)).reshape(B, D_ATTN, n_ctx)
    width = NBR_BLOCK_LANES
    tiles_per_col = D_ATTN // width

    def win_block(qb):
        return jnp.where(qb < n_qb // 2, 0, (n_rows - KEY_WIN_ROWS) // Q_BLOCK_ROWS)

    assert [_key_window_start(qb) // Q_BLOCK_ROWS for qb in range(n_qb)] == \
        [0 if qb < n_qb // 2 else (n_rows - KEY_WIN_ROWS) // Q_BLOCK_ROWS for qb in range(n_qb)]
    qspec = lambda piece: pl.BlockSpec((B, Q_BLOCK, width), lambda t, qb: (0, qb, piece * tiles_per_col + t))
    kspec = lambda piece, j: pl.BlockSpec((B, Q_BLOCK, width),
                                          lambda t, qb: (0, win_block(qb) + j, piece * tiles_per_col + t))
    cspec = pl.BlockSpec((B, width, n_ctx), lambda t, qb: (0, t, 0))
    out = pl.pallas_call(
        functools.partial(_nbr_attn_kernel, n_rows=n_rows),
        out_shape=jax.ShapeDtypeStruct((B, L, D_ATTN), BF16),
        grid=(tiles_per_col, n_qb),
        in_specs=[qspec(COL_Q), kspec(COL_K, 0), kspec(COL_K, 1), kspec(COL_K, 2),
                  kspec(COL_V, 0), kspec(COL_V, 1), kspec(COL_V, 2), cspec, cspec, qspec(COL_GA),
                  pl.BlockSpec((width // HEAD_DIM, 2 * WIN_ROWS - 1, GRID_W, LANES), lambda t, qb: (t, 0, 0, 0))],
        out_specs=pl.BlockSpec((B, Q_BLOCK, width), lambda t, qb: (0, qb, t)),
        scratch_shapes=[pltpu.VMEM((width // HEAD_DIM, Q_BLOCK, KEY_WIN), F32)],
        compiler_params=pltpu.CompilerParams(dimension_semantics=("arbitrary", "arbitrary")),
        name="nbr_attn",
    )(z3, z3, z3, z3, z3, z3, z3, kc, vc, z3, tiles)
    return out.reshape(B * L, D_ATTN)


def _merge_kernel(yh_ref, ya_ref, mh_ref, ma_ref, x_ref, mod_ref, wbh_ref, wba_ref, wout_ref, lng_ref, lnb_ref,
                  o_ref, *, alpha):
    p_h = _dot(yh_ref[...], wbh_ref[...])
    p_a = _dot(ya_ref[...], wba_ref[...])
    m_h = mh_ref[...].astype(F32)
    m_a = ma_ref[...].astype(F32)
    out = _dot((m_h * p_h + m_a * p_a).astype(BF16), wout_ref[...])
    gate = mod_ref[:, 2 * D_MODEL:3 * D_MODEL]
    v = alpha * x_ref[...] + gate * out
    mu = jnp.mean(v, axis=-1, keepdims=True)
    d = v - mu
    var = jnp.mean(d * d, axis=-1, keepdims=True)
    o_ref[...] = d * jax.lax.rsqrt(var + LN_EPS) * lng_ref[...] + lnb_ref[...]


def _merge(yh, ya, z, x2d, mod3, mod_index, w_bh, w_ba, w_out, ln_g, ln_b, *, alpha, tm):
    M = x2d.shape[0]
    tok = lambda width, col: pl.BlockSpec((tm, width), lambda i: (i, col))
    const = lambda shape: pl.BlockSpec(shape, lambda i: (0, 0))
    return pl.pallas_call(
        functools.partial(_merge_kernel, alpha=alpha),
        out_shape=jax.ShapeDtypeStruct((M, D_MODEL), F32),
        grid=(M // tm,),
        in_specs=[tok(D_HYENA, 0), tok(D_ATTN, 0), tok(D_MODEL, COL_MH), tok(D_MODEL, COL_MA), tok(D_MODEL, 0),
                  pl.BlockSpec((None, 1, 3 * D_MODEL), lambda i: (mod_index(i), 0, 0)),
                  const((D_HYENA, D_MODEL)), const((D_ATTN, D_MODEL)), const((D_MODEL, D_MODEL)),
                  const((1, D_MODEL)), const((1, D_MODEL))],
        out_specs=pl.BlockSpec((tm, D_MODEL), lambda i: (i, 0)),
        compiler_params=pltpu.CompilerParams(
            dimension_semantics=("arbitrary",),
            vmem_limit_bytes=_vmem_limit(
                2 * (4 * _nbytes((tm, D_MODEL), BF16) + 2 * _nbytes((tm, D_MODEL), F32)
                     + 3 * _nbytes((D_MODEL, D_MODEL), BF16)),
                8 * _nbytes((tm, D_MODEL), F32))),
        name="merge",
    )(yh, ya, z, z, x2d, mod3, w_bh, w_ba, w_out, ln_g.reshape(1, -1), ln_b.reshape(1, -1))


def _layer(x, mod3, cond_row, ctx_kv, p, filt, tiles, *, alpha):
    B, L, _ = x.shape
    x2d = x.reshape(B * L, D_MODEL)
    emit_kv = ctx_kv is None
    assert not emit_kv or IN_TM == L
    outs = _in_projection(x2d, mod3, lambda i: cond_row(i * IN_TM // L), p["w_in"], emit_kv=emit_kv, tm=IN_TM)
    z = outs[0]
    dft = _dft_matrices(L)
    spectra = _filter_spectra(L, dft[0], p["hyena_d"], *filt)
    yh = _hyena_branch(z, B, L, p["conv_w"], p["conv_b"], dft, spectra,
                       seqs_per_step=max(1, HYENA_ROWS_PER_STEP // L))
    if ctx_kv is None:
        ya = _context_attention(z, B, L)
    else:
        ya = _neighbourhood_attention(z, B, L, ctx_kv[0], ctx_kv[1], tiles)
    y = _merge(yh, ya, z, x2d, mod3, lambda i: cond_row(i * MERGE_TM // L), p["w_bh"], p["w_ba"], p["w_out"],
               p["ln_g"], p["ln_b"], alpha=alpha, tm=MERGE_TM)
    return y.reshape(B, L, D_MODEL), outs[1:]


def kernel(x_prompt, x_sample, c, cache_k, cache_v, c_ctx, w_ada, b_ada, w_in, conv_w, conv_b, filt_w1, filt_b1,
           filt_w2, filt_b2, filt_w3, filt_freq, hyena_d, rpb, w_bh, w_ba, w_out, ln_g, ln_b):
    depth = w_in.shape[0]
    alpha = (2.0 * depth) ** 0.25
    n_lat, lat_len = x_sample.shape[0], x_sample.shape[1]
    n_ctx, ctx_len = x_prompt.shape[0], x_prompt.shape[1]
    ctx_row = n_lat
    cond = jnp.zeros((8, D_MODEL), F32).at[:n_lat].set(c).at[ctx_row].set(c_ctx)

    xp, xs = x_prompt, x_sample
    new_k, new_v = [], []
    for l in range(depth):
        p = {"w_in": w_in[l], "conv_w": conv_w[l], "conv_b": conv_b[l], "hyena_d": hyena_d[l],
             "w_bh": w_bh[l].astype(BF16), "w_ba": w_ba[l].astype(BF16), "w_out": w_out[l].astype(BF16),
             "ln_g": ln_g[l], "ln_b": ln_b[l]}
        mod3 = _modulation(cond, w_ada[l], b_ada[l]).reshape(8, 1, 3 * D_MODEL)
        filt = (filt_w1[l], filt_b1[l], filt_w2[l], filt_b2[l], filt_w3[l], filt_freq[l])
        tiles = _rpb_tiles(rpb[l])
        xp, (k_ctx, v_ctx) = _layer(xp, mod3, lambda b: ctx_row, None, p, filt, None, alpha=alpha)
        new_k.append(jnp.transpose(k_ctx.reshape(n_ctx, N_HEADS, HEAD_DIM, ctx_len), (0, 3, 1, 2)))
        new_v.append(jnp.transpose(v_ctx.reshape(n_ctx, N_HEADS, HEAD_DIM, ctx_len), (0, 3, 1, 2)))
        xs, _ = _layer(xs, mod3, lambda b: b, (cache_k[:, l], cache_v[:, l]), p, filt, tiles, alpha=alpha)
    return xp, xs, jnp.stack(new_k, axis=1), jnp.stack(new_v, axis=1)
```

```python
import functools
import math

import jax
import jax.numpy as jnp
import numpy as np
from jax.experimental import pallas as pl
from jax.experimental.pallas import tpu as pltpu

F32 = jnp.float32
BF16 = jnp.bfloat16

D_MODEL = 1024
D_HYENA = 1024
N_HEADS = 16
HEAD_DIM = 64
D_ATTN = N_HEADS * HEAD_DIM
D_IN = 4 * D_HYENA + 4 * D_ATTN + 2 * D_MODEL
GRID_W = 64
WIN_ROWS = 8
WIN_COLS = 16
FILTER_EMB = 33
FILTER_BANDS = (FILTER_EMB - 1) // 2
FILTER_HIDDEN = 64
DECAY_TARGET = 1e-2
MIN_DECAY = math.log(DECAY_TARGET) / 1.5
MAX_DECAY = math.log(DECAY_TARGET) / 0.3
DECAY_SHIFT = 0.05
LN_EPS = 1e-5
NEG_INF = -1e30

COL_VH, COL_X1, COL_X0, COL_GH, COL_Q, COL_K, COL_V, COL_GA, COL_MH, COL_MA = range(10)

LANES = 128
SUBLANES = 8
CTX_HEAD_TILE = 128
NBR_HEAD_TILE = 128
NBR_BLOCK_LANES = 128
RPB_COPIES = LANES // GRID_W
V7X_VMEM_BYTES = 64 * 1024 * 1024
ATTN_SCALE = HEAD_DIM ** -0.5
LOG2E = math.log2(math.e)

Q_BLOCK = 256
Q_BLOCK_ROWS = Q_BLOCK // GRID_W
KEY_WIN_ROWS = 12
KEY_WIN = KEY_WIN_ROWS * GRID_W

IN_TM = 256
W_STAGES = 2
MERGE_TM = 512
HYENA_ROWS_PER_STEP = 4096


def _vmem_limit(block_bytes, temp_bytes):
    need = int(block_bytes + temp_bytes)
    return min(max(need, 16 * 1024 * 1024), V7X_VMEM_BYTES - 8 * 1024 * 1024)


def _nbytes(shape, dtype):
    return int(np.prod(shape)) * jnp.dtype(dtype).itemsize


def _silu(x):
    return x * jax.nn.sigmoid(x)


def _split_bf16(a):
    hi = a.astype(BF16)
    lo = (a - hi.astype(F32)).astype(BF16)
    return hi, lo


def _dot(a, b):
    return jnp.dot(a, b, preferred_element_type=F32)


def _dot_split(a_hi, a_lo, b):
    b_hi, b_lo = _split_bf16(b)
    return _dot(a_hi, b_hi) + (_dot(a_hi, b_lo) + _dot(a_lo, b_hi))


@functools.lru_cache(maxsize=None)
def _dft_constants(L):
    n = 2 * L
    k = np.arange(L, dtype=np.float64)[:, None]
    s = np.arange(L, dtype=np.float64)[None, :]
    ang = 2.0 * np.pi * k * s / n
    nyq = np.cos(np.pi * np.arange(L, dtype=np.float64))
    C = np.cos(ang)
    S = np.sin(ang)
    S[0, :] = nyq
    F = np.concatenate([C, S], axis=0)
    w = np.full((L,), 2.0)
    w[0] = 1.0
    Gc = (C * w[:, None]).T / n
    Ss = 2.0 * np.sin(ang)
    Ss[0, :] = nyq
    Gs = Ss.T / n
    G = np.concatenate([Gc, Gs], axis=1)
    return F.astype(np.float32), G.astype(np.float32)


def _dft_matrices(L):
    F, G = _dft_constants(L)
    return jnp.asarray(F).astype(BF16), jnp.asarray(G).astype(BF16)


@functools.lru_cache(maxsize=None)
def _filter_constants(L):
    t = np.linspace(0.0, 1.0, L, dtype=np.float32)[:, None]
    bands = np.linspace(1e-4, FILTER_BANDS - 1, FILTER_BANDS, dtype=np.float32)[None]
    w = (2.0 * math.pi * np.arange(L, dtype=np.float32)[:, None] / L).astype(np.float32)
    z = np.concatenate([t, np.cos(bands * w), -np.sin(bands * w)], axis=-1).astype(np.float32)
    z_pad = np.zeros((L, LANES), np.float32)
    z_pad[:, :FILTER_EMB] = z
    deltas = np.linspace(MIN_DECAY, MAX_DECAY, D_HYENA, dtype=np.float32)
    decay = (np.exp(-t * np.abs(deltas)) + np.float32(DECAY_SHIFT)).astype(np.float32)
    return z_pad, decay


@functools.lru_cache(maxsize=None)
def _rpb_constants():
    c = np.arange(GRID_W)[:, None]
    kc = np.arange(GRID_W)[None, :]
    c0 = np.clip(c - WIN_COLS // 2, 0, GRID_W - WIN_COLS)
    in_win = (kc >= c0) & (kc < c0 + WIN_COLS)
    idx = np.clip(kc - c + WIN_COLS - 1, 0, 2 * WIN_COLS - 2)
    onehot = np.zeros((LANES, GRID_W, LANES), np.float32)
    for half in range(RPB_COPIES):
        lanes = half * GRID_W + np.arange(GRID_W)
        onehot[idx, c, lanes[None, :]] = 1.0
    mask = np.concatenate([in_win, in_win], axis=1).astype(np.float32)
    return onehot.reshape(LANES, GRID_W * LANES).astype(BF16), mask.reshape(1, GRID_W * LANES)


def _mod_kernel(cond_ref, w_ref, b_ref, o_ref):
    s = _silu(cond_ref[...]).astype(BF16)
    o_ref[...] = _dot(s, w_ref[...].astype(BF16)) + b_ref[...]


def _modulation(cond, w_ada, b_ada):
    rows = cond.shape[0]
    return pl.pallas_call(
        _mod_kernel,
        out_shape=jax.ShapeDtypeStruct((rows, 3 * D_MODEL), F32),
        grid=(3,),
        in_specs=[pl.BlockSpec((rows, D_MODEL), lambda j: (0, 0)),
                  pl.BlockSpec((D_MODEL, D_MODEL), lambda j: (0, j)),
                  pl.BlockSpec((1, D_MODEL), lambda j: (0, j))],
        out_specs=pl.BlockSpec((rows, D_MODEL), lambda j: (0, j)),
        name="mod",
    )(cond, w_ada, b_ada.reshape(1, -1))


def _spectra_kernel(z_ref, w1_ref, b1_ref, w2_ref, b2_ref, w3f_ref, w3b_ref, freq_ref, decay_ref, d_ref, f_ref,
                    fp_ref, fqa_ref, fpb_ref, hf_ref):
    L = decay_ref.shape[0]

    @pl.when(pl.program_id(0) == 0)
    def _():
        z_hi, z_lo = _split_bf16(z_ref[...])
        hf = jnp.sin(freq_ref[0:1, :] * (_dot_split(z_hi, z_lo, w1_ref[...]) + b1_ref[...]))
        h_hi, h_lo = _split_bf16(hf)
        hf_ref[...] = jnp.sin(freq_ref[1:2, :] * (_dot_split(h_hi, h_lo, w2_ref[...]) + b2_ref[...]))

    h_hi, h_lo = _split_bf16(hf_ref[...])
    decay = decay_ref[...]
    h_fwd = _dot_split(h_hi, h_lo, w3f_ref[...]) * decay
    h_bwd = _dot_split(h_hi, h_lo, w3b_ref[...]) * decay
    row0 = jax.lax.broadcasted_iota(jnp.int32, (L, 1), 0) == 0
    h_bwd = jnp.where(row0, 0.0, h_bwd)
    fsum = (h_fwd + h_bwd).astype(BF16)
    fdif = (h_fwd - h_bwd).astype(BF16)
    skip = d_ref[...]
    fp = _dot(f_ref[0:L, :], fsum) + skip
    fq = _dot(f_ref[L:2 * L, :], fdif)
    nyq = _dot(f_ref[L:L + 2 * SUBLANES, :], fsum)[0:1, :] + skip
    fp_ref[...] = fp
    fqa_ref[...] = jnp.where(row0, 0.0, fq)
    fpb_ref[...] = jnp.where(row0, nyq, fp)


def _filter_spectra(L, f_mat, hyena_d, w1, b1, w2, b2, w3, freq):
    tc = 256
    z_emb, decay = _filter_constants(L)
    w1p = jnp.zeros((LANES, FILTER_HIDDEN), F32).at[:FILTER_EMB].set(w1)
    nblk = D_HYENA // tc
    const = lambda shape: pl.BlockSpec(shape, lambda c: (0, 0), pipeline_mode=pl.Buffered(1))
    out_spec = pl.BlockSpec((L, tc), lambda c: (0, c))
    blocks = (_nbytes((2 * L, L), BF16) + _nbytes((L, LANES), F32) + 2 * 4 * _nbytes((L, tc), F32))
    return pl.pallas_call(
        _spectra_kernel,
        out_shape=[jax.ShapeDtypeStruct((L, D_HYENA), F32)] * 3,
        grid=(nblk,),
        in_specs=[const((L, LANES)),
                  const((LANES, FILTER_HIDDEN)), const((1, FILTER_HIDDEN)),
                  const((FILTER_HIDDEN, FILTER_HIDDEN)), const((1, FILTER_HIDDEN)),
                  pl.BlockSpec((FILTER_HIDDEN, tc), lambda c: (0, c)),
                  pl.BlockSpec((FILTER_HIDDEN, tc), lambda c: (0, nblk + c)),
                  const((2, FILTER_HIDDEN)),
                  pl.BlockSpec((L, tc), lambda c: (0, c)),
                  pl.BlockSpec((1, tc), lambda c: (0, c)),
                  const((2 * L, L))],
        out_specs=[out_spec] * 3,
        scratch_shapes=[pltpu.VMEM((L, FILTER_HIDDEN), F32)],
        compiler_params=pltpu.CompilerParams(
            dimension_semantics=("arbitrary",),
            vmem_limit_bytes=_vmem_limit(blocks, 12 * _nbytes((L, tc), F32))),
        name=f"spectra_{L}",
    )(z_emb, w1p, b1.reshape(1, -1), w2, b2.reshape(1, -1), w3, w3, freq, decay, hyena_d.reshape(1, -1), f_mat)


def _rpb_kernel(r_ref, onehot_ref, mask_ref, o_ref):
    r = r_ref[...]
    r_hi = r.astype(BF16)
    rem = r - r_hi.astype(F32)
    r_mid = rem.astype(BF16)
    r_lo = (rem - r_mid.astype(F32)).astype(BF16)
    oh = onehot_ref[...]
    t = _dot(r_hi, oh) + (_dot(r_mid, oh) + _dot(r_lo, oh))
    t = jnp.where(mask_ref[...] > 0.0, t * LOG2E, NEG_INF)
    for c in range(o_ref.shape[1]):
        o_ref[:, c, :] = t[:, c * LANES:(c + 1) * LANES]


def _rpb_tiles(rpb):
    n_dr = 2 * WIN_ROWS - 1
    rows = N_HEADS * n_dr
    onehot, mask = _rpb_constants()
    r = jnp.zeros((rows, LANES), F32).at[:, :2 * WIN_COLS - 1].set(rpb.reshape(rows, 2 * WIN_COLS - 1))
    cols_per_step = 16
    tn = cols_per_step * LANES
    out = pl.pallas_call(
        _rpb_kernel,
        out_shape=jax.ShapeDtypeStruct((rows, GRID_W, LANES), F32),
        grid=(GRID_W // cols_per_step,),
        in_specs=[pl.BlockSpec((rows, LANES), lambda j: (0, 0)),
                  pl.BlockSpec((LANES, tn), lambda j: (0, j)),
                  pl.BlockSpec((1, tn), lambda j: (0, j))],
        out_specs=pl.BlockSpec((rows, cols_per_step, LANES), lambda j: (0, j, 0)),
        name="rpb_tiles",
    )(r, onehot, mask)
    return out.reshape(N_HEADS, n_dr, GRID_W, LANES)


def _inproj_kernel(x_ref, mod_ref, w_hbm, z_ref, *rest):
    *kv_refs, w_ref, stage_ref, sem = rest
    n_col = D_IN // D_MODEL
    n_stage = stage_ref.shape[0]
    epilogue = {COL_GH: _silu, COL_GA: _silu, COL_MH: jax.nn.sigmoid, COL_MA: jax.nn.sigmoid,
                COL_Q: lambda acc: acc * (ATTN_SCALE * LOG2E)}

    def columns(col):
        return slice(col * D_MODEL, (col + 1) * D_MODEL)

    def fetch(col):
        return pltpu.make_async_copy(w_hbm.at[:, columns(col)], stage_ref.at[col % n_stage], sem.at[col % n_stage])

    def tile(stream_weights):
        if stream_weights:
            for col in range(min(n_stage, n_col)):
                fetch(col).start()
        shift = mod_ref[:, 0:D_MODEL]
        scale = mod_ref[:, D_MODEL:2 * D_MODEL]
        h = (x_ref[...] * (1.0 + scale) + shift).astype(BF16)
        for col in range(n_col):
            if stream_weights:
                fetch(col).wait()
                w_ref[:, columns(col)] = stage_ref[col % n_stage].astype(BF16)
                if col + n_stage < n_col:
                    fetch(col + n_stage).start()
            acc = _dot(h, w_ref[:, columns(col)])
            z_ref[:, columns(col)] = epilogue.get(col, lambda acc: acc)(acc).astype(BF16)
            if kv_refs and col == COL_K:
                kv_refs[0][...] = acc.T
            if kv_refs and col == COL_V:
                kv_refs[1][...] = acc.T

    pl.when(pl.program_id(0) == 0)(functools.partial(tile, True))
    pl.when(pl.program_id(0) != 0)(functools.partial(tile, False))


def _in_projection(x2d, mod3, mod_index, w_in, *, emit_kv, tm):
    M = x2d.shape[0]
    out_shape = [jax.ShapeDtypeStruct((M, D_IN), BF16)]
    out_specs = [pl.BlockSpec((tm, D_IN), lambda i: (i, 0))]
    if emit_kv:
        out_shape += [jax.ShapeDtypeStruct((M // tm, D_ATTN, tm), F32)] * 2
        out_specs += [pl.BlockSpec((None, D_ATTN, tm), lambda i: (i, 0, 0))] * 2
    stage = (W_STAGES, D_MODEL, D_MODEL)
    blocks = (_nbytes((D_MODEL, D_IN), BF16) + _nbytes(stage, F32) + 2 * _nbytes((tm, D_MODEL), F32)
              + 2 * _nbytes((tm, D_IN), BF16) + (4 * _nbytes((tm, D_ATTN), F32) if emit_kv else 0))
    return pl.pallas_call(
        _inproj_kernel,
        out_shape=out_shape,
        grid=(M // tm,),
        in_specs=[pl.BlockSpec((tm, D_MODEL), lambda i: (i, 0)),
                  pl.BlockSpec((None, 1, 3 * D_MODEL), lambda i: (mod_index(i), 0, 0)),
                  pl.BlockSpec(memory_space=pl.ANY)],
        out_specs=out_specs,
        scratch_shapes=[pltpu.VMEM((D_MODEL, D_IN), BF16), pltpu.VMEM(stage, F32),
                        pltpu.SemaphoreType.DMA((W_STAGES,))],
        compiler_params=pltpu.CompilerParams(
            dimension_semantics=("arbitrary",),
            vmem_limit_bytes=_vmem_limit(blocks, 4 * _nbytes((tm, D_MODEL), F32))),
        name="inproj",
    )(x2d, mod3, w_in)


def _hyena_kernel(vh_ref, x1_ref, x0_ref, gh_ref, cwv_ref, cw1_ref, cw0_ref, cbv_ref, cb1_ref, cb0_ref,
                  f_ref, g_ref, fp_ref, fqa_ref, fpb_ref, o_ref):
    L = fp_ref.shape[0]
    slab = jax.lax.broadcasted_iota(jnp.int32, (SUBLANES, 1), 0)

    def short_conv(x_ref, rows, w_ref, b_ref):
        x = x_ref[rows, :].astype(F32)
        prev = pltpu.roll(x, 1, 0)
        nxt = pltpu.roll(x, L - 1, 0)
        prev = jnp.concatenate([jnp.where(slab == 0, 0.0, prev[:SUBLANES]), prev[SUBLANES:]], axis=0)
        nxt = jnp.concatenate([nxt[:-SUBLANES], jnp.where(slab == SUBLANES - 1, 0.0, nxt[-SUBLANES:])], axis=0)
        return prev * w_ref[0:1, :] + x * w_ref[1:2, :] + nxt * w_ref[2:3, :] + b_ref[...]

    for s in range(vh_ref.shape[0] // L):
        rows = slice(s * L, (s + 1) * L)
        u = short_conv(vh_ref, rows, cwv_ref, cbv_ref) * short_conv(x1_ref, rows, cw1_ref, cb1_ref)
        t = _dot(f_ref[...], u.astype(BF16))
        p = t[:L]
        q = t[L:]
        yp = p * fp_ref[...] - q * fqa_ref[...]
        yq = p * fqa_ref[...] + q * fpb_ref[...]
        y = _dot(g_ref[...], jnp.concatenate([yp.astype(BF16), yq.astype(BF16)], axis=0))
        y_h = y * short_conv(x0_ref, rows, cw0_ref, cb0_ref)
        o_ref[rows, :] = (y_h * gh_ref[rows, :].astype(F32)).astype(BF16)


def _hyena_branch(z, B, L, conv_w, conv_b, dft, spectra, *, tc=256, seqs_per_step):
    nblk = D_HYENA // tc
    tm = seqs_per_step * L
    zcol = lambda piece: pl.BlockSpec((tm, tc), lambda c, b: (b, piece * nblk + c))
    wcol = lambda rows, piece: pl.BlockSpec((rows, tc), lambda c, b: (0, piece * nblk + c))
    chan = lambda rows: pl.BlockSpec((rows, tc), lambda c, b: (0, c))
    const = lambda shape: pl.BlockSpec(shape, lambda c, b: (0, 0), pipeline_mode=pl.Buffered(1))
    blocks = (2 * _nbytes((2 * L, L), BF16) + 2 * 5 * _nbytes((tm, tc), BF16) + 2 * 3 * _nbytes((L, tc), F32))
    return pl.pallas_call(
        _hyena_kernel,
        out_shape=jax.ShapeDtypeStruct((B * L, D_HYENA), BF16),
        grid=(nblk, B // seqs_per_step),
        in_specs=[zcol(COL_VH), zcol(COL_X1), zcol(COL_X0), zcol(COL_GH),
                  wcol(3, 0), wcol(3, 1), wcol(3, 2), wcol(1, 0), wcol(1, 1), wcol(1, 2),
                  const((2 * L, L)), const((L, 2 * L)), chan(L), chan(L), chan(L)],
        out_specs=pl.BlockSpec((tm, tc), lambda c, b: (b, c)),
        compiler_params=pltpu.CompilerParams(
            dimension_semantics=("arbitrary", "arbitrary"),
            vmem_limit_bytes=_vmem_limit(blocks, 16 * _nbytes((tm, tc), F32))),
        name=f"hyena_{L}",
    )(z, z, z, z, conv_w, conv_w, conv_w, conv_b.reshape(1, -1), conv_b.reshape(1, -1), conv_b.reshape(1, -1),
      *dft, *spectra)


def _head_masks(width):
    lane = jax.lax.broadcasted_iota(jnp.int32, (1, width), 1)
    return [(lane >= h * HEAD_DIM) & (lane < (h + 1) * HEAD_DIM) for h in range(width // HEAD_DIM)]


def _qk(q, k):
    return jax.lax.dot_general(q, k, (((1,), (1,)), ((), ())), preferred_element_type=F32)


def _stack_heads(q, masks):
    return jnp.concatenate([jnp.where(msk, q, jnp.zeros_like(q)) for msk in masks], axis=0)


def _unstack_heads(o, masks):
    n = o.shape[0] // len(masks)
    out = o[:n]
    for h in range(1, len(masks)):
        out = jnp.where(masks[h], o[h * n:(h + 1) * n], out)
    return out


def _ctx_attn_kernel(q_ref, k_ref, v_ref, g_ref, o_ref):
    masks = _head_masks(CTX_HEAD_TILE)
    for t in range(D_ATTN // CTX_HEAD_TILE):
        cols = slice(t * CTX_HEAD_TILE, (t + 1) * CTX_HEAD_TILE)
        s = _qk(_stack_heads(q_ref[:, cols], masks), k_ref[:, cols])
        m = jnp.max(s, axis=-1, keepdims=True)
        p = jnp.exp2(s - m)
        l = jnp.sum(p, axis=-1, keepdims=True)
        out = _unstack_heads(_dot(p.astype(BF16), v_ref[:, cols]) / l, masks)
        o_ref[:, cols] = (out * g_ref[:, cols].astype(F32)).astype(BF16)


def _context_attention(z, B, L):
    zcol = lambda piece: pl.BlockSpec((L, D_ATTN), lambda b: (b, piece))
    return pl.pallas_call(
        _ctx_attn_kernel,
        out_shape=jax.ShapeDtypeStruct((B * L, D_ATTN), BF16),
        grid=(B,),
        in_specs=[zcol(COL_Q), zcol(COL_K), zcol(COL_V), zcol(COL_GA)],
        out_specs=pl.BlockSpec((L, D_ATTN), lambda b: (b, 0)),
        compiler_params=pltpu.CompilerParams(dimension_semantics=("arbitrary",)),
        name="ctx_attn",
    )(z, z, z, z)


def _key_window_start(qb):
    rows = 1024 // GRID_W
    r_first = qb * Q_BLOCK_ROWS
    r_last = r_first + Q_BLOCK_ROWS - 1
    lo = min(max(r_first - WIN_ROWS // 2, 0), rows - WIN_ROWS)
    hi = min(max(r_last - WIN_ROWS // 2, 0), rows - WIN_ROWS) + WIN_ROWS
    start = min(lo, rows - KEY_WIN_ROWS)
    start -= start % Q_BLOCK_ROWS
    assert start <= lo and hi <= start + KEY_WIN_ROWS
    return start


def _nbr_attn_kernel(q_ref, k0_ref, k1_ref, k2_ref, v0_ref, v1_ref, v2_ref, kc_ref, vc_ref, g_ref, tiles_ref,
                     o_ref, bias_ref, *, n_rows):
    masks = _head_masks(NBR_HEAD_TILE)
    n_heads = q_ref.shape[-1] // HEAD_DIM
    chain_tiles = [slice(c * NBR_HEAD_TILE, (c + 1) * NBR_HEAD_TILE) for c in range(q_ref.shape[-1] // NBR_HEAD_TILE)]
    k_refs = (k0_ref, k1_ref, k2_ref)
    v_refs = (v0_ref, v1_ref, v2_ref)
    key_blocks = KEY_WIN // Q_BLOCK

    def window_row_start(r):
        return min(max(r - WIN_ROWS // 2, 0), n_rows - WIN_ROWS)

    def build_bias(qb):
        k_start = _key_window_start(qb)
        used = set()
        for i in range(Q_BLOCK_ROWS):
            r = qb * Q_BLOCK_ROWS + i
            r0 = window_row_start(r)
            for j in range(KEY_WIN_ROWS):
                kr = k_start + j
                rows = slice(i * GRID_W, (i + 1) * GRID_W)
                cols = slice(j * GRID_W, (j + 1) * GRID_W)
                lanes = slice((j % RPB_COPIES) * GRID_W, (j % RPB_COPIES + 1) * GRID_W)
                inside = r0 <= kr < r0 + WIN_ROWS
                if inside:
                    used.add(j // Q_BLOCK_ROWS)
                for h in range(n_heads):
                    if inside:
                        bias_ref[h, rows, cols] = tiles_ref[h, kr - r + WIN_ROWS - 1, :, lanes]
                    else:
                        bias_ref[h, rows, cols] = jnp.full((GRID_W, GRID_W), NEG_INF, F32)
        return sorted(used)

    def attend(blocks):
        for c, lanes in enumerate(chain_tiles):
            heads = range(c * len(masks), (c + 1) * len(masks))
            bias = [jnp.concatenate([bias_ref[h, :, j * Q_BLOCK:(j + 1) * Q_BLOCK] for h in heads], axis=0)
                    for j in blocks]
            for b in range(q_ref.shape[0]):
                qs = _stack_heads(q_ref[b, :, lanes], masks)
                s = [_qk(qs, k_refs[j][b, :, lanes]) + bias_j for j, bias_j in zip(blocks, bias)]
                s.append(_dot(qs, kc_ref[b, lanes, :].astype(BF16)))
                m = jnp.max(functools.reduce(jnp.maximum, s), axis=-1, keepdims=True)
                p = [jnp.exp2(x - m) for x in s]
                l = jnp.sum(functools.reduce(jnp.add, p), axis=-1, keepdims=True)
                p = [x.astype(BF16) for x in p]
                v_win = jnp.concatenate([v_refs[j][b, :, lanes] for j in blocks], axis=0)
                o = _dot(jnp.concatenate(p[:-1], axis=1), v_win) + _qk(p[-1], vc_ref[b, lanes, :].astype(BF16))
                o_ref[b, :, lanes] = (_unstack_heads(o / l, masks) * g_ref[b, :, lanes].astype(F32)).astype(BF16)

    for qb in range(n_rows // Q_BLOCK_ROWS):
        @pl.when(pl.program_id(0) == qb)
        def _(qb=qb):
            blocks = build_bias(qb)
            assert blocks and all(0 <= j < key_blocks for j in blocks)
            attend(blocks)


def _neighbourhood_attention(z, B, L, cache_k, cache_v, tiles):
    n_rows = L // GRID_W
    n_qb = L // Q_BLOCK
    assert KEY_WIN == 3 * Q_BLOCK and all(_key_window_start(qb) % Q_BLOCK_ROWS == 0 for qb in range(n_qb))
    z3 = z.reshape(B, L, D_IN)
    n_ctx = cache_k.shape[1]
    kc = jnp.transpose(cache_k, (0, 2, 3, 1)).reshape(B, D_ATTN, n_ctx)
    vc = jnp.transpose(cache_v, (0, 2, 3, 1)).reshape(B, D_ATTN, n_ctx)
    width = NBR_BLOCK_LANES
    tiles_per_col = D_ATTN // width

    def win_block(qb):
        return jnp.where(qb < n_qb // 2, 0, (n_rows - KEY_WIN_ROWS) // Q_BLOCK_ROWS)

    assert [_key_window_start(qb) // Q_BLOCK_ROWS for qb in range(n_qb)] == \
        [0 if qb < n_qb // 2 else (n_rows - KEY_WIN_ROWS) // Q_BLOCK_ROWS for qb in range(n_qb)]
    qspec = lambda piece: pl.BlockSpec((B, Q_BLOCK, width), lambda qb, t: (0, qb, piece * tiles_per_col + t))
    kspec = lambda piece, j: pl.BlockSpec((B, Q_BLOCK, width),
                                          lambda qb, t: (0, win_block(qb) + j, piece * tiles_per_col + t))
    cspec = pl.BlockSpec((B, width, n_ctx), lambda qb, t: (0, t, 0))
    out = pl.pallas_call(
        functools.partial(_nbr_attn_kernel, n_rows=n_rows),
        out_shape=jax.ShapeDtypeStruct((B, L, D_ATTN), BF16),
        grid=(n_qb, tiles_per_col),
        in_specs=[qspec(COL_Q), kspec(COL_K, 0), kspec(COL_K, 1), kspec(COL_K, 2),
                  kspec(COL_V, 0), kspec(COL_V, 1), kspec(COL_V, 2), cspec, cspec, qspec(COL_GA),
                  pl.BlockSpec((width // HEAD_DIM, 2 * WIN_ROWS - 1, GRID_W, LANES), lambda qb, t: (t, 0, 0, 0))],
        out_specs=pl.BlockSpec((B, Q_BLOCK, width), lambda qb, t: (0, qb, t)),
        scratch_shapes=[pltpu.VMEM((width // HEAD_DIM, Q_BLOCK, KEY_WIN), F32)],
        compiler_params=pltpu.CompilerParams(dimension_semantics=("arbitrary", "arbitrary")),
        name="nbr_attn",
    )(z3, z3, z3, z3, z3, z3, z3, kc, vc, z3, tiles)
    return out.reshape(B * L, D_ATTN)


def _merge_kernel(yh_ref, ya_ref, mh_ref, ma_ref, x_ref, mod_ref, wbh_ref, wba_ref, wout_ref, lng_ref, lnb_ref,
                  o_ref, *, alpha):
    p_h = _dot(yh_ref[...], wbh_ref[...])
    p_a = _dot(ya_ref[...], wba_ref[...])
    m_h = mh_ref[...].astype(F32)
    m_a = ma_ref[...].astype(F32)
    out = _dot((m_h * p_h + m_a * p_a).astype(BF16), wout_ref[...])
    gate = mod_ref[:, 2 * D_MODEL:3 * D_MODEL]
    v = alpha * x_ref[...] + gate * out
    mu = jnp.mean(v, axis=-1, keepdims=True)
    d = v - mu
    var = jnp.mean(d * d, axis=-1, keepdims=True)
    o_ref[...] = d * jax.lax.rsqrt(var + LN_EPS) * lng_ref[...] + lnb_ref[...]


def _merge(yh, ya, z, x2d, mod3, mod_index, w_bh, w_ba, w_out, ln_g, ln_b, *, alpha, tm):
    M = x2d.shape[0]
    tok = lambda width, col: pl.BlockSpec((tm, width), lambda i: (i, col))
    const = lambda shape: pl.BlockSpec(shape, lambda i: (0, 0))
    return pl.pallas_call(
        functools.partial(_merge_kernel, alpha=alpha),
        out_shape=jax.ShapeDtypeStruct((M, D_MODEL), F32),
        grid=(M // tm,),
        in_specs=[tok(D_HYENA, 0), tok(D_ATTN, 0), tok(D_MODEL, COL_MH), tok(D_MODEL, COL_MA), tok(D_MODEL, 0),
                  pl.BlockSpec((None, 1, 3 * D_MODEL), lambda i: (mod_index(i), 0, 0)),
                  const((D_HYENA, D_MODEL)), const((D_ATTN, D_MODEL)), const((D_MODEL, D_MODEL)),
                  const((1, D_MODEL)), const((1, D_MODEL))],
        out_specs=pl.BlockSpec((tm, D_MODEL), lambda i: (i, 0)),
        compiler_params=pltpu.CompilerParams(
            dimension_semantics=("arbitrary",),
            vmem_limit_bytes=_vmem_limit(
                2 * (4 * _nbytes((tm, D_MODEL), BF16) + 2 * _nbytes((tm, D_MODEL), F32)
                     + 3 * _nbytes((D_MODEL, D_MODEL), BF16)),
                8 * _nbytes((tm, D_MODEL), F32))),
        name="merge",
    )(yh, ya, z, z, x2d, mod3, w_bh, w_ba, w_out, ln_g.reshape(1, -1), ln_b.reshape(1, -1))


def _layer(x, mod3, cond_row, ctx_kv, p, filt, tiles, *, alpha):
    B, L, _ = x.shape
    x2d = x.reshape(B * L, D_MODEL)
    emit_kv = ctx_kv is None
    assert not emit_kv or IN_TM == L
    outs = _in_projection(x2d, mod3, lambda i: cond_row(i * IN_TM // L), p["w_in"], emit_kv=emit_kv, tm=IN_TM)
    z = outs[0]
    dft = _dft_matrices(L)
    spectra = _filter_spectra(L, dft[0], p["hyena_d"], *filt)
    yh = _hyena_branch(z, B, L, p["conv_w"], p["conv_b"], dft, spectra,
                       seqs_per_step=max(1, HYENA_ROWS_PER_STEP // L))
    if ctx_kv is None:
        ya = _context_attention(z, B, L)
    else:
        ya = _neighbourhood_attention(z, B, L, ctx_kv[0], ctx_kv[1], tiles)
    y = _merge(yh, ya, z, x2d, mod3, lambda i: cond_row(i * MERGE_TM // L), p["w_bh"], p["w_ba"], p["w_out"],
               p["ln_g"], p["ln_b"], alpha=alpha, tm=MERGE_TM)
    return y.reshape(B, L, D_MODEL), outs[1:]


def kernel(x_prompt, x_sample, c, cache_k, cache_v, c_ctx, w_ada, b_ada, w_in, conv_w, conv_b, filt_w1, filt_b1,
           filt_w2, filt_b2, filt_w3, filt_freq, hyena_d, rpb, w_bh, w_ba, w_out, ln_g, ln_b):
    depth = w_in.shape[0]
    alpha = (2.0 * depth) ** 0.25
    n_lat, lat_len = x_sample.shape[0], x_sample.shape[1]
    n_ctx, ctx_len = x_prompt.shape[0], x_prompt.shape[1]
    ctx_row = n_lat
    cond = jnp.zeros((8, D_MODEL), F32).at[:n_lat].set(c).at[ctx_row].set(c_ctx)

    xp, xs = x_prompt, x_sample
    new_k, new_v = [], []
    for l in range(depth):
        p = {"w_in": w_in[l], "conv_w": conv_w[l], "conv_b": conv_b[l], "hyena_d": hyena_d[l],
             "w_bh": w_bh[l].astype(BF16), "w_ba": w_ba[l].astype(BF16), "w_out": w_out[l].astype(BF16),
             "ln_g": ln_g[l], "ln_b": ln_b[l]}
        mod3 = _modulation(cond, w_ada[l], b_ada[l]).reshape(8, 1, 3 * D_MODEL)
        filt = (filt_w1[l], filt_b1[l], filt_w2[l], filt_b2[l], filt_w3[l], filt_freq[l])
        tiles = _rpb_tiles(rpb[l])
        xp, (k_ctx, v_ctx) = _layer(xp, mod3, lambda b: ctx_row, None, p, filt, None, alpha=alpha)
        new_k.append(jnp.transpose(k_ctx.reshape(n_ctx, N_HEADS, HEAD_DIM, ctx_len), (0, 3, 1, 2)))
        new_v.append(jnp.transpose(v_ctx.reshape(n_ctx, N_HEADS, HEAD_DIM, ctx_len), (0, 3, 1, 2)))
        xs, _ = _layer(xs, mod3, lambda b: b, (cache_k[:, l], cache_v[:, l]), p, filt, tiles, alpha=alpha)
    return xp, xs, jnp.stack(new_k, axis=1), jnp.stack(new_v, axis=1)
```

```python
import functools
import math

import jax
import jax.numpy as jnp
import numpy as np
from jax.experimental import pallas as pl
from jax.experimental.pallas import tpu as pltpu

F32 = jnp.float32
BF16 = jnp.bfloat16

D_MODEL = 1024
D_HYENA = 1024
N_HEADS = 16
HEAD_DIM = 64
D_ATTN = N_HEADS * HEAD_DIM
D_IN = 4 * D_HYENA + 4 * D_ATTN + 2 * D_MODEL
GRID_W = 64
WIN_ROWS = 8
WIN_COLS = 16
FILTER_EMB = 33
FILTER_BANDS = (FILTER_EMB - 1) // 2
FILTER_HIDDEN = 64
DECAY_TARGET = 1e-2
MIN_DECAY = math.log(DECAY_TARGET) / 1.5
MAX_DECAY = math.log(DECAY_TARGET) / 0.3
DECAY_SHIFT = 0.05
LN_EPS = 1e-5
NEG_INF = -1e30

COL_VH, COL_X1, COL_X0, COL_GH, COL_Q, COL_K, COL_V, COL_GA, COL_MH, COL_MA = range(10)

LANES = 128
SUBLANES = 8
CTX_HEAD_TILE = 128
NBR_HEAD_TILE = 128
NBR_BLOCK_LANES = 128
RPB_COPIES = LANES // GRID_W
V7X_VMEM_BYTES = 64 * 1024 * 1024
ATTN_SCALE = HEAD_DIM ** -0.5
LOG2E = math.log2(math.e)

Q_BLOCK = 256
Q_BLOCK_ROWS = Q_BLOCK // GRID_W
KEY_WIN_ROWS = 12
KEY_WIN = KEY_WIN_ROWS * GRID_W

IN_TM = 256
W_STAGES = 2
MERGE_TM = 512
HYENA_ROWS_PER_STEP = 4096


def _vmem_limit(block_bytes, temp_bytes):
    need = int(block_bytes + temp_bytes)
    return min(max(need, 16 * 1024 * 1024), V7X_VMEM_BYTES - 8 * 1024 * 1024)


def _nbytes(shape, dtype):
    return int(np.prod(shape)) * jnp.dtype(dtype).itemsize


def _silu(x):
    return x * jax.nn.sigmoid(x)


def _split_bf16(a):
    hi = a.astype(BF16)
    lo = (a - hi.astype(F32)).astype(BF16)
    return hi, lo


def _dot(a, b):
    return jnp.dot(a, b, preferred_element_type=F32)


def _dot_split(a_hi, a_lo, b):
    b_hi, b_lo = _split_bf16(b)
    return _dot(a_hi, b_hi) + (_dot(a_hi, b_lo) + _dot(a_lo, b_hi))


@functools.lru_cache(maxsize=None)
def _dft_constants(L):
    n = 2 * L
    k = np.arange(L, dtype=np.float64)[:, None]
    s = np.arange(L, dtype=np.float64)[None, :]
    ang = 2.0 * np.pi * k * s / n
    nyq = np.cos(np.pi * np.arange(L, dtype=np.float64))
    C = np.cos(ang)
    S = np.sin(ang)
    S[0, :] = nyq
    F = np.concatenate([C, S], axis=0)
    w = np.full((L,), 2.0)
    w[0] = 1.0
    Gc = (C * w[:, None]).T / n
    Ss = 2.0 * np.sin(ang)
    Ss[0, :] = nyq
    Gs = Ss.T / n
    G = np.concatenate([Gc, Gs], axis=1)
    return F.astype(np.float32), G.astype(np.float32)


def _dft_matrices(L):
    F, G = _dft_constants(L)
    return jnp.asarray(F).astype(BF16), jnp.asarray(G).astype(BF16)


@functools.lru_cache(maxsize=None)
def _filter_constants(L):
    t = np.linspace(0.0, 1.0, L, dtype=np.float32)[:, None]
    bands = np.linspace(1e-4, FILTER_BANDS - 1, FILTER_BANDS, dtype=np.float32)[None]
    w = (2.0 * math.pi * np.arange(L, dtype=np.float32)[:, None] / L).astype(np.float32)
    z = np.concatenate([t, np.cos(bands * w), -np.sin(bands * w)], axis=-1).astype(np.float32)
    z_pad = np.zeros((L, LANES), np.float32)
    z_pad[:, :FILTER_EMB] = z
    deltas = np.linspace(MIN_DECAY, MAX_DECAY, D_HYENA, dtype=np.float32)
    decay = (np.exp(-t * np.abs(deltas)) + np.float32(DECAY_SHIFT)).astype(np.float32)
    return z_pad, decay


@functools.lru_cache(maxsize=None)
def _rpb_constants():
    c = np.arange(GRID_W)[:, None]
    kc = np.arange(GRID_W)[None, :]
    c0 = np.clip(c - WIN_COLS // 2, 0, GRID_W - WIN_COLS)
    in_win = (kc >= c0) & (kc < c0 + WIN_COLS)
    idx = np.clip(kc - c + WIN_COLS - 1, 0, 2 * WIN_COLS - 2)
    onehot = np.zeros((LANES, GRID_W, LANES), np.float32)
    for half in range(RPB_COPIES):
        lanes = half * GRID_W + np.arange(GRID_W)
        onehot[idx, c, lanes[None, :]] = 1.0
    mask = np.concatenate([in_win, in_win], axis=1).astype(np.float32)
    return onehot.reshape(LANES, GRID_W * LANES).astype(BF16), mask.reshape(1, GRID_W * LANES)


def _mod_kernel(cond_ref, w_ref, b_ref, o_ref):
    s = _silu(cond_ref[...]).astype(BF16)
    o_ref[...] = _dot(s, w_ref[...].astype(BF16)) + b_ref[...]


def _modulation(cond, w_ada, b_ada):
    rows = cond.shape[0]
    return pl.pallas_call(
        _mod_kernel,
        out_shape=jax.ShapeDtypeStruct((rows, 3 * D_MODEL), F32),
        grid=(3,),
        in_specs=[pl.BlockSpec((rows, D_MODEL), lambda j: (0, 0)),
                  pl.BlockSpec((D_MODEL, D_MODEL), lambda j: (0, j)),
                  pl.BlockSpec((1, D_MODEL), lambda j: (0, j))],
        out_specs=pl.BlockSpec((rows, D_MODEL), lambda j: (0, j)),
        name="mod",
    )(cond, w_ada, b_ada.reshape(1, -1))


def _spectra_kernel(z_ref, w1_ref, b1_ref, w2_ref, b2_ref, w3f_ref, w3b_ref, freq_ref, decay_ref, d_ref, f_ref,
                    fp_ref, fqa_ref, fpb_ref, hf_ref):
    L = decay_ref.shape[0]

    @pl.when(pl.program_id(0) == 0)
    def _():
        z_hi, z_lo = _split_bf16(z_ref[...])
        hf = jnp.sin(freq_ref[0:1, :] * (_dot_split(z_hi, z_lo, w1_ref[...]) + b1_ref[...]))
        h_hi, h_lo = _split_bf16(hf)
        hf_ref[...] = jnp.sin(freq_ref[1:2, :] * (_dot_split(h_hi, h_lo, w2_ref[...]) + b2_ref[...]))

    h_hi, h_lo = _split_bf16(hf_ref[...])
    decay = decay_ref[...]
    h_fwd = _dot_split(h_hi, h_lo, w3f_ref[...]) * decay
    h_bwd = _dot_split(h_hi, h_lo, w3b_ref[...]) * decay
    row0 = jax.lax.broadcasted_iota(jnp.int32, (L, 1), 0) == 0
    h_bwd = jnp.where(row0, 0.0, h_bwd)
    fsum = (h_fwd + h_bwd).astype(BF16)
    fdif = (h_fwd - h_bwd).astype(BF16)
    skip = d_ref[...]
    fp = _dot(f_ref[0:L, :], fsum) + skip
    fq = _dot(f_ref[L:2 * L, :], fdif)
    nyq = _dot(f_ref[L:L + 2 * SUBLANES, :], fsum)[0:1, :] + skip
    fp_ref[...] = fp
    fqa_ref[...] = jnp.where(row0, 0.0, fq)
    fpb_ref[...] = jnp.where(row0, nyq, fp)


def _filter_spectra(L, f_mat, hyena_d, w1, b1, w2, b2, w3, freq):
    tc = 256
    z_emb, decay = _filter_constants(L)
    w1p = jnp.zeros((LANES, FILTER_HIDDEN), F32).at[:FILTER_EMB].set(w1)
    nblk = D_HYENA // tc
    const = lambda shape: pl.BlockSpec(shape, lambda c: (0, 0), pipeline_mode=pl.Buffered(1))
    out_spec = pl.BlockSpec((L, tc), lambda c: (0, c))
    blocks = (_nbytes((2 * L, L), BF16) + _nbytes((L, LANES), F32) + 2 * 4 * _nbytes((L, tc), F32))
    return pl.pallas_call(
        _spectra_kernel,
        out_shape=[jax.ShapeDtypeStruct((L, D_HYENA), F32)] * 3,
        grid=(nblk,),
        in_specs=[const((L, LANES)),
                  const((LANES, FILTER_HIDDEN)), const((1, FILTER_HIDDEN)),
                  const((FILTER_HIDDEN, FILTER_HIDDEN)), const((1, FILTER_HIDDEN)),
                  pl.BlockSpec((FILTER_HIDDEN, tc), lambda c: (0, c)),
                  pl.BlockSpec((FILTER_HIDDEN, tc), lambda c: (0, nblk + c)),
                  const((2, FILTER_HIDDEN)),
                  pl.BlockSpec((L, tc), lambda c: (0, c)),
                  pl.BlockSpec((1, tc), lambda c: (0, c)),
                  const((2 * L, L))],
        out_specs=[out_spec] * 3,
        scratch_shapes=[pltpu.VMEM((L, FILTER_HIDDEN), F32)],
        compiler_params=pltpu.CompilerParams(
            dimension_semantics=("arbitrary",),
            vmem_limit_bytes=_vmem_limit(blocks, 12 * _nbytes((L, tc), F32))),
        name=f"spectra_{L}",
    )(z_emb, w1p, b1.reshape(1, -1), w2, b2.reshape(1, -1), w3, w3, freq, decay, hyena_d.reshape(1, -1), f_mat)


def _rpb_kernel(r_ref, onehot_ref, mask_ref, o_ref):
    r = r_ref[...]
    r_hi = r.astype(BF16)
    rem = r - r_hi.astype(F32)
    r_mid = rem.astype(BF16)
    r_lo = (rem - r_mid.astype(F32)).astype(BF16)
    oh = onehot_ref[...]
    t = _dot(r_hi, oh) + (_dot(r_mid, oh) + _dot(r_lo, oh))
    t = jnp.where(mask_ref[...] > 0.0, t * LOG2E, NEG_INF)
    for c in range(o_ref.shape[1]):
        o_ref[:, c, :] = t[:, c * LANES:(c + 1) * LANES]


def _rpb_tiles(rpb):
    n_dr = 2 * WIN_ROWS - 1
    rows = N_HEADS * n_dr
    onehot, mask = _rpb_constants()
    r = jnp.zeros((rows, LANES), F32).at[:, :2 * WIN_COLS - 1].set(rpb.reshape(rows, 2 * WIN_COLS - 1))
    cols_per_step = 16
    tn = cols_per_step * LANES
    out = pl.pallas_call(
        _rpb_kernel,
        out_shape=jax.ShapeDtypeStruct((rows, GRID_W, LANES), F32),
        grid=(GRID_W // cols_per_step,),
        in_specs=[pl.BlockSpec((rows, LANES), lambda j: (0, 0)),
                  pl.BlockSpec((LANES, tn), lambda j: (0, j)),
                  pl.BlockSpec((1, tn), lambda j: (0, j))],
        out_specs=pl.BlockSpec((rows, cols_per_step, LANES), lambda j: (0, j, 0)),
        name="rpb_tiles",
    )(r, onehot, mask)
    return out.reshape(N_HEADS, n_dr, GRID_W, LANES)


def _inproj_kernel(xl_ref, xc_ref, mod_ref, w_hbm, z_ref, kt_ref, vt_ref, w_ref, stage_ref, sem, *, n_lat_tiles):
    n_col = D_IN // D_MODEL
    n_stage = stage_ref.shape[0]
    epilogue = {COL_GH: _silu, COL_GA: _silu, COL_MH: jax.nn.sigmoid, COL_MA: jax.nn.sigmoid,
                COL_Q: lambda acc: acc * (ATTN_SCALE * LOG2E)}

    def columns(col):
        return slice(col * D_MODEL, (col + 1) * D_MODEL)

    def fetch(col):
        return pltpu.make_async_copy(w_hbm.at[:, columns(col)], stage_ref.at[col % n_stage], sem.at[col % n_stage])

    def tile(stream_weights):
        if stream_weights:
            for col in range(min(n_stage, n_col)):
                fetch(col).start()
        shift = mod_ref[:, 0:D_MODEL]
        scale = mod_ref[:, D_MODEL:2 * D_MODEL]
        x = jnp.where(pl.program_id(0) < n_lat_tiles, xl_ref[...], xc_ref[...])
        h = (x * (1.0 + scale) + shift).astype(BF16)
        for col in range(n_col):
            if stream_weights:
                fetch(col).wait()
                w_ref[:, columns(col)] = stage_ref[col % n_stage].astype(BF16)
                if col + n_stage < n_col:
                    fetch(col + n_stage).start()
            acc = _dot(h, w_ref[:, columns(col)])
            z_ref[:, columns(col)] = epilogue.get(col, lambda acc: acc)(acc).astype(BF16)
            if col == COL_K:
                kt_ref[...] = acc.T
            if col == COL_V:
                vt_ref[...] = acc.T

    pl.when(pl.program_id(0) == 0)(functools.partial(tile, True))
    pl.when(pl.program_id(0) != 0)(functools.partial(tile, False))


def _in_projection(x_lat, x_ctx, mod3, lat_row, ctx_row, w_in, *, tm):
    n_lat, n_ctx = x_lat.shape[0] // tm, x_ctx.shape[0] // tm
    is_lat = lambda i: i < n_lat
    lat_tile = lambda i: jnp.minimum(i, n_lat - 1)
    ctx_tile = lambda i: jnp.maximum(i - n_lat, 0)
    kv_shape = jax.ShapeDtypeStruct((n_ctx, D_ATTN, tm), F32)
    kv_spec = pl.BlockSpec((None, D_ATTN, tm), lambda i: (ctx_tile(i), 0, 0))
    stage = (W_STAGES, D_MODEL, D_MODEL)
    blocks = (_nbytes((D_MODEL, D_IN), BF16) + _nbytes(stage, F32) + 4 * _nbytes((tm, D_MODEL), F32)
              + 2 * _nbytes((tm, D_IN), BF16) + 4 * _nbytes((tm, D_ATTN), F32))
    return pl.pallas_call(
        functools.partial(_inproj_kernel, n_lat_tiles=n_lat),
        out_shape=[jax.ShapeDtypeStruct(((n_lat + n_ctx) * tm, D_IN), BF16), kv_shape, kv_shape],
        grid=(n_lat + n_ctx,),
        in_specs=[pl.BlockSpec((tm, D_MODEL), lambda i: (lat_tile(i), 0)),
                  pl.BlockSpec((tm, D_MODEL), lambda i: (ctx_tile(i), 0)),
                  pl.BlockSpec((None, 1, 3 * D_MODEL),
                               lambda i: (jnp.where(is_lat(i), lat_row(lat_tile(i)), ctx_row), 0, 0)),
                  pl.BlockSpec(memory_space=pl.ANY)],
        out_specs=[pl.BlockSpec((tm, D_IN), lambda i: (i, 0)), kv_spec, kv_spec],
        scratch_shapes=[pltpu.VMEM((D_MODEL, D_IN), BF16), pltpu.VMEM(stage, F32),
                        pltpu.SemaphoreType.DMA((W_STAGES,))],
        compiler_params=pltpu.CompilerParams(
            dimension_semantics=("arbitrary",),
            vmem_limit_bytes=_vmem_limit(blocks, 4 * _nbytes((tm, D_MODEL), F32))),
        name="inproj",
    )(x_lat, x_ctx, mod3, w_in)


def _hyena_kernel(vh_ref, x1_ref, x0_ref, gh_ref, cwv_ref, cw1_ref, cw0_ref, cbv_ref, cb1_ref, cb0_ref,
                  f_ref, g_ref, fp_ref, fqa_ref, fpb_ref, o_ref):
    L = fp_ref.shape[0]
    slab = jax.lax.broadcasted_iota(jnp.int32, (SUBLANES, 1), 0)

    def short_conv(x_ref, rows, w_ref, b_ref):
        x = x_ref[rows, :].astype(F32)
        prev = pltpu.roll(x, 1, 0)
        nxt = pltpu.roll(x, L - 1, 0)
        prev = jnp.concatenate([jnp.where(slab == 0, 0.0, prev[:SUBLANES]), prev[SUBLANES:]], axis=0)
        nxt = jnp.concatenate([nxt[:-SUBLANES], jnp.where(slab == SUBLANES - 1, 0.0, nxt[-SUBLANES:])], axis=0)
        return prev * w_ref[0:1, :] + x * w_ref[1:2, :] + nxt * w_ref[2:3, :] + b_ref[...]

    for s in range(vh_ref.shape[0] // L):
        rows = slice(s * L, (s + 1) * L)
        u = short_conv(vh_ref, rows, cwv_ref, cbv_ref) * short_conv(x1_ref, rows, cw1_ref, cb1_ref)
        t = _dot(f_ref[...], u.astype(BF16))
        p = t[:L]
        q = t[L:]
        yp = p * fp_ref[...] - q * fqa_ref[...]
        yq = p * fqa_ref[...] + q * fpb_ref[...]
        y = _dot(g_ref[...], jnp.concatenate([yp.astype(BF16), yq.astype(BF16)], axis=0))
        y_h = y * short_conv(x0_ref, rows, cw0_ref, cb0_ref)
        o_ref[rows, :] = (y_h * gh_ref[rows, :].astype(F32)).astype(BF16)


def _hyena_branch(z, z_row0, B, L, conv_w, conv_b, dft, spectra, *, tc=256, seqs_per_step):
    nblk = D_HYENA // tc
    tm = seqs_per_step * L
    assert z_row0 % tm == 0
    zcol = lambda piece: pl.BlockSpec((tm, tc), lambda c, b: (z_row0 // tm + b, piece * nblk + c))
    wcol = lambda rows, piece: pl.BlockSpec((rows, tc), lambda c, b: (0, piece * nblk + c))
    chan = lambda rows: pl.BlockSpec((rows, tc), lambda c, b: (0, c))
    const = lambda shape: pl.BlockSpec(shape, lambda c, b: (0, 0), pipeline_mode=pl.Buffered(1))
    blocks = (2 * _nbytes((2 * L, L), BF16) + 2 * 5 * _nbytes((tm, tc), BF16) + 2 * 3 * _nbytes((L, tc), F32))
    return pl.pallas_call(
        _hyena_kernel,
        out_shape=jax.ShapeDtypeStruct((B * L, D_HYENA), BF16),
        grid=(nblk, B // seqs_per_step),
        in_specs=[zcol(COL_VH), zcol(COL_X1), zcol(COL_X0), zcol(COL_GH),
                  wcol(3, 0), wcol(3, 1), wcol(3, 2), wcol(1, 0), wcol(1, 1), wcol(1, 2),
                  const((2 * L, L)), const((L, 2 * L)), chan(L), chan(L), chan(L)],
        out_specs=pl.BlockSpec((tm, tc), lambda c, b: (b, c)),
        compiler_params=pltpu.CompilerParams(
            dimension_semantics=("arbitrary", "arbitrary"),
            vmem_limit_bytes=_vmem_limit(blocks, 16 * _nbytes((tm, tc), F32))),
        name=f"hyena_{L}",
    )(z, z, z, z, conv_w, conv_w, conv_w, conv_b.reshape(1, -1), conv_b.reshape(1, -1), conv_b.reshape(1, -1),
      *dft, *spectra)


def _head_masks(width):
    lane = jax.lax.broadcasted_iota(jnp.int32, (1, width), 1)
    return [(lane >= h * HEAD_DIM) & (lane < (h + 1) * HEAD_DIM) for h in range(width // HEAD_DIM)]


def _qk(q, k):
    return jax.lax.dot_general(q, k, (((1,), (1,)), ((), ())), preferred_element_type=F32)


def _stack_heads(q, masks):
    return jnp.concatenate([jnp.where(msk, q, jnp.zeros_like(q)) for msk in masks], axis=0)


def _unstack_heads(o, masks):
    n = o.shape[0] // len(masks)
    out = o[:n]
    for h in range(1, len(masks)):
        out = jnp.where(masks[h], o[h * n:(h + 1) * n], out)
    return out


def _ctx_attn_kernel(q_ref, k_ref, v_ref, g_ref, o_ref):
    masks = _head_masks(CTX_HEAD_TILE)
    for t in range(D_ATTN // CTX_HEAD_TILE):
        cols = slice(t * CTX_HEAD_TILE, (t + 1) * CTX_HEAD_TILE)
        s = _qk(_stack_heads(q_ref[:, cols], masks), k_ref[:, cols])
        m = jnp.max(s, axis=-1, keepdims=True)
        p = jnp.exp2(s - m)
        l = jnp.sum(p, axis=-1, keepdims=True)
        out = _unstack_heads(_dot(p.astype(BF16), v_ref[:, cols]) / l, masks)
        o_ref[:, cols] = (out * g_ref[:, cols].astype(F32)).astype(BF16)


def _context_attention(z, z_row0, B, L):
    assert z_row0 % L == 0
    zcol = lambda piece: pl.BlockSpec((L, D_ATTN), lambda b: (z_row0 // L + b, piece))
    return pl.pallas_call(
        _ctx_attn_kernel,
        out_shape=jax.ShapeDtypeStruct((B * L, D_ATTN), BF16),
        grid=(B,),
        in_specs=[zcol(COL_Q), zcol(COL_K), zcol(COL_V), zcol(COL_GA)],
        out_specs=pl.BlockSpec((L, D_ATTN), lambda b: (b, 0)),
        compiler_params=pltpu.CompilerParams(dimension_semantics=("arbitrary",)),
        name="ctx_attn",
    )(z, z, z, z)


def _key_window_start(qb):
    rows = 1024 // GRID_W
    r_first = qb * Q_BLOCK_ROWS
    r_last = r_first + Q_BLOCK_ROWS - 1
    lo = min(max(r_first - WIN_ROWS // 2, 0), rows - WIN_ROWS)
    hi = min(max(r_last - WIN_ROWS // 2, 0), rows - WIN_ROWS) + WIN_ROWS
    start = min(lo, rows - KEY_WIN_ROWS)
    start -= start % Q_BLOCK_ROWS
    assert start <= lo and hi <= start + KEY_WIN_ROWS
    return start


def _nbr_attn_kernel(q_ref, k0_ref, k1_ref, k2_ref, v0_ref, v1_ref, v2_ref, kc_ref, vc_ref, g_ref, tiles_ref,
                     o_ref, bias_ref, *, n_rows):
    masks = _head_masks(NBR_HEAD_TILE)
    n_heads = q_ref.shape[-1] // HEAD_DIM
    chain_tiles = [slice(c * NBR_HEAD_TILE, (c + 1) * NBR_HEAD_TILE) for c in range(q_ref.shape[-1] // NBR_HEAD_TILE)]
    k_refs = (k0_ref, k1_ref, k2_ref)
    v_refs = (v0_ref, v1_ref, v2_ref)
    key_blocks = KEY_WIN // Q_BLOCK

    def window_row_start(r):
        return min(max(r - WIN_ROWS // 2, 0), n_rows - WIN_ROWS)

    def build_bias(qb):
        k_start = _key_window_start(qb)
        used = set()
        for i in range(Q_BLOCK_ROWS):
            r = qb * Q_BLOCK_ROWS + i
            r0 = window_row_start(r)
            for j in range(KEY_WIN_ROWS):
                kr = k_start + j
                cols = slice(j * GRID_W, (j + 1) * GRID_W)
                lanes = slice((j % RPB_COPIES) * GRID_W, (j % RPB_COPIES + 1) * GRID_W)
                inside = r0 <= kr < r0 + WIN_ROWS
                if inside:
                    used.add(j // Q_BLOCK_ROWS)
                for h in range(n_heads):
                    rows = slice(h * Q_BLOCK + i * GRID_W, h * Q_BLOCK + (i + 1) * GRID_W)
                    if inside:
                        bias_ref[rows, cols] = tiles_ref[h, kr - r + WIN_ROWS - 1, :, lanes]
                    else:
                        bias_ref[rows, cols] = jnp.full((GRID_W, GRID_W), NEG_INF, F32)
        return sorted(used)

    def attend(blocks):
        for c, lanes in enumerate(chain_tiles):
            stacked = slice(c * len(masks) * Q_BLOCK, (c + 1) * len(masks) * Q_BLOCK)
            for b in range(q_ref.shape[0]):
                qs = _stack_heads(q_ref[b, :, lanes], masks)
                s = [_qk(qs, k_refs[j][b, :, lanes]) + bias_ref[stacked, j * Q_BLOCK:(j + 1) * Q_BLOCK] for j in blocks]
                s.append(_dot(qs, kc_ref[b, lanes, :].astype(BF16)))
                m = jnp.max(functools.reduce(jnp.maximum, s), axis=-1, keepdims=True)
                p = [jnp.exp2(x - m) for x in s]
                l = jnp.sum(functools.reduce(jnp.add, p), axis=-1, keepdims=True)
                p = [x.astype(BF16) for x in p]
                v_win = jnp.concatenate([v_refs[j][b, :, lanes] for j in blocks], axis=0)
                o = _dot(jnp.concatenate(p[:-1], axis=1), v_win) + _qk(p[-1], vc_ref[b, lanes, :].astype(BF16))
                o_ref[b, :, lanes] = (_unstack_heads(o / l, masks) * g_ref[b, :, lanes].astype(F32)).astype(BF16)

    for qb in range(n_rows // Q_BLOCK_ROWS):
        @pl.when(pl.program_id(0) == qb)
        def _(qb=qb):
            blocks = build_bias(qb)
            assert blocks and all(0 <= j < key_blocks for j in blocks)
            attend(blocks)


def _neighbourhood_attention(z, z_row0, B, L, cache_k, cache_v, tiles):
    assert z_row0 % (B * L) == 0
    group = z_row0 // (B * L)
    n_rows = L // GRID_W
    n_qb = L // Q_BLOCK
    assert KEY_WIN == 3 * Q_BLOCK and all(_key_window_start(qb) % Q_BLOCK_ROWS == 0 for qb in range(n_qb))
    z3 = z.reshape(z.shape[0] // L, L, D_IN)
    n_ctx = cache_k.shape[1]
    kc = jnp.transpose(cache_k, (0, 2, 3, 1)).reshape(B, D_ATTN, n_ctx)
    vc = jnp.transpose(cache_v, (0, 2, 3, 1)).reshape(B, D_ATTN, n_ctx)
    width = NBR_BLOCK_LANES
    tiles_per_col = D_ATTN // width

    def win_block(qb):
        return jnp.where(qb < n_qb // 2, 0, (n_rows - KEY_WIN_ROWS) // Q_BLOCK_ROWS)

    assert [_key_window_start(qb) // Q_BLOCK_ROWS for qb in range(n_qb)] == \
        [0 if qb < n_qb // 2 else (n_rows - KEY_WIN_ROWS) // Q_BLOCK_ROWS for qb in range(n_qb)]
    qspec = lambda piece: pl.BlockSpec((B, Q_BLOCK, width), lambda qb, t: (group, qb, piece * tiles_per_col + t))
    kspec = lambda piece, j: pl.BlockSpec((B, Q_BLOCK, width),
                                          lambda qb, t: (group, win_block(qb) + j, piece * tiles_per_col + t))
    cspec = pl.BlockSpec((B, width, n_ctx), lambda qb, t: (0, t, 0))
    out = pl.pallas_call(
        functools.partial(_nbr_attn_kernel, n_rows=n_rows),
        out_shape=jax.ShapeDtypeStruct((B, L, D_ATTN), BF16),
        grid=(n_qb, tiles_per_col),
        in_specs=[qspec(COL_Q), kspec(COL_K, 0), kspec(COL_K, 1), kspec(COL_K, 2),
                  kspec(COL_V, 0), kspec(COL_V, 1), kspec(COL_V, 2), cspec, cspec, qspec(COL_GA),
                  pl.BlockSpec((width // HEAD_DIM, 2 * WIN_ROWS - 1, GRID_W, LANES), lambda qb, t: (t, 0, 0, 0))],
        out_specs=pl.BlockSpec((B, Q_BLOCK, width), lambda qb, t: (0, qb, t)),
        scratch_shapes=[pltpu.VMEM((width // HEAD_DIM * Q_BLOCK, KEY_WIN), F32)],
        compiler_params=pltpu.CompilerParams(dimension_semantics=("arbitrary", "arbitrary")),
        name="nbr_attn",
    )(z3, z3, z3, z3, z3, z3, z3, kc, vc, z3, tiles)
    return out.reshape(B * L, D_ATTN)


def _merge_kernel(yh_ref, ya_ref, mh_ref, ma_ref, x_ref, mod_ref, wbh_ref, wba_ref, wout_ref, lng_ref, lnb_ref,
                  o_ref, *, alpha):
    p_h = _dot(yh_ref[...], wbh_ref[...])
    p_a = _dot(ya_ref[...], wba_ref[...])
    m_h = mh_ref[...].astype(F32)
    m_a = ma_ref[...].astype(F32)
    out = _dot((m_h * p_h + m_a * p_a).astype(BF16), wout_ref[...])
    gate = mod_ref[:, 2 * D_MODEL:3 * D_MODEL]
    v = alpha * x_ref[...] + gate * out
    mu = jnp.mean(v, axis=-1, keepdims=True)
    d = v - mu
    var = jnp.mean(d * d, axis=-1, keepdims=True)
    o_ref[...] = d * jax.lax.rsqrt(var + LN_EPS) * lng_ref[...] + lnb_ref[...]


def _merge(yh, ya, z, z_row0, x2d, mod3, mod_index, w_bh, w_ba, w_out, ln_g, ln_b, *, alpha, tm):
    M = x2d.shape[0]
    assert z_row0 % tm == 0
    tok = lambda width, col: pl.BlockSpec((tm, width), lambda i: (i, col))
    ztok = lambda col: pl.BlockSpec((tm, D_MODEL), lambda i: (z_row0 // tm + i, col))
    const = lambda shape: pl.BlockSpec(shape, lambda i: (0, 0))
    return pl.pallas_call(
        functools.partial(_merge_kernel, alpha=alpha),
        out_shape=jax.ShapeDtypeStruct((M, D_MODEL), F32),
        grid=(M // tm,),
        in_specs=[tok(D_HYENA, 0), tok(D_ATTN, 0), ztok(COL_MH), ztok(COL_MA), tok(D_MODEL, 0),
                  pl.BlockSpec((None, 1, 3 * D_MODEL), lambda i: (mod_index(i), 0, 0)),
                  const((D_HYENA, D_MODEL)), const((D_ATTN, D_MODEL)), const((D_MODEL, D_MODEL)),
                  const((1, D_MODEL)), const((1, D_MODEL))],
        out_specs=pl.BlockSpec((tm, D_MODEL), lambda i: (i, 0)),
        compiler_params=pltpu.CompilerParams(
            dimension_semantics=("arbitrary",),
            vmem_limit_bytes=_vmem_limit(
                2 * (4 * _nbytes((tm, D_MODEL), BF16) + 2 * _nbytes((tm, D_MODEL), F32)
                     + 3 * _nbytes((D_MODEL, D_MODEL), BF16)),
                8 * _nbytes((tm, D_MODEL), F32))),
        name="merge",
    )(yh, ya, z, z, x2d, mod3, w_bh, w_ba, w_out, ln_g.reshape(1, -1), ln_b.reshape(1, -1))


def _mixers(x, z, z_row0, mod3, cond_row, ctx_kv, p, filt, tiles, *, alpha):
    B, L, _ = x.shape
    dft = _dft_matrices(L)
    spectra = _filter_spectra(L, dft[0], p["hyena_d"], *filt)
    yh = _hyena_branch(z, z_row0, B, L, p["conv_w"], p["conv_b"], dft, spectra,
                       seqs_per_step=max(1, HYENA_ROWS_PER_STEP // L))
    if ctx_kv is None:
        ya = _context_attention(z, z_row0, B, L)
    else:
        ya = _neighbourhood_attention(z, z_row0, B, L, ctx_kv[0], ctx_kv[1], tiles)
    y = _merge(yh, ya, z, z_row0, x.reshape(B * L, D_MODEL), mod3, lambda i: cond_row(i * MERGE_TM // L),
               p["w_bh"], p["w_ba"], p["w_out"], p["ln_g"], p["ln_b"], alpha=alpha, tm=MERGE_TM)
    return y.reshape(B, L, D_MODEL)


def kernel(x_prompt, x_sample, c, cache_k, cache_v, c_ctx, w_ada, b_ada, w_in, conv_w, conv_b, filt_w1, filt_b1,
           filt_w2, filt_b2, filt_w3, filt_freq, hyena_d, rpb, w_bh, w_ba, w_out, ln_g, ln_b):
    depth = w_in.shape[0]
    alpha = (2.0 * depth) ** 0.25
    n_lat, lat_len = x_sample.shape[0], x_sample.shape[1]
    n_ctx, ctx_len = x_prompt.shape[0], x_prompt.shape[1]
    assert ctx_len == IN_TM
    ctx_row = n_lat
    cond = jnp.zeros((8, D_MODEL), F32).at[:n_lat].set(c).at[ctx_row].set(c_ctx)

    xp, xs = x_prompt, x_sample
    new_k, new_v = [], []
    for l in range(depth):
        p = {"conv_w": conv_w[l], "conv_b": conv_b[l], "hyena_d": hyena_d[l],
             "w_bh": w_bh[l].astype(BF16), "w_ba": w_ba[l].astype(BF16), "w_out": w_out[l].astype(BF16),
             "ln_g": ln_g[l], "ln_b": ln_b[l]}
        mod3 = _modulation(cond, w_ada[l], b_ada[l]).reshape(8, 1, 3 * D_MODEL)
        filt = (filt_w1[l], filt_b1[l], filt_w2[l], filt_b2[l], filt_w3[l], filt_freq[l])
        tiles = _rpb_tiles(rpb[l])
        z, k_ctx, v_ctx = _in_projection(xs.reshape(-1, D_MODEL), xp.reshape(-1, D_MODEL), mod3,
                                         lambda tile: tile * IN_TM // lat_len, ctx_row, w_in[l], tm=IN_TM)
        new_k.append(jnp.transpose(k_ctx.reshape(n_ctx, N_HEADS, HEAD_DIM, ctx_len), (0, 3, 1, 2)))
        new_v.append(jnp.transpose(v_ctx.reshape(n_ctx, N_HEADS, HEAD_DIM, ctx_len), (0, 3, 1, 2)))
        xp = _mixers(xp, z, n_lat * lat_len, mod3, lambda b: ctx_row, None, p, filt, None, alpha=alpha)
        xs = _mixers(xs, z, 0, mod3, lambda b: b, (cache_k[:, l], cache_v[:, l]), p, filt, tiles, alpha=alpha)
    return xp, xs, jnp.stack(new_k, axis=1), jnp.stack(new_v, axis=1)
```

```python
import functools
import math

import jax
import jax.numpy as jnp
import numpy as np
from jax.experimental import pallas as pl
from jax.experimental.pallas import tpu as pltpu

F32 = jnp.float32
BF16 = jnp.bfloat16

D_MODEL = 1024
D_HYENA = 1024
N_HEADS = 16
HEAD_DIM = 64
D_ATTN = N_HEADS * HEAD_DIM
D_IN = 4 * D_HYENA + 4 * D_ATTN + 2 * D_MODEL
GRID_W = 64
WIN_ROWS = 8
WIN_COLS = 16
FILTER_EMB = 33
FILTER_BANDS = (FILTER_EMB - 1) // 2
FILTER_HIDDEN = 64
DECAY_TARGET = 1e-2
MIN_DECAY = math.log(DECAY_TARGET) / 1.5
MAX_DECAY = math.log(DECAY_TARGET) / 0.3
DECAY_SHIFT = 0.05
LN_EPS = 1e-5
NEG_INF = -1e30

COL_VH, COL_X1, COL_X0, COL_GH, COL_Q, COL_K, COL_V, COL_GA, COL_MH, COL_MA = range(10)
PROJ_ORDER = (COL_GH, COL_GA, COL_MH, COL_MA, COL_K, COL_V, COL_Q, COL_VH, COL_X1, COL_X0)

LANES = 128
SUBLANES = 8
CTX_HEAD_TILE = 128
NBR_HEAD_TILE = 128
NBR_BLOCK_LANES = 128
RPB_COPIES = LANES // GRID_W
V7X_VMEM_BYTES = 64 * 1024 * 1024
ATTN_SCALE = HEAD_DIM ** -0.5
LOG2E = math.log2(math.e)

Q_BLOCK = 256
Q_BLOCK_ROWS = Q_BLOCK // GRID_W
KEY_WIN_ROWS = 12
KEY_WIN = KEY_WIN_ROWS * GRID_W

IN_TM = 256
W_STAGES = 2
MERGE_TM = 512
HYENA_ROWS_PER_STEP = 4096


def _vmem_limit(block_bytes, temp_bytes):
    need = int(block_bytes + temp_bytes)
    return min(max(need, 16 * 1024 * 1024), V7X_VMEM_BYTES - 8 * 1024 * 1024)


def _nbytes(shape, dtype):
    return int(np.prod(shape)) * jnp.dtype(dtype).itemsize


def _silu(x):
    return x * jax.nn.sigmoid(x)


def _split_bf16(a):
    hi = a.astype(BF16)
    lo = (a - hi.astype(F32)).astype(BF16)
    return hi, lo


def _dot(a, b):
    return jnp.dot(a, b, preferred_element_type=F32)


def _dot_split(a_hi, a_lo, b):
    b_hi, b_lo = _split_bf16(b)
    return _dot(a_hi, b_hi) + (_dot(a_hi, b_lo) + _dot(a_lo, b_hi))


@functools.lru_cache(maxsize=None)
def _dft_constants(L):
    n = 2 * L
    k = np.arange(L, dtype=np.float64)[:, None]
    s = np.arange(L, dtype=np.float64)[None, :]
    ang = 2.0 * np.pi * k * s / n
    nyq = np.cos(np.pi * np.arange(L, dtype=np.float64))
    C = np.cos(ang)
    S = np.sin(ang)
    S[0, :] = nyq
    F = np.concatenate([C, S], axis=0)
    w = np.full((L,), 2.0)
    w[0] = 1.0
    Gc = (C * w[:, None]).T / n
    Ss = 2.0 * np.sin(ang)
    Ss[0, :] = nyq
    Gs = Ss.T / n
    G = np.concatenate([Gc, Gs], axis=1)
    return F.astype(np.float32), G.astype(np.float32)


def _dft_matrices(L):
    F, G = _dft_constants(L)
    return jnp.asarray(F).astype(BF16), jnp.asarray(G).astype(BF16)


@functools.lru_cache(maxsize=None)
def _filter_constants(L):
    t = np.linspace(0.0, 1.0, L, dtype=np.float32)[:, None]
    bands = np.linspace(1e-4, FILTER_BANDS - 1, FILTER_BANDS, dtype=np.float32)[None]
    w = (2.0 * math.pi * np.arange(L, dtype=np.float32)[:, None] / L).astype(np.float32)
    z = np.concatenate([t, np.cos(bands * w), -np.sin(bands * w)], axis=-1).astype(np.float32)
    z_pad = np.zeros((L, LANES), np.float32)
    z_pad[:, :FILTER_EMB] = z
    deltas = np.linspace(MIN_DECAY, MAX_DECAY, D_HYENA, dtype=np.float32)
    decay = (np.exp(-t * np.abs(deltas)) + np.float32(DECAY_SHIFT)).astype(np.float32)
    return z_pad, decay


@functools.lru_cache(maxsize=None)
def _rpb_constants():
    c = np.arange(GRID_W)[:, None]
    kc = np.arange(GRID_W)[None, :]
    c0 = np.clip(c - WIN_COLS // 2, 0, GRID_W - WIN_COLS)
    in_win = (kc >= c0) & (kc < c0 + WIN_COLS)
    idx = np.clip(kc - c + WIN_COLS - 1, 0, 2 * WIN_COLS - 2)
    onehot = np.zeros((LANES, GRID_W, LANES), np.float32)
    for half in range(RPB_COPIES):
        lanes = half * GRID_W + np.arange(GRID_W)
        onehot[idx, c, lanes[None, :]] = 1.0
    mask = np.concatenate([in_win, in_win], axis=1).astype(np.float32)
    return onehot.reshape(LANES, GRID_W * LANES).astype(BF16), mask.reshape(1, GRID_W * LANES)


def _mod_kernel(cond_ref, w_ref, b_ref, o_ref):
    s = _silu(cond_ref[...]).astype(BF16)
    o_ref[...] = _dot(s, w_ref[...].astype(BF16)) + b_ref[...]


def _modulation(cond, w_ada, b_ada):
    rows = cond.shape[0]
    return pl.pallas_call(
        _mod_kernel,
        out_shape=jax.ShapeDtypeStruct((rows, 3 * D_MODEL), F32),
        grid=(3,),
        in_specs=[pl.BlockSpec((rows, D_MODEL), lambda j: (0, 0)),
                  pl.BlockSpec((D_MODEL, D_MODEL), lambda j: (0, j)),
                  pl.BlockSpec((1, D_MODEL), lambda j: (0, j))],
        out_specs=pl.BlockSpec((rows, D_MODEL), lambda j: (0, j)),
        name="mod",
    )(cond, w_ada, b_ada.reshape(1, -1))


def _spectra_kernel(z_ref, w1_ref, b1_ref, w2_ref, b2_ref, w3f_ref, w3b_ref, freq_ref, decay_ref, d_ref, f_ref,
                    fp_ref, fqa_ref, fpb_ref, hf_ref):
    L = decay_ref.shape[0]

    @pl.when(pl.program_id(0) == 0)
    def _():
        z_hi, z_lo = _split_bf16(z_ref[...])
        hf = jnp.sin(freq_ref[0:1, :] * (_dot_split(z_hi, z_lo, w1_ref[...]) + b1_ref[...]))
        h_hi, h_lo = _split_bf16(hf)
        hf_ref[...] = jnp.sin(freq_ref[1:2, :] * (_dot_split(h_hi, h_lo, w2_ref[...]) + b2_ref[...]))

    h_hi, h_lo = _split_bf16(hf_ref[...])
    decay = decay_ref[...]
    h_fwd = _dot_split(h_hi, h_lo, w3f_ref[...]) * decay
    h_bwd = _dot_split(h_hi, h_lo, w3b_ref[...]) * decay
    row0 = jax.lax.broadcasted_iota(jnp.int32, (L, 1), 0) == 0
    h_bwd = jnp.where(row0, 0.0, h_bwd)
    fsum = (h_fwd + h_bwd).astype(BF16)
    fdif = (h_fwd - h_bwd).astype(BF16)
    skip = d_ref[...]
    fp = _dot(f_ref[0:L, :], fsum) + skip
    fq = _dot(f_ref[L:2 * L, :], fdif)
    nyq = _dot(f_ref[L:L + 2 * SUBLANES, :], fsum)[0:1, :] + skip
    fp_ref[...] = fp
    fqa_ref[...] = jnp.where(row0, 0.0, fq)
    fpb_ref[...] = jnp.where(row0, nyq, fp)


def _filter_spectra(L, f_mat, hyena_d, w1, b1, w2, b2, w3, freq):
    tc = 256
    z_emb, decay = _filter_constants(L)
    w1p = jnp.zeros((LANES, FILTER_HIDDEN), F32).at[:FILTER_EMB].set(w1)
    nblk = D_HYENA // tc
    const = lambda shape: pl.BlockSpec(shape, lambda c: (0, 0), pipeline_mode=pl.Buffered(1))
    out_spec = pl.BlockSpec((L, tc), lambda c: (0, c))
    blocks = (_nbytes((2 * L, L), BF16) + _nbytes((L, LANES), F32) + 2 * 4 * _nbytes((L, tc), F32))
    return pl.pallas_call(
        _spectra_kernel,
        out_shape=[jax.ShapeDtypeStruct((L, D_HYENA), F32)] * 3,
        grid=(nblk,),
        in_specs=[const((L, LANES)),
                  const((LANES, FILTER_HIDDEN)), const((1, FILTER_HIDDEN)),
                  const((FILTER_HIDDEN, FILTER_HIDDEN)), const((1, FILTER_HIDDEN)),
                  pl.BlockSpec((FILTER_HIDDEN, tc), lambda c: (0, c)),
                  pl.BlockSpec((FILTER_HIDDEN, tc), lambda c: (0, nblk + c)),
                  const((2, FILTER_HIDDEN)),
                  pl.BlockSpec((L, tc), lambda c: (0, c)),
                  pl.BlockSpec((1, tc), lambda c: (0, c)),
                  const((2 * L, L))],
        out_specs=[out_spec] * 3,
        scratch_shapes=[pltpu.VMEM((L, FILTER_HIDDEN), F32)],
        compiler_params=pltpu.CompilerParams(
            dimension_semantics=("arbitrary",),
            vmem_limit_bytes=_vmem_limit(blocks, 12 * _nbytes((L, tc), F32))),
        name=f"spectra_{L}",
    )(z_emb, w1p, b1.reshape(1, -1), w2, b2.reshape(1, -1), w3, w3, freq, decay, hyena_d.reshape(1, -1), f_mat)


def _rpb_kernel(r_ref, onehot_ref, mask_ref, o_ref):
    r = r_ref[...]
    r_hi = r.astype(BF16)
    rem = r - r_hi.astype(F32)
    r_mid = rem.astype(BF16)
    r_lo = (rem - r_mid.astype(F32)).astype(BF16)
    oh = onehot_ref[...]
    t = _dot(r_hi, oh) + (_dot(r_mid, oh) + _dot(r_lo, oh))
    t = jnp.where(mask_ref[...] > 0.0, t * LOG2E, NEG_INF)
    for c in range(o_ref.shape[1]):
        o_ref[:, c, :] = t[:, c * LANES:(c + 1) * LANES]


def _rpb_tiles(rpb):
    n_dr = 2 * WIN_ROWS - 1
    rows = N_HEADS * n_dr
    onehot, mask = _rpb_constants()
    r = jnp.zeros((rows, LANES), F32).at[:, :2 * WIN_COLS - 1].set(rpb.reshape(rows, 2 * WIN_COLS - 1))
    cols_per_step = 16
    tn = cols_per_step * LANES
    out = pl.pallas_call(
        _rpb_kernel,
        out_shape=jax.ShapeDtypeStruct((rows, GRID_W, LANES), F32),
        grid=(GRID_W // cols_per_step,),
        in_specs=[pl.BlockSpec((rows, LANES), lambda j: (0, 0)),
                  pl.BlockSpec((LANES, tn), lambda j: (0, j)),
                  pl.BlockSpec((1, tn), lambda j: (0, j))],
        out_specs=pl.BlockSpec((rows, cols_per_step, LANES), lambda j: (0, j, 0)),
        name="rpb_tiles",
    )(r, onehot, mask)
    return out.reshape(N_HEADS, n_dr, GRID_W, LANES)


def _inproj_kernel(xl_ref, xc_ref, mod_ref, w_hbm, z_ref, kt_ref, vt_ref, w_ref, stage_ref, sem, *, n_lat_tiles):
    n_col = D_IN // D_MODEL
    n_stage = stage_ref.shape[0]
    epilogue = {COL_GH: _silu, COL_GA: _silu, COL_MH: jax.nn.sigmoid, COL_MA: jax.nn.sigmoid,
                COL_Q: lambda acc: acc * (ATTN_SCALE * LOG2E)}

    def columns(col):
        return slice(col * D_MODEL, (col + 1) * D_MODEL)

    def fetch(n):
        return pltpu.make_async_copy(w_hbm.at[:, columns(PROJ_ORDER[n])], stage_ref.at[n % n_stage], sem.at[n % n_stage])

    def tile(stream_weights):
        if stream_weights:
            for n in range(min(n_stage, n_col)):
                fetch(n).start()
        shift = mod_ref[:, 0:D_MODEL]
        scale = mod_ref[:, D_MODEL:2 * D_MODEL]
        x = jnp.where(pl.program_id(0) < n_lat_tiles, xl_ref[...], xc_ref[...])
        h = (x * (1.0 + scale) + shift).astype(BF16)
        for n, col in enumerate(PROJ_ORDER):
            if stream_weights:
                fetch(n).wait()
                w_ref[:, columns(col)] = stage_ref[n % n_stage].astype(BF16)
                if n + n_stage < n_col:
                    fetch(n + n_stage).start()
            acc = _dot(h, w_ref[:, columns(col)])
            z_ref[:, columns(col)] = epilogue.get(col, lambda acc: acc)(acc).astype(BF16)
            if col == COL_K:
                kt_ref[...] = acc.T
            if col == COL_V:
                vt_ref[...] = acc.T

    pl.when(pl.program_id(0) == 0)(functools.partial(tile, True))
    pl.when(pl.program_id(0) != 0)(functools.partial(tile, False))


def _in_projection(x_lat, x_ctx, mod3, lat_row, ctx_row, w_in, *, tm):
    n_lat, n_ctx = x_lat.shape[0] // tm, x_ctx.shape[0] // tm
    is_lat = lambda i: i < n_lat
    lat_tile = lambda i: jnp.minimum(i, n_lat - 1)
    ctx_tile = lambda i: jnp.maximum(i - n_lat, 0)
    kv_shape = jax.ShapeDtypeStruct((n_ctx, D_ATTN, tm), F32)
    kv_spec = pl.BlockSpec((None, D_ATTN, tm), lambda i: (ctx_tile(i), 0, 0))
    stage = (W_STAGES, D_MODEL, D_MODEL)
    blocks = (_nbytes((D_MODEL, D_IN), BF16) + _nbytes(stage, F32) + 4 * _nbytes((tm, D_MODEL), F32)
              + 2 * _nbytes((tm, D_IN), BF16) + 4 * _nbytes((tm, D_ATTN), F32))
    return pl.pallas_call(
        functools.partial(_inproj_kernel, n_lat_tiles=n_lat),
        out_shape=[jax.ShapeDtypeStruct(((n_lat + n_ctx) * tm, D_IN), BF16), kv_shape, kv_shape],
        grid=(n_lat + n_ctx,),
        in_specs=[pl.BlockSpec((tm, D_MODEL), lambda i: (lat_tile(i), 0)),
                  pl.BlockSpec((tm, D_MODEL), lambda i: (ctx_tile(i), 0)),
                  pl.BlockSpec((None, 1, 3 * D_MODEL),
                               lambda i: (jnp.where(is_lat(i), lat_row(lat_tile(i)), ctx_row), 0, 0)),
                  pl.BlockSpec(memory_space=pl.ANY)],
        out_specs=[pl.BlockSpec((tm, D_IN), lambda i: (i, 0)), kv_spec, kv_spec],
        scratch_shapes=[pltpu.VMEM((D_MODEL, D_IN), BF16), pltpu.VMEM(stage, F32),
                        pltpu.SemaphoreType.DMA((W_STAGES,))],
        compiler_params=pltpu.CompilerParams(
            dimension_semantics=("arbitrary",),
            vmem_limit_bytes=_vmem_limit(blocks, 4 * _nbytes((tm, D_MODEL), F32))),
        name="inproj",
    )(x_lat, x_ctx, mod3, w_in)


def _hyena_kernel(vh_ref, x1_ref, x0_ref, gh_ref, cwv_ref, cw1_ref, cw0_ref, cbv_ref, cb1_ref, cb0_ref,
                  f_ref, g_ref, fp_ref, fqa_ref, fpb_ref, o_ref):
    L = fp_ref.shape[0]
    slab = jax.lax.broadcasted_iota(jnp.int32, (SUBLANES, 1), 0)

    def short_conv(x_ref, rows, w_ref, b_ref):
        x = x_ref[rows, :].astype(F32)
        prev = pltpu.roll(x, 1, 0)
        nxt = pltpu.roll(x, L - 1, 0)
        prev = jnp.concatenate([jnp.where(slab == 0, 0.0, prev[:SUBLANES]), prev[SUBLANES:]], axis=0)
        nxt = jnp.concatenate([nxt[:-SUBLANES], jnp.where(slab == SUBLANES - 1, 0.0, nxt[-SUBLANES:])], axis=0)
        return prev * w_ref[0:1, :] + x * w_ref[1:2, :] + nxt * w_ref[2:3, :] + b_ref[...]

    for s in range(vh_ref.shape[0] // L):
        rows = slice(s * L, (s + 1) * L)
        u = short_conv(vh_ref, rows, cwv_ref, cbv_ref) * short_conv(x1_ref, rows, cw1_ref, cb1_ref)
        t = _dot(f_ref[...], u.astype(BF16))
        p = t[:L]
        q = t[L:]
        yp = p * fp_ref[...] - q * fqa_ref[...]
        yq = p * fqa_ref[...] + q * fpb_ref[...]
        y = _dot(g_ref[...], jnp.concatenate([yp.astype(BF16), yq.astype(BF16)], axis=0))
        y_h = y * short_conv(x0_ref, rows, cw0_ref, cb0_ref)
        o_ref[rows, :] = (y_h * gh_ref[rows, :].astype(F32)).astype(BF16)


def _hyena_branch(z, z_row0, B, L, conv_w, conv_b, dft, spectra, *, tc=256, seqs_per_step):
    nblk = D_HYENA // tc
    tm = seqs_per_step * L
    assert z_row0 % tm == 0
    zcol = lambda piece: pl.BlockSpec((tm, tc), lambda c, b: (z_row0 // tm + b, piece * nblk + c))
    wcol = lambda rows, piece: pl.BlockSpec((rows, tc), lambda c, b: (0, piece * nblk + c))
    chan = lambda rows: pl.BlockSpec((rows, tc), lambda c, b: (0, c))
    const = lambda shape: pl.BlockSpec(shape, lambda c, b: (0, 0), pipeline_mode=pl.Buffered(1))
    blocks = (2 * _nbytes((2 * L, L), BF16) + 2 * 5 * _nbytes((tm, tc), BF16) + 2 * 3 * _nbytes((L, tc), F32))
    return pl.pallas_call(
        _hyena_kernel,
        out_shape=jax.ShapeDtypeStruct((B * L, D_HYENA), BF16),
        grid=(nblk, B // seqs_per_step),
        in_specs=[zcol(COL_VH), zcol(COL_X1), zcol(COL_X0), zcol(COL_GH),
                  wcol(3, 0), wcol(3, 1), wcol(3, 2), wcol(1, 0), wcol(1, 1), wcol(1, 2),
                  const((2 * L, L)), const((L, 2 * L)), chan(L), chan(L), chan(L)],
        out_specs=pl.BlockSpec((tm, tc), lambda c, b: (b, c)),
        compiler_params=pltpu.CompilerParams(
            dimension_semantics=("arbitrary", "arbitrary"),
            vmem_limit_bytes=_vmem_limit(blocks, 16 * _nbytes((tm, tc), F32))),
        name=f"hyena_{L}",
    )(z, z, z, z, conv_w, conv_w, conv_w, conv_b.reshape(1, -1), conv_b.reshape(1, -1), conv_b.reshape(1, -1),
      *dft, *spectra)


def _head_masks(width):
    lane = jax.lax.broadcasted_iota(jnp.int32, (1, width), 1)
    return [(lane >= h * HEAD_DIM) & (lane < (h + 1) * HEAD_DIM) for h in range(width // HEAD_DIM)]


def _qk(q, k):
    return jax.lax.dot_general(q, k, (((1,), (1,)), ((), ())), preferred_element_type=F32)


def _stack_heads(q, masks):
    return jnp.concatenate([jnp.where(msk, q, jnp.zeros_like(q)) for msk in masks], axis=0)


def _unstack_heads(o, masks):
    n = o.shape[0] // len(masks)
    out = o[:n]
    for h in range(1, len(masks)):
        out = jnp.where(masks[h], o[h * n:(h + 1) * n], out)
    return out


def _ctx_attn_kernel(q_ref, k_ref, v_ref, g_ref, o_ref):
    masks = _head_masks(CTX_HEAD_TILE)
    for t in range(D_ATTN // CTX_HEAD_TILE):
        cols = slice(t * CTX_HEAD_TILE, (t + 1) * CTX_HEAD_TILE)
        s = _qk(_stack_heads(q_ref[:, cols], masks), k_ref[:, cols])
        m = jnp.max(s, axis=-1, keepdims=True)
        p = jnp.exp2(s - m)
        l = jnp.sum(p, axis=-1, keepdims=True)
        out = _unstack_heads(_dot(p.astype(BF16), v_ref[:, cols]) / l, masks)
        o_ref[:, cols] = (out * g_ref[:, cols].astype(F32)).astype(BF16)


def _context_attention(z, z_row0, B, L):
    assert z_row0 % L == 0
    zcol = lambda piece: pl.BlockSpec((L, D_ATTN), lambda b: (z_row0 // L + b, piece))
    return pl.pallas_call(
        _ctx_attn_kernel,
        out_shape=jax.ShapeDtypeStruct((B * L, D_ATTN), BF16),
        grid=(B,),
        in_specs=[zcol(COL_Q), zcol(COL_K), zcol(COL_V), zcol(COL_GA)],
        out_specs=pl.BlockSpec((L, D_ATTN), lambda b: (b, 0)),
        compiler_params=pltpu.CompilerParams(dimension_semantics=("arbitrary",)),
        name="ctx_attn",
    )(z, z, z, z)


def _key_window_start(qb):
    rows = 1024 // GRID_W
    r_first = qb * Q_BLOCK_ROWS
    r_last = r_first + Q_BLOCK_ROWS - 1
    lo = min(max(r_first - WIN_ROWS // 2, 0), rows - WIN_ROWS)
    hi = min(max(r_last - WIN_ROWS // 2, 0), rows - WIN_ROWS) + WIN_ROWS
    start = min(lo, rows - KEY_WIN_ROWS)
    start -= start % Q_BLOCK_ROWS
    assert start <= lo and hi <= start + KEY_WIN_ROWS
    return start


def _nbr_attn_kernel(q_ref, k0_ref, k1_ref, k2_ref, v0_ref, v1_ref, v2_ref, kc_ref, vc_ref, g_ref, tiles_ref,
                     o_ref, bias_ref, *, n_rows):
    masks = _head_masks(NBR_HEAD_TILE)
    n_heads = q_ref.shape[-1] // HEAD_DIM
    chain_tiles = [slice(c * NBR_HEAD_TILE, (c + 1) * NBR_HEAD_TILE) for c in range(q_ref.shape[-1] // NBR_HEAD_TILE)]
    k_refs = (k0_ref, k1_ref, k2_ref)
    v_refs = (v0_ref, v1_ref, v2_ref)
    key_blocks = KEY_WIN // Q_BLOCK

    def window_row_start(r):
        return min(max(r - WIN_ROWS // 2, 0), n_rows - WIN_ROWS)

    def build_bias(qb):
        k_start = _key_window_start(qb)
        used = set()
        for i in range(Q_BLOCK_ROWS):
            r = qb * Q_BLOCK_ROWS + i
            r0 = window_row_start(r)
            for j in range(KEY_WIN_ROWS):
                kr = k_start + j
                cols = slice(j * GRID_W, (j + 1) * GRID_W)
                lanes = slice((j % RPB_COPIES) * GRID_W, (j % RPB_COPIES + 1) * GRID_W)
                inside = r0 <= kr < r0 + WIN_ROWS
                if inside:
                    used.add(j // Q_BLOCK_ROWS)
                for h in range(n_heads):
                    rows = slice(h * Q_BLOCK + i * GRID_W, h * Q_BLOCK + (i + 1) * GRID_W)
                    if inside:
                        bias_ref[rows, cols] = tiles_ref[h, kr - r + WIN_ROWS - 1, :, lanes]
                    else:
                        bias_ref[rows, cols] = jnp.full((GRID_W, GRID_W), NEG_INF, F32)
        return sorted(used)

    def attend(blocks):
        for c, lanes in enumerate(chain_tiles):
            stacked = slice(c * len(masks) * Q_BLOCK, (c + 1) * len(masks) * Q_BLOCK)
            for b in range(q_ref.shape[0]):
                qs = _stack_heads(q_ref[b, :, lanes], masks)
                s = [_qk(qs, k_refs[j][b, :, lanes]) + bias_ref[stacked, j * Q_BLOCK:(j + 1) * Q_BLOCK] for j in blocks]
                s.append(_dot(qs, kc_ref[b, lanes, :].astype(BF16)))
                m = jnp.max(functools.reduce(jnp.maximum, s), axis=-1, keepdims=True)
                p = [jnp.exp2(x - m) for x in s]
                l = jnp.sum(functools.reduce(jnp.add, p), axis=-1, keepdims=True)
                p = [x.astype(BF16) for x in p]
                v_win = jnp.concatenate([v_refs[j][b, :, lanes] for j in blocks], axis=0)
                o = _dot(jnp.concatenate(p[:-1], axis=1), v_win) + _qk(p[-1], vc_ref[b, lanes, :].astype(BF16))
                o_ref[b, :, lanes] = (_unstack_heads(o / l, masks) * g_ref[b, :, lanes].astype(F32)).astype(BF16)

    for qb in range(n_rows // Q_BLOCK_ROWS):
        @pl.when(pl.program_id(0) == qb)
        def _(qb=qb):
            blocks = build_bias(qb)
            assert blocks and all(0 <= j < key_blocks for j in blocks)
            attend(blocks)


def _neighbourhood_attention(z, z_row0, B, L, cache_k, cache_v, tiles):
    assert z_row0 % (B * L) == 0
    group = z_row0 // (B * L)
    n_rows = L // GRID_W
    n_qb = L // Q_BLOCK
    assert KEY_WIN == 3 * Q_BLOCK and all(_key_window_start(qb) % Q_BLOCK_ROWS == 0 for qb in range(n_qb))
    z3 = z.reshape(z.shape[0] // L, L, D_IN)
    n_ctx = cache_k.shape[1]
    kc = jnp.transpose(cache_k, (0, 2, 3, 1)).reshape(B, D_ATTN, n_ctx)
    vc = jnp.transpose(cache_v, (0, 2, 3, 1)).reshape(B, D_ATTN, n_ctx)
    width = NBR_BLOCK_LANES
    tiles_per_col = D_ATTN // width

    def win_block(qb):
        return jnp.where(qb < n_qb // 2, 0, (n_rows - KEY_WIN_ROWS) // Q_BLOCK_ROWS)

    assert [_key_window_start(qb) // Q_BLOCK_ROWS for qb in range(n_qb)] == \
        [0 if qb < n_qb // 2 else (n_rows - KEY_WIN_ROWS) // Q_BLOCK_ROWS for qb in range(n_qb)]
    qspec = lambda piece: pl.BlockSpec((B, Q_BLOCK, width), lambda qb, t: (group, qb, piece * tiles_per_col + t))
    kspec = lambda piece, j: pl.BlockSpec((B, Q_BLOCK, width),
                                          lambda qb, t: (group, win_block(qb) + j, piece * tiles_per_col + t))
    cspec = pl.BlockSpec((B, width, n_ctx), lambda qb, t: (0, t, 0))
    out = pl.pallas_call(
        functools.partial(_nbr_attn_kernel, n_rows=n_rows),
        out_shape=jax.ShapeDtypeStruct((B, L, D_ATTN), BF16),
        grid=(n_qb, tiles_per_col),
        in_specs=[qspec(COL_Q), kspec(COL_K, 0), kspec(COL_K, 1), kspec(COL_K, 2),
                  kspec(COL_V, 0), kspec(COL_V, 1), kspec(COL_V, 2), cspec, cspec, qspec(COL_GA),
                  pl.BlockSpec((width // HEAD_DIM, 2 * WIN_ROWS - 1, GRID_W, LANES), lambda qb, t: (t, 0, 0, 0))],
        out_specs=pl.BlockSpec((B, Q_BLOCK, width), lambda qb, t: (0, qb, t)),
        scratch_shapes=[pltpu.VMEM((width // HEAD_DIM * Q_BLOCK, KEY_WIN), F32)],
        compiler_params=pltpu.CompilerParams(dimension_semantics=("arbitrary", "arbitrary")),
        name="nbr_attn",
    )(z3, z3, z3, z3, z3, z3, z3, kc, vc, z3, tiles)
    return out.reshape(B * L, D_ATTN)


def _merge_kernel(yh_ref, ya_ref, mh_ref, ma_ref, x_ref, mod_ref, wbh_ref, wba_ref, wout_ref, lng_ref, lnb_ref,
                  o_ref, *, alpha):
    p_h = _dot(yh_ref[...], wbh_ref[...])
    p_a = _dot(ya_ref[...], wba_ref[...])
    m_h = mh_ref[...].astype(F32)
    m_a = ma_ref[...].astype(F32)
    out = _dot((m_h * p_h + m_a * p_a).astype(BF16), wout_ref[...])
    gate = mod_ref[:, 2 * D_MODEL:3 * D_MODEL]
    v = alpha * x_ref[...] + gate * out
    mu = jnp.mean(v, axis=-1, keepdims=True)
    d = v - mu
    var = jnp.mean(d * d, axis=-1, keepdims=True)
    o_ref[...] = d * jax.lax.rsqrt(var + LN_EPS) * lng_ref[...] + lnb_ref[...]


def _merge(yh, ya, z, z_row0, x2d, mod3, mod_index, w_bh, w_ba, w_out, ln_g, ln_b, *, alpha, tm):
    M = x2d.shape[0]
    assert z_row0 % tm == 0
    tok = lambda width, col: pl.BlockSpec((tm, width), lambda i: (i, col))
    ztok = lambda col: pl.BlockSpec((tm, D_MODEL), lambda i: (z_row0 // tm + i, col))
    const = lambda shape: pl.BlockSpec(shape, lambda i: (0, 0))
    return pl.pallas_call(
        functools.partial(_merge_kernel, alpha=alpha),
        out_shape=jax.ShapeDtypeStruct((M, D_MODEL), F32),
        grid=(M // tm,),
        in_specs=[tok(D_HYENA, 0), tok(D_ATTN, 0), ztok(COL_MH), ztok(COL_MA), tok(D_MODEL, 0),
                  pl.BlockSpec((None, 1, 3 * D_MODEL), lambda i: (mod_index(i), 0, 0)),
                  const((D_HYENA, D_MODEL)), const((D_ATTN, D_MODEL)), const((D_MODEL, D_MODEL)),
                  const((1, D_MODEL)), const((1, D_MODEL))],
        out_specs=pl.BlockSpec((tm, D_MODEL), lambda i: (i, 0)),
        compiler_params=pltpu.CompilerParams(
            dimension_semantics=("arbitrary",),
            vmem_limit_bytes=_vmem_limit(
                2 * (4 * _nbytes((tm, D_MODEL), BF16) + 2 * _nbytes((tm, D_MODEL), F32)
                     + 3 * _nbytes((D_MODEL, D_MODEL), BF16)),
                8 * _nbytes((tm, D_MODEL), F32))),
        name="merge",
    )(yh, ya, z, z, x2d, mod3, w_bh, w_ba, w_out, ln_g.reshape(1, -1), ln_b.reshape(1, -1))


def _mixers(x, z, z_row0, mod3, cond_row, ctx_kv, p, filt, tiles, *, alpha):
    B, L, _ = x.shape
    dft = _dft_matrices(L)
    spectra = _filter_spectra(L, dft[0], p["hyena_d"], *filt)
    yh = _hyena_branch(z, z_row0, B, L, p["conv_w"], p["conv_b"], dft, spectra,
                       seqs_per_step=max(1, HYENA_ROWS_PER_STEP // L))
    if ctx_kv is None:
        ya = _context_attention(z, z_row0, B, L)
    else:
        ya = _neighbourhood_attention(z, z_row0, B, L, ctx_kv[0], ctx_kv[1], tiles)
    y = _merge(yh, ya, z, z_row0, x.reshape(B * L, D_MODEL), mod3, lambda i: cond_row(i * MERGE_TM // L),
               p["w_bh"], p["w_ba"], p["w_out"], p["ln_g"], p["ln_b"], alpha=alpha, tm=MERGE_TM)
    return y.reshape(B, L, D_MODEL)


def kernel(x_prompt, x_sample, c, cache_k, cache_v, c_ctx, w_ada, b_ada, w_in, conv_w, conv_b, filt_w1, filt_b1,
           filt_w2, filt_b2, filt_w3, filt_freq, hyena_d, rpb, w_bh, w_ba, w_out, ln_g, ln_b):
    depth = w_in.shape[0]
    alpha = (2.0 * depth) ** 0.25
    n_lat, lat_len = x_sample.shape[0], x_sample.shape[1]
    n_ctx, ctx_len = x_prompt.shape[0], x_prompt.shape[1]
    assert ctx_len == IN_TM
    ctx_row = n_lat
    cond = jnp.zeros((8, D_MODEL), F32).at[:n_lat].set(c).at[ctx_row].set(c_ctx)

    xp, xs = x_prompt, x_sample
    new_k, new_v = [], []
    for l in range(depth):
        p = {"conv_w": conv_w[l], "conv_b": conv_b[l], "hyena_d": hyena_d[l],
             "w_bh": w_bh[l].astype(BF16), "w_ba": w_ba[l].astype(BF16), "w_out": w_out[l].astype(BF16),
             "ln_g": ln_g[l], "ln_b": ln_b[l]}
        mod3 = _modulation(cond, w_ada[l], b_ada[l]).reshape(8, 1, 3 * D_MODEL)
        filt = (filt_w1[l], filt_b1[l], filt_w2[l], filt_b2[l], filt_w3[l], filt_freq[l])
        tiles = _rpb_tiles(rpb[l])
        z, k_ctx, v_ctx = _in_projection(xs.reshape(-1, D_MODEL), xp.reshape(-1, D_MODEL), mod3,
                                         lambda tile: tile * IN_TM // lat_len, ctx_row, w_in[l], tm=IN_TM)
        new_k.append(jnp.transpose(k_ctx.reshape(n_ctx, N_HEADS, HEAD_DIM, ctx_len), (0, 3, 1, 2)))
        new_v.append(jnp.transpose(v_ctx.reshape(n_ctx, N_HEADS, HEAD_DIM, ctx_len), (0, 3, 1, 2)))
        xp = _mixers(xp, z, n_lat * lat_len, mod3, lambda b: ctx_row, None, p, filt, None, alpha=alpha)
        xs = _mixers(xs, z, 0, mod3, lambda b: b, (cache_k[:, l], cache_v[:, l]), p, filt, tiles, alpha=alpha)
    return xp, xs, jnp.stack(new_k, axis=1), jnp.stack(new_v, axis=1)
```

```python
import functools
import math

import jax
import jax.numpy as jnp
import numpy as np
from jax.experimental import pallas as pl
from jax.experimental.pallas import tpu as pltpu

F32 = jnp.float32
BF16 = jnp.bfloat16

D_MODEL = 1024
D_HYENA = 1024
N_HEADS = 16
HEAD_DIM = 64
D_ATTN = N_HEADS * HEAD_DIM
D_IN = 4 * D_HYENA + 4 * D_ATTN + 2 * D_MODEL
GRID_W = 64
WIN_ROWS = 8
WIN_COLS = 16
FILTER_EMB = 33
FILTER_BANDS = (FILTER_EMB - 1) // 2
FILTER_HIDDEN = 64
DECAY_TARGET = 1e-2
MIN_DECAY = math.log(DECAY_TARGET) / 1.5
MAX_DECAY = math.log(DECAY_TARGET) / 0.3
DECAY_SHIFT = 0.05
LN_EPS = 1e-5
NEG_INF = -1e30

COL_VH, COL_X1, COL_X0, COL_GH, COL_Q, COL_K, COL_V, COL_GA, COL_MH, COL_MA = range(10)
PROJ_ORDER = (COL_GH, COL_GA, COL_MH, COL_MA, COL_K, COL_V, COL_Q, COL_VH, COL_X1, COL_X0)

LANES = 128
SUBLANES = 8
CTX_HEAD_TILE = 128
NBR_HEAD_TILE = 128
NBR_BLOCK_LANES = 128
RPB_COPIES = LANES // GRID_W
V7X_VMEM_BYTES = 64 * 1024 * 1024
ATTN_SCALE = HEAD_DIM ** -0.5
LOG2E = math.log2(math.e)

Q_BLOCK = 256
Q_BLOCK_ROWS = Q_BLOCK // GRID_W
KEY_WIN_ROWS = 12
KEY_WIN = KEY_WIN_ROWS * GRID_W

IN_TM = 256
W_STAGES = 2
CAST_ROWS = 64
MERGE_TM = 512
HYENA_ROWS_PER_STEP = 4096


def _vmem_limit(block_bytes, temp_bytes):
    need = int(block_bytes + temp_bytes)
    return min(max(need, 16 * 1024 * 1024), V7X_VMEM_BYTES - 8 * 1024 * 1024)


def _nbytes(shape, dtype):
    return int(np.prod(shape)) * jnp.dtype(dtype).itemsize


def _silu(x):
    return x * jax.nn.sigmoid(x)


def _split_bf16(a):
    hi = a.astype(BF16)
    lo = (a - hi.astype(F32)).astype(BF16)
    return hi, lo


def _dot(a, b):
    return jnp.dot(a, b, preferred_element_type=F32)


def _dot_split(a_hi, a_lo, b):
    b_hi, b_lo = _split_bf16(b)
    return _dot(a_hi, b_hi) + (_dot(a_hi, b_lo) + _dot(a_lo, b_hi))


@functools.lru_cache(maxsize=None)
def _dft_constants(L):
    n = 2 * L
    k = np.arange(L, dtype=np.float64)[:, None]
    s = np.arange(L, dtype=np.float64)[None, :]
    ang = 2.0 * np.pi * k * s / n
    nyq = np.cos(np.pi * np.arange(L, dtype=np.float64))
    C = np.cos(ang)
    S = np.sin(ang)
    S[0, :] = nyq
    F = np.concatenate([C, S], axis=0)
    w = np.full((L,), 2.0)
    w[0] = 1.0
    Gc = (C * w[:, None]).T / n
    Ss = 2.0 * np.sin(ang)
    Ss[0, :] = nyq
    Gs = Ss.T / n
    G = np.concatenate([Gc, Gs], axis=1)
    return F.astype(np.float32), G.astype(np.float32)


def _dft_matrices(L):
    F, G = _dft_constants(L)
    return jnp.asarray(F).astype(BF16), jnp.asarray(G).astype(BF16)


@functools.lru_cache(maxsize=None)
def _filter_constants(L):
    t = np.linspace(0.0, 1.0, L, dtype=np.float32)[:, None]
    bands = np.linspace(1e-4, FILTER_BANDS - 1, FILTER_BANDS, dtype=np.float32)[None]
    w = (2.0 * math.pi * np.arange(L, dtype=np.float32)[:, None] / L).astype(np.float32)
    z = np.concatenate([t, np.cos(bands * w), -np.sin(bands * w)], axis=-1).astype(np.float32)
    z_pad = np.zeros((L, LANES), np.float32)
    z_pad[:, :FILTER_EMB] = z
    deltas = np.linspace(MIN_DECAY, MAX_DECAY, D_HYENA, dtype=np.float32)
    decay = (np.exp(-t * np.abs(deltas)) + np.float32(DECAY_SHIFT)).astype(np.float32)
    return z_pad, decay


@functools.lru_cache(maxsize=None)
def _rpb_constants():
    c = np.arange(GRID_W)[:, None]
    kc = np.arange(GRID_W)[None, :]
    c0 = np.clip(c - WIN_COLS // 2, 0, GRID_W - WIN_COLS)
    in_win = (kc >= c0) & (kc < c0 + WIN_COLS)
    idx = np.clip(kc - c + WIN_COLS - 1, 0, 2 * WIN_COLS - 2)
    onehot = np.zeros((LANES, GRID_W, LANES), np.float32)
    for half in range(RPB_COPIES):
        lanes = half * GRID_W + np.arange(GRID_W)
        onehot[idx, c, lanes[None, :]] = 1.0
    mask = np.concatenate([in_win, in_win], axis=1).astype(np.float32)
    return onehot.reshape(LANES, GRID_W * LANES).astype(BF16), mask.reshape(1, GRID_W * LANES)


def _mod_kernel(c_ref, cctx_ref, w_ref, b_ref, o_ref, cond_ref):
    n_lat = c_ref.shape[0]
    cond_ref[...] = jnp.zeros(cond_ref.shape, F32)
    cond_ref[0:n_lat, :] = c_ref[...]
    cond_ref[n_lat:n_lat + 1, :] = cctx_ref[...]
    s = _silu(cond_ref[...]).astype(BF16)
    o_ref[:, 0, :] = _dot(s, w_ref[...].astype(BF16)) + b_ref[...]


def _modulation(c, c_ctx, w_ada, b_ada):
    n_lat = c.shape[0]
    assert n_lat + 1 <= SUBLANES
    return pl.pallas_call(
        _mod_kernel,
        out_shape=jax.ShapeDtypeStruct((SUBLANES, 1, 3 * D_MODEL), F32),
        grid=(3,),
        in_specs=[pl.BlockSpec((n_lat, D_MODEL), lambda j: (0, 0)),
                  pl.BlockSpec((1, D_MODEL), lambda j: (0, 0)),
                  pl.BlockSpec((D_MODEL, D_MODEL), lambda j: (0, j)),
                  pl.BlockSpec((1, D_MODEL), lambda j: (0, j))],
        out_specs=pl.BlockSpec((SUBLANES, 1, D_MODEL), lambda j: (0, 0, j)),
        scratch_shapes=[pltpu.VMEM((SUBLANES, D_MODEL), F32)],
        name="mod",
    )(c, c_ctx, w_ada, b_ada.reshape(1, -1))


def _spectra_kernel(z_ref, w1_ref, b1_ref, w2_ref, b2_ref, w3f_ref, w3b_ref, freq_ref, decay_ref, d_ref, f_ref,
                    fp_ref, fqa_ref, fpb_ref, hf_ref):
    L = decay_ref.shape[0]

    @pl.when(pl.program_id(0) == 0)
    def _():
        z_hi, z_lo = _split_bf16(z_ref[...])
        hf = jnp.sin(freq_ref[0:1, :] * (_dot_split(z_hi, z_lo, w1_ref[...]) + b1_ref[...]))
        h_hi, h_lo = _split_bf16(hf)
        hf_ref[...] = jnp.sin(freq_ref[1:2, :] * (_dot_split(h_hi, h_lo, w2_ref[...]) + b2_ref[...]))

    h_hi, h_lo = _split_bf16(hf_ref[...])
    decay = decay_ref[...]
    h_fwd = _dot_split(h_hi, h_lo, w3f_ref[...]) * decay
    h_bwd = _dot_split(h_hi, h_lo, w3b_ref[...]) * decay
    row0 = jax.lax.broadcasted_iota(jnp.int32, (L, 1), 0) == 0
    h_bwd = jnp.where(row0, 0.0, h_bwd)
    fsum = (h_fwd + h_bwd).astype(BF16)
    fdif = (h_fwd - h_bwd).astype(BF16)
    skip = d_ref[...]
    fp = _dot(f_ref[0:L, :], fsum) + skip
    fq = _dot(f_ref[L:2 * L, :], fdif)
    nyq = _dot(f_ref[L:L + 2 * SUBLANES, :], fsum)[0:1, :] + skip
    fp_ref[...] = fp
    fqa_ref[...] = jnp.where(row0, 0.0, fq)
    fpb_ref[...] = jnp.where(row0, nyq, fp)


def _filter_spectra(L, f_mat, hyena_d, w1, b1, w2, b2, w3, freq):
    tc = 256
    z_emb, decay = _filter_constants(L)
    w1p = jnp.zeros((LANES, FILTER_HIDDEN), F32).at[:FILTER_EMB].set(w1)
    nblk = D_HYENA // tc
    const = lambda shape: pl.BlockSpec(shape, lambda c: (0, 0), pipeline_mode=pl.Buffered(1))
    out_spec = pl.BlockSpec((L, tc), lambda c: (0, c))
    blocks = (_nbytes((2 * L, L), BF16) + _nbytes((L, LANES), F32) + 2 * 4 * _nbytes((L, tc), F32))
    return pl.pallas_call(
        _spectra_kernel,
        out_shape=[jax.ShapeDtypeStruct((L, D_HYENA), F32)] * 3,
        grid=(nblk,),
        in_specs=[const((L, LANES)),
                  const((LANES, FILTER_HIDDEN)), const((1, FILTER_HIDDEN)),
                  const((FILTER_HIDDEN, FILTER_HIDDEN)), const((1, FILTER_HIDDEN)),
                  pl.BlockSpec((FILTER_HIDDEN, tc), lambda c: (0, c)),
                  pl.BlockSpec((FILTER_HIDDEN, tc), lambda c: (0, nblk + c)),
                  const((2, FILTER_HIDDEN)),
                  pl.BlockSpec((L, tc), lambda c: (0, c)),
                  pl.BlockSpec((1, tc), lambda c: (0, c)),
                  const((2 * L, L))],
        out_specs=[out_spec] * 3,
        scratch_shapes=[pltpu.VMEM((L, FILTER_HIDDEN), F32)],
        compiler_params=pltpu.CompilerParams(
            dimension_semantics=("arbitrary",),
            vmem_limit_bytes=_vmem_limit(blocks, 12 * _nbytes((L, tc), F32))),
        name=f"spectra_{L}",
    )(z_emb, w1p, b1.reshape(1, -1), w2, b2.reshape(1, -1), w3, w3, freq, decay, hyena_d.reshape(1, -1), f_mat)


def _rpb_kernel(r_ref, onehot_ref, mask_ref, o_ref):
    r = r_ref[...]
    r_hi = r.astype(BF16)
    rem = r - r_hi.astype(F32)
    r_mid = rem.astype(BF16)
    r_lo = (rem - r_mid.astype(F32)).astype(BF16)
    oh = onehot_ref[...]
    t = _dot(r_hi, oh) + (_dot(r_mid, oh) + _dot(r_lo, oh))
    t = jnp.where(mask_ref[...] > 0.0, t * LOG2E, NEG_INF)
    for c in range(o_ref.shape[1]):
        o_ref[:, c, :] = t[:, c * LANES:(c + 1) * LANES]


def _rpb_tiles(rpb):
    n_dr = 2 * WIN_ROWS - 1
    rows = N_HEADS * n_dr
    onehot, mask = _rpb_constants()
    r = jnp.pad(rpb.reshape(rows, 2 * WIN_COLS - 1), ((0, 0), (0, LANES - (2 * WIN_COLS - 1))))
    cols_per_step = 16
    tn = cols_per_step * LANES
    out = pl.pallas_call(
        _rpb_kernel,
        out_shape=jax.ShapeDtypeStruct((rows, GRID_W, LANES), F32),
        grid=(GRID_W // cols_per_step,),
        in_specs=[pl.BlockSpec((rows, LANES), lambda j: (0, 0)),
                  pl.BlockSpec((LANES, tn), lambda j: (0, j)),
                  pl.BlockSpec((1, tn), lambda j: (0, j))],
        out_specs=pl.BlockSpec((rows, cols_per_step, LANES), lambda j: (0, j, 0)),
        name="rpb_tiles",
    )(r, onehot, mask)
    return out.reshape(N_HEADS, n_dr, GRID_W, LANES)


def _inproj_kernel(xl_ref, xc_ref, mod_ref, w_hbm, wbh_ref, wba_ref, wout_ref,
                   z_ref, kt_ref, vt_ref, wbh16_ref, wba16_ref, wout16_ref, w_ref, stage_ref, sem, *, n_lat_tiles):
    n_col = D_IN // D_MODEL
    n_stage = stage_ref.shape[0]
    epilogue = {COL_GH: _silu, COL_GA: _silu, COL_MH: jax.nn.sigmoid, COL_MA: jax.nn.sigmoid,
                COL_Q: lambda acc: acc * (ATTN_SCALE * LOG2E)}

    def columns(col):
        return slice(col * D_MODEL, (col + 1) * D_MODEL)

    def fetch(n):
        return pltpu.make_async_copy(w_hbm.at[:, columns(PROJ_ORDER[n])], stage_ref.at[n % n_stage], sem.at[n % n_stage])

    def tile(stream_weights):
        if stream_weights:
            for n in range(min(n_stage, n_col)):
                fetch(n).start()
        shift = mod_ref[:, 0:D_MODEL]
        scale = mod_ref[:, D_MODEL:2 * D_MODEL]
        x = jnp.where(pl.program_id(0) < n_lat_tiles, xl_ref[...], xc_ref[...])
        h = (x * (1.0 + scale) + shift).astype(BF16)
        for src_ref, dst_ref in ((wbh_ref, wbh16_ref), (wba_ref, wba16_ref), (wout_ref, wout16_ref)):
            dst_ref[...] = src_ref[...].astype(BF16)
        for n, col in enumerate(PROJ_ORDER):
            if stream_weights:
                fetch(n).wait()
                w_ref[:, columns(col)] = stage_ref[n % n_stage].astype(BF16)
                if n + n_stage < n_col:
                    fetch(n + n_stage).start()
            acc = _dot(h, w_ref[:, columns(col)])
            z_ref[:, columns(col)] = epilogue.get(col, lambda acc: acc)(acc).astype(BF16)
            if col == COL_K:
                kt_ref[...] = acc.T
            if col == COL_V:
                vt_ref[...] = acc.T

    pl.when(pl.program_id(0) == 0)(functools.partial(tile, True))
    pl.when(pl.program_id(0) != 0)(functools.partial(tile, False))


def _in_projection(x_lat, x_ctx, mod3, lat_row, ctx_row, w_in, merge_weights, *, tm):
    n_lat, n_ctx = x_lat.shape[0] // tm, x_ctx.shape[0] // tm
    is_lat = lambda i: i < n_lat
    lat_tile = lambda i: jnp.minimum(i, n_lat - 1)
    ctx_tile = lambda i: jnp.maximum(i - n_lat, 0)
    kv_shape = jax.ShapeDtypeStruct((n_ctx, D_ATTN, tm), F32)
    kv_spec = pl.BlockSpec((None, D_ATTN, tm), lambda i: (ctx_tile(i), 0, 0))
    n_cast = D_MODEL // CAST_ROWS
    assert n_cast <= n_lat + n_ctx and all(w.shape == (D_MODEL, D_MODEL) for w in merge_weights)
    cast_spec = pl.BlockSpec((CAST_ROWS, D_MODEL), lambda i: (jnp.minimum(i, n_cast - 1), 0))
    stage = (W_STAGES, D_MODEL, D_MODEL)
    blocks = (_nbytes((D_MODEL, D_IN), BF16) + _nbytes(stage, F32) + 4 * _nbytes((tm, D_MODEL), F32)
              + 2 * _nbytes((tm, D_IN), BF16) + 4 * _nbytes((tm, D_ATTN), F32)
              + 3 * 2 * (_nbytes((CAST_ROWS, D_MODEL), F32) + _nbytes((CAST_ROWS, D_MODEL), BF16)))
    return pl.pallas_call(
        functools.partial(_inproj_kernel, n_lat_tiles=n_lat),
        out_shape=[jax.ShapeDtypeStruct(((n_lat + n_ctx) * tm, D_IN), BF16), kv_shape, kv_shape]
                  + [jax.ShapeDtypeStruct((D_MODEL, D_MODEL), BF16)] * 3,
        grid=(n_lat + n_ctx,),
        in_specs=[pl.BlockSpec((tm, D_MODEL), lambda i: (lat_tile(i), 0)),
                  pl.BlockSpec((tm, D_MODEL), lambda i: (ctx_tile(i), 0)),
                  pl.BlockSpec((None, 1, 3 * D_MODEL),
                               lambda i: (jnp.where(is_lat(i), lat_row(lat_tile(i)), ctx_row), 0, 0)),
                  pl.BlockSpec(memory_space=pl.ANY)] + [cast_spec] * 3,
        out_specs=[pl.BlockSpec((tm, D_IN), lambda i: (i, 0)), kv_spec, kv_spec] + [cast_spec] * 3,
        scratch_shapes=[pltpu.VMEM((D_MODEL, D_IN), BF16), pltpu.VMEM(stage, F32),
                        pltpu.SemaphoreType.DMA((W_STAGES,))],
        compiler_params=pltpu.CompilerParams(
            dimension_semantics=("arbitrary",),
            vmem_limit_bytes=_vmem_limit(blocks, 4 * _nbytes((tm, D_MODEL), F32))),
        name="inproj",
    )(x_lat, x_ctx, mod3, w_in, *merge_weights)


def _hyena_kernel(vh_ref, x1_ref, x0_ref, gh_ref, cwv_ref, cw1_ref, cw0_ref, cbv_ref, cb1_ref, cb0_ref,
                  f_ref, g_ref, fp_ref, fqa_ref, fpb_ref, o_ref):
    L = fp_ref.shape[0]
    slab = jax.lax.broadcasted_iota(jnp.int32, (SUBLANES, 1), 0)

    def short_conv(x_ref, rows, w_ref, b_ref):
        x = x_ref[rows, :].astype(F32)
        prev = pltpu.roll(x, 1, 0)
        nxt = pltpu.roll(x, L - 1, 0)
        prev = jnp.concatenate([jnp.where(slab == 0, 0.0, prev[:SUBLANES]), prev[SUBLANES:]], axis=0)
        nxt = jnp.concatenate([nxt[:-SUBLANES], jnp.where(slab == SUBLANES - 1, 0.0, nxt[-SUBLANES:])], axis=0)
        return prev * w_ref[0:1, :] + x * w_ref[1:2, :] + nxt * w_ref[2:3, :] + b_ref[...]

    for s in range(vh_ref.shape[0] // L):
        rows = slice(s * L, (s + 1) * L)
        u = short_conv(vh_ref, rows, cwv_ref, cbv_ref) * short_conv(x1_ref, rows, cw1_ref, cb1_ref)
        t = _dot(f_ref[...], u.astype(BF16))
        p = t[:L]
        q = t[L:]
        yp = p * fp_ref[...] - q * fqa_ref[...]
        yq = p * fqa_ref[...] + q * fpb_ref[...]
        y = _dot(g_ref[...], jnp.concatenate([yp.astype(BF16), yq.astype(BF16)], axis=0))
        y_h = y * short_conv(x0_ref, rows, cw0_ref, cb0_ref)
        o_ref[rows, :] = (y_h * gh_ref[rows, :].astype(F32)).astype(BF16)


def _hyena_branch(z, z_row0, B, L, conv_w, conv_b, dft, spectra, *, tc=256, seqs_per_step):
    nblk = D_HYENA // tc
    tm = seqs_per_step * L
    assert z_row0 % tm == 0
    zcol = lambda piece: pl.BlockSpec((tm, tc), lambda c, b: (z_row0 // tm + b, piece * nblk + c))
    wcol = lambda rows, piece: pl.BlockSpec((rows, tc), lambda c, b: (0, piece * nblk + c))
    taps = lambda piece: pl.BlockSpec((None, 3, tc), lambda c, b: (0, 0, piece * nblk + c))
    chan = lambda rows: pl.BlockSpec((rows, tc), lambda c, b: (0, c))
    const = lambda shape: pl.BlockSpec(shape, lambda c, b: (0, 0), pipeline_mode=pl.Buffered(1))
    blocks = (2 * _nbytes((2 * L, L), BF16) + 2 * 5 * _nbytes((tm, tc), BF16) + 2 * 3 * _nbytes((L, tc), F32))
    return pl.pallas_call(
        _hyena_kernel,
        out_shape=jax.ShapeDtypeStruct((B * L, D_HYENA), BF16),
        grid=(nblk, B // seqs_per_step),
        in_specs=[zcol(COL_VH), zcol(COL_X1), zcol(COL_X0), zcol(COL_GH),
                  taps(0), taps(1), taps(2), wcol(1, 0), wcol(1, 1), wcol(1, 2),
                  const((2 * L, L)), const((L, 2 * L)), chan(L), chan(L), chan(L)],
        out_specs=pl.BlockSpec((tm, tc), lambda c, b: (b, c)),
        compiler_params=pltpu.CompilerParams(
            dimension_semantics=("arbitrary", "arbitrary"),
            vmem_limit_bytes=_vmem_limit(blocks, 16 * _nbytes((tm, tc), F32))),
        name=f"hyena_{L}",
    )(z, z, z, z, conv_w, conv_w, conv_w, conv_b.reshape(1, -1), conv_b.reshape(1, -1), conv_b.reshape(1, -1),
      *dft, *spectra)


def _head_masks(width):
    lane = jax.lax.broadcasted_iota(jnp.int32, (1, width), 1)
    return [(lane >= h * HEAD_DIM) & (lane < (h + 1) * HEAD_DIM) for h in range(width // HEAD_DIM)]


def _qk(q, k):
    return jax.lax.dot_general(q, k, (((1,), (1,)), ((), ())), preferred_element_type=F32)


def _stack_heads(q, masks):
    return jnp.concatenate([jnp.where(msk, q, jnp.zeros_like(q)) for msk in masks], axis=0)


def _unstack_heads(o, masks):
    n = o.shape[0] // len(masks)
    out = o[:n]
    for h in range(1, len(masks)):
        out = jnp.where(masks[h], o[h * n:(h + 1) * n], out)
    return out


def _ctx_attn_kernel(q_ref, k_ref, v_ref, g_ref, o_ref):
    masks = _head_masks(CTX_HEAD_TILE)
    for t in range(D_ATTN // CTX_HEAD_TILE):
        cols = slice(t * CTX_HEAD_TILE, (t + 1) * CTX_HEAD_TILE)
        s = _qk(_stack_heads(q_ref[:, cols], masks), k_ref[:, cols])
        m = jnp.max(s, axis=-1, keepdims=True)
        p = jnp.exp2(s - m)
        l = jnp.sum(p, axis=-1, keepdims=True)
        out = _unstack_heads(_dot(p.astype(BF16), v_ref[:, cols]) / l, masks)
        o_ref[:, cols] = (out * g_ref[:, cols].astype(F32)).astype(BF16)


def _context_attention(z, z_row0, B, L):
    assert z_row0 % L == 0
    zcol = lambda piece: pl.BlockSpec((L, D_ATTN), lambda b: (z_row0 // L + b, piece))
    return pl.pallas_call(
        _ctx_attn_kernel,
        out_shape=jax.ShapeDtypeStruct((B * L, D_ATTN), BF16),
        grid=(B,),
        in_specs=[zcol(COL_Q), zcol(COL_K), zcol(COL_V), zcol(COL_GA)],
        out_specs=pl.BlockSpec((L, D_ATTN), lambda b: (b, 0)),
        compiler_params=pltpu.CompilerParams(dimension_semantics=("arbitrary",)),
        name="ctx_attn",
    )(z, z, z, z)


def _key_window_start(qb):
    rows = 1024 // GRID_W
    r_first = qb * Q_BLOCK_ROWS
    r_last = r_first + Q_BLOCK_ROWS - 1
    lo = min(max(r_first - WIN_ROWS // 2, 0), rows - WIN_ROWS)
    hi = min(max(r_last - WIN_ROWS // 2, 0), rows - WIN_ROWS) + WIN_ROWS
    start = min(lo, rows - KEY_WIN_ROWS)
    start -= start % Q_BLOCK_ROWS
    assert start <= lo and hi <= start + KEY_WIN_ROWS
    return start


def _nbr_attn_kernel(q_ref, k0_ref, k1_ref, k2_ref, v0_ref, v1_ref, v2_ref, kc_ref, vc_ref, g_ref, tiles_ref,
                     o_ref, bias_ref, *, n_rows):
    masks = _head_masks(NBR_HEAD_TILE)
    n_heads = q_ref.shape[-1] // HEAD_DIM
    chain_tiles = [slice(c * NBR_HEAD_TILE, (c + 1) * NBR_HEAD_TILE) for c in range(q_ref.shape[-1] // NBR_HEAD_TILE)]
    k_refs = (k0_ref, k1_ref, k2_ref)
    v_refs = (v0_ref, v1_ref, v2_ref)
    key_blocks = KEY_WIN // Q_BLOCK

    def window_row_start(r):
        return min(max(r - WIN_ROWS // 2, 0), n_rows - WIN_ROWS)

    def build_bias(qb):
        k_start = _key_window_start(qb)
        used = set()
        for i in range(Q_BLOCK_ROWS):
            r = qb * Q_BLOCK_ROWS + i
            r0 = window_row_start(r)
            for j in range(KEY_WIN_ROWS):
                kr = k_start + j
                cols = slice(j * GRID_W, (j + 1) * GRID_W)
                lanes = slice((j % RPB_COPIES) * GRID_W, (j % RPB_COPIES + 1) * GRID_W)
                inside = r0 <= kr < r0 + WIN_ROWS
                if inside:
                    used.add(j // Q_BLOCK_ROWS)
                for h in range(n_heads):
                    rows = slice(h * Q_BLOCK + i * GRID_W, h * Q_BLOCK + (i + 1) * GRID_W)
                    if inside:
                        bias_ref[rows, cols] = tiles_ref[h, kr - r + WIN_ROWS - 1, :, lanes]
                    else:
                        bias_ref[rows, cols] = jnp.full((GRID_W, GRID_W), NEG_INF, F32)
        return sorted(used)

    def attend(blocks):
        for c, lanes in enumerate(chain_tiles):
            stacked = slice(c * len(masks) * Q_BLOCK, (c + 1) * len(masks) * Q_BLOCK)
            for b in range(q_ref.shape[0]):
                qs = _stack_heads(q_ref[b, :, lanes], masks)
                s = [_qk(qs, k_refs[j][b, :, lanes]) + bias_ref[stacked, j * Q_BLOCK:(j + 1) * Q_BLOCK] for j in blocks]
                s.append(_dot(qs, kc_ref[b, lanes, :].astype(BF16)))
                m = jnp.max(functools.reduce(jnp.maximum, s), axis=-1, keepdims=True)
                p = [jnp.exp2(x - m) for x in s]
                l = jnp.sum(functools.reduce(jnp.add, p), axis=-1, keepdims=True)
                p = [x.astype(BF16) for x in p]
                v_win = jnp.concatenate([v_refs[j][b, :, lanes] for j in blocks], axis=0)
                o = _dot(jnp.concatenate(p[:-1], axis=1), v_win) + _qk(p[-1], vc_ref[b, lanes, :].astype(BF16))
                o_ref[b, :, lanes] = (_unstack_heads(o / l, masks) * g_ref[b, :, lanes].astype(F32)).astype(BF16)

    for qb in range(n_rows // Q_BLOCK_ROWS):
        @pl.when(pl.program_id(0) == qb)
        def _(qb=qb):
            blocks = build_bias(qb)
            assert blocks and all(0 <= j < key_blocks for j in blocks)
            attend(blocks)


def _neighbourhood_attention(z, z_row0, B, L, cache_k, cache_v, tiles):
    assert z_row0 % (B * L) == 0
    group = z_row0 // (B * L)
    n_rows = L // GRID_W
    n_qb = L // Q_BLOCK
    assert KEY_WIN == 3 * Q_BLOCK and all(_key_window_start(qb) % Q_BLOCK_ROWS == 0 for qb in range(n_qb))
    z3 = z.reshape(z.shape[0] // L, L, D_IN)
    n_ctx = cache_k.shape[1]
    kc = jnp.transpose(cache_k, (0, 2, 3, 1)).reshape(B, D_ATTN, n_ctx)
    vc = jnp.transpose(cache_v, (0, 2, 3, 1)).reshape(B, D_ATTN, n_ctx)
    width = NBR_BLOCK_LANES
    tiles_per_col = D_ATTN // width

    def win_block(qb):
        return jnp.where(qb < n_qb // 2, 0, (n_rows - KEY_WIN_ROWS) // Q_BLOCK_ROWS)

    assert [_key_window_start(qb) // Q_BLOCK_ROWS for qb in range(n_qb)] == \
        [0 if qb < n_qb // 2 else (n_rows - KEY_WIN_ROWS) // Q_BLOCK_ROWS for qb in range(n_qb)]
    qspec = lambda piece: pl.BlockSpec((B, Q_BLOCK, width), lambda qb, t: (group, qb, piece * tiles_per_col + t))
    kspec = lambda piece, j: pl.BlockSpec((B, Q_BLOCK, width),
                                          lambda qb, t: (group, win_block(qb) + j, piece * tiles_per_col + t))
    cspec = pl.BlockSpec((B, width, n_ctx), lambda qb, t: (0, t, 0))
    out = pl.pallas_call(
        functools.partial(_nbr_attn_kernel, n_rows=n_rows),
        out_shape=jax.ShapeDtypeStruct((B, L, D_ATTN), BF16),
        grid=(n_qb, tiles_per_col),
        in_specs=[qspec(COL_Q), kspec(COL_K, 0), kspec(COL_K, 1), kspec(COL_K, 2),
                  kspec(COL_V, 0), kspec(COL_V, 1), kspec(COL_V, 2), cspec, cspec, qspec(COL_GA),
                  pl.BlockSpec((width // HEAD_DIM, 2 * WIN_ROWS - 1, GRID_W, LANES), lambda qb, t: (t, 0, 0, 0))],
        out_specs=pl.BlockSpec((B, Q_BLOCK, width), lambda qb, t: (0, qb, t)),
        scratch_shapes=[pltpu.VMEM((width // HEAD_DIM * Q_BLOCK, KEY_WIN), F32)],
        compiler_params=pltpu.CompilerParams(dimension_semantics=("arbitrary", "arbitrary")),
        name="nbr_attn",
    )(z3, z3, z3, z3, z3, z3, z3, kc, vc, z3, tiles)
    return out.reshape(B * L, D_ATTN)


def _merge_kernel(yh_ref, ya_ref, mh_ref, ma_ref, x_ref, mod_ref, wbh_ref, wba_ref, wout_ref, lng_ref, lnb_ref,
                  o_ref, *, alpha):
    p_h = _dot(yh_ref[...], wbh_ref[...])
    p_a = _dot(ya_ref[...], wba_ref[...])
    m_h = mh_ref[...].astype(F32)
    m_a = ma_ref[...].astype(F32)
    out = _dot((m_h * p_h + m_a * p_a).astype(BF16), wout_ref[...])
    gate = mod_ref[:, 2 * D_MODEL:3 * D_MODEL]
    v = alpha * x_ref[...] + gate * out
    mu = jnp.mean(v, axis=-1, keepdims=True)
    d = v - mu
    var = jnp.mean(d * d, axis=-1, keepdims=True)
    o_ref[...] = d * jax.lax.rsqrt(var + LN_EPS) * lng_ref[...] + lnb_ref[...]


def _merge(yh, ya, z, z_row0, x2d, mod3, mod_index, w_bh, w_ba, w_out, ln_g, ln_b, *, alpha, tm):
    M = x2d.shape[0]
    assert z_row0 % tm == 0
    tok = lambda width, col: pl.BlockSpec((tm, width), lambda i: (i, col))
    ztok = lambda col: pl.BlockSpec((tm, D_MODEL), lambda i: (z_row0 // tm + i, col))
    const = lambda shape: pl.BlockSpec(shape, lambda i: (0, 0))
    return pl.pallas_call(
        functools.partial(_merge_kernel, alpha=alpha),
        out_shape=jax.ShapeDtypeStruct((M, D_MODEL), F32),
        grid=(M // tm,),
        in_specs=[tok(D_HYENA, 0), tok(D_ATTN, 0), ztok(COL_MH), ztok(COL_MA), tok(D_MODEL, 0),
                  pl.BlockSpec((None, 1, 3 * D_MODEL), lambda i: (mod_index(i), 0, 0)),
                  const((D_HYENA, D_MODEL)), const((D_ATTN, D_MODEL)), const((D_MODEL, D_MODEL)),
                  const((1, D_MODEL)), const((1, D_MODEL))],
        out_specs=pl.BlockSpec((tm, D_MODEL), lambda i: (i, 0)),
        compiler_params=pltpu.CompilerParams(
            dimension_semantics=("arbitrary",),
            vmem_limit_bytes=_vmem_limit(
                2 * (4 * _nbytes((tm, D_MODEL), BF16) + 2 * _nbytes((tm, D_MODEL), F32)
                     + 3 * _nbytes((D_MODEL, D_MODEL), BF16)),
                8 * _nbytes((tm, D_MODEL), F32))),
        name="merge",
    )(yh, ya, z, z, x2d, mod3, w_bh, w_ba, w_out, ln_g.reshape(1, -1), ln_b.reshape(1, -1))


def _mixers(x, z, z_row0, mod3, cond_row, ctx_kv, p, filt, tiles, *, alpha):
    B, L, _ = x.shape
    dft = _dft_matrices(L)
    spectra = _filter_spectra(L, dft[0], p["hyena_d"], *filt)
    yh = _hyena_branch(z, z_row0, B, L, p["conv_w"], p["conv_b"], dft, spectra,
                       seqs_per_step=max(1, HYENA_ROWS_PER_STEP // L))
    if ctx_kv is None:
        ya = _context_attention(z, z_row0, B, L)
    else:
        ya = _neighbourhood_attention(z, z_row0, B, L, ctx_kv[0], ctx_kv[1], tiles)
    y = _merge(yh, ya, z, z_row0, x.reshape(B * L, D_MODEL), mod3, lambda i: cond_row(i * MERGE_TM // L),
               p["w_bh"], p["w_ba"], p["w_out"], p["ln_g"], p["ln_b"], alpha=alpha, tm=MERGE_TM)
    return y.reshape(B, L, D_MODEL)


def kernel(x_prompt, x_sample, c, cache_k, cache_v, c_ctx, w_ada, b_ada, w_in, conv_w, conv_b, filt_w1, filt_b1,
           filt_w2, filt_b2, filt_w3, filt_freq, hyena_d, rpb, w_bh, w_ba, w_out, ln_g, ln_b):
    depth = w_in.shape[0]
    alpha = (2.0 * depth) ** 0.25
    n_lat, lat_len = x_sample.shape[0], x_sample.shape[1]
    n_ctx, ctx_len = x_prompt.shape[0], x_prompt.shape[1]
    assert ctx_len == IN_TM
    ctx_row = n_lat

    xp, xs = x_prompt, x_sample
    new_k, new_v = [], []
    for l in range(depth):
        mod3 = _modulation(c, c_ctx.reshape(1, -1), w_ada[l], b_ada[l])
        filt = (filt_w1[l], filt_b1[l], filt_w2[l], filt_b2[l], filt_w3[l], filt_freq[l])
        tiles = _rpb_tiles(rpb[l])
        z, k_ctx, v_ctx, w_bh16, w_ba16, w_out16 = _in_projection(
            xs.reshape(-1, D_MODEL), xp.reshape(-1, D_MODEL), mod3, lambda tile: tile * IN_TM // lat_len, ctx_row,
            w_in[l], (w_bh[l], w_ba[l], w_out[l]), tm=IN_TM)
        p = {"conv_w": conv_w[l:l + 1], "conv_b": conv_b[l], "hyena_d": hyena_d[l],
             "w_bh": w_bh16, "w_ba": w_ba16, "w_out": w_out16, "ln_g": ln_g[l], "ln_b": ln_b[l]}
        new_k.append(jnp.transpose(k_ctx.reshape(n_ctx, N_HEADS, HEAD_DIM, ctx_len), (0, 3, 1, 2)))
        new_v.append(jnp.transpose(v_ctx.reshape(n_ctx, N_HEADS, HEAD_DIM, ctx_len), (0, 3, 1, 2)))
        xp = _mixers(xp, z, n_lat * lat_len, mod3, lambda b: ctx_row, None, p, filt, None, alpha=alpha)
        xs = _mixers(xs, z, 0, mod3, lambda b: b, (cache_k[:, l], cache_v[:, l]), p, filt, tiles, alpha=alpha)
    return xp, xs, jnp.stack(new_k, axis=1), jnp.stack(new_v, axis=1)
```

```python
import functools
import math

import jax
import jax.numpy as jnp
import numpy as np
from jax.experimental import pallas as pl
from jax.experimental.pallas import tpu as pltpu

F32 = jnp.float32
BF16 = jnp.bfloat16

D_MODEL = 1024
D_HYENA = 1024
N_HEADS = 16
HEAD_DIM = 64
D_ATTN = N_HEADS * HEAD_DIM
D_IN = 4 * D_HYENA + 4 * D_ATTN + 2 * D_MODEL
GRID_W = 64
WIN_ROWS = 8
WIN_COLS = 16
FILTER_EMB = 33
FILTER_BANDS = (FILTER_EMB - 1) // 2
FILTER_HIDDEN = 64
DECAY_TARGET = 1e-2
MIN_DECAY = math.log(DECAY_TARGET) / 1.5
MAX_DECAY = math.log(DECAY_TARGET) / 0.3
DECAY_SHIFT = 0.05
LN_EPS = 1e-5
NEG_INF = -1e30

COL_VH, COL_X1, COL_X0, COL_GH, COL_Q, COL_K, COL_V, COL_GA, COL_MH, COL_MA = range(10)
PROJ_ORDER = (COL_GH, COL_GA, COL_MH, COL_MA, COL_K, COL_V, COL_Q, COL_VH, COL_X1, COL_X0)

LANES = 128
SUBLANES = 8
CTX_HEAD_TILE = 128
NBR_HEAD_TILE = 128
NBR_BLOCK_LANES = 128
RPB_COPIES = LANES // GRID_W
V7X_VMEM_BYTES = 64 * 1024 * 1024
ATTN_SCALE = HEAD_DIM ** -0.5
LOG2E = math.log2(math.e)

Q_BLOCK = 256
Q_BLOCK_ROWS = Q_BLOCK // GRID_W
KEY_WIN_ROWS = 12
KEY_WIN = KEY_WIN_ROWS * GRID_W

IN_TM = 256
W_STAGES = 2
CAST_ROWS = 64
MERGE_TM = 512
HYENA_ROWS_PER_STEP = 4096


def _vmem_limit(block_bytes, temp_bytes):
    need = int(block_bytes + temp_bytes)
    return min(max(need, 16 * 1024 * 1024), V7X_VMEM_BYTES - 8 * 1024 * 1024)


def _nbytes(shape, dtype):
    return int(np.prod(shape)) * jnp.dtype(dtype).itemsize


def _silu(x):
    return x * jax.nn.sigmoid(x)


def _split_bf16(a):
    hi = a.astype(BF16)
    lo = (a - hi.astype(F32)).astype(BF16)
    return hi, lo


def _dot(a, b):
    return jnp.dot(a, b, preferred_element_type=F32)


def _dot_split(a_hi, a_lo, b):
    b_hi, b_lo = _split_bf16(b)
    return _dot(a_hi, b_hi) + (_dot(a_hi, b_lo) + _dot(a_lo, b_hi))


@functools.lru_cache(maxsize=None)
def _dft_constants(L):
    n = 2 * L
    k = np.arange(L, dtype=np.float64)[:, None]
    s = np.arange(L, dtype=np.float64)[None, :]
    ang = 2.0 * np.pi * k * s / n
    nyq = np.cos(np.pi * np.arange(L, dtype=np.float64))
    C = np.cos(ang)
    S = np.sin(ang)
    S[0, :] = nyq
    F = np.concatenate([C, S], axis=0)
    w = np.full((L,), 2.0)
    w[0] = 1.0
    Gc = (C * w[:, None]).T / n
    Ss = 2.0 * np.sin(ang)
    Ss[0, :] = nyq
    Gs = Ss.T / n
    G = np.concatenate([Gc, Gs], axis=1)
    return F.astype(np.float32), G.astype(np.float32)


def _dft_matrices(L):
    F, G = _dft_constants(L)
    return jnp.asarray(F).astype(BF16), jnp.asarray(G).astype(BF16)


@functools.lru_cache(maxsize=None)
def _filter_constants(L):
    t = np.linspace(0.0, 1.0, L, dtype=np.float32)[:, None]
    bands = np.linspace(1e-4, FILTER_BANDS - 1, FILTER_BANDS, dtype=np.float32)[None]
    w = (2.0 * math.pi * np.arange(L, dtype=np.float32)[:, None] / L).astype(np.float32)
    z = np.concatenate([t, np.cos(bands * w), -np.sin(bands * w)], axis=-1).astype(np.float32)
    z_pad = np.zeros((L, LANES), np.float32)
    z_pad[:, :FILTER_EMB] = z
    deltas = np.linspace(MIN_DECAY, MAX_DECAY, D_HYENA, dtype=np.float32)
    decay = (np.exp(-t * np.abs(deltas)) + np.float32(DECAY_SHIFT)).astype(np.float32)
    return z_pad, decay


@functools.lru_cache(maxsize=None)
def _rpb_constants():
    c = np.arange(GRID_W)[:, None]
    kc = np.arange(GRID_W)[None, :]
    c0 = np.clip(c - WIN_COLS // 2, 0, GRID_W - WIN_COLS)
    in_win = (kc >= c0) & (kc < c0 + WIN_COLS)
    idx = np.clip(kc - c + WIN_COLS - 1, 0, 2 * WIN_COLS - 2)
    onehot = np.zeros((LANES, GRID_W, LANES), np.float32)
    for half in range(RPB_COPIES):
        lanes = half * GRID_W + np.arange(GRID_W)
        onehot[idx, c, lanes[None, :]] = 1.0
    mask = np.concatenate([in_win, in_win], axis=1).astype(np.float32)
    return onehot.reshape(LANES, GRID_W * LANES).astype(BF16), mask.reshape(1, GRID_W * LANES)


def _mod_kernel(c_ref, cctx_ref, w_ref, b_ref, o_ref, cond_ref):
    n_lat = c_ref.shape[0]
    cond_ref[...] = jnp.zeros(cond_ref.shape, F32)
    cond_ref[0:n_lat, :] = c_ref[...]
    cond_ref[n_lat:n_lat + 1, :] = cctx_ref[...]
    s = _silu(cond_ref[...]).astype(BF16)
    o_ref[:, 0, :] = _dot(s, w_ref[...].astype(BF16)) + b_ref[...]


def _modulation(c, c_ctx, w_ada, b_ada):
    n_lat = c.shape[0]
    assert n_lat + 1 <= SUBLANES
    return pl.pallas_call(
        _mod_kernel,
        out_shape=jax.ShapeDtypeStruct((SUBLANES, 1, 3 * D_MODEL), F32),
        grid=(3,),
        in_specs=[pl.BlockSpec((n_lat, D_MODEL), lambda j: (0, 0)),
                  pl.BlockSpec((1, D_MODEL), lambda j: (0, 0)),
                  pl.BlockSpec((D_MODEL, D_MODEL), lambda j: (0, j)),
                  pl.BlockSpec((1, D_MODEL), lambda j: (0, j))],
        out_specs=pl.BlockSpec((SUBLANES, 1, D_MODEL), lambda j: (0, 0, j)),
        scratch_shapes=[pltpu.VMEM((SUBLANES, D_MODEL), F32)],
        name="mod",
    )(c, c_ctx, w_ada, b_ada.reshape(1, -1))


def _filter_spectra(z_ref, w1_ref, b1_ref, w2_ref, b2_ref, w3f_ref, w3b_ref, freq_ref, decay_ref, d_ref, f_ref,
                    hf_ref, fp_ref, fqa_ref, fpb_ref):
    L = decay_ref.shape[0]

    @pl.when(pl.program_id(0) == 0)
    def _():
        z_hi, z_lo = _split_bf16(z_ref[...])
        hf = jnp.sin(freq_ref[0:1, :] * (_dot_split(z_hi, z_lo, w1_ref[...]) + b1_ref[...]))
        h_hi, h_lo = _split_bf16(hf)
        hf_ref[...] = jnp.sin(freq_ref[1:2, :] * (_dot_split(h_hi, h_lo, w2_ref[...]) + b2_ref[...]))

    h_hi, h_lo = _split_bf16(hf_ref[...])
    decay = decay_ref[...]
    h_fwd = _dot_split(h_hi, h_lo, w3f_ref[...]) * decay
    h_bwd = _dot_split(h_hi, h_lo, w3b_ref[...]) * decay
    row0 = jax.lax.broadcasted_iota(jnp.int32, (L, 1), 0) == 0
    h_bwd = jnp.where(row0, 0.0, h_bwd)
    fsum = (h_fwd + h_bwd).astype(BF16)
    fdif = (h_fwd - h_bwd).astype(BF16)
    skip = d_ref[...]
    fp = _dot(f_ref[0:L, :], fsum) + skip
    fq = _dot(f_ref[L:2 * L, :], fdif)
    nyq = _dot(f_ref[L:L + 2 * SUBLANES, :], fsum)[0:1, :] + skip
    fp_ref[...] = fp
    fqa_ref[...] = jnp.where(row0, 0.0, fq)
    fpb_ref[...] = jnp.where(row0, nyq, fp)


def _rpb_kernel(r_ref, onehot_ref, mask_ref, o_ref):
    r = r_ref[...]
    r_hi = r.astype(BF16)
    rem = r - r_hi.astype(F32)
    r_mid = rem.astype(BF16)
    r_lo = (rem - r_mid.astype(F32)).astype(BF16)
    oh = onehot_ref[...]
    t = _dot(r_hi, oh) + (_dot(r_mid, oh) + _dot(r_lo, oh))
    t = jnp.where(mask_ref[...] > 0.0, t * LOG2E, NEG_INF)
    for c in range(o_ref.shape[1]):
        o_ref[:, c, :] = t[:, c * LANES:(c + 1) * LANES]


def _rpb_tiles(rpb):
    n_dr = 2 * WIN_ROWS - 1
    rows = N_HEADS * n_dr
    onehot, mask = _rpb_constants()
    r = jnp.pad(rpb.reshape(rows, 2 * WIN_COLS - 1), ((0, 0), (0, LANES - (2 * WIN_COLS - 1))))
    cols_per_step = 16
    tn = cols_per_step * LANES
    out = pl.pallas_call(
        _rpb_kernel,
        out_shape=jax.ShapeDtypeStruct((rows, GRID_W, LANES), F32),
        grid=(GRID_W // cols_per_step,),
        in_specs=[pl.BlockSpec((rows, LANES), lambda j: (0, 0)),
                  pl.BlockSpec((LANES, tn), lambda j: (0, j)),
                  pl.BlockSpec((1, tn), lambda j: (0, j))],
        out_specs=pl.BlockSpec((rows, cols_per_step, LANES), lambda j: (0, j, 0)),
        name="rpb_tiles",
    )(r, onehot, mask)
    return out.reshape(N_HEADS, n_dr, GRID_W, LANES)


def _inproj_kernel(xl_ref, xc_ref, mod_ref, w_hbm, wbh_ref, wba_ref, wout_ref,
                   z_ref, kt_ref, vt_ref, wbh16_ref, wba16_ref, wout16_ref, w_ref, stage_ref, sem, *, n_lat_tiles):
    n_col = D_IN // D_MODEL
    n_stage = stage_ref.shape[0]
    epilogue = {COL_GH: _silu, COL_GA: _silu, COL_MH: jax.nn.sigmoid, COL_MA: jax.nn.sigmoid,
                COL_Q: lambda acc: acc * (ATTN_SCALE * LOG2E)}

    def columns(col):
        return slice(col * D_MODEL, (col + 1) * D_MODEL)

    def fetch(n):
        return pltpu.make_async_copy(w_hbm.at[:, columns(PROJ_ORDER[n])], stage_ref.at[n % n_stage], sem.at[n % n_stage])

    def tile(stream_weights):
        if stream_weights:
            for n in range(min(n_stage, n_col)):
                fetch(n).start()
        shift = mod_ref[:, 0:D_MODEL]
        scale = mod_ref[:, D_MODEL:2 * D_MODEL]
        x = jnp.where(pl.program_id(0) < n_lat_tiles, xl_ref[...], xc_ref[...])
        h = (x * (1.0 + scale) + shift).astype(BF16)
        for src_ref, dst_ref in ((wbh_ref, wbh16_ref), (wba_ref, wba16_ref), (wout_ref, wout16_ref)):
            dst_ref[...] = src_ref[...].astype(BF16)
        for n, col in enumerate(PROJ_ORDER):
            if stream_weights:
                fetch(n).wait()
                w_ref[:, columns(col)] = stage_ref[n % n_stage].astype(BF16)
                if n + n_stage < n_col:
                    fetch(n + n_stage).start()
            acc = _dot(h, w_ref[:, columns(col)])
            z_ref[:, columns(col)] = epilogue.get(col, lambda acc: acc)(acc).astype(BF16)
            if col == COL_K:
                kt_ref[...] = acc.T
            if col == COL_V:
                vt_ref[...] = acc.T

    pl.when(pl.program_id(0) == 0)(functools.partial(tile, True))
    pl.when(pl.program_id(0) != 0)(functools.partial(tile, False))


def _in_projection(x_lat, x_ctx, mod3, lat_row, ctx_row, w_in, merge_weights, *, tm):
    n_lat, n_ctx = x_lat.shape[0] // tm, x_ctx.shape[0] // tm
    is_lat = lambda i: i < n_lat
    lat_tile = lambda i: jnp.minimum(i, n_lat - 1)
    ctx_tile = lambda i: jnp.maximum(i - n_lat, 0)
    kv_shape = jax.ShapeDtypeStruct((n_ctx, D_ATTN, tm), F32)
    kv_spec = pl.BlockSpec((None, D_ATTN, tm), lambda i: (ctx_tile(i), 0, 0))
    n_cast = D_MODEL // CAST_ROWS
    assert n_cast <= n_lat + n_ctx and all(w.shape == (D_MODEL, D_MODEL) for w in merge_weights)
    cast_spec = pl.BlockSpec((CAST_ROWS, D_MODEL), lambda i: (jnp.minimum(i, n_cast - 1), 0))
    stage = (W_STAGES, D_MODEL, D_MODEL)
    blocks = (_nbytes((D_MODEL, D_IN), BF16) + _nbytes(stage, F32) + 4 * _nbytes((tm, D_MODEL), F32)
              + 2 * _nbytes((tm, D_IN), BF16) + 4 * _nbytes((tm, D_ATTN), F32)
              + 3 * 2 * (_nbytes((CAST_ROWS, D_MODEL), F32) + _nbytes((CAST_ROWS, D_MODEL), BF16)))
    return pl.pallas_call(
        functools.partial(_inproj_kernel, n_lat_tiles=n_lat),
        out_shape=[jax.ShapeDtypeStruct(((n_lat + n_ctx) * tm, D_IN), BF16), kv_shape, kv_shape]
                  + [jax.ShapeDtypeStruct((D_MODEL, D_MODEL), BF16)] * 3,
        grid=(n_lat + n_ctx,),
        in_specs=[pl.BlockSpec((tm, D_MODEL), lambda i: (lat_tile(i), 0)),
                  pl.BlockSpec((tm, D_MODEL), lambda i: (ctx_tile(i), 0)),
                  pl.BlockSpec((None, 1, 3 * D_MODEL),
                               lambda i: (jnp.where(is_lat(i), lat_row(lat_tile(i)), ctx_row), 0, 0)),
                  pl.BlockSpec(memory_space=pl.ANY)] + [cast_spec] * 3,
        out_specs=[pl.BlockSpec((tm, D_IN), lambda i: (i, 0)), kv_spec, kv_spec] + [cast_spec] * 3,
        scratch_shapes=[pltpu.VMEM((D_MODEL, D_IN), BF16), pltpu.VMEM(stage, F32),
                        pltpu.SemaphoreType.DMA((W_STAGES,))],
        compiler_params=pltpu.CompilerParams(
            dimension_semantics=("arbitrary",),
            vmem_limit_bytes=_vmem_limit(blocks, 4 * _nbytes((tm, D_MODEL), F32))),
        name="inproj",
    )(x_lat, x_ctx, mod3, w_in, *merge_weights)


def _hyena_kernel(vh_ref, x1_ref, x0_ref, gh_ref, cwv_ref, cw1_ref, cw0_ref, cbv_ref, cb1_ref, cb0_ref,
                  f_ref, g_ref, z_ref, w1_ref, b1_ref, w2_ref, b2_ref, w3f_ref, w3b_ref, freq_ref, decay_ref, d_ref,
                  o_ref, hf_ref, fp_ref, fqa_ref, fpb_ref):
    L = fp_ref.shape[0]
    slab = jax.lax.broadcasted_iota(jnp.int32, (SUBLANES, 1), 0)

    @pl.when(pl.program_id(1) == 0)
    def _():
        _filter_spectra(z_ref, w1_ref, b1_ref, w2_ref, b2_ref, w3f_ref, w3b_ref, freq_ref, decay_ref, d_ref, f_ref,
                        hf_ref, fp_ref, fqa_ref, fpb_ref)

    def short_conv(x_ref, rows, w_ref, b_ref):
        x = x_ref[rows, :].astype(F32)
        prev = pltpu.roll(x, 1, 0)
        nxt = pltpu.roll(x, L - 1, 0)
        prev = jnp.concatenate([jnp.where(slab == 0, 0.0, prev[:SUBLANES]), prev[SUBLANES:]], axis=0)
        nxt = jnp.concatenate([nxt[:-SUBLANES], jnp.where(slab == SUBLANES - 1, 0.0, nxt[-SUBLANES:])], axis=0)
        return prev * w_ref[0:1, :] + x * w_ref[1:2, :] + nxt * w_ref[2:3, :] + b_ref[...]

    for s in range(vh_ref.shape[0] // L):
        rows = slice(s * L, (s + 1) * L)
        u = short_conv(vh_ref, rows, cwv_ref, cbv_ref) * short_conv(x1_ref, rows, cw1_ref, cb1_ref)
        t = _dot(f_ref[...], u.astype(BF16))
        p = t[:L]
        q = t[L:]
        yp = p * fp_ref[...] - q * fqa_ref[...]
        yq = p * fqa_ref[...] + q * fpb_ref[...]
        y = _dot(g_ref[...], jnp.concatenate([yp.astype(BF16), yq.astype(BF16)], axis=0))
        y_h = y * short_conv(x0_ref, rows, cw0_ref, cb0_ref)
        o_ref[rows, :] = (y_h * gh_ref[rows, :].astype(F32)).astype(BF16)


def _hyena_branch(z, z_row0, B, L, conv_w, conv_b, hyena_d, w1, b1, w2, b2, w3, freq, *, tc=256, seqs_per_step):
    nblk = D_HYENA // tc
    tm = seqs_per_step * L
    assert z_row0 % tm == 0
    f_mat, g_mat = _dft_matrices(L)
    z_emb, decay = _filter_constants(L)
    w1p = jnp.pad(w1, ((0, LANES - FILTER_EMB), (0, 0)))
    zcol = lambda piece: pl.BlockSpec((tm, tc), lambda c, b: (z_row0 // tm + b, piece * nblk + c))
    wcol = lambda rows, piece: pl.BlockSpec((rows, tc), lambda c, b: (0, piece * nblk + c))
    taps = lambda piece: pl.BlockSpec((None, 3, tc), lambda c, b: (0, 0, piece * nblk + c))
    chan = lambda rows: pl.BlockSpec((rows, tc), lambda c, b: (0, c))
    const = lambda shape: pl.BlockSpec(shape, lambda c, b: (0, 0), pipeline_mode=pl.Buffered(1))
    blocks = (2 * _nbytes((2 * L, L), BF16) + 2 * 5 * _nbytes((tm, tc), BF16) + 5 * _nbytes((L, tc), F32)
              + _nbytes((L, LANES), F32) + _nbytes((L, FILTER_HIDDEN), F32))
    return pl.pallas_call(
        _hyena_kernel,
        out_shape=jax.ShapeDtypeStruct((B * L, D_HYENA), BF16),
        grid=(nblk, B // seqs_per_step),
        in_specs=[zcol(COL_VH), zcol(COL_X1), zcol(COL_X0), zcol(COL_GH),
                  taps(0), taps(1), taps(2), wcol(1, 0), wcol(1, 1), wcol(1, 2),
                  const((2 * L, L)), const((L, 2 * L)),
                  const((L, LANES)),
                  const((LANES, FILTER_HIDDEN)), const((1, FILTER_HIDDEN)),
                  const((FILTER_HIDDEN, FILTER_HIDDEN)), const((1, FILTER_HIDDEN)),
                  wcol(FILTER_HIDDEN, 0), wcol(FILTER_HIDDEN, 1),
                  const((2, FILTER_HIDDEN)), chan(L), chan(1)],
        out_specs=pl.BlockSpec((tm, tc), lambda c, b: (b, c)),
        scratch_shapes=[pltpu.VMEM((L, FILTER_HIDDEN), F32)] + [pltpu.VMEM((L, tc), F32)] * 3,
        compiler_params=pltpu.CompilerParams(
            dimension_semantics=("arbitrary", "arbitrary"),
            vmem_limit_bytes=_vmem_limit(blocks, 16 * _nbytes((tm, tc), F32))),
        name=f"hyena_{L}",
    )(z, z, z, z, conv_w, conv_w, conv_w, conv_b.reshape(1, -1), conv_b.reshape(1, -1), conv_b.reshape(1, -1),
      f_mat, g_mat, z_emb, w1p, b1.reshape(1, -1), w2, b2.reshape(1, -1), w3, w3, freq, decay, hyena_d.reshape(1, -1))


def _head_masks(width):
    lane = jax.lax.broadcasted_iota(jnp.int32, (1, width), 1)
    return [(lane >= h * HEAD_DIM) & (lane < (h + 1) * HEAD_DIM) for h in range(width // HEAD_DIM)]


def _qk(q, k):
    return jax.lax.dot_general(q, k, (((1,), (1,)), ((), ())), preferred_element_type=F32)


def _stack_heads(q, masks):
    return jnp.concatenate([jnp.where(msk, q, jnp.zeros_like(q)) for msk in masks], axis=0)


def _unstack_heads(o, masks):
    n = o.shape[0] // len(masks)
    out = o[:n]
    for h in range(1, len(masks)):
        out = jnp.where(masks[h], o[h * n:(h + 1) * n], out)
    return out


def _ctx_attn_kernel(q_ref, k_ref, v_ref, g_ref, o_ref):
    masks = _head_masks(CTX_HEAD_TILE)
    for t in range(D_ATTN // CTX_HEAD_TILE):
        cols = slice(t * CTX_HEAD_TILE, (t + 1) * CTX_HEAD_TILE)
        s = _qk(_stack_heads(q_ref[:, cols], masks), k_ref[:, cols])
        m = jnp.max(s, axis=-1, keepdims=True)
        p = jnp.exp2(s - m)
        l = jnp.sum(p, axis=-1, keepdims=True)
        out = _unstack_heads(_dot(p.astype(BF16), v_ref[:, cols]) / l, masks)
        o_ref[:, cols] = (out * g_ref[:, cols].astype(F32)).astype(BF16)


def _context_attention(z, z_row0, B, L):
    assert z_row0 % L == 0
    zcol = lambda piece: pl.BlockSpec((L, D_ATTN), lambda b: (z_row0 // L + b, piece))
    return pl.pallas_call(
        _ctx_attn_kernel,
        out_shape=jax.ShapeDtypeStruct((B * L, D_ATTN), BF16),
        grid=(B,),
        in_specs=[zcol(COL_Q), zcol(COL_K), zcol(COL_V), zcol(COL_GA)],
        out_specs=pl.BlockSpec((L, D_ATTN), lambda b: (b, 0)),
        compiler_params=pltpu.CompilerParams(dimension_semantics=("arbitrary",)),
        name="ctx_attn",
    )(z, z, z, z)


def _key_window_start(qb):
    rows = 1024 // GRID_W
    r_first = qb * Q_BLOCK_ROWS
    r_last = r_first + Q_BLOCK_ROWS - 1
    lo = min(max(r_first - WIN_ROWS // 2, 0), rows - WIN_ROWS)
    hi = min(max(r_last - WIN_ROWS // 2, 0), rows - WIN_ROWS) + WIN_ROWS
    start = min(lo, rows - KEY_WIN_ROWS)
    start -= start % Q_BLOCK_ROWS
    assert start <= lo and hi <= start + KEY_WIN_ROWS
    return start


def _nbr_attn_kernel(q_ref, k0_ref, k1_ref, k2_ref, v0_ref, v1_ref, v2_ref, kc_ref, vc_ref, g_ref, tiles_ref,
                     o_ref, bias_ref, *, n_rows):
    masks = _head_masks(NBR_HEAD_TILE)
    n_heads = q_ref.shape[-1] // HEAD_DIM
    chain_tiles = [slice(c * NBR_HEAD_TILE, (c + 1) * NBR_HEAD_TILE) for c in range(q_ref.shape[-1] // NBR_HEAD_TILE)]
    k_refs = (k0_ref, k1_ref, k2_ref)
    v_refs = (v0_ref, v1_ref, v2_ref)
    key_blocks = KEY_WIN // Q_BLOCK

    def window_row_start(r):
        return min(max(r - WIN_ROWS // 2, 0), n_rows - WIN_ROWS)

    def build_bias(qb):
        k_start = _key_window_start(qb)
        used = set()
        for i in range(Q_BLOCK_ROWS):
            r = qb * Q_BLOCK_ROWS + i
            r0 = window_row_start(r)
            for j in range(KEY_WIN_ROWS):
                kr = k_start + j
                cols = slice(j * GRID_W, (j + 1) * GRID_W)
                lanes = slice((j % RPB_COPIES) * GRID_W, (j % RPB_COPIES + 1) * GRID_W)
                inside = r0 <= kr < r0 + WIN_ROWS
                if inside:
                    used.add(j // Q_BLOCK_ROWS)
                for h in range(n_heads):
                    rows = slice(h * Q_BLOCK + i * GRID_W, h * Q_BLOCK + (i + 1) * GRID_W)
                    if inside:
                        bias_ref[rows, cols] = tiles_ref[h, kr - r + WIN_ROWS - 1, :, lanes]
                    else:
                        bias_ref[rows, cols] = jnp.full((GRID_W, GRID_W), NEG_INF, F32)
        return sorted(used)

    def attend(blocks):
        for c, lanes in enumerate(chain_tiles):
            stacked = slice(c * len(masks) * Q_BLOCK, (c + 1) * len(masks) * Q_BLOCK)
            for b in range(q_ref.shape[0]):
                qs = _stack_heads(q_ref[b, :, lanes], masks)
                s = [_qk(qs, k_refs[j][b, :, lanes]) + bias_ref[stacked, j * Q_BLOCK:(j + 1) * Q_BLOCK] for j in blocks]
                s.append(_dot(qs, kc_ref[b, lanes, :].astype(BF16)))
                m = jnp.max(functools.reduce(jnp.maximum, s), axis=-1, keepdims=True)
                p = [jnp.exp2(x - m) for x in s]
                l = jnp.sum(functools.reduce(jnp.add, p), axis=-1, keepdims=True)
                p = [x.astype(BF16) for x in p]
                v_win = jnp.concatenate([v_refs[j][b, :, lanes] for j in blocks], axis=0)
                o = _dot(jnp.concatenate(p[:-1], axis=1), v_win) + _qk(p[-1], vc_ref[b, lanes, :].astype(BF16))
                o_ref[b, :, lanes] = (_unstack_heads(o / l, masks) * g_ref[b, :, lanes].astype(F32)).astype(BF16)

    for qb in range(n_rows // Q_BLOCK_ROWS):
        @pl.when(pl.program_id(0) == qb)
        def _(qb=qb):
            blocks = build_bias(qb)
            assert blocks and all(0 <= j < key_blocks for j in blocks)
            attend(blocks)


def _neighbourhood_attention(z, z_row0, B, L, cache_k, cache_v, tiles):
    assert z_row0 % (B * L) == 0
    group = z_row0 // (B * L)
    n_rows = L // GRID_W
    n_qb = L // Q_BLOCK
    assert KEY_WIN == 3 * Q_BLOCK and all(_key_window_start(qb) % Q_BLOCK_ROWS == 0 for qb in range(n_qb))
    z3 = z.reshape(z.shape[0] // L, L, D_IN)
    n_ctx = cache_k.shape[1]
    kc = jnp.transpose(cache_k, (0, 2, 3, 1)).reshape(B, D_ATTN, n_ctx)
    vc = jnp.transpose(cache_v, (0, 2, 3, 1)).reshape(B, D_ATTN, n_ctx)
    width = NBR_BLOCK_LANES
    tiles_per_col = D_ATTN // width

    def win_block(qb):
        return jnp.where(qb < n_qb // 2, 0, (n_rows - KEY_WIN_ROWS) // Q_BLOCK_ROWS)

    assert [_key_window_start(qb) // Q_BLOCK_ROWS for qb in range(n_qb)] == \
        [0 if qb < n_qb // 2 else (n_rows - KEY_WIN_ROWS) // Q_BLOCK_ROWS for qb in range(n_qb)]
    qspec = lambda piece: pl.BlockSpec((B, Q_BLOCK, width), lambda qb, t: (group, qb, piece * tiles_per_col + t))
    kspec = lambda piece, j: pl.BlockSpec((B, Q_BLOCK, width),
                                          lambda qb, t: (group, win_block(qb) + j, piece * tiles_per_col + t))
    cspec = pl.BlockSpec((B, width, n_ctx), lambda qb, t: (0, t, 0))
    out = pl.pallas_call(
        functools.partial(_nbr_attn_kernel, n_rows=n_rows),
        out_shape=jax.ShapeDtypeStruct((B, L, D_ATTN), BF16),
        grid=(n_qb, tiles_per_col),
        in_specs=[qspec(COL_Q), kspec(COL_K, 0), kspec(COL_K, 1), kspec(COL_K, 2),
                  kspec(COL_V, 0), kspec(COL_V, 1), kspec(COL_V, 2), cspec, cspec, qspec(COL_GA),
                  pl.BlockSpec((width // HEAD_DIM, 2 * WIN_ROWS - 1, GRID_W, LANES), lambda qb, t: (t, 0, 0, 0))],
        out_specs=pl.BlockSpec((B, Q_BLOCK, width), lambda qb, t: (0, qb, t)),
        scratch_shapes=[pltpu.VMEM((width // HEAD_DIM * Q_BLOCK, KEY_WIN), F32)],
        compiler_params=pltpu.CompilerParams(dimension_semantics=("arbitrary", "arbitrary")),
        name="nbr_attn",
    )(z3, z3, z3, z3, z3, z3, z3, kc, vc, z3, tiles)
    return out.reshape(B * L, D_ATTN)


def _merge_kernel(yh_ref, ya_ref, mh_ref, ma_ref, x_ref, mod_ref, wbh_ref, wba_ref, wout_ref, lng_ref, lnb_ref,
                  o_ref, *, alpha):
    p_h = _dot(yh_ref[...], wbh_ref[...])
    p_a = _dot(ya_ref[...], wba_ref[...])
    m_h = mh_ref[...].astype(F32)
    m_a = ma_ref[...].astype(F32)
    out = _dot((m_h * p_h + m_a * p_a).astype(BF16), wout_ref[...])
    gate = mod_ref[:, 2 * D_MODEL:3 * D_MODEL]
    v = alpha * x_ref[...] + gate * out
    mu = jnp.mean(v, axis=-1, keepdims=True)
    d = v - mu
    var = jnp.mean(d * d, axis=-1, keepdims=True)
    o_ref[...] = d * jax.lax.rsqrt(var + LN_EPS) * lng_ref[...] + lnb_ref[...]


def _merge(yh, ya, z, z_row0, x2d, mod3, mod_index, w_bh, w_ba, w_out, ln_g, ln_b, *, alpha, tm):
    M = x2d.shape[0]
    assert z_row0 % tm == 0
    tok = lambda width, col: pl.BlockSpec((tm, width), lambda i: (i, col))
    ztok = lambda col: pl.BlockSpec((tm, D_MODEL), lambda i: (z_row0 // tm + i, col))
    const = lambda shape: pl.BlockSpec(shape, lambda i: (0, 0))
    return pl.pallas_call(
        functools.partial(_merge_kernel, alpha=alpha),
        out_shape=jax.ShapeDtypeStruct((M, D_MODEL), F32),
        grid=(M // tm,),
        in_specs=[tok(D_HYENA, 0), tok(D_ATTN, 0), ztok(COL_MH), ztok(COL_MA), tok(D_MODEL, 0),
                  pl.BlockSpec((None, 1, 3 * D_MODEL), lambda i: (mod_index(i), 0, 0)),
                  const((D_HYENA, D_MODEL)), const((D_ATTN, D_MODEL)), const((D_MODEL, D_MODEL)),
                  const((1, D_MODEL)), const((1, D_MODEL))],
        out_specs=pl.BlockSpec((tm, D_MODEL), lambda i: (i, 0)),
        compiler_params=pltpu.CompilerParams(
            dimension_semantics=("arbitrary",),
            vmem_limit_bytes=_vmem_limit(
                2 * (4 * _nbytes((tm, D_MODEL), BF16) + 2 * _nbytes((tm, D_MODEL), F32)
                     + 3 * _nbytes((D_MODEL, D_MODEL), BF16)),
                8 * _nbytes((tm, D_MODEL), F32))),
        name="merge",
    )(yh, ya, z, z, x2d, mod3, w_bh, w_ba, w_out, ln_g.reshape(1, -1), ln_b.reshape(1, -1))


def _mixers(x, z, z_row0, mod3, cond_row, ctx_kv, p, filt, tiles, *, alpha):
    B, L, _ = x.shape
    yh = _hyena_branch(z, z_row0, B, L, p["conv_w"], p["conv_b"], p["hyena_d"], *filt,
                       seqs_per_step=max(1, HYENA_ROWS_PER_STEP // L))
    if ctx_kv is None:
        ya = _context_attention(z, z_row0, B, L)
    else:
        ya = _neighbourhood_attention(z, z_row0, B, L, ctx_kv[0], ctx_kv[1], tiles)
    y = _merge(yh, ya, z, z_row0, x.reshape(B * L, D_MODEL), mod3, lambda i: cond_row(i * MERGE_TM // L),
               p["w_bh"], p["w_ba"], p["w_out"], p["ln_g"], p["ln_b"], alpha=alpha, tm=MERGE_TM)
    return y.reshape(B, L, D_MODEL)


def kernel(x_prompt, x_sample, c, cache_k, cache_v, c_ctx, w_ada, b_ada, w_in, conv_w, conv_b, filt_w1, filt_b1,
           filt_w2, filt_b2, filt_w3, filt_freq, hyena_d, rpb, w_bh, w_ba, w_out, ln_g, ln_b):
    depth = w_in.shape[0]
    alpha = (2.0 * depth) ** 0.25
    n_lat, lat_len = x_sample.shape[0], x_sample.shape[1]
    n_ctx, ctx_len = x_prompt.shape[0], x_prompt.shape[1]
    assert ctx_len == IN_TM
    ctx_row = n_lat

    xp, xs = x_prompt, x_sample
    new_k, new_v = [], []
    for l in range(depth):
        mod3 = _modulation(c, c_ctx.reshape(1, -1), w_ada[l], b_ada[l])
        filt = (filt_w1[l], filt_b1[l], filt_w2[l], filt_b2[l], filt_w3[l], filt_freq[l])
        tiles = _rpb_tiles(rpb[l])
        z, k_ctx, v_ctx, w_bh16, w_ba16, w_out16 = _in_projection(
            xs.reshape(-1, D_MODEL), xp.reshape(-1, D_MODEL), mod3, lambda tile: tile * IN_TM // lat_len, ctx_row,
            w_in[l], (w_bh[l], w_ba[l], w_out[l]), tm=IN_TM)
        p = {"conv_w": conv_w[l:l + 1], "conv_b": conv_b[l], "hyena_d": hyena_d[l],
             "w_bh": w_bh16, "w_ba": w_ba16, "w_out": w_out16, "ln_g": ln_g[l], "ln_b": ln_b[l]}
        new_k.append(jnp.transpose(k_ctx.reshape(n_ctx, N_HEADS, HEAD_DIM, ctx_len), (0, 3, 1, 2)))
        new_v.append(jnp.transpose(v_ctx.reshape(n_ctx, N_HEADS, HEAD_DIM, ctx_len), (0, 3, 1, 2)))
        xp = _mixers(xp, z, n_lat * lat_len, mod3, lambda b: ctx_row, None, p, filt, None, alpha=alpha)
        xs = _mixers(xs, z, 0, mod3, lambda b: b, (cache_k[:, l], cache_v[:, l]), p, filt, tiles, alpha=alpha)
    return xp, xs, jnp.stack(new_k, axis=1), jnp.stack(new_v, axis=1)
```

```python
import functools
import math

import jax
import jax.numpy as jnp
import numpy as np
from jax.experimental import pallas as pl
from jax.experimental.pallas import tpu as pltpu

F32 = jnp.float32
BF16 = jnp.bfloat16

D_MODEL = 1024
D_HYENA = 1024
N_HEADS = 16
HEAD_DIM = 64
D_ATTN = N_HEADS * HEAD_DIM
D_IN = 4 * D_HYENA + 4 * D_ATTN + 2 * D_MODEL
GRID_W = 64
WIN_ROWS = 8
WIN_COLS = 16
FILTER_EMB = 33
FILTER_BANDS = (FILTER_EMB - 1) // 2
FILTER_HIDDEN = 64
DECAY_TARGET = 1e-2
MIN_DECAY = math.log(DECAY_TARGET) / 1.5
MAX_DECAY = math.log(DECAY_TARGET) / 0.3
DECAY_SHIFT = 0.05
LN_EPS = 1e-5
NEG_INF = -1e30

COL_VH, COL_X1, COL_X0, COL_GH, COL_Q, COL_K, COL_V, COL_GA, COL_MH, COL_MA = range(10)
PROJ_ORDER = (COL_GH, COL_GA, COL_MH, COL_MA, COL_K, COL_V, COL_Q, COL_VH, COL_X1, COL_X0)

LANES = 128
SUBLANES = 8
CTX_HEAD_TILE = 128
NBR_HEAD_TILE = 128
NBR_BLOCK_LANES = 128
RPB_COPIES = LANES // GRID_W
V7X_VMEM_BYTES = 64 * 1024 * 1024
ATTN_SCALE = HEAD_DIM ** -0.5
LOG2E = math.log2(math.e)

Q_BLOCK = 256
Q_BLOCK_ROWS = Q_BLOCK // GRID_W
KEY_WIN_ROWS = 12
KEY_WIN = KEY_WIN_ROWS * GRID_W

IN_TM = 256
W_STAGES = 2
CAST_ROWS = 64
MERGE_TM = 512
HYENA_ROWS_PER_STEP = 4096
MLP_ROWS = 128


def _vmem_limit(block_bytes, temp_bytes):
    need = int(block_bytes + temp_bytes)
    return min(max(need, 16 * 1024 * 1024), V7X_VMEM_BYTES - 8 * 1024 * 1024)


def _nbytes(shape, dtype):
    return int(np.prod(shape)) * jnp.dtype(dtype).itemsize


def _silu(x):
    return x * jax.nn.sigmoid(x)


def _split_bf16(a):
    hi = a.astype(BF16)
    lo = (a - hi.astype(F32)).astype(BF16)
    return hi, lo


def _dot(a, b):
    return jnp.dot(a, b, preferred_element_type=F32)


def _dot_split(a_hi, a_lo, b):
    b_hi, b_lo = _split_bf16(b)
    return _dot(a_hi, b_hi) + (_dot(a_hi, b_lo) + _dot(a_lo, b_hi))


@functools.lru_cache(maxsize=None)
def _dft_constants(L):
    n = 2 * L
    k = np.arange(L, dtype=np.float64)[:, None]
    s = np.arange(L, dtype=np.float64)[None, :]
    ang = 2.0 * np.pi * k * s / n
    nyq = np.cos(np.pi * np.arange(L, dtype=np.float64))
    C = np.cos(ang)
    S = np.sin(ang)
    S[0, :] = nyq
    F = np.concatenate([C, S], axis=0)
    w = np.full((L,), 2.0)
    w[0] = 1.0
    Gc = (C * w[:, None]).T / n
    Ss = 2.0 * np.sin(ang)
    Ss[0, :] = nyq
    Gs = Ss.T / n
    G = np.concatenate([Gc, Gs], axis=1)
    return F.astype(np.float32), G.astype(np.float32)


def _dft_matrices(L):
    F, G = _dft_constants(L)
    return jnp.asarray(F).astype(BF16), jnp.asarray(G).astype(BF16)


@functools.lru_cache(maxsize=None)
def _filter_constants(L):
    t = np.linspace(0.0, 1.0, L, dtype=np.float32)[:, None]
    bands = np.linspace(1e-4, FILTER_BANDS - 1, FILTER_BANDS, dtype=np.float32)[None]
    w = (2.0 * math.pi * np.arange(L, dtype=np.float32)[:, None] / L).astype(np.float32)
    z = np.concatenate([t, np.cos(bands * w), -np.sin(bands * w)], axis=-1).astype(np.float32)
    z_pad = np.zeros((L, LANES), np.float32)
    z_pad[:, :FILTER_EMB] = z
    deltas = np.linspace(MIN_DECAY, MAX_DECAY, D_HYENA, dtype=np.float32)
    decay = (np.exp(-t * np.abs(deltas)) + np.float32(DECAY_SHIFT)).astype(np.float32)
    return z_pad, decay


@functools.lru_cache(maxsize=None)
def _rpb_constants():
    c = np.arange(GRID_W)[:, None]
    kc = np.arange(GRID_W)[None, :]
    c0 = np.clip(c - WIN_COLS // 2, 0, GRID_W - WIN_COLS)
    in_win = (kc >= c0) & (kc < c0 + WIN_COLS)
    idx = np.clip(kc - c + WIN_COLS - 1, 0, 2 * WIN_COLS - 2)
    onehot = np.zeros((LANES, GRID_W, LANES), np.float32)
    for half in range(RPB_COPIES):
        lanes = half * GRID_W + np.arange(GRID_W)
        onehot[idx, c, lanes[None, :]] = 1.0
    mask = np.concatenate([in_win, in_win], axis=1).astype(np.float32)
    return onehot.reshape(LANES, GRID_W * LANES).astype(BF16), mask.reshape(1, GRID_W * LANES)


def _mod_kernel(c_ref, cctx_ref, w_ref, b_ref, o_ref, cond_ref):
    n_lat = c_ref.shape[0]
    cond_ref[...] = jnp.zeros(cond_ref.shape, F32)
    cond_ref[0:n_lat, :] = c_ref[...]
    cond_ref[n_lat:n_lat + 1, :] = cctx_ref[...]
    s = _silu(cond_ref[...]).astype(BF16)
    o_ref[:, 0, :] = _dot(s, w_ref[...].astype(BF16)) + b_ref[...]


def _modulation(c, c_ctx, w_ada, b_ada):
    n_lat = c.shape[0]
    assert n_lat + 1 <= SUBLANES
    return pl.pallas_call(
        _mod_kernel,
        out_shape=jax.ShapeDtypeStruct((SUBLANES, 1, 3 * D_MODEL), F32),
        grid=(3,),
        in_specs=[pl.BlockSpec((n_lat, D_MODEL), lambda j: (0, 0)),
                  pl.BlockSpec((1, D_MODEL), lambda j: (0, 0)),
                  pl.BlockSpec((D_MODEL, D_MODEL), lambda j: (0, j)),
                  pl.BlockSpec((1, D_MODEL), lambda j: (0, j))],
        out_specs=pl.BlockSpec((SUBLANES, 1, D_MODEL), lambda j: (0, 0, j)),
        scratch_shapes=[pltpu.VMEM((SUBLANES, D_MODEL), F32)],
        name="mod",
    )(c, c_ctx, w_ada, b_ada.reshape(1, -1))


def _filter_spectra(z_ref, w1_ref, b1_ref, w2_ref, b2_ref, w3f_ref, w3b_ref, freq_ref, decay_ref, d_ref, f_ref,
                    hf_ref, fp_ref, fqa_ref, fpb_ref):
    L = decay_ref.shape[0]

    @pl.when(pl.program_id(0) == 0)
    def _():
        def chunk(i, carry):
            rows = pl.ds(pl.multiple_of(i * MLP_ROWS, MLP_ROWS), MLP_ROWS)
            z_hi, z_lo = _split_bf16(z_ref[rows, :])
            hf = jnp.sin(freq_ref[0:1, :] * (_dot_split(z_hi, z_lo, w1_ref[...]) + b1_ref[...]))
            h_hi, h_lo = _split_bf16(hf)
            hf_ref[rows, :] = jnp.sin(freq_ref[1:2, :] * (_dot_split(h_hi, h_lo, w2_ref[...]) + b2_ref[...]))
            return carry

        jax.lax.fori_loop(0, L // MLP_ROWS, chunk, 0)

    h_hi, h_lo = _split_bf16(hf_ref[...])
    decay = decay_ref[...]
    h_fwd = _dot_split(h_hi, h_lo, w3f_ref[...]) * decay
    h_bwd = _dot_split(h_hi, h_lo, w3b_ref[...]) * decay
    row0 = jax.lax.broadcasted_iota(jnp.int32, (L, 1), 0) == 0
    h_bwd = jnp.where(row0, 0.0, h_bwd)
    fsum = (h_fwd + h_bwd).astype(BF16)
    fdif = (h_fwd - h_bwd).astype(BF16)
    skip = d_ref[...]
    fp = _dot(f_ref[0:L, :], fsum) + skip
    fq = _dot(f_ref[L:2 * L, :], fdif)
    nyq = _dot(f_ref[L:L + 2 * SUBLANES, :], fsum)[0:1, :] + skip
    fp_ref[...] = fp
    fqa_ref[...] = jnp.where(row0, 0.0, fq)
    fpb_ref[...] = jnp.where(row0, nyq, fp)


def _rpb_kernel(r_ref, onehot_ref, mask_ref, o_ref):
    r = r_ref[...]
    r_hi = r.astype(BF16)
    rem = r - r_hi.astype(F32)
    r_mid = rem.astype(BF16)
    r_lo = (rem - r_mid.astype(F32)).astype(BF16)
    oh = onehot_ref[...]
    t = _dot(r_hi, oh) + (_dot(r_mid, oh) + _dot(r_lo, oh))
    t = jnp.where(mask_ref[...] > 0.0, t * LOG2E, NEG_INF)
    for c in range(o_ref.shape[1]):
        o_ref[:, c, :] = t[:, c * LANES:(c + 1) * LANES]


def _rpb_tiles(rpb):
    n_dr = 2 * WIN_ROWS - 1
    rows = N_HEADS * n_dr
    onehot, mask = _rpb_constants()
    r = jnp.pad(rpb.reshape(rows, 2 * WIN_COLS - 1), ((0, 0), (0, LANES - (2 * WIN_COLS - 1))))
    cols_per_step = 16
    tn = cols_per_step * LANES
    out = pl.pallas_call(
        _rpb_kernel,
        out_shape=jax.ShapeDtypeStruct((rows, GRID_W, LANES), F32),
        grid=(GRID_W // cols_per_step,),
        in_specs=[pl.BlockSpec((rows, LANES), lambda j: (0, 0)),
                  pl.BlockSpec((LANES, tn), lambda j: (0, j)),
                  pl.BlockSpec((1, tn), lambda j: (0, j))],
        out_specs=pl.BlockSpec((rows, cols_per_step, LANES), lambda j: (0, j, 0)),
        name="rpb_tiles",
    )(r, onehot, mask)
    return out.reshape(N_HEADS, n_dr, GRID_W, LANES)


def _inproj_kernel(xl_ref, xc_ref, mod_ref, w_hbm, wbh_ref, wba_ref, wout_ref,
                   z_ref, kt_ref, vt_ref, wbh16_ref, wba16_ref, wout16_ref, w_ref, stage_ref, sem, *, n_lat_tiles):
    n_col = D_IN // D_MODEL
    n_stage = stage_ref.shape[0]
    epilogue = {COL_GH: _silu, COL_GA: _silu, COL_MH: jax.nn.sigmoid, COL_MA: jax.nn.sigmoid,
                COL_Q: lambda acc: acc * (ATTN_SCALE * LOG2E)}

    def columns(col):
        return slice(col * D_MODEL, (col + 1) * D_MODEL)

    def fetch(n):
        return pltpu.make_async_copy(w_hbm.at[:, columns(PROJ_ORDER[n])], stage_ref.at[n % n_stage], sem.at[n % n_stage])

    def tile(stream_weights):
        if stream_weights:
            for n in range(min(n_stage, n_col)):
                fetch(n).start()
        shift = mod_ref[:, 0:D_MODEL]
        scale = mod_ref[:, D_MODEL:2 * D_MODEL]
        x = jnp.where(pl.program_id(0) < n_lat_tiles, xl_ref[...], xc_ref[...])
        h = (x * (1.0 + scale) + shift).astype(BF16)
        for src_ref, dst_ref in ((wbh_ref, wbh16_ref), (wba_ref, wba16_ref), (wout_ref, wout16_ref)):
            dst_ref[...] = src_ref[...].astype(BF16)
        for n, col in enumerate(PROJ_ORDER):
            if stream_weights:
                fetch(n).wait()
                w_ref[:, columns(col)] = stage_ref[n % n_stage].astype(BF16)
                if n + n_stage < n_col:
                    fetch(n + n_stage).start()
            acc = _dot(h, w_ref[:, columns(col)])
            z_ref[:, columns(col)] = epilogue.get(col, lambda acc: acc)(acc).astype(BF16)
            if col == COL_K:
                kt_ref[...] = acc.T
            if col == COL_V:
                vt_ref[...] = acc.T

    pl.when(pl.program_id(0) == 0)(functools.partial(tile, True))
    pl.when(pl.program_id(0) != 0)(functools.partial(tile, False))


def _in_projection(x_lat, x_ctx, mod3, lat_row, ctx_row, w_in, merge_weights, *, tm):
    n_lat, n_ctx = x_lat.shape[0] // tm, x_ctx.shape[0] // tm
    is_lat = lambda i: i < n_lat
    lat_tile = lambda i: jnp.minimum(i, n_lat - 1)
    ctx_tile = lambda i: jnp.maximum(i - n_lat, 0)
    kv_shape = jax.ShapeDtypeStruct((n_ctx, D_ATTN, tm), F32)
    kv_spec = pl.BlockSpec((None, D_ATTN, tm), lambda i: (ctx_tile(i), 0, 0))
    n_cast = D_MODEL // CAST_ROWS
    assert n_cast <= n_lat + n_ctx and all(w.shape == (D_MODEL, D_MODEL) for w in merge_weights)
    cast_spec = pl.BlockSpec((CAST_ROWS, D_MODEL), lambda i: (jnp.minimum(i, n_cast - 1), 0))
    stage = (W_STAGES, D_MODEL, D_MODEL)
    blocks = (_nbytes((D_MODEL, D_IN), BF16) + _nbytes(stage, F32) + 4 * _nbytes((tm, D_MODEL), F32)
              + 2 * _nbytes((tm, D_IN), BF16) + 4 * _nbytes((tm, D_ATTN), F32)
              + 3 * 2 * (_nbytes((CAST_ROWS, D_MODEL), F32) + _nbytes((CAST_ROWS, D_MODEL), BF16)))
    return pl.pallas_call(
        functools.partial(_inproj_kernel, n_lat_tiles=n_lat),
        out_shape=[jax.ShapeDtypeStruct(((n_lat + n_ctx) * tm, D_IN), BF16), kv_shape, kv_shape]
                  + [jax.ShapeDtypeStruct((D_MODEL, D_MODEL), BF16)] * 3,
        grid=(n_lat + n_ctx,),
        in_specs=[pl.BlockSpec((tm, D_MODEL), lambda i: (lat_tile(i), 0)),
                  pl.BlockSpec((tm, D_MODEL), lambda i: (ctx_tile(i), 0)),
                  pl.BlockSpec((None, 1, 3 * D_MODEL),
                               lambda i: (jnp.where(is_lat(i), lat_row(lat_tile(i)), ctx_row), 0, 0)),
                  pl.BlockSpec(memory_space=pl.ANY)] + [cast_spec] * 3,
        out_specs=[pl.BlockSpec((tm, D_IN), lambda i: (i, 0)), kv_spec, kv_spec] + [cast_spec] * 3,
        scratch_shapes=[pltpu.VMEM((D_MODEL, D_IN), BF16), pltpu.VMEM(stage, F32),
                        pltpu.SemaphoreType.DMA((W_STAGES,))],
        compiler_params=pltpu.CompilerParams(
            dimension_semantics=("arbitrary",),
            vmem_limit_bytes=_vmem_limit(blocks, 4 * _nbytes((tm, D_MODEL), F32))),
        name="inproj",
    )(x_lat, x_ctx, mod3, w_in, *merge_weights)


def _hyena_kernel(vh_ref, x1_ref, x0_ref, gh_ref, cwv_ref, cw1_ref, cw0_ref, cbv_ref, cb1_ref, cb0_ref,
                  f_ref, g_ref, z_ref, w1_ref, b1_ref, w2_ref, b2_ref, w3f_ref, w3b_ref, freq_ref, decay_ref, d_ref,
                  o_ref, hf_ref, fp_ref, fqa_ref, fpb_ref):
    L = fp_ref.shape[0]
    slab = jax.lax.broadcasted_iota(jnp.int32, (SUBLANES, 1), 0)

    @pl.when(pl.program_id(1) == 0)
    def _():
        _filter_spectra(z_ref, w1_ref, b1_ref, w2_ref, b2_ref, w3f_ref, w3b_ref, freq_ref, decay_ref, d_ref, f_ref,
                        hf_ref, fp_ref, fqa_ref, fpb_ref)

    def short_conv(x_ref, rows, w_ref, b_ref):
        x = x_ref[rows, :].astype(F32)
        prev = pltpu.roll(x, 1, 0)
        nxt = pltpu.roll(x, L - 1, 0)
        prev = jnp.concatenate([jnp.where(slab == 0, 0.0, prev[:SUBLANES]), prev[SUBLANES:]], axis=0)
        nxt = jnp.concatenate([nxt[:-SUBLANES], jnp.where(slab == SUBLANES - 1, 0.0, nxt[-SUBLANES:])], axis=0)
        return prev * w_ref[0:1, :] + x * w_ref[1:2, :] + nxt * w_ref[2:3, :] + b_ref[...]

    for s in range(vh_ref.shape[0] // L):
        rows = slice(s * L, (s + 1) * L)
        u = short_conv(vh_ref, rows, cwv_ref, cbv_ref) * short_conv(x1_ref, rows, cw1_ref, cb1_ref)
        t = _dot(f_ref[...], u.astype(BF16))
        p = t[:L]
        q = t[L:]
        yp = p * fp_ref[...] - q * fqa_ref[...]
        yq = p * fqa_ref[...] + q * fpb_ref[...]
        y = _dot(g_ref[...], jnp.concatenate([yp.astype(BF16), yq.astype(BF16)], axis=0))
        y_h = y * short_conv(x0_ref, rows, cw0_ref, cb0_ref)
        o_ref[rows, :] = (y_h * gh_ref[rows, :].astype(F32)).astype(BF16)


def _hyena_branch(z, z_row0, B, L, conv_w, conv_b, hyena_d, w1, b1, w2, b2, w3, freq, *, tc=256, seqs_per_step):
    nblk = D_HYENA // tc
    tm = seqs_per_step * L
    assert z_row0 % tm == 0
    f_mat, g_mat = _dft_matrices(L)
    z_emb, decay = _filter_constants(L)
    w1p = jnp.pad(w1, ((0, LANES - FILTER_EMB), (0, 0)))
    zcol = lambda piece: pl.BlockSpec((tm, tc), lambda c, b: (z_row0 // tm + b, piece * nblk + c))
    wcol = lambda rows, piece: pl.BlockSpec((rows, tc), lambda c, b: (0, piece * nblk + c))
    taps = lambda piece: pl.BlockSpec((None, 3, tc), lambda c, b: (0, 0, piece * nblk + c))
    chan = lambda rows: pl.BlockSpec((rows, tc), lambda c, b: (0, c))
    const = lambda shape: pl.BlockSpec(shape, lambda c, b: (0, 0), pipeline_mode=pl.Buffered(1))
    blocks = (2 * _nbytes((2 * L, L), BF16) + 2 * 5 * _nbytes((tm, tc), BF16) + 5 * _nbytes((L, tc), F32)
              + _nbytes((L, LANES), F32) + _nbytes((L, FILTER_HIDDEN), F32))
    return pl.pallas_call(
        _hyena_kernel,
        out_shape=jax.ShapeDtypeStruct((B * L, D_HYENA), BF16),
        grid=(nblk, B // seqs_per_step),
        in_specs=[zcol(COL_VH), zcol(COL_X1), zcol(COL_X0), zcol(COL_GH),
                  taps(0), taps(1), taps(2), wcol(1, 0), wcol(1, 1), wcol(1, 2),
                  const((2 * L, L)), const((L, 2 * L)),
                  const((L, LANES)),
                  const((LANES, FILTER_HIDDEN)), const((1, FILTER_HIDDEN)),
                  const((FILTER_HIDDEN, FILTER_HIDDEN)), const((1, FILTER_HIDDEN)),
                  wcol(FILTER_HIDDEN, 0), wcol(FILTER_HIDDEN, 1),
                  const((2, FILTER_HIDDEN)), chan(L), chan(1)],
        out_specs=pl.BlockSpec((tm, tc), lambda c, b: (b, c)),
        scratch_shapes=[pltpu.VMEM((L, FILTER_HIDDEN), F32)] + [pltpu.VMEM((L, tc), F32)] * 3,
        compiler_params=pltpu.CompilerParams(
            dimension_semantics=("arbitrary", "arbitrary"),
            vmem_limit_bytes=_vmem_limit(blocks, 16 * _nbytes((tm, tc), F32))),
        name=f"hyena_{L}",
    )(z, z, z, z, conv_w, conv_w, conv_w, conv_b.reshape(1, -1), conv_b.reshape(1, -1), conv_b.reshape(1, -1),
      f_mat, g_mat, z_emb, w1p, b1.reshape(1, -1), w2, b2.reshape(1, -1), w3, w3, freq, decay, hyena_d.reshape(1, -1))


def _head_masks(width):
    lane = jax.lax.broadcasted_iota(jnp.int32, (1, width), 1)
    return [(lane >= h * HEAD_DIM) & (lane < (h + 1) * HEAD_DIM) for h in range(width // HEAD_DIM)]


def _qk(q, k):
    return jax.lax.dot_general(q, k, (((1,), (1,)), ((), ())), preferred_element_type=F32)


def _stack_heads(q, masks):
    return jnp.concatenate([jnp.where(msk, q, jnp.zeros_like(q)) for msk in masks], axis=0)


def _unstack_heads(o, masks):
    n = o.shape[0] // len(masks)
    out = o[:n]
    for h in range(1, len(masks)):
        out = jnp.where(masks[h], o[h * n:(h + 1) * n], out)
    return out


def _ctx_attn_kernel(q_ref, k_ref, v_ref, g_ref, o_ref):
    masks = _head_masks(CTX_HEAD_TILE)
    for t in range(D_ATTN // CTX_HEAD_TILE):
        cols = slice(t * CTX_HEAD_TILE, (t + 1) * CTX_HEAD_TILE)
        s = _qk(_stack_heads(q_ref[:, cols], masks), k_ref[:, cols])
        m = jnp.max(s, axis=-1, keepdims=True)
        p = jnp.exp2(s - m)
        l = jnp.sum(p, axis=-1, keepdims=True)
        out = _unstack_heads(_dot(p.astype(BF16), v_ref[:, cols]) / l, masks)
        o_ref[:, cols] = (out * g_ref[:, cols].astype(F32)).astype(BF16)


def _context_attention(z, z_row0, B, L):
    assert z_row0 % L == 0
    zcol = lambda piece: pl.BlockSpec((L, D_ATTN), lambda b: (z_row0 // L + b, piece))
    return pl.pallas_call(
        _ctx_attn_kernel,
        out_shape=jax.ShapeDtypeStruct((B * L, D_ATTN), BF16),
        grid=(B,),
        in_specs=[zcol(COL_Q), zcol(COL_K), zcol(COL_V), zcol(COL_GA)],
        out_specs=pl.BlockSpec((L, D_ATTN), lambda b: (b, 0)),
        compiler_params=pltpu.CompilerParams(dimension_semantics=("arbitrary",)),
        name="ctx_attn",
    )(z, z, z, z)


def _key_window_start(qb):
    rows = 1024 // GRID_W
    r_first = qb * Q_BLOCK_ROWS
    r_last = r_first + Q_BLOCK_ROWS - 1
    lo = min(max(r_first - WIN_ROWS // 2, 0), rows - WIN_ROWS)
    hi = min(max(r_last - WIN_ROWS // 2, 0), rows - WIN_ROWS) + WIN_ROWS
    start = min(lo, rows - KEY_WIN_ROWS)
    start -= start % Q_BLOCK_ROWS
    assert start <= lo and hi <= start + KEY_WIN_ROWS
    return start


def _nbr_attn_kernel(q_ref, k0_ref, k1_ref, k2_ref, v0_ref, v1_ref, v2_ref, kc_ref, vc_ref, g_ref, tiles_ref,
                     o_ref, bias_ref, *, n_rows):
    masks = _head_masks(NBR_HEAD_TILE)
    n_heads = q_ref.shape[-1] // HEAD_DIM
    chain_tiles = [slice(c * NBR_HEAD_TILE, (c + 1) * NBR_HEAD_TILE) for c in range(q_ref.shape[-1] // NBR_HEAD_TILE)]
    k_refs = (k0_ref, k1_ref, k2_ref)
    v_refs = (v0_ref, v1_ref, v2_ref)
    key_blocks = KEY_WIN // Q_BLOCK

    def window_row_start(r):
        return min(max(r - WIN_ROWS // 2, 0), n_rows - WIN_ROWS)

    def build_bias(qb):
        k_start = _key_window_start(qb)
        used = set()
        for i in range(Q_BLOCK_ROWS):
            r = qb * Q_BLOCK_ROWS + i
            r0 = window_row_start(r)
            for j in range(KEY_WIN_ROWS):
                kr = k_start + j
                cols = slice(j * GRID_W, (j + 1) * GRID_W)
                lanes = slice((j % RPB_COPIES) * GRID_W, (j % RPB_COPIES + 1) * GRID_W)
                inside = r0 <= kr < r0 + WIN_ROWS
                if inside:
                    used.add(j // Q_BLOCK_ROWS)
                for h in range(n_heads):
                    rows = slice(h * Q_BLOCK + i * GRID_W, h * Q_BLOCK + (i + 1) * GRID_W)
                    if inside:
                        bias_ref[rows, cols] = tiles_ref[h, kr - r + WIN_ROWS - 1, :, lanes]
                    else:
                        bias_ref[rows, cols] = jnp.full((GRID_W, GRID_W), NEG_INF, F32)
        return sorted(used)

    def attend(blocks):
        for c, lanes in enumerate(chain_tiles):
            stacked = slice(c * len(masks) * Q_BLOCK, (c + 1) * len(masks) * Q_BLOCK)
            for b in range(q_ref.shape[0]):
                qs = _stack_heads(q_ref[b, :, lanes], masks)
                s = [_qk(qs, k_refs[j][b, :, lanes]) + bias_ref[stacked, j * Q_BLOCK:(j + 1) * Q_BLOCK] for j in blocks]
                s.append(_dot(qs, kc_ref[b, lanes, :].astype(BF16)))
                m = jnp.max(functools.reduce(jnp.maximum, s), axis=-1, keepdims=True)
                p = [jnp.exp2(x - m) for x in s]
                l = jnp.sum(functools.reduce(jnp.add, p), axis=-1, keepdims=True)
                p = [x.astype(BF16) for x in p]
                v_win = jnp.concatenate([v_refs[j][b, :, lanes] for j in blocks], axis=0)
                o = _dot(jnp.concatenate(p[:-1], axis=1), v_win) + _qk(p[-1], vc_ref[b, lanes, :].astype(BF16))
                o_ref[b, :, lanes] = (_unstack_heads(o / l, masks) * g_ref[b, :, lanes].astype(F32)).astype(BF16)

    for qb in range(n_rows // Q_BLOCK_ROWS):
        @pl.when(pl.program_id(0) == qb)
        def _(qb=qb):
            blocks = build_bias(qb)
            assert blocks and all(0 <= j < key_blocks for j in blocks)
            attend(blocks)


def _neighbourhood_attention(z, z_row0, B, L, cache_k, cache_v, tiles):
    assert z_row0 % (B * L) == 0
    group = z_row0 // (B * L)
    n_rows = L // GRID_W
    n_qb = L // Q_BLOCK
    assert KEY_WIN == 3 * Q_BLOCK and all(_key_window_start(qb) % Q_BLOCK_ROWS == 0 for qb in range(n_qb))
    z3 = z.reshape(z.shape[0] // L, L, D_IN)
    n_ctx = cache_k.shape[1]
    kc = jnp.transpose(cache_k, (0, 2, 3, 1)).reshape(B, D_ATTN, n_ctx)
    vc = jnp.transpose(cache_v, (0, 2, 3, 1)).reshape(B, D_ATTN, n_ctx)
    width = NBR_BLOCK_LANES
    tiles_per_col = D_ATTN // width

    def win_block(qb):
        return jnp.where(qb < n_qb // 2, 0, (n_rows - KEY_WIN_ROWS) // Q_BLOCK_ROWS)

    assert [_key_window_start(qb) // Q_BLOCK_ROWS for qb in range(n_qb)] == \
        [0 if qb < n_qb // 2 else (n_rows - KEY_WIN_ROWS) // Q_BLOCK_ROWS for qb in range(n_qb)]
    qspec = lambda piece: pl.BlockSpec((B, Q_BLOCK, width), lambda qb, t: (group, qb, piece * tiles_per_col + t))
    kspec = lambda piece, j: pl.BlockSpec((B, Q_BLOCK, width),
                                          lambda qb, t: (group, win_block(qb) + j, piece * tiles_per_col + t))
    cspec = pl.BlockSpec((B, width, n_ctx), lambda qb, t: (0, t, 0))
    out = pl.pallas_call(
        functools.partial(_nbr_attn_kernel, n_rows=n_rows),
        out_shape=jax.ShapeDtypeStruct((B, L, D_ATTN), BF16),
        grid=(n_qb, tiles_per_col),
        in_specs=[qspec(COL_Q), kspec(COL_K, 0), kspec(COL_K, 1), kspec(COL_K, 2),
                  kspec(COL_V, 0), kspec(COL_V, 1), kspec(COL_V, 2), cspec, cspec, qspec(COL_GA),
                  pl.BlockSpec((width // HEAD_DIM, 2 * WIN_ROWS - 1, GRID_W, LANES), lambda qb, t: (t, 0, 0, 0))],
        out_specs=pl.BlockSpec((B, Q_BLOCK, width), lambda qb, t: (0, qb, t)),
        scratch_shapes=[pltpu.VMEM((width // HEAD_DIM * Q_BLOCK, KEY_WIN), F32)],
        compiler_params=pltpu.CompilerParams(dimension_semantics=("arbitrary", "arbitrary")),
        name="nbr_attn",
    )(z3, z3, z3, z3, z3, z3, z3, kc, vc, z3, tiles)
    return out.reshape(B * L, D_ATTN)


def _merge_kernel(yh_ref, ya_ref, mh_ref, ma_ref, x_ref, mod_ref, wbh_ref, wba_ref, wout_ref, lng_ref, lnb_ref,
                  o_ref, *, alpha):
    p_h = _dot(yh_ref[...], wbh_ref[...])
    p_a = _dot(ya_ref[...], wba_ref[...])
    m_h = mh_ref[...].astype(F32)
    m_a = ma_ref[...].astype(F32)
    out = _dot((m_h * p_h + m_a * p_a).astype(BF16), wout_ref[...])
    gate = mod_ref[:, 2 * D_MODEL:3 * D_MODEL]
    v = alpha * x_ref[...] + gate * out
    mu = jnp.mean(v, axis=-1, keepdims=True)
    d = v - mu
    var = jnp.mean(d * d, axis=-1, keepdims=True)
    o_ref[...] = d * jax.lax.rsqrt(var + LN_EPS) * lng_ref[...] + lnb_ref[...]


def _merge(yh, ya, z, z_row0, x2d, mod3, mod_index, w_bh, w_ba, w_out, ln_g, ln_b, *, alpha, tm):
    M = x2d.shape[0]
    assert z_row0 % tm == 0
    tok = lambda width, col: pl.BlockSpec((tm, width), lambda i: (i, col))
    ztok = lambda col: pl.BlockSpec((tm, D_MODEL), lambda i: (z_row0 // tm + i, col))
    const = lambda shape: pl.BlockSpec(shape, lambda i: (0, 0))
    return pl.pallas_call(
        functools.partial(_merge_kernel, alpha=alpha),
        out_shape=jax.ShapeDtypeStruct((M, D_MODEL), F32),
        grid=(M // tm,),
        in_specs=[tok(D_HYENA, 0), tok(D_ATTN, 0), ztok(COL_MH), ztok(COL_MA), tok(D_MODEL, 0),
                  pl.BlockSpec((None, 1, 3 * D_MODEL), lambda i: (mod_index(i), 0, 0)),
                  const((D_HYENA, D_MODEL)), const((D_ATTN, D_MODEL)), const((D_MODEL, D_MODEL)),
                  const((1, D_MODEL)), const((1, D_MODEL))],
        out_specs=pl.BlockSpec((tm, D_MODEL), lambda i: (i, 0)),
        compiler_params=pltpu.CompilerParams(
            dimension_semantics=("arbitrary",),
            vmem_limit_bytes=_vmem_limit(
                2 * (4 * _nbytes((tm, D_MODEL), BF16) + 2 * _nbytes((tm, D_MODEL), F32)
                     + 3 * _nbytes((D_MODEL, D_MODEL), BF16)),
                8 * _nbytes((tm, D_MODEL), F32))),
        name="merge",
    )(yh, ya, z, z, x2d, mod3, w_bh, w_ba, w_out, ln_g.reshape(1, -1), ln_b.reshape(1, -1))


def _mixers(x, z, z_row0, mod3, cond_row, ctx_kv, p, filt, tiles, *, alpha):
    B, L, _ = x.shape
    yh = _hyena_branch(z, z_row0, B, L, p["conv_w"], p["conv_b"], p["hyena_d"], *filt,
                       seqs_per_step=max(1, HYENA_ROWS_PER_STEP // L))
    if ctx_kv is None:
        ya = _context_attention(z, z_row0, B, L)
    else:
        ya = _neighbourhood_attention(z, z_row0, B, L, ctx_kv[0], ctx_kv[1], tiles)
    y = _merge(yh, ya, z, z_row0, x.reshape(B * L, D_MODEL), mod3, lambda i: cond_row(i * MERGE_TM // L),
               p["w_bh"], p["w_ba"], p["w_out"], p["ln_g"], p["ln_b"], alpha=alpha, tm=MERGE_TM)
    return y.reshape(B, L, D_MODEL)


def kernel(x_prompt, x_sample, c, cache_k, cache_v, c_ctx, w_ada, b_ada, w_in, conv_w, conv_b, filt_w1, filt_b1,
           filt_w2, filt_b2, filt_w3, filt_freq, hyena_d, rpb, w_bh, w_ba, w_out, ln_g, ln_b):
    depth = w_in.shape[0]
    alpha = (2.0 * depth) ** 0.25
    n_lat, lat_len = x_sample.shape[0], x_sample.shape[1]
    n_ctx, ctx_len = x_prompt.shape[0], x_prompt.shape[1]
    assert ctx_len == IN_TM
    ctx_row = n_lat

    xp, xs = x_prompt, x_sample
    new_k, new_v = [], []
    for l in range(depth):
        mod3 = _modulation(c, c_ctx.reshape(1, -1), w_ada[l], b_ada[l])
        filt = (filt_w1[l], filt_b1[l], filt_w2[l], filt_b2[l], filt_w3[l], filt_freq[l])
        tiles = _rpb_tiles(rpb[l])
        z, k_ctx, v_ctx, w_bh16, w_ba16, w_out16 = _in_projection(
            xs.reshape(-1, D_MODEL), xp.reshape(-1, D_MODEL), mod3, lambda tile: tile * IN_TM // lat_len, ctx_row,
            w_in[l], (w_bh[l], w_ba[l], w_out[l]), tm=IN_TM)
        p = {"conv_w": conv_w[l:l + 1], "conv_b": conv_b[l], "hyena_d": hyena_d[l],
             "w_bh": w_bh16, "w_ba": w_ba16, "w_out": w_out16, "ln_g": ln_g[l], "ln_b": ln_b[l]}
        new_k.append(jnp.transpose(k_ctx.reshape(n_ctx, N_HEADS, HEAD_DIM, ctx_len), (0, 3, 1, 2)))
        new_v.append(jnp.transpose(v_ctx.reshape(n_ctx, N_HEADS, HEAD_DIM, ctx_len), (0, 3, 1, 2)))
        xp = _mixers(xp, z, n_lat * lat_len, mod3, lambda b: ctx_row, None, p, filt, None, alpha=alpha)
        xs = _mixers(xs, z, 0, mod3, lambda b: b, (cache_k[:, l], cache_v[:, l]), p, filt, tiles, alpha=alpha)
    return xp, xs, jnp.stack(new_k, axis=1), jnp.stack(new_v, axis=1)
```

```python
import functools
import math

import jax
import jax.numpy as jnp
import numpy as np
from jax.experimental import pallas as pl
from jax.experimental.pallas import tpu as pltpu

F32 = jnp.float32
BF16 = jnp.bfloat16

D_MODEL = 1024
D_HYENA = 1024
N_HEADS = 16
HEAD_DIM = 64
D_ATTN = N_HEADS * HEAD_DIM
D_IN = 4 * D_HYENA + 4 * D_ATTN + 2 * D_MODEL
GRID_W = 64
WIN_ROWS = 8
WIN_COLS = 16
FILTER_EMB = 33
FILTER_BANDS = (FILTER_EMB - 1) // 2
FILTER_HIDDEN = 64
DECAY_TARGET = 1e-2
MIN_DECAY = math.log(DECAY_TARGET) / 1.5
MAX_DECAY = math.log(DECAY_TARGET) / 0.3
DECAY_SHIFT = 0.05
LN_EPS = 1e-5
NEG_INF = -1e30

COL_VH, COL_X1, COL_X0, COL_GH, COL_Q, COL_K, COL_V, COL_GA, COL_MH, COL_MA = range(10)
PROJ_ORDER = (COL_GH, COL_GA, COL_MH, COL_MA, COL_K, COL_V, COL_Q, COL_VH, COL_X1, COL_X0)

LANES = 128
SUBLANES = 8
CTX_HEAD_TILE = 128
NBR_HEAD_TILE = 128
NBR_BLOCK_LANES = 128
RPB_COPIES = LANES // GRID_W
V7X_VMEM_BYTES = 64 * 1024 * 1024
ATTN_SCALE = HEAD_DIM ** -0.5
LOG2E = math.log2(math.e)

Q_BLOCK = 256
Q_BLOCK_ROWS = Q_BLOCK // GRID_W
KEY_WIN_ROWS = 12
KEY_WIN = KEY_WIN_ROWS * GRID_W

IN_TM = 256
W_STAGES = 2
CAST_ROWS = 64
MERGE_TM = 512
HYENA_ROWS_PER_STEP = 4096


def _vmem_limit(block_bytes, temp_bytes):
    need = int(block_bytes + temp_bytes)
    return min(max(need, 16 * 1024 * 1024), V7X_VMEM_BYTES - 8 * 1024 * 1024)


def _nbytes(shape, dtype):
    return int(np.prod(shape)) * jnp.dtype(dtype).itemsize


def _silu(x):
    return x * jax.nn.sigmoid(x)


def _split_bf16(a):
    hi = a.astype(BF16)
    lo = (a - hi.astype(F32)).astype(BF16)
    return hi, lo


def _dot(a, b):
    return jnp.dot(a, b, preferred_element_type=F32)


def _dot_split(a_hi, a_lo, b):
    b_hi, b_lo = _split_bf16(b)
    return _dot(a_hi, b_hi) + (_dot(a_hi, b_lo) + _dot(a_lo, b_hi))


@functools.lru_cache(maxsize=None)
def _dft_constants(L):
    n = 2 * L
    k = np.arange(L, dtype=np.float64)[:, None]
    s = np.arange(L, dtype=np.float64)[None, :]
    ang = 2.0 * np.pi * k * s / n
    nyq = np.cos(np.pi * np.arange(L, dtype=np.float64))
    C = np.cos(ang)
    S = np.sin(ang)
    S[0, :] = nyq
    F = np.concatenate([C, S], axis=0)
    w = np.full((L,), 2.0)
    w[0] = 1.0
    Gc = (C * w[:, None]).T / n
    Ss = 2.0 * np.sin(ang)
    Ss[0, :] = nyq
    Gs = Ss.T / n
    G = np.concatenate([Gc, Gs], axis=1)
    return F.astype(np.float32), G.astype(np.float32)


def _dft_matrices(L):
    F, G = _dft_constants(L)
    return jnp.asarray(F).astype(BF16), jnp.asarray(G).astype(BF16)


@functools.lru_cache(maxsize=None)
def _filter_constants(L):
    t = np.linspace(0.0, 1.0, L, dtype=np.float32)[:, None]
    bands = np.linspace(1e-4, FILTER_BANDS - 1, FILTER_BANDS, dtype=np.float32)[None]
    w = (2.0 * math.pi * np.arange(L, dtype=np.float32)[:, None] / L).astype(np.float32)
    z = np.concatenate([t, np.cos(bands * w), -np.sin(bands * w)], axis=-1).astype(np.float32)
    z_pad = np.zeros((L, LANES), np.float32)
    z_pad[:, :FILTER_EMB] = z
    deltas = np.linspace(MIN_DECAY, MAX_DECAY, D_HYENA, dtype=np.float32)
    decay = (np.exp(-t * np.abs(deltas)) + np.float32(DECAY_SHIFT)).astype(np.float32)
    return z_pad, decay


@functools.lru_cache(maxsize=None)
def _rpb_window_mask():
    c = np.arange(GRID_W)[:, None]
    kc = np.arange(GRID_W)[None, :]
    c0 = np.clip(c - WIN_COLS // 2, 0, GRID_W - WIN_COLS)
    in_win = (kc >= c0) & (kc < c0 + WIN_COLS)
    return np.tile(in_win, (1, RPB_COPIES)).astype(np.float32)


def _mod_kernel(c_ref, cctx_ref, w_ref, b_ref, o_ref, cond_ref):
    n_lat = c_ref.shape[0]
    cond_ref[...] = jnp.zeros(cond_ref.shape, F32)
    cond_ref[0:n_lat, :] = c_ref[...]
    cond_ref[n_lat:n_lat + 1, :] = cctx_ref[...]
    s = _silu(cond_ref[...]).astype(BF16)
    o_ref[:, 0, :] = _dot(s, w_ref[...].astype(BF16)) + b_ref[...]


def _modulation(c, c_ctx, w_ada, b_ada):
    n_lat = c.shape[0]
    assert n_lat + 1 <= SUBLANES
    return pl.pallas_call(
        _mod_kernel,
        out_shape=jax.ShapeDtypeStruct((SUBLANES, 1, 3 * D_MODEL), F32),
        grid=(3,),
        in_specs=[pl.BlockSpec((n_lat, D_MODEL), lambda j: (0, 0)),
                  pl.BlockSpec((1, D_MODEL), lambda j: (0, 0)),
                  pl.BlockSpec((D_MODEL, D_MODEL), lambda j: (0, j)),
                  pl.BlockSpec((1, D_MODEL), lambda j: (0, j))],
        out_specs=pl.BlockSpec((SUBLANES, 1, D_MODEL), lambda j: (0, 0, j)),
        scratch_shapes=[pltpu.VMEM((SUBLANES, D_MODEL), F32)],
        name="mod",
    )(c, c_ctx, w_ada, b_ada.reshape(1, -1))


def _filter_spectra(z_ref, w1_ref, b1_ref, w2_ref, b2_ref, w3f_ref, w3b_ref, freq_ref, decay_ref, d_ref, f_ref,
                    hf_ref, fp_ref, fqa_ref, fpb_ref):
    L = decay_ref.shape[0]

    @pl.when(pl.program_id(0) == 0)
    def _():
        z_hi, z_lo = _split_bf16(z_ref[...])
        hf = jnp.sin(freq_ref[0:1, :] * (_dot_split(z_hi, z_lo, w1_ref[...]) + b1_ref[...]))
        h_hi, h_lo = _split_bf16(hf)
        hf_ref[...] = jnp.sin(freq_ref[1:2, :] * (_dot_split(h_hi, h_lo, w2_ref[...]) + b2_ref[...]))

    h_hi, h_lo = _split_bf16(hf_ref[...])
    decay = decay_ref[...]
    h_fwd = _dot_split(h_hi, h_lo, w3f_ref[...]) * decay
    h_bwd = _dot_split(h_hi, h_lo, w3b_ref[...]) * decay
    row0 = jax.lax.broadcasted_iota(jnp.int32, (L, 1), 0) == 0
    h_bwd = jnp.where(row0, 0.0, h_bwd)
    fsum = (h_fwd + h_bwd).astype(BF16)
    fdif = (h_fwd - h_bwd).astype(BF16)
    skip = d_ref[...]
    fp = _dot(f_ref[0:L, :], fsum) + skip
    fq = _dot(f_ref[L:2 * L, :], fdif)
    nyq = _dot(f_ref[L:L + 2 * SUBLANES, :], fsum)[0:1, :] + skip
    fp_ref[...] = fp
    fqa_ref[...] = jnp.where(row0, 0.0, fq)
    fpb_ref[...] = jnp.where(row0, nyq, fp)


def _rpb_kernel(r_ref, mask_ref, o_ref):
    inside = mask_ref[...] > 0.0
    for i in range(o_ref.shape[0]):
        r = jnp.broadcast_to(r_ref[i:i + 1, :], (GRID_W, LANES))
        t = pltpu.roll(r, LANES - (WIN_COLS - 1), 1, stride=1, stride_axis=0)
        o_ref[i] = jnp.where(inside, t * LOG2E, NEG_INF)


def _rpb_tiles(rpb):
    n_dr = 2 * WIN_ROWS - 1
    n_off = 2 * WIN_COLS - 1
    heads_per_step = 4
    rows = heads_per_step * n_dr
    r = jnp.tile(jnp.pad(rpb.reshape(N_HEADS * n_dr, n_off), ((0, 0), (0, GRID_W - n_off))), (1, RPB_COPIES))
    out = pl.pallas_call(
        _rpb_kernel,
        out_shape=jax.ShapeDtypeStruct((N_HEADS * n_dr, GRID_W, LANES), F32),
        grid=(N_HEADS // heads_per_step,),
        in_specs=[pl.BlockSpec((None, rows, LANES), lambda j: (j, 0, 0)),
                  pl.BlockSpec((GRID_W, LANES), lambda j: (0, 0))],
        out_specs=pl.BlockSpec((rows, GRID_W, LANES), lambda j: (j, 0, 0)),
        name="rpb_tiles",
    )(r.reshape(N_HEADS // heads_per_step, rows, LANES), jnp.asarray(_rpb_window_mask()))
    return out.reshape(N_HEADS, n_dr, GRID_W, LANES)


def _inproj_kernel(xl_ref, xc_ref, mod_ref, w_hbm, wbh_ref, wba_ref, wout_ref,
                   z_ref, kt_ref, vt_ref, wbh16_ref, wba16_ref, wout16_ref, w_ref, stage_ref, sem, *, n_lat_tiles):
    n_col = D_IN // D_MODEL
    n_stage = stage_ref.shape[0]
    epilogue = {COL_GH: _silu, COL_GA: _silu, COL_MH: jax.nn.sigmoid, COL_MA: jax.nn.sigmoid,
                COL_Q: lambda acc: acc * (ATTN_SCALE * LOG2E)}

    def columns(col):
        return slice(col * D_MODEL, (col + 1) * D_MODEL)

    def fetch(n):
        return pltpu.make_async_copy(w_hbm.at[:, columns(PROJ_ORDER[n])], stage_ref.at[n % n_stage], sem.at[n % n_stage])

    def tile(stream_weights):
        if stream_weights:
            for n in range(min(n_stage, n_col)):
                fetch(n).start()
        shift = mod_ref[:, 0:D_MODEL]
        scale = mod_ref[:, D_MODEL:2 * D_MODEL]
        x = jnp.where(pl.program_id(0) < n_lat_tiles, xl_ref[...], xc_ref[...])
        h = (x * (1.0 + scale) + shift).astype(BF16)
        for src_ref, dst_ref in ((wbh_ref, wbh16_ref), (wba_ref, wba16_ref), (wout_ref, wout16_ref)):
            dst_ref[...] = src_ref[...].astype(BF16)
        for n, col in enumerate(PROJ_ORDER):
            if stream_weights:
                fetch(n).wait()
                w_ref[:, columns(col)] = stage_ref[n % n_stage].astype(BF16)
                if n + n_stage < n_col:
                    fetch(n + n_stage).start()
            acc = _dot(h, w_ref[:, columns(col)])
            z_ref[:, columns(col)] = epilogue.get(col, lambda acc: acc)(acc).astype(BF16)
            if col == COL_K:
                kt_ref[...] = acc.T
            if col == COL_V:
                vt_ref[...] = acc.T

    pl.when(pl.program_id(0) == 0)(functools.partial(tile, True))
    pl.when(pl.program_id(0) != 0)(functools.partial(tile, False))


def _in_projection(x_lat, x_ctx, mod3, lat_row, ctx_row, w_in, merge_weights, *, tm):
    n_lat, n_ctx = x_lat.shape[0] // tm, x_ctx.shape[0] // tm
    is_lat = lambda i: i < n_lat
    lat_tile = lambda i: jnp.minimum(i, n_lat - 1)
    ctx_tile = lambda i: jnp.maximum(i - n_lat, 0)
    kv_shape = jax.ShapeDtypeStruct((n_ctx, D_ATTN, tm), F32)
    kv_spec = pl.BlockSpec((None, D_ATTN, tm), lambda i: (ctx_tile(i), 0, 0))
    n_cast = D_MODEL // CAST_ROWS
    assert n_cast <= n_lat + n_ctx and all(w.shape == (D_MODEL, D_MODEL) for w in merge_weights)
    cast_spec = pl.BlockSpec((CAST_ROWS, D_MODEL), lambda i: (jnp.minimum(i, n_cast - 1), 0))
    stage = (W_STAGES, D_MODEL, D_MODEL)
    blocks = (_nbytes((D_MODEL, D_IN), BF16) + _nbytes(stage, F32) + 4 * _nbytes((tm, D_MODEL), F32)
              + 2 * _nbytes((tm, D_IN), BF16) + 4 * _nbytes((tm, D_ATTN), F32)
              + 3 * 2 * (_nbytes((CAST_ROWS, D_MODEL), F32) + _nbytes((CAST_ROWS, D_MODEL), BF16)))
    return pl.pallas_call(
        functools.partial(_inproj_kernel, n_lat_tiles=n_lat),
        out_shape=[jax.ShapeDtypeStruct(((n_lat + n_ctx) * tm, D_IN), BF16), kv_shape, kv_shape]
                  + [jax.ShapeDtypeStruct((D_MODEL, D_MODEL), BF16)] * 3,
        grid=(n_lat + n_ctx,),
        in_specs=[pl.BlockSpec((tm, D_MODEL), lambda i: (lat_tile(i), 0)),
                  pl.BlockSpec((tm, D_MODEL), lambda i: (ctx_tile(i), 0)),
                  pl.BlockSpec((None, 1, 3 * D_MODEL),
                               lambda i: (jnp.where(is_lat(i), lat_row(lat_tile(i)), ctx_row), 0, 0)),
                  pl.BlockSpec(memory_space=pl.ANY)] + [cast_spec] * 3,
        out_specs=[pl.BlockSpec((tm, D_IN), lambda i: (i, 0)), kv_spec, kv_spec] + [cast_spec] * 3,
        scratch_shapes=[pltpu.VMEM((D_MODEL, D_IN), BF16), pltpu.VMEM(stage, F32),
                        pltpu.SemaphoreType.DMA((W_STAGES,))],
        compiler_params=pltpu.CompilerParams(
            dimension_semantics=("arbitrary",),
            vmem_limit_bytes=_vmem_limit(blocks, 4 * _nbytes((tm, D_MODEL), F32))),
        name="inproj",
    )(x_lat, x_ctx, mod3, w_in, *merge_weights)


def _hyena_kernel(vh_ref, x1_ref, x0_ref, gh_ref, cwv_ref, cw1_ref, cw0_ref, cbv_ref, cb1_ref, cb0_ref,
                  f_ref, g_ref, z_ref, w1_ref, b1_ref, w2_ref, b2_ref, w3f_ref, w3b_ref, freq_ref, decay_ref, d_ref,
                  o_ref, hf_ref, fp_ref, fqa_ref, fpb_ref):
    L = fp_ref.shape[0]
    slab = jax.lax.broadcasted_iota(jnp.int32, (SUBLANES, 1), 0)

    @pl.when(pl.program_id(1) == 0)
    def _():
        _filter_spectra(z_ref, w1_ref, b1_ref, w2_ref, b2_ref, w3f_ref, w3b_ref, freq_ref, decay_ref, d_ref, f_ref,
                        hf_ref, fp_ref, fqa_ref, fpb_ref)

    def short_conv(x_ref, rows, w_ref, b_ref):
        x = x_ref[rows, :].astype(F32)
        prev = pltpu.roll(x, 1, 0)
        nxt = pltpu.roll(x, L - 1, 0)
        prev = jnp.concatenate([jnp.where(slab == 0, 0.0, prev[:SUBLANES]), prev[SUBLANES:]], axis=0)
        nxt = jnp.concatenate([nxt[:-SUBLANES], jnp.where(slab == SUBLANES - 1, 0.0, nxt[-SUBLANES:])], axis=0)
        return prev * w_ref[0:1, :] + x * w_ref[1:2, :] + nxt * w_ref[2:3, :] + b_ref[...]

    for s in range(vh_ref.shape[0] // L):
        rows = slice(s * L, (s + 1) * L)
        u = short_conv(vh_ref, rows, cwv_ref, cbv_ref) * short_conv(x1_ref, rows, cw1_ref, cb1_ref)
        t = _dot(f_ref[...], u.astype(BF16))
        p = t[:L]
        q = t[L:]
        yp = p * fp_ref[...] - q * fqa_ref[...]
        yq = p * fqa_ref[...] + q * fpb_ref[...]
        y = _dot(g_ref[...], jnp.concatenate([yp.astype(BF16), yq.astype(BF16)], axis=0))
        y_h = y * short_conv(x0_ref, rows, cw0_ref, cb0_ref)
        o_ref[rows, :] = (y_h * gh_ref[rows, :].astype(F32)).astype(BF16)


def _hyena_branch(z, z_row0, B, L, conv_w, conv_b, hyena_d, w1, b1, w2, b2, w3, freq, *, tc=256, seqs_per_step):
    nblk = D_HYENA // tc
    tm = seqs_per_step * L
    assert z_row0 % tm == 0
    f_mat, g_mat = _dft_matrices(L)
    z_emb, decay = _filter_constants(L)
    w1p = jnp.pad(w1, ((0, LANES - FILTER_EMB), (0, 0)))
    zcol = lambda piece: pl.BlockSpec((tm, tc), lambda c, b: (z_row0 // tm + b, piece * nblk + c))
    wcol = lambda rows, piece: pl.BlockSpec((rows, tc), lambda c, b: (0, piece * nblk + c))
    taps = lambda piece: pl.BlockSpec((None, 3, tc), lambda c, b: (0, 0, piece * nblk + c))
    chan = lambda rows: pl.BlockSpec((rows, tc), lambda c, b: (0, c))
    const = lambda shape: pl.BlockSpec(shape, lambda c, b: (0, 0), pipeline_mode=pl.Buffered(1))
    blocks = (2 * _nbytes((2 * L, L), BF16) + 2 * 5 * _nbytes((tm, tc), BF16) + 5 * _nbytes((L, tc), F32)
              + _nbytes((L, LANES), F32) + _nbytes((L, FILTER_HIDDEN), F32))
    return pl.pallas_call(
        _hyena_kernel,
        out_shape=jax.ShapeDtypeStruct((B * L, D_HYENA), BF16),
        grid=(nblk, B // seqs_per_step),
        in_specs=[zcol(COL_VH), zcol(COL_X1), zcol(COL_X0), zcol(COL_GH),
                  taps(0), taps(1), taps(2), wcol(1, 0), wcol(1, 1), wcol(1, 2),
                  const((2 * L, L)), const((L, 2 * L)),
                  const((L, LANES)),
                  const((LANES, FILTER_HIDDEN)), const((1, FILTER_HIDDEN)),
                  const((FILTER_HIDDEN, FILTER_HIDDEN)), const((1, FILTER_HIDDEN)),
                  wcol(FILTER_HIDDEN, 0), wcol(FILTER_HIDDEN, 1),
                  const((2, FILTER_HIDDEN)), chan(L), chan(1)],
        out_specs=pl.BlockSpec((tm, tc), lambda c, b: (b, c)),
        scratch_shapes=[pltpu.VMEM((L, FILTER_HIDDEN), F32)] + [pltpu.VMEM((L, tc), F32)] * 3,
        compiler_params=pltpu.CompilerParams(
            dimension_semantics=("arbitrary", "arbitrary"),
            vmem_limit_bytes=_vmem_limit(blocks, 16 * _nbytes((tm, tc), F32))),
        name=f"hyena_{L}",
    )(z, z, z, z, conv_w, conv_w, conv_w, conv_b.reshape(1, -1), conv_b.reshape(1, -1), conv_b.reshape(1, -1),
      f_mat, g_mat, z_emb, w1p, b1.reshape(1, -1), w2, b2.reshape(1, -1), w3, w3, freq, decay, hyena_d.reshape(1, -1))


def _head_masks(width):
    lane = jax.lax.broadcasted_iota(jnp.int32, (1, width), 1)
    return [(lane >= h * HEAD_DIM) & (lane < (h + 1) * HEAD_DIM) for h in range(width // HEAD_DIM)]


def _qk(q, k):
    return jax.lax.dot_general(q, k, (((1,), (1,)), ((), ())), preferred_element_type=F32)


def _stack_heads(q, masks):
    return jnp.concatenate([jnp.where(msk, q, jnp.zeros_like(q)) for msk in masks], axis=0)


def _unstack_heads(o, masks):
    n = o.shape[0] // len(masks)
    out = o[:n]
    for h in range(1, len(masks)):
        out = jnp.where(masks[h], o[h * n:(h + 1) * n], out)
    return out


def _ctx_attn_kernel(q_ref, k_ref, v_ref, g_ref, o_ref):
    masks = _head_masks(CTX_HEAD_TILE)
    for t in range(D_ATTN // CTX_HEAD_TILE):
        cols = slice(t * CTX_HEAD_TILE, (t + 1) * CTX_HEAD_TILE)
        s = _qk(_stack_heads(q_ref[:, cols], masks), k_ref[:, cols])
        m = jnp.max(s, axis=-1, keepdims=True)
        p = jnp.exp2(s - m)
        l = jnp.sum(p, axis=-1, keepdims=True)
        out = _unstack_heads(_dot(p.astype(BF16), v_ref[:, cols]) / l, masks)
        o_ref[:, cols] = (out * g_ref[:, cols].astype(F32)).astype(BF16)


def _context_attention(z, z_row0, B, L):
    assert z_row0 % L == 0
    zcol = lambda piece: pl.BlockSpec((L, D_ATTN), lambda b: (z_row0 // L + b, piece))
    return pl.pallas_call(
        _ctx_attn_kernel,
        out_shape=jax.ShapeDtypeStruct((B * L, D_ATTN), BF16),
        grid=(B,),
        in_specs=[zcol(COL_Q), zcol(COL_K), zcol(COL_V), zcol(COL_GA)],
        out_specs=pl.BlockSpec((L, D_ATTN), lambda b: (b, 0)),
        compiler_params=pltpu.CompilerParams(dimension_semantics=("arbitrary",)),
        name="ctx_attn",
    )(z, z, z, z)


def _key_window_start(qb):
    rows = 1024 // GRID_W
    r_first = qb * Q_BLOCK_ROWS
    r_last = r_first + Q_BLOCK_ROWS - 1
    lo = min(max(r_first - WIN_ROWS // 2, 0), rows - WIN_ROWS)
    hi = min(max(r_last - WIN_ROWS // 2, 0), rows - WIN_ROWS) + WIN_ROWS
    start = min(lo, rows - KEY_WIN_ROWS)
    start -= start % Q_BLOCK_ROWS
    assert start <= lo and hi <= start + KEY_WIN_ROWS
    return start


def _nbr_attn_kernel(q_ref, k0_ref, k1_ref, k2_ref, v0_ref, v1_ref, v2_ref, kc_ref, vc_ref, g_ref, tiles_ref,
                     o_ref, bias_ref, *, n_rows):
    masks = _head_masks(NBR_HEAD_TILE)
    n_heads = q_ref.shape[-1] // HEAD_DIM
    chain_tiles = [slice(c * NBR_HEAD_TILE, (c + 1) * NBR_HEAD_TILE) for c in range(q_ref.shape[-1] // NBR_HEAD_TILE)]
    k_refs = (k0_ref, k1_ref, k2_ref)
    v_refs = (v0_ref, v1_ref, v2_ref)
    key_blocks = KEY_WIN // Q_BLOCK

    def window_row_start(r):
        return min(max(r - WIN_ROWS // 2, 0), n_rows - WIN_ROWS)

    def build_bias(qb):
        k_start = _key_window_start(qb)
        used = set()
        for i in range(Q_BLOCK_ROWS):
            r = qb * Q_BLOCK_ROWS + i
            r0 = window_row_start(r)
            for j in range(KEY_WIN_ROWS):
                kr = k_start + j
                cols = slice(j * GRID_W, (j + 1) * GRID_W)
                lanes = slice((j % RPB_COPIES) * GRID_W, (j % RPB_COPIES + 1) * GRID_W)
                inside = r0 <= kr < r0 + WIN_ROWS
                if inside:
                    used.add(j // Q_BLOCK_ROWS)
                for h in range(n_heads):
                    rows = slice(h * Q_BLOCK + i * GRID_W, h * Q_BLOCK + (i + 1) * GRID_W)
                    if inside:
                        bias_ref[rows, cols] = tiles_ref[h, kr - r + WIN_ROWS - 1, :, lanes]
                    else:
                        bias_ref[rows, cols] = jnp.full((GRID_W, GRID_W), NEG_INF, F32)
        return sorted(used)

    def attend(blocks):
        for c, lanes in enumerate(chain_tiles):
            stacked = slice(c * len(masks) * Q_BLOCK, (c + 1) * len(masks) * Q_BLOCK)
            for b in range(q_ref.shape[0]):
                qs = _stack_heads(q_ref[b, :, lanes], masks)
                s = [_qk(qs, k_refs[j][b, :, lanes]) + bias_ref[stacked, j * Q_BLOCK:(j + 1) * Q_BLOCK] for j in blocks]
                s.append(_dot(qs, kc_ref[b, lanes, :].astype(BF16)))
                m = jnp.max(functools.reduce(jnp.maximum, s), axis=-1, keepdims=True)
                p = [jnp.exp2(x - m) for x in s]
                l = jnp.sum(functools.reduce(jnp.add, p), axis=-1, keepdims=True)
                p = [x.astype(BF16) for x in p]
                v_win = jnp.concatenate([v_refs[j][b, :, lanes] for j in blocks], axis=0)
                o = _dot(jnp.concatenate(p[:-1], axis=1), v_win) + _qk(p[-1], vc_ref[b, lanes, :].astype(BF16))
                o_ref[b, :, lanes] = (_unstack_heads(o / l, masks) * g_ref[b, :, lanes].astype(F32)).astype(BF16)

    for qb in range(n_rows // Q_BLOCK_ROWS):
        @pl.when(pl.program_id(0) == qb)
        def _(qb=qb):
            blocks = build_bias(qb)
            assert blocks and all(0 <= j < key_blocks for j in blocks)
            attend(blocks)


def _neighbourhood_attention(z, z_row0, B, L, cache_k, cache_v, tiles):
    assert z_row0 % (B * L) == 0
    group = z_row0 // (B * L)
    n_rows = L // GRID_W
    n_qb = L // Q_BLOCK
    assert KEY_WIN == 3 * Q_BLOCK and all(_key_window_start(qb) % Q_BLOCK_ROWS == 0 for qb in range(n_qb))
    z3 = z.reshape(z.shape[0] // L, L, D_IN)
    n_ctx = cache_k.shape[1]
    kc = jnp.transpose(cache_k, (0, 2, 3, 1)).reshape(B, D_ATTN, n_ctx)
    vc = jnp.transpose(cache_v, (0, 2, 3, 1)).reshape(B, D_ATTN, n_ctx)
    width = NBR_BLOCK_LANES
    tiles_per_col = D_ATTN // width

    def win_block(qb):
        return jnp.where(qb < n_qb // 2, 0, (n_rows - KEY_WIN_ROWS) // Q_BLOCK_ROWS)

    assert [_key_window_start(qb) // Q_BLOCK_ROWS for qb in range(n_qb)] == \
        [0 if qb < n_qb // 2 else (n_rows - KEY_WIN_ROWS) // Q_BLOCK_ROWS for qb in range(n_qb)]
    qspec = lambda piece: pl.BlockSpec((B, Q_BLOCK, width), lambda qb, t: (group, qb, piece * tiles_per_col + t))
    kspec = lambda piece, j: pl.BlockSpec((B, Q_BLOCK, width),
                                          lambda qb, t: (group, win_block(qb) + j, piece * tiles_per_col + t))
    cspec = pl.BlockSpec((B, width, n_ctx), lambda qb, t: (0, t, 0))
    out = pl.pallas_call(
        functools.partial(_nbr_attn_kernel, n_rows=n_rows),
        out_shape=jax.ShapeDtypeStruct((B, L, D_ATTN), BF16),
        grid=(n_qb, tiles_per_col),
        in_specs=[qspec(COL_Q), kspec(COL_K, 0), kspec(COL_K, 1), kspec(COL_K, 2),
                  kspec(COL_V, 0), kspec(COL_V, 1), kspec(COL_V, 2), cspec, cspec, qspec(COL_GA),
                  pl.BlockSpec((width // HEAD_DIM, 2 * WIN_ROWS - 1, GRID_W, LANES), lambda qb, t: (t, 0, 0, 0))],
        out_specs=pl.BlockSpec((B, Q_BLOCK, width), lambda qb, t: (0, qb, t)),
        scratch_shapes=[pltpu.VMEM((width // HEAD_DIM * Q_BLOCK, KEY_WIN), F32)],
        compiler_params=pltpu.CompilerParams(dimension_semantics=("arbitrary", "arbitrary")),
        name="nbr_attn",
    )(z3, z3, z3, z3, z3, z3, z3, kc, vc, z3, tiles)
    return out.reshape(B * L, D_ATTN)


def _merge_kernel(yh_ref, ya_ref, mh_ref, ma_ref, x_ref, mod_ref, wbh_ref, wba_ref, wout_ref, lng_ref, lnb_ref,
                  o_ref, *, alpha):
    p_h = _dot(yh_ref[...], wbh_ref[...])
    p_a = _dot(ya_ref[...], wba_ref[...])
    m_h = mh_ref[...].astype(F32)
    m_a = ma_ref[...].astype(F32)
    out = _dot((m_h * p_h + m_a * p_a).astype(BF16), wout_ref[...])
    gate = mod_ref[:, 2 * D_MODEL:3 * D_MODEL]
    v = alpha * x_ref[...] + gate * out
    mu = jnp.mean(v, axis=-1, keepdims=True)
    d = v - mu
    var = jnp.mean(d * d, axis=-1, keepdims=True)
    o_ref[...] = d * jax.lax.rsqrt(var + LN_EPS) * lng_ref[...] + lnb_ref[...]


def _merge(yh, ya, z, z_row0, x2d, mod3, mod_index, w_bh, w_ba, w_out, ln_g, ln_b, *, alpha, tm):
    M = x2d.shape[0]
    assert z_row0 % tm == 0
    tok = lambda width, col: pl.BlockSpec((tm, width), lambda i: (i, col))
    ztok = lambda col: pl.BlockSpec((tm, D_MODEL), lambda i: (z_row0 // tm + i, col))
    const = lambda shape: pl.BlockSpec(shape, lambda i: (0, 0))
    return pl.pallas_call(
        functools.partial(_merge_kernel, alpha=alpha),
        out_shape=jax.ShapeDtypeStruct((M, D_MODEL), F32),
        grid=(M // tm,),
        in_specs=[tok(D_HYENA, 0), tok(D_ATTN, 0), ztok(COL_MH), ztok(COL_MA), tok(D_MODEL, 0),
                  pl.BlockSpec((None, 1, 3 * D_MODEL), lambda i: (mod_index(i), 0, 0)),
                  const((D_HYENA, D_MODEL)), const((D_ATTN, D_MODEL)), const((D_MODEL, D_MODEL)),
                  const((1, D_MODEL)), const((1, D_MODEL))],
        out_specs=pl.BlockSpec((tm, D_MODEL), lambda i: (i, 0)),
        compiler_params=pltpu.CompilerParams(
            dimension_semantics=("arbitrary",),
            vmem_limit_bytes=_vmem_limit(
                2 * (4 * _nbytes((tm, D_MODEL), BF16) + 2 * _nbytes((tm, D_MODEL), F32)
                     + 3 * _nbytes((D_MODEL, D_MODEL), BF16)),
                8 * _nbytes((tm, D_MODEL), F32))),
        name="merge",
    )(yh, ya, z, z, x2d, mod3, w_bh, w_ba, w_out, ln_g.reshape(1, -1), ln_b.reshape(1, -1))


def _mixers(x, z, z_row0, mod3, cond_row, ctx_kv, p, filt, tiles, *, alpha):
    B, L, _ = x.shape
    yh = _hyena_branch(z, z_row0, B, L, p["conv_w"], p["conv_b"], p["hyena_d"], *filt,
                       seqs_per_step=max(1, HYENA_ROWS_PER_STEP // L))
    if ctx_kv is None:
        ya = _context_attention(z, z_row0, B, L)
    else:
        ya = _neighbourhood_attention(z, z_row0, B, L, ctx_kv[0], ctx_kv[1], tiles)
    y = _merge(yh, ya, z, z_row0, x.reshape(B * L, D_MODEL), mod3, lambda i: cond_row(i * MERGE_TM // L),
               p["w_bh"], p["w_ba"], p["w_out"], p["ln_g"], p["ln_b"], alpha=alpha, tm=MERGE_TM)
    return y.reshape(B, L, D_MODEL)


def kernel(x_prompt, x_sample, c, cache_k, cache_v, c_ctx, w_ada, b_ada, w_in, conv_w, conv_b, filt_w1, filt_b1,
           filt_w2, filt_b2, filt_w3, filt_freq, hyena_d, rpb, w_bh, w_ba, w_out, ln_g, ln_b):
    depth = w_in.shape[0]
    alpha = (2.0 * depth) ** 0.25
    n_lat, lat_len = x_sample.shape[0], x_sample.shape[1]
    n_ctx, ctx_len = x_prompt.shape[0], x_prompt.shape[1]
    assert ctx_len == IN_TM
    ctx_row = n_lat

    xp, xs = x_prompt, x_sample
    new_k, new_v = [], []
    for l in range(depth):
        mod3 = _modulation(c, c_ctx.reshape(1, -1), w_ada[l], b_ada[l])
        filt = (filt_w1[l], filt_b1[l], filt_w2[l], filt_b2[l], filt_w3[l], filt_freq[l])
        tiles = _rpb_tiles(rpb[l])
        z, k_ctx, v_ctx, w_bh16, w_ba16, w_out16 = _in_projection(
            xs.reshape(-1, D_MODEL), xp.reshape(-1, D_MODEL), mod3, lambda tile: tile * IN_TM // lat_len, ctx_row,
            w_in[l], (w_bh[l], w_ba[l], w_out[l]), tm=IN_TM)
        p = {"conv_w": conv_w[l:l + 1], "conv_b": conv_b[l], "hyena_d": hyena_d[l],
             "w_bh": w_bh16, "w_ba": w_ba16, "w_out": w_out16, "ln_g": ln_g[l], "ln_b": ln_b[l]}
        new_k.append(jnp.transpose(k_ctx.reshape(n_ctx, N_HEADS, HEAD_DIM, ctx_len), (0, 3, 1, 2)))
        new_v.append(jnp.transpose(v_ctx.reshape(n_ctx, N_HEADS, HEAD_DIM, ctx_len), (0, 3, 1, 2)))
        xp = _mixers(xp, z, n_lat * lat_len, mod3, lambda b: ctx_row, None, p, filt, None, alpha=alpha)
        xs = _mixers(xs, z, 0, mod3, lambda b: b, (cache_k[:, l], cache_v[:, l]), p, filt, tiles, alpha=alpha)
    return xp, xs, jnp.stack(new_k, axis=1), jnp.stack(new_v, axis=1)
```

```python
import functools
import math

import jax
import jax.numpy as jnp
import numpy as np
from jax.experimental import pallas as pl
from jax.experimental.pallas import tpu as pltpu

F32 = jnp.float32
BF16 = jnp.bfloat16

D_MODEL = 1024
D_HYENA = 1024
N_HEADS = 16
HEAD_DIM = 64
D_ATTN = N_HEADS * HEAD_DIM
D_IN = 4 * D_HYENA + 4 * D_ATTN + 2 * D_MODEL
GRID_W = 64
WIN_ROWS = 8
WIN_COLS = 16
FILTER_EMB = 33
FILTER_BANDS = (FILTER_EMB - 1) // 2
FILTER_HIDDEN = 64
DECAY_TARGET = 1e-2
MIN_DECAY = math.log(DECAY_TARGET) / 1.5
MAX_DECAY = math.log(DECAY_TARGET) / 0.3
DECAY_SHIFT = 0.05
LN_EPS = 1e-5
NEG_INF = -1e30

COL_VH, COL_X1, COL_X0, COL_GH, COL_Q, COL_K, COL_V, COL_GA, COL_MH, COL_MA = range(10)
PROJ_ORDER = (COL_GH, COL_GA, COL_MH, COL_MA, COL_K, COL_V, COL_Q, COL_VH, COL_X1, COL_X0)

LANES = 128
SUBLANES = 8
CTX_HEAD_TILE = 128
NBR_HEAD_TILE = 128
NBR_BLOCK_LANES = 128
RPB_COPIES = LANES // GRID_W
V7X_VMEM_BYTES = 64 * 1024 * 1024
ATTN_SCALE = HEAD_DIM ** -0.5
LOG2E = math.log2(math.e)

Q_BLOCK = 256
Q_BLOCK_ROWS = Q_BLOCK // GRID_W
KEY_WIN_ROWS = 12
KEY_WIN = KEY_WIN_ROWS * GRID_W

IN_TM = 256
W_STAGES = 2
CAST_ROWS = 64
MERGE_TM = 512
HYENA_ROWS_PER_STEP = 2048


def _vmem_limit(block_bytes, temp_bytes):
    need = int(block_bytes + temp_bytes)
    return min(max(need, 16 * 1024 * 1024), V7X_VMEM_BYTES - 8 * 1024 * 1024)


def _nbytes(shape, dtype):
    return int(np.prod(shape)) * jnp.dtype(dtype).itemsize


def _silu(x):
    return x * jax.nn.sigmoid(x)


def _split_bf16(a):
    hi = a.astype(BF16)
    lo = (a - hi.astype(F32)).astype(BF16)
    return hi, lo


def _dot(a, b):
    return jnp.dot(a, b, preferred_element_type=F32)


def _dot_split(a_hi, a_lo, b):
    b_hi, b_lo = _split_bf16(b)
    return _dot(a_hi, b_hi) + (_dot(a_hi, b_lo) + _dot(a_lo, b_hi))


@functools.lru_cache(maxsize=None)
def _dft_constants(L):
    n = 2 * L
    k = np.arange(L, dtype=np.float64)[:, None]
    s = np.arange(L, dtype=np.float64)[None, :]
    ang = 2.0 * np.pi * k * s / n
    nyq = np.cos(np.pi * np.arange(L, dtype=np.float64))
    C = np.cos(ang)
    S = np.sin(ang)
    S[0, :] = nyq
    F = np.concatenate([C, S], axis=0)
    w = np.full((L,), 2.0)
    w[0] = 1.0
    Gc = (C * w[:, None]).T / n
    Ss = 2.0 * np.sin(ang)
    Ss[0, :] = nyq
    Gs = Ss.T / n
    G = np.concatenate([Gc, Gs], axis=1)
    return F.astype(np.float32), G.astype(np.float32)


def _dft_matrices(L):
    F, G = _dft_constants(L)
    return jnp.asarray(F).astype(BF16), jnp.asarray(G).astype(BF16)


@functools.lru_cache(maxsize=None)
def _filter_constants(L):
    t = np.linspace(0.0, 1.0, L, dtype=np.float32)[:, None]
    bands = np.linspace(1e-4, FILTER_BANDS - 1, FILTER_BANDS, dtype=np.float32)[None]
    w = (2.0 * math.pi * np.arange(L, dtype=np.float32)[:, None] / L).astype(np.float32)
    z = np.concatenate([t, np.cos(bands * w), -np.sin(bands * w)], axis=-1).astype(np.float32)
    z_pad = np.zeros((L, LANES), np.float32)
    z_pad[:, :FILTER_EMB] = z
    deltas = np.linspace(MIN_DECAY, MAX_DECAY, D_HYENA, dtype=np.float32)
    decay = (np.exp(-t * np.abs(deltas)) + np.float32(DECAY_SHIFT)).astype(np.float32)
    return z_pad, decay


@functools.lru_cache(maxsize=None)
def _rpb_window_mask():
    c = np.arange(GRID_W)[:, None]
    kc = np.arange(GRID_W)[None, :]
    c0 = np.clip(c - WIN_COLS // 2, 0, GRID_W - WIN_COLS)
    in_win = (kc >= c0) & (kc < c0 + WIN_COLS)
    return np.tile(in_win, (1, RPB_COPIES)).astype(np.float32)


def _mod_kernel(c_ref, cctx_ref, w_ref, b_ref, o_ref, cond_ref):
    n_lat = c_ref.shape[0]
    cond_ref[...] = jnp.zeros(cond_ref.shape, F32)
    cond_ref[0:n_lat, :] = c_ref[...]
    cond_ref[n_lat:n_lat + 1, :] = cctx_ref[...]
    s = _silu(cond_ref[...]).astype(BF16)
    o_ref[:, 0, :] = _dot(s, w_ref[...].astype(BF16)) + b_ref[...]


def _modulation(c, c_ctx, w_ada, b_ada):
    n_lat = c.shape[0]
    assert n_lat + 1 <= SUBLANES
    return pl.pallas_call(
        _mod_kernel,
        out_shape=jax.ShapeDtypeStruct((SUBLANES, 1, 3 * D_MODEL), F32),
        grid=(3,),
        in_specs=[pl.BlockSpec((n_lat, D_MODEL), lambda j: (0, 0)),
                  pl.BlockSpec((1, D_MODEL), lambda j: (0, 0)),
                  pl.BlockSpec((D_MODEL, D_MODEL), lambda j: (0, j)),
                  pl.BlockSpec((1, D_MODEL), lambda j: (0, j))],
        out_specs=pl.BlockSpec((SUBLANES, 1, D_MODEL), lambda j: (0, 0, j)),
        scratch_shapes=[pltpu.VMEM((SUBLANES, D_MODEL), F32)],
        name="mod",
    )(c, c_ctx, w_ada, b_ada.reshape(1, -1))


def _filter_spectra(z_ref, w1_ref, b1_ref, w2_ref, b2_ref, w3f_ref, w3b_ref, freq_ref, decay_ref, d_ref, f_ref,
                    hf_ref, fp_ref, fqa_ref, fpb_ref):
    L = decay_ref.shape[0]

    @pl.when(pl.program_id(0) == 0)
    def _():
        z_hi, z_lo = _split_bf16(z_ref[...])
        hf = jnp.sin(freq_ref[0:1, :] * (_dot_split(z_hi, z_lo, w1_ref[...]) + b1_ref[...]))
        h_hi, h_lo = _split_bf16(hf)
        hf_ref[...] = jnp.sin(freq_ref[1:2, :] * (_dot_split(h_hi, h_lo, w2_ref[...]) + b2_ref[...]))

    h_hi, h_lo = _split_bf16(hf_ref[...])
    decay = decay_ref[...]
    h_fwd = _dot_split(h_hi, h_lo, w3f_ref[...]) * decay
    h_bwd = _dot_split(h_hi, h_lo, w3b_ref[...]) * decay
    row0 = jax.lax.broadcasted_iota(jnp.int32, (L, 1), 0) == 0
    h_bwd = jnp.where(row0, 0.0, h_bwd)
    fsum = (h_fwd + h_bwd).astype(BF16)
    fdif = (h_fwd - h_bwd).astype(BF16)
    skip = d_ref[...]
    fp = _dot(f_ref[0:L, :], fsum) + skip
    fq = _dot(f_ref[L:2 * L, :], fdif)
    nyq = _dot(f_ref[L:L + 2 * SUBLANES, :], fsum)[0:1, :] + skip
    fp_ref[...] = fp
    fqa_ref[...] = jnp.where(row0, 0.0, fq)
    fpb_ref[...] = jnp.where(row0, nyq, fp)


def _rpb_kernel(r_ref, mask_ref, o_ref):
    inside = mask_ref[...] > 0.0
    for i in range(o_ref.shape[0]):
        r = jnp.broadcast_to(r_ref[i:i + 1, :], (GRID_W, LANES))
        t = pltpu.roll(r, LANES - (WIN_COLS - 1), 1, stride=1, stride_axis=0)
        o_ref[i] = jnp.where(inside, t * LOG2E, NEG_INF)


def _rpb_tiles(rpb):
    n_dr = 2 * WIN_ROWS - 1
    n_off = 2 * WIN_COLS - 1
    heads_per_step = 4
    rows = heads_per_step * n_dr
    r = jnp.tile(jnp.pad(rpb.reshape(N_HEADS * n_dr, n_off), ((0, 0), (0, GRID_W - n_off))), (1, RPB_COPIES))
    out = pl.pallas_call(
        _rpb_kernel,
        out_shape=jax.ShapeDtypeStruct((N_HEADS * n_dr, GRID_W, LANES), F32),
        grid=(N_HEADS // heads_per_step,),
        in_specs=[pl.BlockSpec((None, rows, LANES), lambda j: (j, 0, 0)),
                  pl.BlockSpec((GRID_W, LANES), lambda j: (0, 0))],
        out_specs=pl.BlockSpec((rows, GRID_W, LANES), lambda j: (j, 0, 0)),
        name="rpb_tiles",
    )(r.reshape(N_HEADS // heads_per_step, rows, LANES), jnp.asarray(_rpb_window_mask()))
    return out.reshape(N_HEADS, n_dr, GRID_W, LANES)


def _inproj_kernel(xl_ref, xc_ref, mod_ref, w_hbm, wbh_ref, wba_ref, wout_ref,
                   z_ref, kt_ref, vt_ref, wbh16_ref, wba16_ref, wout16_ref, w_ref, stage_ref, sem, *, n_lat_tiles):
    n_col = D_IN // D_MODEL
    n_stage = stage_ref.shape[0]
    epilogue = {COL_GH: _silu, COL_GA: _silu, COL_MH: jax.nn.sigmoid, COL_MA: jax.nn.sigmoid,
                COL_Q: lambda acc: acc * (ATTN_SCALE * LOG2E)}

    def columns(col):
        return slice(col * D_MODEL, (col + 1) * D_MODEL)

    def fetch(n):
        return pltpu.make_async_copy(w_hbm.at[:, columns(PROJ_ORDER[n])], stage_ref.at[n % n_stage], sem.at[n % n_stage])

    def tile(stream_weights):
        if stream_weights:
            for n in range(min(n_stage, n_col)):
                fetch(n).start()
        shift = mod_ref[:, 0:D_MODEL]
        scale = mod_ref[:, D_MODEL:2 * D_MODEL]
        x = jnp.where(pl.program_id(0) < n_lat_tiles, xl_ref[...], xc_ref[...])
        h = (x * (1.0 + scale) + shift).astype(BF16)
        for src_ref, dst_ref in ((wbh_ref, wbh16_ref), (wba_ref, wba16_ref), (wout_ref, wout16_ref)):
            dst_ref[...] = src_ref[...].astype(BF16)
        for n, col in enumerate(PROJ_ORDER):
            if stream_weights:
                fetch(n).wait()
                w_ref[:, columns(col)] = stage_ref[n % n_stage].astype(BF16)
                if n + n_stage < n_col:
                    fetch(n + n_stage).start()
            acc = _dot(h, w_ref[:, columns(col)])
            z_ref[:, columns(col)] = epilogue.get(col, lambda acc: acc)(acc).astype(BF16)
            if col == COL_K:
                kt_ref[...] = acc.T
            if col == COL_V:
                vt_ref[...] = acc.T

    pl.when(pl.program_id(0) == 0)(functools.partial(tile, True))
    pl.when(pl.program_id(0) != 0)(functools.partial(tile, False))


def _in_projection(x_lat, x_ctx, mod3, lat_row, ctx_row, w_in, merge_weights, *, tm):
    n_lat, n_ctx = x_lat.shape[0] // tm, x_ctx.shape[0] // tm
    is_lat = lambda i: i < n_lat
    lat_tile = lambda i: jnp.minimum(i, n_lat - 1)
    ctx_tile = lambda i: jnp.maximum(i - n_lat, 0)
    kv_shape = jax.ShapeDtypeStruct((n_ctx, D_ATTN, tm), F32)
    kv_spec = pl.BlockSpec((None, D_ATTN, tm), lambda i: (ctx_tile(i), 0, 0))
    n_cast = D_MODEL // CAST_ROWS
    assert n_cast <= n_lat + n_ctx and all(w.shape == (D_MODEL, D_MODEL) for w in merge_weights)
    cast_spec = pl.BlockSpec((CAST_ROWS, D_MODEL), lambda i: (jnp.minimum(i, n_cast - 1), 0))
    stage = (W_STAGES, D_MODEL, D_MODEL)
    blocks = (_nbytes((D_MODEL, D_IN), BF16) + _nbytes(stage, F32) + 4 * _nbytes((tm, D_MODEL), F32)
              + 2 * _nbytes((tm, D_IN), BF16) + 4 * _nbytes((tm, D_ATTN), F32)
              + 3 * 2 * (_nbytes((CAST_ROWS, D_MODEL), F32) + _nbytes((CAST_ROWS, D_MODEL), BF16)))
    return pl.pallas_call(
        functools.partial(_inproj_kernel, n_lat_tiles=n_lat),
        out_shape=[jax.ShapeDtypeStruct(((n_lat + n_ctx) * tm, D_IN), BF16), kv_shape, kv_shape]
                  + [jax.ShapeDtypeStruct((D_MODEL, D_MODEL), BF16)] * 3,
        grid=(n_lat + n_ctx,),
        in_specs=[pl.BlockSpec((tm, D_MODEL), lambda i: (lat_tile(i), 0)),
                  pl.BlockSpec((tm, D_MODEL), lambda i: (ctx_tile(i), 0)),
                  pl.BlockSpec((None, 1, 3 * D_MODEL),
                               lambda i: (jnp.where(is_lat(i), lat_row(lat_tile(i)), ctx_row), 0, 0)),
                  pl.BlockSpec(memory_space=pl.ANY)] + [cast_spec] * 3,
        out_specs=[pl.BlockSpec((tm, D_IN), lambda i: (i, 0)), kv_spec, kv_spec] + [cast_spec] * 3,
        scratch_shapes=[pltpu.VMEM((D_MODEL, D_IN), BF16), pltpu.VMEM(stage, F32),
                        pltpu.SemaphoreType.DMA((W_STAGES,))],
        compiler_params=pltpu.CompilerParams(
            dimension_semantics=("arbitrary",),
            vmem_limit_bytes=_vmem_limit(blocks, 4 * _nbytes((tm, D_MODEL), F32))),
        name="inproj",
    )(x_lat, x_ctx, mod3, w_in, *merge_weights)


def _hyena_kernel(vh_ref, x1_ref, x0_ref, gh_ref, cwv_ref, cw1_ref, cw0_ref, cbv_ref, cb1_ref, cb0_ref,
                  f_ref, g_ref, z_ref, w1_ref, b1_ref, w2_ref, b2_ref, w3f_ref, w3b_ref, freq_ref, decay_ref, d_ref,
                  o_ref, hf_ref, fp_ref, fqa_ref, fpb_ref):
    L = fp_ref.shape[0]
    slab = jax.lax.broadcasted_iota(jnp.int32, (SUBLANES, 1), 0)

    @pl.when(pl.program_id(1) == 0)
    def _():
        _filter_spectra(z_ref, w1_ref, b1_ref, w2_ref, b2_ref, w3f_ref, w3b_ref, freq_ref, decay_ref, d_ref, f_ref,
                        hf_ref, fp_ref, fqa_ref, fpb_ref)

    def short_conv(x_ref, rows, w_ref, b_ref):
        x = x_ref[rows, :].astype(F32)
        prev = pltpu.roll(x, 1, 0)
        nxt = pltpu.roll(x, L - 1, 0)
        prev = jnp.concatenate([jnp.where(slab == 0, 0.0, prev[:SUBLANES]), prev[SUBLANES:]], axis=0)
        nxt = jnp.concatenate([nxt[:-SUBLANES], jnp.where(slab == SUBLANES - 1, 0.0, nxt[-SUBLANES:])], axis=0)
        return prev * w_ref[0:1, :] + x * w_ref[1:2, :] + nxt * w_ref[2:3, :] + b_ref[...]

    for s in range(vh_ref.shape[0] // L):
        rows = slice(s * L, (s + 1) * L)
        u = short_conv(vh_ref, rows, cwv_ref, cbv_ref) * short_conv(x1_ref, rows, cw1_ref, cb1_ref)
        t = _dot(f_ref[...], u.astype(BF16))
        p = t[:L]
        q = t[L:]
        yp = p * fp_ref[...] - q * fqa_ref[...]
        yq = p * fqa_ref[...] + q * fpb_ref[...]
        y = _dot(g_ref[...], jnp.concatenate([yp.astype(BF16), yq.astype(BF16)], axis=0))
        y_h = y * short_conv(x0_ref, rows, cw0_ref, cb0_ref)
        o_ref[rows, :] = (y_h * gh_ref[rows, :].astype(F32)).astype(BF16)


def _hyena_branch(z, z_row0, B, L, conv_w, conv_b, hyena_d, w1, b1, w2, b2, w3, freq, *, tc=256, seqs_per_step):
    nblk = D_HYENA // tc
    tm = seqs_per_step * L
    assert z_row0 % tm == 0
    f_mat, g_mat = _dft_matrices(L)
    z_emb, decay = _filter_constants(L)
    w1p = jnp.pad(w1, ((0, LANES - FILTER_EMB), (0, 0)))
    zcol = lambda piece: pl.BlockSpec((tm, tc), lambda c, b: (z_row0 // tm + b, piece * nblk + c))
    wcol = lambda rows, piece: pl.BlockSpec((rows, tc), lambda c, b: (0, piece * nblk + c))
    taps = lambda piece: pl.BlockSpec((None, 3, tc), lambda c, b: (0, 0, piece * nblk + c))
    chan = lambda rows: pl.BlockSpec((rows, tc), lambda c, b: (0, c))
    const = lambda shape: pl.BlockSpec(shape, lambda c, b: (0, 0), pipeline_mode=pl.Buffered(1))
    blocks = (2 * _nbytes((2 * L, L), BF16) + 2 * 5 * _nbytes((tm, tc), BF16) + 5 * _nbytes((L, tc), F32)
              + _nbytes((L, LANES), F32) + _nbytes((L, FILTER_HIDDEN), F32))
    return pl.pallas_call(
        _hyena_kernel,
        out_shape=jax.ShapeDtypeStruct((B * L, D_HYENA), BF16),
        grid=(nblk, B // seqs_per_step),
        in_specs=[zcol(COL_VH), zcol(COL_X1), zcol(COL_X0), zcol(COL_GH),
                  taps(0), taps(1), taps(2), wcol(1, 0), wcol(1, 1), wcol(1, 2),
                  const((2 * L, L)), const((L, 2 * L)),
                  const((L, LANES)),
                  const((LANES, FILTER_HIDDEN)), const((1, FILTER_HIDDEN)),
                  const((FILTER_HIDDEN, FILTER_HIDDEN)), const((1, FILTER_HIDDEN)),
                  wcol(FILTER_HIDDEN, 0), wcol(FILTER_HIDDEN, 1),
                  const((2, FILTER_HIDDEN)), chan(L), chan(1)],
        out_specs=pl.BlockSpec((tm, tc), lambda c, b: (b, c)),
        scratch_shapes=[pltpu.VMEM((L, FILTER_HIDDEN), F32)] + [pltpu.VMEM((L, tc), F32)] * 3,
        compiler_params=pltpu.CompilerParams(
            dimension_semantics=("arbitrary", "arbitrary"),
            vmem_limit_bytes=_vmem_limit(blocks, 16 * _nbytes((tm, tc), F32))),
        name=f"hyena_{L}",
    )(z, z, z, z, conv_w, conv_w, conv_w, conv_b.reshape(1, -1), conv_b.reshape(1, -1), conv_b.reshape(1, -1),
      f_mat, g_mat, z_emb, w1p, b1.reshape(1, -1), w2, b2.reshape(1, -1), w3, w3, freq, decay, hyena_d.reshape(1, -1))


def _head_masks(width):
    lane = jax.lax.broadcasted_iota(jnp.int32, (1, width), 1)
    return [(lane >= h * HEAD_DIM) & (lane < (h + 1) * HEAD_DIM) for h in range(width // HEAD_DIM)]


def _qk(q, k):
    return jax.lax.dot_general(q, k, (((1,), (1,)), ((), ())), preferred_element_type=F32)


def _stack_heads(q, masks):
    return jnp.concatenate([jnp.where(msk, q, jnp.zeros_like(q)) for msk in masks], axis=0)


def _unstack_heads(o, masks):
    n = o.shape[0] // len(masks)
    out = o[:n]
    for h in range(1, len(masks)):
        out = jnp.where(masks[h], o[h * n:(h + 1) * n], out)
    return out


def _ctx_attn_kernel(q_ref, k_ref, v_ref, g_ref, o_ref):
    masks = _head_masks(CTX_HEAD_TILE)
    for t in range(D_ATTN // CTX_HEAD_TILE):
        cols = slice(t * CTX_HEAD_TILE, (t + 1) * CTX_HEAD_TILE)
        s = _qk(_stack_heads(q_ref[:, cols], masks), k_ref[:, cols])
        m = jnp.max(s, axis=-1, keepdims=True)
        p = jnp.exp2(s - m)
        l = jnp.sum(p, axis=-1, keepdims=True)
        out = _unstack_heads(_dot(p.astype(BF16), v_ref[:, cols]) / l, masks)
        o_ref[:, cols] = (out * g_ref[:, cols].astype(F32)).astype(BF16)


def _context_attention(z, z_row0, B, L):
    assert z_row0 % L == 0
    zcol = lambda piece: pl.BlockSpec((L, D_ATTN), lambda b: (z_row0 // L + b, piece))
    return pl.pallas_call(
        _ctx_attn_kernel,
        out_shape=jax.ShapeDtypeStruct((B * L, D_ATTN), BF16),
        grid=(B,),
        in_specs=[zcol(COL_Q), zcol(COL_K), zcol(COL_V), zcol(COL_GA)],
        out_specs=pl.BlockSpec((L, D_ATTN), lambda b: (b, 0)),
        compiler_params=pltpu.CompilerParams(dimension_semantics=("arbitrary",)),
        name="ctx_attn",
    )(z, z, z, z)


def _key_window_start(qb):
    rows = 1024 // GRID_W
    r_first = qb * Q_BLOCK_ROWS
    r_last = r_first + Q_BLOCK_ROWS - 1
    lo = min(max(r_first - WIN_ROWS // 2, 0), rows - WIN_ROWS)
    hi = min(max(r_last - WIN_ROWS // 2, 0), rows - WIN_ROWS) + WIN_ROWS
    start = min(lo, rows - KEY_WIN_ROWS)
    start -= start % Q_BLOCK_ROWS
    assert start <= lo and hi <= start + KEY_WIN_ROWS
    return start


def _nbr_attn_kernel(q_ref, k0_ref, k1_ref, k2_ref, v0_ref, v1_ref, v2_ref, kc_ref, vc_ref, g_ref, tiles_ref,
                     o_ref, bias_ref, *, n_rows):
    masks = _head_masks(NBR_HEAD_TILE)
    n_heads = q_ref.shape[-1] // HEAD_DIM
    chain_tiles = [slice(c * NBR_HEAD_TILE, (c + 1) * NBR_HEAD_TILE) for c in range(q_ref.shape[-1] // NBR_HEAD_TILE)]
    k_refs = (k0_ref, k1_ref, k2_ref)
    v_refs = (v0_ref, v1_ref, v2_ref)
    key_blocks = KEY_WIN // Q_BLOCK

    def window_row_start(r):
        return min(max(r - WIN_ROWS // 2, 0), n_rows - WIN_ROWS)

    def build_bias(qb):
        k_start = _key_window_start(qb)
        used = set()
        for i in range(Q_BLOCK_ROWS):
            r = qb * Q_BLOCK_ROWS + i
            r0 = window_row_start(r)
            for j in range(KEY_WIN_ROWS):
                kr = k_start + j
                cols = slice(j * GRID_W, (j + 1) * GRID_W)
                lanes = slice((j % RPB_COPIES) * GRID_W, (j % RPB_COPIES + 1) * GRID_W)
                inside = r0 <= kr < r0 + WIN_ROWS
                if inside:
                    used.add(j // Q_BLOCK_ROWS)
                for h in range(n_heads):
                    rows = slice(h * Q_BLOCK + i * GRID_W, h * Q_BLOCK + (i + 1) * GRID_W)
                    if inside:
                        bias_ref[rows, cols] = tiles_ref[h, kr - r + WIN_ROWS - 1, :, lanes]
                    else:
                        bias_ref[rows, cols] = jnp.full((GRID_W, GRID_W), NEG_INF, F32)
        return sorted(used)

    def attend(blocks):
        for c, lanes in enumerate(chain_tiles):
            stacked = slice(c * len(masks) * Q_BLOCK, (c + 1) * len(masks) * Q_BLOCK)
            for b in range(q_ref.shape[0]):
                qs = _stack_heads(q_ref[b, :, lanes], masks)
                s = [_qk(qs, k_refs[j][b, :, lanes]) + bias_ref[stacked, j * Q_BLOCK:(j + 1) * Q_BLOCK] for j in blocks]
                s.append(_dot(qs, kc_ref[b, lanes, :].astype(BF16)))
                m = jnp.max(functools.reduce(jnp.maximum, s), axis=-1, keepdims=True)
                p = [jnp.exp2(x - m) for x in s]
                l = jnp.sum(functools.reduce(jnp.add, p), axis=-1, keepdims=True)
                p = [x.astype(BF16) for x in p]
                v_win = jnp.concatenate([v_refs[j][b, :, lanes] for j in blocks], axis=0)
                o = _dot(jnp.concatenate(p[:-1], axis=1), v_win) + _qk(p[-1], vc_ref[b, lanes, :].astype(BF16))
                o_ref[b, :, lanes] = (_unstack_heads(o / l, masks) * g_ref[b, :, lanes].astype(F32)).astype(BF16)

    for qb in range(n_rows // Q_BLOCK_ROWS):
        @pl.when(pl.program_id(0) == qb)
        def _(qb=qb):
            blocks = build_bias(qb)
            assert blocks and all(0 <= j < key_blocks for j in blocks)
            attend(blocks)


def _neighbourhood_attention(z, z_row0, B, L, cache_k, cache_v, tiles):
    assert z_row0 % (B * L) == 0
    group = z_row0 // (B * L)
    n_rows = L // GRID_W
    n_qb = L // Q_BLOCK
    assert KEY_WIN == 3 * Q_BLOCK and all(_key_window_start(qb) % Q_BLOCK_ROWS == 0 for qb in range(n_qb))
    z3 = z.reshape(z.shape[0] // L, L, D_IN)
    n_ctx = cache_k.shape[1]
    kc = jnp.transpose(cache_k, (0, 2, 3, 1)).reshape(B, D_ATTN, n_ctx)
    vc = jnp.transpose(cache_v, (0, 2, 3, 1)).reshape(B, D_ATTN, n_ctx)
    width = NBR_BLOCK_LANES
    tiles_per_col = D_ATTN // width

    def win_block(qb):
        return jnp.where(qb < n_qb // 2, 0, (n_rows - KEY_WIN_ROWS) // Q_BLOCK_ROWS)

    assert [_key_window_start(qb) // Q_BLOCK_ROWS for qb in range(n_qb)] == \
        [0 if qb < n_qb // 2 else (n_rows - KEY_WIN_ROWS) // Q_BLOCK_ROWS for qb in range(n_qb)]
    qspec = lambda piece: pl.BlockSpec((B, Q_BLOCK, width), lambda qb, t: (group, qb, piece * tiles_per_col + t))
    kspec = lambda piece, j: pl.BlockSpec((B, Q_BLOCK, width),
                                          lambda qb, t: (group, win_block(qb) + j, piece * tiles_per_col + t))
    cspec = pl.BlockSpec((B, width, n_ctx), lambda qb, t: (0, t, 0))
    out = pl.pallas_call(
        functools.partial(_nbr_attn_kernel, n_rows=n_rows),
        out_shape=jax.ShapeDtypeStruct((B, L, D_ATTN), BF16),
        grid=(n_qb, tiles_per_col),
        in_specs=[qspec(COL_Q), kspec(COL_K, 0), kspec(COL_K, 1), kspec(COL_K, 2),
                  kspec(COL_V, 0), kspec(COL_V, 1), kspec(COL_V, 2), cspec, cspec, qspec(COL_GA),
                  pl.BlockSpec((width // HEAD_DIM, 2 * WIN_ROWS - 1, GRID_W, LANES), lambda qb, t: (t, 0, 0, 0))],
        out_specs=pl.BlockSpec((B, Q_BLOCK, width), lambda qb, t: (0, qb, t)),
        scratch_shapes=[pltpu.VMEM((width // HEAD_DIM * Q_BLOCK, KEY_WIN), F32)],
        compiler_params=pltpu.CompilerParams(dimension_semantics=("arbitrary", "arbitrary")),
        name="nbr_attn",
    )(z3, z3, z3, z3, z3, z3, z3, kc, vc, z3, tiles)
    return out.reshape(B * L, D_ATTN)


def _merge_kernel(yh_ref, ya_ref, mh_ref, ma_ref, x_ref, mod_ref, wbh_ref, wba_ref, wout_ref, lng_ref, lnb_ref,
                  o_ref, *, alpha):
    p_h = _dot(yh_ref[...], wbh_ref[...])
    p_a = _dot(ya_ref[...], wba_ref[...])
    m_h = mh_ref[...].astype(F32)
    m_a = ma_ref[...].astype(F32)
    out = _dot((m_h * p_h + m_a * p_a).astype(BF16), wout_ref[...])
    gate = mod_ref[:, 2 * D_MODEL:3 * D_MODEL]
    v = alpha * x_ref[...] + gate * out
    mu = jnp.mean(v, axis=-1, keepdims=True)
    d = v - mu
    var = jnp.mean(d * d, axis=-1, keepdims=True)
    o_ref[...] = d * jax.lax.rsqrt(var + LN_EPS) * lng_ref[...] + lnb_ref[...]


def _merge(yh, ya, z, z_row0, x2d, mod3, mod_index, w_bh, w_ba, w_out, ln_g, ln_b, *, alpha, tm):
    M = x2d.shape[0]
    assert z_row0 % tm == 0
    tok = lambda width, col: pl.BlockSpec((tm, width), lambda i: (i, col))
    ztok = lambda col: pl.BlockSpec((tm, D_MODEL), lambda i: (z_row0 // tm + i, col))
    const = lambda shape: pl.BlockSpec(shape, lambda i: (0, 0))
    return pl.pallas_call(
        functools.partial(_merge_kernel, alpha=alpha),
        out_shape=jax.ShapeDtypeStruct((M, D_MODEL), F32),
        grid=(M // tm,),
        in_specs=[tok(D_HYENA, 0), tok(D_ATTN, 0), ztok(COL_MH), ztok(COL_MA), tok(D_MODEL, 0),
                  pl.BlockSpec((None, 1, 3 * D_MODEL), lambda i: (mod_index(i), 0, 0)),
                  const((D_HYENA, D_MODEL)), const((D_ATTN, D_MODEL)), const((D_MODEL, D_MODEL)),
                  const((1, D_MODEL)), const((1, D_MODEL))],
        out_specs=pl.BlockSpec((tm, D_MODEL), lambda i: (i, 0)),
        compiler_params=pltpu.CompilerParams(
            dimension_semantics=("arbitrary",),
            vmem_limit_bytes=_vmem_limit(
                2 * (4 * _nbytes((tm, D_MODEL), BF16) + 2 * _nbytes((tm, D_MODEL), F32)
                     + 3 * _nbytes((D_MODEL, D_MODEL), BF16)),
                8 * _nbytes((tm, D_MODEL), F32))),
        name="merge",
    )(yh, ya, z, z, x2d, mod3, w_bh, w_ba, w_out, ln_g.reshape(1, -1), ln_b.reshape(1, -1))


def _mixers(x, z, z_row0, mod3, cond_row, ctx_kv, p, filt, tiles, *, alpha):
    B, L, _ = x.shape
    yh = _hyena_branch(z, z_row0, B, L, p["conv_w"], p["conv_b"], p["hyena_d"], *filt,
                       seqs_per_step=max(1, HYENA_ROWS_PER_STEP // L))
    if ctx_kv is None:
        ya = _context_attention(z, z_row0, B, L)
    else:
        ya = _neighbourhood_attention(z, z_row0, B, L, ctx_kv[0], ctx_kv[1], tiles)
    y = _merge(yh, ya, z, z_row0, x.reshape(B * L, D_MODEL), mod3, lambda i: cond_row(i * MERGE_TM // L),
               p["w_bh"], p["w_ba"], p["w_out"], p["ln_g"], p["ln_b"], alpha=alpha, tm=MERGE_TM)
    return y.reshape(B, L, D_MODEL)


def kernel(x_prompt, x_sample, c, cache_k, cache_v, c_ctx, w_ada, b_ada, w_in, conv_w, conv_b, filt_w1, filt_b1,
           filt_w2, filt_b2, filt_w3, filt_freq, hyena_d, rpb, w_bh, w_ba, w_out, ln_g, ln_b):
    depth = w_in.shape[0]
    alpha = (2.0 * depth) ** 0.25
    n_lat, lat_len = x_sample.shape[0], x_sample.shape[1]
    n_ctx, ctx_len = x_prompt.shape[0], x_prompt.shape[1]
    assert ctx_len == IN_TM
    ctx_row = n_lat

    xp, xs = x_prompt, x_sample
    new_k, new_v = [], []
    for l in range(depth):
        mod3 = _modulation(c, c_ctx.reshape(1, -1), w_ada[l], b_ada[l])
        filt = (filt_w1[l], filt_b1[l], filt_w2[l], filt_b2[l], filt_w3[l], filt_freq[l])
        tiles = _rpb_tiles(rpb[l])
        z, k_ctx, v_ctx, w_bh16, w_ba16, w_out16 = _in_projection(
            xs.reshape(-1, D_MODEL), xp.reshape(-1, D_MODEL), mod3, lambda tile: tile * IN_TM // lat_len, ctx_row,
            w_in[l], (w_bh[l], w_ba[l], w_out[l]), tm=IN_TM)
        p = {"conv_w": conv_w[l:l + 1], "conv_b": conv_b[l], "hyena_d": hyena_d[l],
             "w_bh": w_bh16, "w_ba": w_ba16, "w_out": w_out16, "ln_g": ln_g[l], "ln_b": ln_b[l]}
        new_k.append(jnp.transpose(k_ctx.reshape(n_ctx, N_HEADS, HEAD_DIM, ctx_len), (0, 3, 1, 2)))
        new_v.append(jnp.transpose(v_ctx.reshape(n_ctx, N_HEADS, HEAD_DIM, ctx_len), (0, 3, 1, 2)))
        xp = _mixers(xp, z, n_lat * lat_len, mod3, lambda b: ctx_row, None, p, filt, None, alpha=alpha)
        xs = _mixers(xs, z, 0, mod3, lambda b: b, (cache_k[:, l], cache_v[:, l]), p, filt, tiles, alpha=alpha)
    return xp, xs, jnp.stack(new_k, axis=1), jnp.stack(new_v, axis=1)
```

```python
import functools
import math

import jax
import jax.numpy as jnp
import numpy as np
from jax.experimental import pallas as pl
from jax.experimental.pallas import tpu as pltpu

F32 = jnp.float32
BF16 = jnp.bfloat16

D_MODEL = 1024
D_HYENA = 1024
N_HEADS = 16
HEAD_DIM = 64
D_ATTN = N_HEADS * HEAD_DIM
D_IN = 4 * D_HYENA + 4 * D_ATTN + 2 * D_MODEL
GRID_W = 64
WIN_ROWS = 8
WIN_COLS = 16
FILTER_EMB = 33
FILTER_BANDS = (FILTER_EMB - 1) // 2
FILTER_HIDDEN = 64
DECAY_TARGET = 1e-2
MIN_DECAY = math.log(DECAY_TARGET) / 1.5
MAX_DECAY = math.log(DECAY_TARGET) / 0.3
DECAY_SHIFT = 0.05
LN_EPS = 1e-5
NEG_INF = -1e30

COL_VH, COL_X1, COL_X0, COL_GH, COL_Q, COL_K, COL_V, COL_GA, COL_MH, COL_MA = range(10)
PROJ_ORDER = (COL_GH, COL_GA, COL_MH, COL_MA, COL_K, COL_V, COL_Q, COL_VH, COL_X1, COL_X0)

LANES = 128
SUBLANES = 8
CTX_HEAD_TILE = 128
NBR_HEAD_TILE = 128
NBR_BLOCK_LANES = 128
RPB_COPIES = LANES // GRID_W
V7X_VMEM_BYTES = 64 * 1024 * 1024
ATTN_SCALE = HEAD_DIM ** -0.5
LOG2E = math.log2(math.e)

Q_BLOCK = 256
Q_BLOCK_ROWS = Q_BLOCK // GRID_W
KEY_WIN_ROWS = 12
KEY_WIN = KEY_WIN_ROWS * GRID_W

IN_TM = 256
W_STAGES = 2
CAST_ROWS = 64
MERGE_TM = 512
MERGE_CHUNK = 256
HYENA_ROWS_PER_STEP = 2048


def _vmem_limit(block_bytes, temp_bytes):
    need = int(block_bytes + temp_bytes)
    return min(max(need, 16 * 1024 * 1024), V7X_VMEM_BYTES - 8 * 1024 * 1024)


def _nbytes(shape, dtype):
    return int(np.prod(shape)) * jnp.dtype(dtype).itemsize


def _silu(x):
    return x * jax.nn.sigmoid(x)


def _split_bf16(a):
    hi = a.astype(BF16)
    lo = (a - hi.astype(F32)).astype(BF16)
    return hi, lo


def _dot(a, b):
    return jnp.dot(a, b, preferred_element_type=F32)


def _dot_split(a_hi, a_lo, b):
    b_hi, b_lo = _split_bf16(b)
    return _dot(a_hi, b_hi) + (_dot(a_hi, b_lo) + _dot(a_lo, b_hi))


@functools.lru_cache(maxsize=None)
def _dft_constants(L):
    n = 2 * L
    k = np.arange(L, dtype=np.float64)[:, None]
    s = np.arange(L, dtype=np.float64)[None, :]
    ang = 2.0 * np.pi * k * s / n
    nyq = np.cos(np.pi * np.arange(L, dtype=np.float64))
    C = np.cos(ang)
    S = np.sin(ang)
    S[0, :] = nyq
    F = np.concatenate([C, S], axis=0)
    w = np.full((L,), 2.0)
    w[0] = 1.0
    Gc = (C * w[:, None]).T / n
    Ss = 2.0 * np.sin(ang)
    Ss[0, :] = nyq
    Gs = Ss.T / n
    G = np.concatenate([Gc, Gs], axis=1)
    return F.astype(np.float32), G.astype(np.float32)


def _dft_matrices(L):
    F, G = _dft_constants(L)
    return jnp.asarray(F).astype(BF16), jnp.asarray(G).astype(BF16)


@functools.lru_cache(maxsize=None)
def _filter_constants(L):
    t = np.linspace(0.0, 1.0, L, dtype=np.float32)[:, None]
    bands = np.linspace(1e-4, FILTER_BANDS - 1, FILTER_BANDS, dtype=np.float32)[None]
    w = (2.0 * math.pi * np.arange(L, dtype=np.float32)[:, None] / L).astype(np.float32)
    z = np.concatenate([t, np.cos(bands * w), -np.sin(bands * w)], axis=-1).astype(np.float32)
    z_pad = np.zeros((L, LANES), np.float32)
    z_pad[:, :FILTER_EMB] = z
    deltas = np.linspace(MIN_DECAY, MAX_DECAY, D_HYENA, dtype=np.float32)
    decay = (np.exp(-t * np.abs(deltas)) + np.float32(DECAY_SHIFT)).astype(np.float32)
    return z_pad, decay


@functools.lru_cache(maxsize=None)
def _rpb_window_mask():
    c = np.arange(GRID_W)[:, None]
    kc = np.arange(GRID_W)[None, :]
    c0 = np.clip(c - WIN_COLS // 2, 0, GRID_W - WIN_COLS)
    in_win = (kc >= c0) & (kc < c0 + WIN_COLS)
    return np.tile(in_win, (1, RPB_COPIES)).astype(np.float32)


def _mod_kernel(c_ref, cctx_ref, w_ref, b_ref, o_ref, cond_ref):
    n_lat = c_ref.shape[0]
    cond_ref[...] = jnp.zeros(cond_ref.shape, F32)
    cond_ref[0:n_lat, :] = c_ref[...]
    cond_ref[n_lat:n_lat + 1, :] = cctx_ref[...]
    s = _silu(cond_ref[...]).astype(BF16)
    o_ref[:, 0, :] = _dot(s, w_ref[...].astype(BF16)) + b_ref[...]


def _modulation(c, c_ctx, w_ada, b_ada):
    n_lat = c.shape[0]
    assert n_lat + 1 <= SUBLANES
    return pl.pallas_call(
        _mod_kernel,
        out_shape=jax.ShapeDtypeStruct((SUBLANES, 1, 3 * D_MODEL), F32),
        grid=(3,),
        in_specs=[pl.BlockSpec((n_lat, D_MODEL), lambda j: (0, 0)),
                  pl.BlockSpec((1, D_MODEL), lambda j: (0, 0)),
                  pl.BlockSpec((D_MODEL, D_MODEL), lambda j: (0, j)),
                  pl.BlockSpec((1, D_MODEL), lambda j: (0, j))],
        out_specs=pl.BlockSpec((SUBLANES, 1, D_MODEL), lambda j: (0, 0, j)),
        scratch_shapes=[pltpu.VMEM((SUBLANES, D_MODEL), F32)],
        name="mod",
    )(c, c_ctx, w_ada, b_ada.reshape(1, -1))


def _filter_spectra(z_ref, w1_ref, b1_ref, w2_ref, b2_ref, w3f_ref, w3b_ref, freq_ref, decay_ref, d_ref, f_ref,
                    hf_ref, fp_ref, fqa_ref, fpb_ref):
    L = decay_ref.shape[0]

    @pl.when(pl.program_id(0) == 0)
    def _():
        z_hi, z_lo = _split_bf16(z_ref[...])
        hf = jnp.sin(freq_ref[0:1, :] * (_dot_split(z_hi, z_lo, w1_ref[...]) + b1_ref[...]))
        h_hi, h_lo = _split_bf16(hf)
        hf_ref[...] = jnp.sin(freq_ref[1:2, :] * (_dot_split(h_hi, h_lo, w2_ref[...]) + b2_ref[...]))

    h_hi, h_lo = _split_bf16(hf_ref[...])
    decay = decay_ref[...]
    h_fwd = _dot_split(h_hi, h_lo, w3f_ref[...]) * decay
    h_bwd = _dot_split(h_hi, h_lo, w3b_ref[...]) * decay
    row0 = jax.lax.broadcasted_iota(jnp.int32, (L, 1), 0) == 0
    h_bwd = jnp.where(row0, 0.0, h_bwd)
    fsum = (h_fwd + h_bwd).astype(BF16)
    fdif = (h_fwd - h_bwd).astype(BF16)
    skip = d_ref[...]
    fp = _dot(f_ref[0:L, :], fsum) + skip
    fq = _dot(f_ref[L:2 * L, :], fdif)
    nyq = _dot(f_ref[L:L + 2 * SUBLANES, :], fsum)[0:1, :] + skip
    fp_ref[...] = fp
    fqa_ref[...] = jnp.where(row0, 0.0, fq)
    fpb_ref[...] = jnp.where(row0, nyq, fp)


def _rpb_kernel(r_ref, mask_ref, o_ref):
    inside = mask_ref[...] > 0.0
    for i in range(o_ref.shape[0]):
        r = jnp.broadcast_to(r_ref[i:i + 1, :], (GRID_W, LANES))
        t = pltpu.roll(r, LANES - (WIN_COLS - 1), 1, stride=1, stride_axis=0)
        o_ref[i] = jnp.where(inside, t * LOG2E, NEG_INF)


def _rpb_tiles(rpb):
    n_dr = 2 * WIN_ROWS - 1
    n_off = 2 * WIN_COLS - 1
    heads_per_step = 4
    rows = heads_per_step * n_dr
    r = jnp.tile(jnp.pad(rpb.reshape(N_HEADS * n_dr, n_off), ((0, 0), (0, GRID_W - n_off))), (1, RPB_COPIES))
    out = pl.pallas_call(
        _rpb_kernel,
        out_shape=jax.ShapeDtypeStruct((N_HEADS * n_dr, GRID_W, LANES), F32),
        grid=(N_HEADS // heads_per_step,),
        in_specs=[pl.BlockSpec((None, rows, LANES), lambda j: (j, 0, 0)),
                  pl.BlockSpec((GRID_W, LANES), lambda j: (0, 0))],
        out_specs=pl.BlockSpec((rows, GRID_W, LANES), lambda j: (j, 0, 0)),
        name="rpb_tiles",
    )(r.reshape(N_HEADS // heads_per_step, rows, LANES), jnp.asarray(_rpb_window_mask()))
    return out.reshape(N_HEADS, n_dr, GRID_W, LANES)


def _inproj_kernel(xl_ref, xc_ref, mod_ref, w_hbm, wbh_ref, wba_ref, wout_ref,
                   z_ref, kt_ref, vt_ref, wbh16_ref, wba16_ref, wout16_ref, w_ref, stage_ref, sem, *, n_lat_tiles):
    n_col = D_IN // D_MODEL
    n_stage = stage_ref.shape[0]
    epilogue = {COL_GH: _silu, COL_GA: _silu, COL_MH: jax.nn.sigmoid, COL_MA: jax.nn.sigmoid,
                COL_Q: lambda acc: acc * (ATTN_SCALE * LOG2E)}

    def columns(col):
        return slice(col * D_MODEL, (col + 1) * D_MODEL)

    def fetch(n):
        return pltpu.make_async_copy(w_hbm.at[:, columns(PROJ_ORDER[n])], stage_ref.at[n % n_stage], sem.at[n % n_stage])

    def tile(stream_weights):
        if stream_weights:
            for n in range(min(n_stage, n_col)):
                fetch(n).start()
        shift = mod_ref[:, 0:D_MODEL]
        scale = mod_ref[:, D_MODEL:2 * D_MODEL]
        x = jnp.where(pl.program_id(0) < n_lat_tiles, xl_ref[...], xc_ref[...])
        h = (x * (1.0 + scale) + shift).astype(BF16)
        for src_ref, dst_ref in ((wbh_ref, wbh16_ref), (wba_ref, wba16_ref), (wout_ref, wout16_ref)):
            dst_ref[...] = src_ref[...].astype(BF16)
        for n, col in enumerate(PROJ_ORDER):
            if stream_weights:
                fetch(n).wait()
                w_ref[:, columns(col)] = stage_ref[n % n_stage].astype(BF16)
                if n + n_stage < n_col:
                    fetch(n + n_stage).start()
            acc = _dot(h, w_ref[:, columns(col)])
            z_ref[:, columns(col)] = epilogue.get(col, lambda acc: acc)(acc).astype(BF16)
            if col == COL_K:
                kt_ref[...] = acc.T
            if col == COL_V:
                vt_ref[...] = acc.T

    pl.when(pl.program_id(0) == 0)(functools.partial(tile, True))
    pl.when(pl.program_id(0) != 0)(functools.partial(tile, False))


def _in_projection(x_lat, x_ctx, mod3, lat_row, ctx_row, w_in, merge_weights, *, tm):
    n_lat, n_ctx = x_lat.shape[0] // tm, x_ctx.shape[0] // tm
    is_lat = lambda i: i < n_lat
    lat_tile = lambda i: jnp.minimum(i, n_lat - 1)
    ctx_tile = lambda i: jnp.maximum(i - n_lat, 0)
    kv_shape = jax.ShapeDtypeStruct((n_ctx, D_ATTN, tm), F32)
    kv_spec = pl.BlockSpec((None, D_ATTN, tm), lambda i: (ctx_tile(i), 0, 0))
    n_cast = D_MODEL // CAST_ROWS
    assert n_cast <= n_lat + n_ctx and all(w.shape == (D_MODEL, D_MODEL) for w in merge_weights)
    cast_spec = pl.BlockSpec((CAST_ROWS, D_MODEL), lambda i: (jnp.minimum(i, n_cast - 1), 0))
    stage = (W_STAGES, D_MODEL, D_MODEL)
    blocks = (_nbytes((D_MODEL, D_IN), BF16) + _nbytes(stage, F32) + 4 * _nbytes((tm, D_MODEL), F32)
              + 2 * _nbytes((tm, D_IN), BF16) + 4 * _nbytes((tm, D_ATTN), F32)
              + 3 * 2 * (_nbytes((CAST_ROWS, D_MODEL), F32) + _nbytes((CAST_ROWS, D_MODEL), BF16)))
    return pl.pallas_call(
        functools.partial(_inproj_kernel, n_lat_tiles=n_lat),
        out_shape=[jax.ShapeDtypeStruct(((n_lat + n_ctx) * tm, D_IN), BF16), kv_shape, kv_shape]
                  + [jax.ShapeDtypeStruct((D_MODEL, D_MODEL), BF16)] * 3,
        grid=(n_lat + n_ctx,),
        in_specs=[pl.BlockSpec((tm, D_MODEL), lambda i: (lat_tile(i), 0)),
                  pl.BlockSpec((tm, D_MODEL), lambda i: (ctx_tile(i), 0)),
                  pl.BlockSpec((None, 1, 3 * D_MODEL),
                               lambda i: (jnp.where(is_lat(i), lat_row(lat_tile(i)), ctx_row), 0, 0)),
                  pl.BlockSpec(memory_space=pl.ANY)] + [cast_spec] * 3,
        out_specs=[pl.BlockSpec((tm, D_IN), lambda i: (i, 0)), kv_spec, kv_spec] + [cast_spec] * 3,
        scratch_shapes=[pltpu.VMEM((D_MODEL, D_IN), BF16), pltpu.VMEM(stage, F32),
                        pltpu.SemaphoreType.DMA((W_STAGES,))],
        compiler_params=pltpu.CompilerParams(
            dimension_semantics=("arbitrary",),
            vmem_limit_bytes=_vmem_limit(blocks, 4 * _nbytes((tm, D_MODEL), F32))),
        name="inproj",
    )(x_lat, x_ctx, mod3, w_in, *merge_weights)


def _hyena_kernel(vh_ref, x1_ref, x0_ref, gh_ref, cwv_ref, cw1_ref, cw0_ref, cbv_ref, cb1_ref, cb0_ref,
                  f_ref, g_ref, z_ref, w1_ref, b1_ref, w2_ref, b2_ref, w3f_ref, w3b_ref, freq_ref, decay_ref, d_ref,
                  o_ref, hf_ref, fp_ref, fqa_ref, fpb_ref):
    L = fp_ref.shape[0]
    slab = jax.lax.broadcasted_iota(jnp.int32, (SUBLANES, 1), 0)

    @pl.when(pl.program_id(1) == 0)
    def _():
        _filter_spectra(z_ref, w1_ref, b1_ref, w2_ref, b2_ref, w3f_ref, w3b_ref, freq_ref, decay_ref, d_ref, f_ref,
                        hf_ref, fp_ref, fqa_ref, fpb_ref)

    def short_conv(x_ref, rows, w_ref, b_ref):
        x = x_ref[rows, :].astype(F32)
        prev = pltpu.roll(x, 1, 0)
        nxt = pltpu.roll(x, L - 1, 0)
        prev = jnp.concatenate([jnp.where(slab == 0, 0.0, prev[:SUBLANES]), prev[SUBLANES:]], axis=0)
        nxt = jnp.concatenate([nxt[:-SUBLANES], jnp.where(slab == SUBLANES - 1, 0.0, nxt[-SUBLANES:])], axis=0)
        return prev * w_ref[0:1, :] + x * w_ref[1:2, :] + nxt * w_ref[2:3, :] + b_ref[...]

    for s in range(vh_ref.shape[0] // L):
        rows = slice(s * L, (s + 1) * L)
        u = short_conv(vh_ref, rows, cwv_ref, cbv_ref) * short_conv(x1_ref, rows, cw1_ref, cb1_ref)
        t = _dot(f_ref[...], u.astype(BF16))
        p = t[:L]
        q = t[L:]
        yp = p * fp_ref[...] - q * fqa_ref[...]
        yq = p * fqa_ref[...] + q * fpb_ref[...]
        y = _dot(g_ref[...], jnp.concatenate([yp.astype(BF16), yq.astype(BF16)], axis=0))
        y_h = y * short_conv(x0_ref, rows, cw0_ref, cb0_ref)
        o_ref[rows, :] = (y_h * gh_ref[rows, :].astype(F32)).astype(BF16)


def _hyena_branch(z, z_row0, B, L, conv_w, conv_b, hyena_d, w1, b1, w2, b2, w3, freq, *, tc=256, seqs_per_step):
    nblk = D_HYENA // tc
    tm = seqs_per_step * L
    assert z_row0 % tm == 0
    f_mat, g_mat = _dft_matrices(L)
    z_emb, decay = _filter_constants(L)
    w1p = jnp.pad(w1, ((0, LANES - FILTER_EMB), (0, 0)))
    zcol = lambda piece: pl.BlockSpec((tm, tc), lambda c, b: (z_row0 // tm + b, piece * nblk + c))
    wcol = lambda rows, piece: pl.BlockSpec((rows, tc), lambda c, b: (0, piece * nblk + c))
    taps = lambda piece: pl.BlockSpec((None, 3, tc), lambda c, b: (0, 0, piece * nblk + c))
    chan = lambda rows: pl.BlockSpec((rows, tc), lambda c, b: (0, c))
    const = lambda shape: pl.BlockSpec(shape, lambda c, b: (0, 0), pipeline_mode=pl.Buffered(1))
    blocks = (2 * _nbytes((2 * L, L), BF16) + 2 * 5 * _nbytes((tm, tc), BF16) + 5 * _nbytes((L, tc), F32)
              + _nbytes((L, LANES), F32) + _nbytes((L, FILTER_HIDDEN), F32))
    return pl.pallas_call(
        _hyena_kernel,
        out_shape=jax.ShapeDtypeStruct((B * L, D_HYENA), BF16),
        grid=(nblk, B // seqs_per_step),
        in_specs=[zcol(COL_VH), zcol(COL_X1), zcol(COL_X0), zcol(COL_GH),
                  taps(0), taps(1), taps(2), wcol(1, 0), wcol(1, 1), wcol(1, 2),
                  const((2 * L, L)), const((L, 2 * L)),
                  const((L, LANES)),
                  const((LANES, FILTER_HIDDEN)), const((1, FILTER_HIDDEN)),
                  const((FILTER_HIDDEN, FILTER_HIDDEN)), const((1, FILTER_HIDDEN)),
                  wcol(FILTER_HIDDEN, 0), wcol(FILTER_HIDDEN, 1),
                  const((2, FILTER_HIDDEN)), chan(L), chan(1)],
        out_specs=pl.BlockSpec((tm, tc), lambda c, b: (b, c)),
        scratch_shapes=[pltpu.VMEM((L, FILTER_HIDDEN), F32)] + [pltpu.VMEM((L, tc), F32)] * 3,
        compiler_params=pltpu.CompilerParams(
            dimension_semantics=("arbitrary", "arbitrary"),
            vmem_limit_bytes=_vmem_limit(blocks, 16 * _nbytes((tm, tc), F32))),
        name=f"hyena_{L}",
    )(z, z, z, z, conv_w, conv_w, conv_w, conv_b.reshape(1, -1), conv_b.reshape(1, -1), conv_b.reshape(1, -1),
      f_mat, g_mat, z_emb, w1p, b1.reshape(1, -1), w2, b2.reshape(1, -1), w3, w3, freq, decay, hyena_d.reshape(1, -1))


def _head_masks(width):
    lane = jax.lax.broadcasted_iota(jnp.int32, (1, width), 1)
    return [(lane >= h * HEAD_DIM) & (lane < (h + 1) * HEAD_DIM) for h in range(width // HEAD_DIM)]


def _qk(q, k):
    return jax.lax.dot_general(q, k, (((1,), (1,)), ((), ())), preferred_element_type=F32)


def _stack_heads(q, masks):
    return jnp.concatenate([jnp.where(msk, q, jnp.zeros_like(q)) for msk in masks], axis=0)


def _unstack_heads(o, masks):
    n = o.shape[0] // len(masks)
    out = o[:n]
    for h in range(1, len(masks)):
        out = jnp.where(masks[h], o[h * n:(h + 1) * n], out)
    return out


def _ctx_attn_kernel(q_ref, k_ref, v_ref, g_ref, o_ref):
    masks = _head_masks(CTX_HEAD_TILE)
    for t in range(D_ATTN // CTX_HEAD_TILE):
        cols = slice(t * CTX_HEAD_TILE, (t + 1) * CTX_HEAD_TILE)
        s = _qk(_stack_heads(q_ref[:, cols], masks), k_ref[:, cols])
        m = jnp.max(s, axis=-1, keepdims=True)
        p = jnp.exp2(s - m)
        l = jnp.sum(p, axis=-1, keepdims=True)
        out = _unstack_heads(_dot(p.astype(BF16), v_ref[:, cols]) / l, masks)
        o_ref[:, cols] = (out * g_ref[:, cols].astype(F32)).astype(BF16)


def _context_attention(z, z_row0, B, L):
    assert z_row0 % L == 0
    zcol = lambda piece: pl.BlockSpec((L, D_ATTN), lambda b: (z_row0 // L + b, piece))
    return pl.pallas_call(
        _ctx_attn_kernel,
        out_shape=jax.ShapeDtypeStruct((B * L, D_ATTN), BF16),
        grid=(B,),
        in_specs=[zcol(COL_Q), zcol(COL_K), zcol(COL_V), zcol(COL_GA)],
        out_specs=pl.BlockSpec((L, D_ATTN), lambda b: (b, 0)),
        compiler_params=pltpu.CompilerParams(dimension_semantics=("arbitrary",)),
        name="ctx_attn",
    )(z, z, z, z)


def _key_window_start(qb):
    rows = 1024 // GRID_W
    r_first = qb * Q_BLOCK_ROWS
    r_last = r_first + Q_BLOCK_ROWS - 1
    lo = min(max(r_first - WIN_ROWS // 2, 0), rows - WIN_ROWS)
    hi = min(max(r_last - WIN_ROWS // 2, 0), rows - WIN_ROWS) + WIN_ROWS
    start = min(lo, rows - KEY_WIN_ROWS)
    start -= start % Q_BLOCK_ROWS
    assert start <= lo and hi <= start + KEY_WIN_ROWS
    return start


def _nbr_attn_kernel(q_ref, k0_ref, k1_ref, k2_ref, v0_ref, v1_ref, v2_ref, kc_ref, vc_ref, g_ref, tiles_ref,
                     o_ref, bias_ref, *, n_rows):
    masks = _head_masks(NBR_HEAD_TILE)
    n_heads = q_ref.shape[-1] // HEAD_DIM
    chain_tiles = [slice(c * NBR_HEAD_TILE, (c + 1) * NBR_HEAD_TILE) for c in range(q_ref.shape[-1] // NBR_HEAD_TILE)]
    k_refs = (k0_ref, k1_ref, k2_ref)
    v_refs = (v0_ref, v1_ref, v2_ref)
    key_blocks = KEY_WIN // Q_BLOCK

    def window_row_start(r):
        return min(max(r - WIN_ROWS // 2, 0), n_rows - WIN_ROWS)

    def build_bias(qb):
        k_start = _key_window_start(qb)
        used = set()
        for i in range(Q_BLOCK_ROWS):
            r = qb * Q_BLOCK_ROWS + i
            r0 = window_row_start(r)
            for j in range(KEY_WIN_ROWS):
                kr = k_start + j
                cols = slice(j * GRID_W, (j + 1) * GRID_W)
                lanes = slice((j % RPB_COPIES) * GRID_W, (j % RPB_COPIES + 1) * GRID_W)
                inside = r0 <= kr < r0 + WIN_ROWS
                if inside:
                    used.add(j // Q_BLOCK_ROWS)
                for h in range(n_heads):
                    rows = slice(h * Q_BLOCK + i * GRID_W, h * Q_BLOCK + (i + 1) * GRID_W)
                    if inside:
                        bias_ref[rows, cols] = tiles_ref[h, kr - r + WIN_ROWS - 1, :, lanes]
                    else:
                        bias_ref[rows, cols] = jnp.full((GRID_W, GRID_W), NEG_INF, F32)
        return sorted(used)

    def attend(blocks):
        for c, lanes in enumerate(chain_tiles):
            stacked = slice(c * len(masks) * Q_BLOCK, (c + 1) * len(masks) * Q_BLOCK)
            for b in range(q_ref.shape[0]):
                qs = _stack_heads(q_ref[b, :, lanes], masks)
                s = [_qk(qs, k_refs[j][b, :, lanes]) + bias_ref[stacked, j * Q_BLOCK:(j + 1) * Q_BLOCK] for j in blocks]
                s.append(_dot(qs, kc_ref[b, lanes, :].astype(BF16)))
                m = jnp.max(functools.reduce(jnp.maximum, s), axis=-1, keepdims=True)
                p = [jnp.exp2(x - m) for x in s]
                l = jnp.sum(functools.reduce(jnp.add, p), axis=-1, keepdims=True)
                p = [x.astype(BF16) for x in p]
                v_win = jnp.concatenate([v_refs[j][b, :, lanes] for j in blocks], axis=0)
                o = _dot(jnp.concatenate(p[:-1], axis=1), v_win) + _qk(p[-1], vc_ref[b, lanes, :].astype(BF16))
                o_ref[b, :, lanes] = (_unstack_heads(o / l, masks) * g_ref[b, :, lanes].astype(F32)).astype(BF16)

    for qb in range(n_rows // Q_BLOCK_ROWS):
        @pl.when(pl.program_id(0) == qb)
        def _(qb=qb):
            blocks = build_bias(qb)
            assert blocks and all(0 <= j < key_blocks for j in blocks)
            attend(blocks)


def _neighbourhood_attention(z, z_row0, B, L, cache_k, cache_v, tiles):
    assert z_row0 % (B * L) == 0
    group = z_row0 // (B * L)
    n_rows = L // GRID_W
    n_qb = L // Q_BLOCK
    assert KEY_WIN == 3 * Q_BLOCK and all(_key_window_start(qb) % Q_BLOCK_ROWS == 0 for qb in range(n_qb))
    z3 = z.reshape(z.shape[0] // L, L, D_IN)
    n_ctx = cache_k.shape[1]
    kc = jnp.transpose(cache_k, (0, 2, 3, 1)).reshape(B, D_ATTN, n_ctx)
    vc = jnp.transpose(cache_v, (0, 2, 3, 1)).reshape(B, D_ATTN, n_ctx)
    width = NBR_BLOCK_LANES
    tiles_per_col = D_ATTN // width

    def win_block(qb):
        return jnp.where(qb < n_qb // 2, 0, (n_rows - KEY_WIN_ROWS) // Q_BLOCK_ROWS)

    assert [_key_window_start(qb) // Q_BLOCK_ROWS for qb in range(n_qb)] == \
        [0 if qb < n_qb // 2 else (n_rows - KEY_WIN_ROWS) // Q_BLOCK_ROWS for qb in range(n_qb)]
    qspec = lambda piece: pl.BlockSpec((B, Q_BLOCK, width), lambda qb, t: (group, qb, piece * tiles_per_col + t))
    kspec = lambda piece, j: pl.BlockSpec((B, Q_BLOCK, width),
                                          lambda qb, t: (group, win_block(qb) + j, piece * tiles_per_col + t))
    cspec = pl.BlockSpec((B, width, n_ctx), lambda qb, t: (0, t, 0))
    out = pl.pallas_call(
        functools.partial(_nbr_attn_kernel, n_rows=n_rows),
        out_shape=jax.ShapeDtypeStruct((B, L, D_ATTN), BF16),
        grid=(n_qb, tiles_per_col),
        in_specs=[qspec(COL_Q), kspec(COL_K, 0), kspec(COL_K, 1), kspec(COL_K, 2),
                  kspec(COL_V, 0), kspec(COL_V, 1), kspec(COL_V, 2), cspec, cspec, qspec(COL_GA),
                  pl.BlockSpec((width // HEAD_DIM, 2 * WIN_ROWS - 1, GRID_W, LANES), lambda qb, t: (t, 0, 0, 0))],
        out_specs=pl.BlockSpec((B, Q_BLOCK, width), lambda qb, t: (0, qb, t)),
        scratch_shapes=[pltpu.VMEM((width // HEAD_DIM * Q_BLOCK, KEY_WIN), F32)],
        compiler_params=pltpu.CompilerParams(dimension_semantics=("arbitrary", "arbitrary")),
        name="nbr_attn",
    )(z3, z3, z3, z3, z3, z3, z3, kc, vc, z3, tiles)
    return out.reshape(B * L, D_ATTN)


def _merge_kernel(yh_ref, ya_ref, mh_ref, ma_ref, x_ref, mod_ref, wbh_ref, wba_ref, wout_ref, lng_ref, lnb_ref,
                  o_ref, *, alpha):
    gate = mod_ref[:, 2 * D_MODEL:3 * D_MODEL]
    for r in range(o_ref.shape[0] // MERGE_CHUNK):
        rows = slice(r * MERGE_CHUNK, (r + 1) * MERGE_CHUNK)
        p_h = _dot(yh_ref[rows, :], wbh_ref[...])
        p_a = _dot(ya_ref[rows, :], wba_ref[...])
        m_h = mh_ref[rows, :].astype(F32)
        m_a = ma_ref[rows, :].astype(F32)
        out = _dot((m_h * p_h + m_a * p_a).astype(BF16), wout_ref[...])
        v = alpha * x_ref[rows, :] + gate * out
        mu = jnp.mean(v, axis=-1, keepdims=True)
        d = v - mu
        var = jnp.mean(d * d, axis=-1, keepdims=True)
        o_ref[rows, :] = d * jax.lax.rsqrt(var + LN_EPS) * lng_ref[...] + lnb_ref[...]


def _merge(yh, ya, z, z_row0, x2d, mod3, mod_index, w_bh, w_ba, w_out, ln_g, ln_b, *, alpha, tm):
    M = x2d.shape[0]
    assert z_row0 % tm == 0
    tok = lambda width, col: pl.BlockSpec((tm, width), lambda i: (i, col))
    ztok = lambda col: pl.BlockSpec((tm, D_MODEL), lambda i: (z_row0 // tm + i, col))
    const = lambda shape: pl.BlockSpec(shape, lambda i: (0, 0))
    return pl.pallas_call(
        functools.partial(_merge_kernel, alpha=alpha),
        out_shape=jax.ShapeDtypeStruct((M, D_MODEL), F32),
        grid=(M // tm,),
        in_specs=[tok(D_HYENA, 0), tok(D_ATTN, 0), ztok(COL_MH), ztok(COL_MA), tok(D_MODEL, 0),
                  pl.BlockSpec((None, 1, 3 * D_MODEL), lambda i: (mod_index(i), 0, 0)),
                  const((D_HYENA, D_MODEL)), const((D_ATTN, D_MODEL)), const((D_MODEL, D_MODEL)),
                  const((1, D_MODEL)), const((1, D_MODEL))],
        out_specs=pl.BlockSpec((tm, D_MODEL), lambda i: (i, 0)),
        compiler_params=pltpu.CompilerParams(
            dimension_semantics=("arbitrary",),
            vmem_limit_bytes=_vmem_limit(
                2 * (4 * _nbytes((tm, D_MODEL), BF16) + 2 * _nbytes((tm, D_MODEL), F32)
                     + 3 * _nbytes((D_MODEL, D_MODEL), BF16)),
                8 * _nbytes((tm, D_MODEL), F32))),
        name="merge",
    )(yh, ya, z, z, x2d, mod3, w_bh, w_ba, w_out, ln_g.reshape(1, -1), ln_b.reshape(1, -1))


def _mixers(x, z, z_row0, mod3, cond_row, ctx_kv, p, filt, tiles, *, alpha):
    B, L, _ = x.shape
    yh = _hyena_branch(z, z_row0, B, L, p["conv_w"], p["conv_b"], p["hyena_d"], *filt,
                       seqs_per_step=max(1, HYENA_ROWS_PER_STEP // L))
    if ctx_kv is None:
        ya = _context_attention(z, z_row0, B, L)
    else:
        ya = _neighbourhood_attention(z, z_row0, B, L, ctx_kv[0], ctx_kv[1], tiles)
    y = _merge(yh, ya, z, z_row0, x.reshape(B * L, D_MODEL), mod3, lambda i: cond_row(i * MERGE_TM // L),
               p["w_bh"], p["w_ba"], p["w_out"], p["ln_g"], p["ln_b"], alpha=alpha, tm=MERGE_TM)
    return y.reshape(B, L, D_MODEL)


def kernel(x_prompt, x_sample, c, cache_k, cache_v, c_ctx, w_ada, b_ada, w_in, conv_w, conv_b, filt_w1, filt_b1,
           filt_w2, filt_b2, filt_w3, filt_freq, hyena_d, rpb, w_bh, w_ba, w_out, ln_g, ln_b):
    depth = w_in.shape[0]
    alpha = (2.0 * depth) ** 0.25
    n_lat, lat_len = x_sample.shape[0], x_sample.shape[1]
    n_ctx, ctx_len = x_prompt.shape[0], x_prompt.shape[1]
    assert ctx_len == IN_TM
    ctx_row = n_lat

    xp, xs = x_prompt, x_sample
    new_k, new_v = [], []
    for l in range(depth):
        mod3 = _modulation(c, c_ctx.reshape(1, -1), w_ada[l], b_ada[l])
        filt = (filt_w1[l], filt_b1[l], filt_w2[l], filt_b2[l], filt_w3[l], filt_freq[l])
        tiles = _rpb_tiles(rpb[l])
        z, k_ctx, v_ctx, w_bh16, w_ba16, w_out16 = _in_projection(
            xs.reshape(-1, D_MODEL), xp.reshape(-1, D_MODEL), mod3, lambda tile: tile * IN_TM // lat_len, ctx_row,
            w_in[l], (w_bh[l], w_ba[l], w_out[l]), tm=IN_TM)
        p = {"conv_w": conv_w[l:l + 1], "conv_b": conv_b[l], "hyena_d": hyena_d[l],
             "w_bh": w_bh16, "w_ba": w_ba16, "w_out": w_out16, "ln_g": ln_g[l], "ln_b": ln_b[l]}
        new_k.append(jnp.transpose(k_ctx.reshape(n_ctx, N_HEADS, HEAD_DIM, ctx_len), (0, 3, 1, 2)))
        new_v.append(jnp.transpose(v_ctx.reshape(n_ctx, N_HEADS, HEAD_DIM, ctx_len), (0, 3, 1, 2)))
        xp = _mixers(xp, z, n_lat * lat_len, mod3, lambda b: ctx_row, None, p, filt, None, alpha=alpha)
        xs = _mixers(xs, z, 0, mod3, lambda b: b, (cache_k[:, l], cache_v[:, l]), p, filt, tiles, alpha=alpha)
    return xp, xs, jnp.stack(new_k, axis=1), jnp.stack(new_v, axis=1)
```

```python
import functools
import math

import jax
import jax.numpy as jnp
import numpy as np
from jax.experimental import pallas as pl
from jax.experimental.pallas import tpu as pltpu

F32 = jnp.float32
BF16 = jnp.bfloat16

D_MODEL = 1024
D_HYENA = 1024
N_HEADS = 16
HEAD_DIM = 64
D_ATTN = N_HEADS * HEAD_DIM
D_IN = 4 * D_HYENA + 4 * D_ATTN + 2 * D_MODEL
GRID_W = 64
WIN_ROWS = 8
WIN_COLS = 16
FILTER_EMB = 33
FILTER_BANDS = (FILTER_EMB - 1) // 2
FILTER_HIDDEN = 64
DECAY_TARGET = 1e-2
MIN_DECAY = math.log(DECAY_TARGET) / 1.5
MAX_DECAY = math.log(DECAY_TARGET) / 0.3
DECAY_SHIFT = 0.05
LN_EPS = 1e-5
NEG_INF = -1e30

COL_VH, COL_X1, COL_X0, COL_GH, COL_Q, COL_K, COL_V, COL_GA, COL_MH, COL_MA = range(10)
PROJ_ORDER = (COL_GH, COL_GA, COL_MH, COL_MA, COL_K, COL_V, COL_Q, COL_VH, COL_X1, COL_X0)

LANES = 128
SUBLANES = 8
CTX_HEAD_TILE = 128
NBR_HEAD_TILE = 128
NBR_BLOCK_LANES = 128
RPB_COPIES = LANES // GRID_W
V7X_VMEM_BYTES = 64 * 1024 * 1024
ATTN_SCALE = HEAD_DIM ** -0.5
LOG2E = math.log2(math.e)

Q_BLOCK = 256
Q_BLOCK_ROWS = Q_BLOCK // GRID_W
KEY_WIN_ROWS = 12
KEY_WIN = KEY_WIN_ROWS * GRID_W

IN_TM = 256
W_STAGES = 2
CAST_ROWS = 64
MERGE_TM = 512
MERGE_CHUNK = 256
HYENA_ROWS_PER_STEP = 2048
HYENA_PAIR_MIN_LEN = 512


def _vmem_limit(block_bytes, temp_bytes):
    need = int(block_bytes + temp_bytes)
    return min(max(need, 16 * 1024 * 1024), V7X_VMEM_BYTES - 8 * 1024 * 1024)


def _nbytes(shape, dtype):
    return int(np.prod(shape)) * jnp.dtype(dtype).itemsize


def _silu(x):
    return x * jax.nn.sigmoid(x)


def _split_bf16(a):
    hi = a.astype(BF16)
    lo = (a - hi.astype(F32)).astype(BF16)
    return hi, lo


def _dot(a, b):
    return jnp.dot(a, b, preferred_element_type=F32)


def _dot_split(a_hi, a_lo, b):
    b_hi, b_lo = _split_bf16(b)
    return _dot(a_hi, b_hi) + (_dot(a_hi, b_lo) + _dot(a_lo, b_hi))


@functools.lru_cache(maxsize=None)
def _dft_constants(L):
    n = 2 * L
    k = np.arange(L, dtype=np.float64)[:, None]
    s = np.arange(L, dtype=np.float64)[None, :]
    ang = 2.0 * np.pi * k * s / n
    nyq = np.cos(np.pi * np.arange(L, dtype=np.float64))
    C = np.cos(ang)
    S = np.sin(ang)
    S[0, :] = nyq
    F = np.concatenate([C, S], axis=0)
    w = np.full((L,), 2.0)
    w[0] = 1.0
    Gc = (C * w[:, None]).T / n
    Ss = 2.0 * np.sin(ang)
    Ss[0, :] = nyq
    Gs = Ss.T / n
    G = np.concatenate([Gc, Gs], axis=1)
    return F.astype(np.float32), G.astype(np.float32)


def _dft_matrices(L):
    F, G = _dft_constants(L)
    return jnp.asarray(F).astype(BF16), jnp.asarray(G).astype(BF16)


@functools.lru_cache(maxsize=None)
def _filter_constants(L):
    t = np.linspace(0.0, 1.0, L, dtype=np.float32)[:, None]
    bands = np.linspace(1e-4, FILTER_BANDS - 1, FILTER_BANDS, dtype=np.float32)[None]
    w = (2.0 * math.pi * np.arange(L, dtype=np.float32)[:, None] / L).astype(np.float32)
    z = np.concatenate([t, np.cos(bands * w), -np.sin(bands * w)], axis=-1).astype(np.float32)
    z_pad = np.zeros((L, LANES), np.float32)
    z_pad[:, :FILTER_EMB] = z
    deltas = np.linspace(MIN_DECAY, MAX_DECAY, D_HYENA, dtype=np.float32)
    decay = (np.exp(-t * np.abs(deltas)) + np.float32(DECAY_SHIFT)).astype(np.float32)
    return z_pad, decay


@functools.lru_cache(maxsize=None)
def _rpb_window_mask():
    c = np.arange(GRID_W)[:, None]
    kc = np.arange(GRID_W)[None, :]
    c0 = np.clip(c - WIN_COLS // 2, 0, GRID_W - WIN_COLS)
    in_win = (kc >= c0) & (kc < c0 + WIN_COLS)
    return np.tile(in_win, (1, RPB_COPIES)).astype(np.float32)


def _mod_kernel(c_ref, cctx_ref, w_ref, b_ref, o_ref, cond_ref):
    n_lat = c_ref.shape[0]
    cond_ref[...] = jnp.zeros(cond_ref.shape, F32)
    cond_ref[0:n_lat, :] = c_ref[...]
    cond_ref[n_lat:n_lat + 1, :] = cctx_ref[...]
    s = _silu(cond_ref[...]).astype(BF16)
    o_ref[:, 0, :] = _dot(s, w_ref[...].astype(BF16)) + b_ref[...]


def _modulation(c, c_ctx, w_ada, b_ada):
    n_lat = c.shape[0]
    assert n_lat + 1 <= SUBLANES
    return pl.pallas_call(
        _mod_kernel,
        out_shape=jax.ShapeDtypeStruct((SUBLANES, 1, 3 * D_MODEL), F32),
        grid=(3,),
        in_specs=[pl.BlockSpec((n_lat, D_MODEL), lambda j: (0, 0)),
                  pl.BlockSpec((1, D_MODEL), lambda j: (0, 0)),
                  pl.BlockSpec((D_MODEL, D_MODEL), lambda j: (0, j)),
                  pl.BlockSpec((1, D_MODEL), lambda j: (0, j))],
        out_specs=pl.BlockSpec((SUBLANES, 1, D_MODEL), lambda j: (0, 0, j)),
        scratch_shapes=[pltpu.VMEM((SUBLANES, D_MODEL), F32)],
        name="mod",
    )(c, c_ctx, w_ada, b_ada.reshape(1, -1))


def _filter_spectra(z_ref, w1_ref, b1_ref, w2_ref, b2_ref, w3f_ref, w3b_ref, freq_ref, decay_ref, d_ref, f_ref,
                    hf_ref, fp_ref, fqa_ref, fpb_ref):
    L = decay_ref.shape[0]

    @pl.when(pl.program_id(0) == 0)
    def _():
        z_hi, z_lo = _split_bf16(z_ref[...])
        hf = jnp.sin(freq_ref[0:1, :] * (_dot_split(z_hi, z_lo, w1_ref[...]) + b1_ref[...]))
        h_hi, h_lo = _split_bf16(hf)
        hf_ref[...] = jnp.sin(freq_ref[1:2, :] * (_dot_split(h_hi, h_lo, w2_ref[...]) + b2_ref[...]))

    h_hi, h_lo = _split_bf16(hf_ref[...])
    decay = decay_ref[...]
    h_fwd = _dot_split(h_hi, h_lo, w3f_ref[...]) * decay
    h_bwd = _dot_split(h_hi, h_lo, w3b_ref[...]) * decay
    row0 = jax.lax.broadcasted_iota(jnp.int32, (L, 1), 0) == 0
    h_bwd = jnp.where(row0, 0.0, h_bwd)
    fsum = (h_fwd + h_bwd).astype(BF16)
    fdif = (h_fwd - h_bwd).astype(BF16)
    skip = d_ref[...]
    fp = _dot(f_ref[0:L, :], fsum) + skip
    fq = _dot(f_ref[L:2 * L, :], fdif)
    nyq = _dot(f_ref[L:L + 2 * SUBLANES, :], fsum)[0:1, :] + skip
    fp_ref[...] = fp
    fqa_ref[...] = jnp.where(row0, 0.0, fq)
    fpb_ref[...] = jnp.where(row0, nyq, fp)


def _rpb_kernel(r_ref, mask_ref, o_ref):
    inside = mask_ref[...] > 0.0
    for i in range(o_ref.shape[0]):
        r = jnp.broadcast_to(r_ref[i:i + 1, :], (GRID_W, LANES))
        t = pltpu.roll(r, LANES - (WIN_COLS - 1), 1, stride=1, stride_axis=0)
        o_ref[i] = jnp.where(inside, t * LOG2E, NEG_INF)


def _rpb_tiles(rpb):
    n_dr = 2 * WIN_ROWS - 1
    n_off = 2 * WIN_COLS - 1
    heads_per_step = 4
    rows = heads_per_step * n_dr
    r = jnp.tile(jnp.pad(rpb.reshape(N_HEADS * n_dr, n_off), ((0, 0), (0, GRID_W - n_off))), (1, RPB_COPIES))
    out = pl.pallas_call(
        _rpb_kernel,
        out_shape=jax.ShapeDtypeStruct((N_HEADS * n_dr, GRID_W, LANES), F32),
        grid=(N_HEADS // heads_per_step,),
        in_specs=[pl.BlockSpec((None, rows, LANES), lambda j: (j, 0, 0)),
                  pl.BlockSpec((GRID_W, LANES), lambda j: (0, 0))],
        out_specs=pl.BlockSpec((rows, GRID_W, LANES), lambda j: (j, 0, 0)),
        name="rpb_tiles",
    )(r.reshape(N_HEADS // heads_per_step, rows, LANES), jnp.asarray(_rpb_window_mask()))
    return out.reshape(N_HEADS, n_dr, GRID_W, LANES)


def _inproj_kernel(xl_ref, xc_ref, mod_ref, w_hbm, wbh_ref, wba_ref, wout_ref,
                   z_ref, kt_ref, vt_ref, wbh16_ref, wba16_ref, wout16_ref, w_ref, stage_ref, sem, *, n_lat_tiles):
    n_col = D_IN // D_MODEL
    n_stage = stage_ref.shape[0]
    epilogue = {COL_GH: _silu, COL_GA: _silu, COL_MH: jax.nn.sigmoid, COL_MA: jax.nn.sigmoid,
                COL_Q: lambda acc: acc * (ATTN_SCALE * LOG2E)}

    def columns(col):
        return slice(col * D_MODEL, (col + 1) * D_MODEL)

    def fetch(n):
        return pltpu.make_async_copy(w_hbm.at[:, columns(PROJ_ORDER[n])], stage_ref.at[n % n_stage], sem.at[n % n_stage])

    def tile(stream_weights):
        if stream_weights:
            for n in range(min(n_stage, n_col)):
                fetch(n).start()
        shift = mod_ref[:, 0:D_MODEL]
        scale = mod_ref[:, D_MODEL:2 * D_MODEL]
        x = jnp.where(pl.program_id(0) < n_lat_tiles, xl_ref[...], xc_ref[...])
        h = (x * (1.0 + scale) + shift).astype(BF16)
        for src_ref, dst_ref in ((wbh_ref, wbh16_ref), (wba_ref, wba16_ref), (wout_ref, wout16_ref)):
            dst_ref[...] = src_ref[...].astype(BF16)
        for n, col in enumerate(PROJ_ORDER):
            if stream_weights:
                fetch(n).wait()
                w_ref[:, columns(col)] = stage_ref[n % n_stage].astype(BF16)
                if n + n_stage < n_col:
                    fetch(n + n_stage).start()
            acc = _dot(h, w_ref[:, columns(col)])
            z_ref[:, columns(col)] = epilogue.get(col, lambda acc: acc)(acc).astype(BF16)
            if col == COL_K:
                kt_ref[...] = acc.T
            if col == COL_V:
                vt_ref[...] = acc.T

    pl.when(pl.program_id(0) == 0)(functools.partial(tile, True))
    pl.when(pl.program_id(0) != 0)(functools.partial(tile, False))


def _in_projection(x_lat, x_ctx, mod3, lat_row, ctx_row, w_in, merge_weights, *, tm):
    n_lat, n_ctx = x_lat.shape[0] // tm, x_ctx.shape[0] // tm
    is_lat = lambda i: i < n_lat
    lat_tile = lambda i: jnp.minimum(i, n_lat - 1)
    ctx_tile = lambda i: jnp.maximum(i - n_lat, 0)
    kv_shape = jax.ShapeDtypeStruct((n_ctx, D_ATTN, tm), F32)
    kv_spec = pl.BlockSpec((None, D_ATTN, tm), lambda i: (ctx_tile(i), 0, 0))
    n_cast = D_MODEL // CAST_ROWS
    assert n_cast <= n_lat + n_ctx and all(w.shape == (D_MODEL, D_MODEL) for w in merge_weights)
    cast_spec = pl.BlockSpec((CAST_ROWS, D_MODEL), lambda i: (jnp.minimum(i, n_cast - 1), 0))
    stage = (W_STAGES, D_MODEL, D_MODEL)
    blocks = (_nbytes((D_MODEL, D_IN), BF16) + _nbytes(stage, F32) + 4 * _nbytes((tm, D_MODEL), F32)
              + 2 * _nbytes((tm, D_IN), BF16) + 4 * _nbytes((tm, D_ATTN), F32)
              + 3 * 2 * (_nbytes((CAST_ROWS, D_MODEL), F32) + _nbytes((CAST_ROWS, D_MODEL), BF16)))
    return pl.pallas_call(
        functools.partial(_inproj_kernel, n_lat_tiles=n_lat),
        out_shape=[jax.ShapeDtypeStruct(((n_lat + n_ctx) * tm, D_IN), BF16), kv_shape, kv_shape]
                  + [jax.ShapeDtypeStruct((D_MODEL, D_MODEL), BF16)] * 3,
        grid=(n_lat + n_ctx,),
        in_specs=[pl.BlockSpec((tm, D_MODEL), lambda i: (lat_tile(i), 0)),
                  pl.BlockSpec((tm, D_MODEL), lambda i: (ctx_tile(i), 0)),
                  pl.BlockSpec((None, 1, 3 * D_MODEL),
                               lambda i: (jnp.where(is_lat(i), lat_row(lat_tile(i)), ctx_row), 0, 0)),
                  pl.BlockSpec(memory_space=pl.ANY)] + [cast_spec] * 3,
        out_specs=[pl.BlockSpec((tm, D_IN), lambda i: (i, 0)), kv_spec, kv_spec] + [cast_spec] * 3,
        scratch_shapes=[pltpu.VMEM((D_MODEL, D_IN), BF16), pltpu.VMEM(stage, F32),
                        pltpu.SemaphoreType.DMA((W_STAGES,))],
        compiler_params=pltpu.CompilerParams(
            dimension_semantics=("arbitrary",),
            vmem_limit_bytes=_vmem_limit(blocks, 4 * _nbytes((tm, D_MODEL), F32))),
        name="inproj",
    )(x_lat, x_ctx, mod3, w_in, *merge_weights)


def _hyena_kernel(vh_ref, x1_ref, x0_ref, gh_ref, cwv_ref, cw1_ref, cw0_ref, cbv_ref, cb1_ref, cb0_ref,
                  f_ref, g_ref, z_ref, w1_ref, b1_ref, w2_ref, b2_ref, w3f_ref, w3b_ref, freq_ref, decay_ref, d_ref,
                  o_ref, hf_ref, fp_ref, fqa_ref, fpb_ref):
    L = fp_ref.shape[0]
    slab = jax.lax.broadcasted_iota(jnp.int32, (SUBLANES, 1), 0)

    @pl.when(pl.program_id(1) == 0)
    def _():
        _filter_spectra(z_ref, w1_ref, b1_ref, w2_ref, b2_ref, w3f_ref, w3b_ref, freq_ref, decay_ref, d_ref, f_ref,
                        hf_ref, fp_ref, fqa_ref, fpb_ref)

    def short_conv(x_ref, rows, w_ref, b_ref):
        x = x_ref[rows, :].astype(F32)
        prev = pltpu.roll(x, 1, 0)
        nxt = pltpu.roll(x, L - 1, 0)
        prev = jnp.concatenate([jnp.where(slab == 0, 0.0, prev[:SUBLANES]), prev[SUBLANES:]], axis=0)
        nxt = jnp.concatenate([nxt[:-SUBLANES], jnp.where(slab == SUBLANES - 1, 0.0, nxt[-SUBLANES:])], axis=0)
        return prev * w_ref[0:1, :] + x * w_ref[1:2, :] + nxt * w_ref[2:3, :] + b_ref[...]

    tc = fp_ref.shape[1]
    n_seq = vh_ref.shape[0] // L
    group = 2 if L >= HYENA_PAIR_MIN_LEN and n_seq % 2 == 0 else 1
    for s in range(0, n_seq, group):
        seqs = [slice((s + i) * L, (s + i + 1) * L) for i in range(group)]
        u = [short_conv(vh_ref, rows, cwv_ref, cbv_ref) * short_conv(x1_ref, rows, cw1_ref, cb1_ref) for rows in seqs]
        t = _dot(f_ref[...], jnp.concatenate([ui.astype(BF16) for ui in u], axis=1))
        spec = []
        for i in range(group):
            p = t[:L, i * tc:(i + 1) * tc]
            q = t[L:, i * tc:(i + 1) * tc]
            yp = p * fp_ref[...] - q * fqa_ref[...]
            yq = p * fqa_ref[...] + q * fpb_ref[...]
            spec.append(jnp.concatenate([yp.astype(BF16), yq.astype(BF16)], axis=0))
        y = _dot(g_ref[...], jnp.concatenate(spec, axis=1))
        for i, rows in enumerate(seqs):
            y_h = y[:, i * tc:(i + 1) * tc] * short_conv(x0_ref, rows, cw0_ref, cb0_ref)
            o_ref[rows, :] = (y_h * gh_ref[rows, :].astype(F32)).astype(BF16)


def _hyena_branch(z, z_row0, B, L, conv_w, conv_b, hyena_d, w1, b1, w2, b2, w3, freq, *, tc=256, seqs_per_step):
    nblk = D_HYENA // tc
    tm = seqs_per_step * L
    assert z_row0 % tm == 0
    f_mat, g_mat = _dft_matrices(L)
    z_emb, decay = _filter_constants(L)
    w1p = jnp.pad(w1, ((0, LANES - FILTER_EMB), (0, 0)))
    zcol = lambda piece: pl.BlockSpec((tm, tc), lambda c, b: (z_row0 // tm + b, piece * nblk + c))
    wcol = lambda rows, piece: pl.BlockSpec((rows, tc), lambda c, b: (0, piece * nblk + c))
    taps = lambda piece: pl.BlockSpec((None, 3, tc), lambda c, b: (0, 0, piece * nblk + c))
    chan = lambda rows: pl.BlockSpec((rows, tc), lambda c, b: (0, c))
    const = lambda shape: pl.BlockSpec(shape, lambda c, b: (0, 0), pipeline_mode=pl.Buffered(1))
    blocks = (2 * _nbytes((2 * L, L), BF16) + 2 * 5 * _nbytes((tm, tc), BF16) + 5 * _nbytes((L, tc), F32)
              + _nbytes((L, LANES), F32) + _nbytes((L, FILTER_HIDDEN), F32))
    return pl.pallas_call(
        _hyena_kernel,
        out_shape=jax.ShapeDtypeStruct((B * L, D_HYENA), BF16),
        grid=(nblk, B // seqs_per_step),
        in_specs=[zcol(COL_VH), zcol(COL_X1), zcol(COL_X0), zcol(COL_GH),
                  taps(0), taps(1), taps(2), wcol(1, 0), wcol(1, 1), wcol(1, 2),
                  const((2 * L, L)), const((L, 2 * L)),
                  const((L, LANES)),
                  const((LANES, FILTER_HIDDEN)), const((1, FILTER_HIDDEN)),
                  const((FILTER_HIDDEN, FILTER_HIDDEN)), const((1, FILTER_HIDDEN)),
                  wcol(FILTER_HIDDEN, 0), wcol(FILTER_HIDDEN, 1),
                  const((2, FILTER_HIDDEN)), chan(L), chan(1)],
        out_specs=pl.BlockSpec((tm, tc), lambda c, b: (b, c)),
        scratch_shapes=[pltpu.VMEM((L, FILTER_HIDDEN), F32)] + [pltpu.VMEM((L, tc), F32)] * 3,
        compiler_params=pltpu.CompilerParams(
            dimension_semantics=("arbitrary", "arbitrary"),
            vmem_limit_bytes=_vmem_limit(blocks, 16 * _nbytes((tm, tc), F32))),
        name=f"hyena_{L}",
    )(z, z, z, z, conv_w, conv_w, conv_w, conv_b.reshape(1, -1), conv_b.reshape(1, -1), conv_b.reshape(1, -1),
      f_mat, g_mat, z_emb, w1p, b1.reshape(1, -1), w2, b2.reshape(1, -1), w3, w3, freq, decay, hyena_d.reshape(1, -1))


def _head_masks(width):
    lane = jax.lax.broadcasted_iota(jnp.int32, (1, width), 1)
    return [(lane >= h * HEAD_DIM) & (lane < (h + 1) * HEAD_DIM) for h in range(width // HEAD_DIM)]


def _qk(q, k):
    return jax.lax.dot_general(q, k, (((1,), (1,)), ((), ())), preferred_element_type=F32)


def _stack_heads(q, masks):
    return jnp.concatenate([jnp.where(msk, q, jnp.zeros_like(q)) for msk in masks], axis=0)


def _unstack_heads(o, masks):
    n = o.shape[0] // len(masks)
    out = o[:n]
    for h in range(1, len(masks)):
        out = jnp.where(masks[h], o[h * n:(h + 1) * n], out)
    return out


def _ctx_attn_kernel(q_ref, k_ref, v_ref, g_ref, o_ref):
    masks = _head_masks(CTX_HEAD_TILE)
    for t in range(D_ATTN // CTX_HEAD_TILE):
        cols = slice(t * CTX_HEAD_TILE, (t + 1) * CTX_HEAD_TILE)
        s = _qk(_stack_heads(q_ref[:, cols], masks), k_ref[:, cols])
        m = jnp.max(s, axis=-1, keepdims=True)
        p = jnp.exp2(s - m)
        l = jnp.sum(p, axis=-1, keepdims=True)
        out = _unstack_heads(_dot(p.astype(BF16), v_ref[:, cols]) / l, masks)
        o_ref[:, cols] = (out * g_ref[:, cols].astype(F32)).astype(BF16)


def _context_attention(z, z_row0, B, L):
    assert z_row0 % L == 0
    zcol = lambda piece: pl.BlockSpec((L, D_ATTN), lambda b: (z_row0 // L + b, piece))
    return pl.pallas_call(
        _ctx_attn_kernel,
        out_shape=jax.ShapeDtypeStruct((B * L, D_ATTN), BF16),
        grid=(B,),
        in_specs=[zcol(COL_Q), zcol(COL_K), zcol(COL_V), zcol(COL_GA)],
        out_specs=pl.BlockSpec((L, D_ATTN), lambda b: (b, 0)),
        compiler_params=pltpu.CompilerParams(dimension_semantics=("arbitrary",)),
        name="ctx_attn",
    )(z, z, z, z)


def _key_window_start(qb):
    rows = 1024 // GRID_W
    r_first = qb * Q_BLOCK_ROWS
    r_last = r_first + Q_BLOCK_ROWS - 1
    lo = min(max(r_first - WIN_ROWS // 2, 0), rows - WIN_ROWS)
    hi = min(max(r_last - WIN_ROWS // 2, 0), rows - WIN_ROWS) + WIN_ROWS
    start = min(lo, rows - KEY_WIN_ROWS)
    start -= start % Q_BLOCK_ROWS
    assert start <= lo and hi <= start + KEY_WIN_ROWS
    return start


def _nbr_attn_kernel(q_ref, k0_ref, k1_ref, k2_ref, v0_ref, v1_ref, v2_ref, kc_ref, vc_ref, g_ref, tiles_ref,
                     o_ref, bias_ref, *, n_rows):
    masks = _head_masks(NBR_HEAD_TILE)
    n_heads = q_ref.shape[-1] // HEAD_DIM
    chain_tiles = [slice(c * NBR_HEAD_TILE, (c + 1) * NBR_HEAD_TILE) for c in range(q_ref.shape[-1] // NBR_HEAD_TILE)]
    k_refs = (k0_ref, k1_ref, k2_ref)
    v_refs = (v0_ref, v1_ref, v2_ref)
    key_blocks = KEY_WIN // Q_BLOCK

    def window_row_start(r):
        return min(max(r - WIN_ROWS // 2, 0), n_rows - WIN_ROWS)

    def build_bias(qb):
        k_start = _key_window_start(qb)
        used = set()
        for i in range(Q_BLOCK_ROWS):
            r = qb * Q_BLOCK_ROWS + i
            r0 = window_row_start(r)
            for j in range(KEY_WIN_ROWS):
                kr = k_start + j
                cols = slice(j * GRID_W, (j + 1) * GRID_W)
                lanes = slice((j % RPB_COPIES) * GRID_W, (j % RPB_COPIES + 1) * GRID_W)
                inside = r0 <= kr < r0 + WIN_ROWS
                if inside:
                    used.add(j // Q_BLOCK_ROWS)
                for h in range(n_heads):
                    rows = slice(h * Q_BLOCK + i * GRID_W, h * Q_BLOCK + (i + 1) * GRID_W)
                    if inside:
                        bias_ref[rows, cols] = tiles_ref[h, kr - r + WIN_ROWS - 1, :, lanes]
                    else:
                        bias_ref[rows, cols] = jnp.full((GRID_W, GRID_W), NEG_INF, F32)
        return sorted(used)

    def attend(blocks):
        for c, lanes in enumerate(chain_tiles):
            stacked = slice(c * len(masks) * Q_BLOCK, (c + 1) * len(masks) * Q_BLOCK)
            for b in range(q_ref.shape[0]):
                qs = _stack_heads(q_ref[b, :, lanes], masks)
                s = [_qk(qs, k_refs[j][b, :, lanes]) + bias_ref[stacked, j * Q_BLOCK:(j + 1) * Q_BLOCK] for j in blocks]
                s.append(_dot(qs, kc_ref[b, lanes, :].astype(BF16)))
                m = jnp.max(functools.reduce(jnp.maximum, s), axis=-1, keepdims=True)
                p = [jnp.exp2(x - m) for x in s]
                l = jnp.sum(functools.reduce(jnp.add, p), axis=-1, keepdims=True)
                p = [x.astype(BF16) for x in p]
                v_win = jnp.concatenate([v_refs[j][b, :, lanes] for j in blocks], axis=0)
                o = _dot(jnp.concatenate(p[:-1], axis=1), v_win) + _qk(p[-1], vc_ref[b, lanes, :].astype(BF16))
                o_ref[b, :, lanes] = (_unstack_heads(o / l, masks) * g_ref[b, :, lanes].astype(F32)).astype(BF16)

    for qb in range(n_rows // Q_BLOCK_ROWS):
        @pl.when(pl.program_id(0) == qb)
        def _(qb=qb):
            blocks = build_bias(qb)
            assert blocks and all(0 <= j < key_blocks for j in blocks)
            attend(blocks)


def _neighbourhood_attention(z, z_row0, B, L, cache_k, cache_v, tiles):
    assert z_row0 % (B * L) == 0
    group = z_row0 // (B * L)
    n_rows = L // GRID_W
    n_qb = L // Q_BLOCK
    assert KEY_WIN == 3 * Q_BLOCK and all(_key_window_start(qb) % Q_BLOCK_ROWS == 0 for qb in range(n_qb))
    z3 = z.reshape(z.shape[0] // L, L, D_IN)
    n_ctx = cache_k.shape[1]
    kc = jnp.transpose(cache_k, (0, 2, 3, 1)).reshape(B, D_ATTN, n_ctx)
    vc = jnp.transpose(cache_v, (0, 2, 3, 1)).reshape(B, D_ATTN, n_ctx)
    width = NBR_BLOCK_LANES
    tiles_per_col = D_ATTN // width

    def win_block(qb):
        return jnp.where(qb < n_qb // 2, 0, (n_rows - KEY_WIN_ROWS) // Q_BLOCK_ROWS)

    assert [_key_window_start(qb) // Q_BLOCK_ROWS for qb in range(n_qb)] == \
        [0 if qb < n_qb // 2 else (n_rows - KEY_WIN_ROWS) // Q_BLOCK_ROWS for qb in range(n_qb)]
    qspec = lambda piece: pl.BlockSpec((B, Q_BLOCK, width), lambda qb, t: (group, qb, piece * tiles_per_col + t))
    kspec = lambda piece, j: pl.BlockSpec((B, Q_BLOCK, width),
                                          lambda qb, t: (group, win_block(qb) + j, piece * tiles_per_col + t))
    cspec = pl.BlockSpec((B, width, n_ctx), lambda qb, t: (0, t, 0))
    out = pl.pallas_call(
        functools.partial(_nbr_attn_kernel, n_rows=n_rows),
        out_shape=jax.ShapeDtypeStruct((B, L, D_ATTN), BF16),
        grid=(n_qb, tiles_per_col),
        in_specs=[qspec(COL_Q), kspec(COL_K, 0), kspec(COL_K, 1), kspec(COL_K, 2),
                  kspec(COL_V, 0), kspec(COL_V, 1), kspec(COL_V, 2), cspec, cspec, qspec(COL_GA),
                  pl.BlockSpec((width // HEAD_DIM, 2 * WIN_ROWS - 1, GRID_W, LANES), lambda qb, t: (t, 0, 0, 0))],
        out_specs=pl.BlockSpec((B, Q_BLOCK, width), lambda qb, t: (0, qb, t)),
        scratch_shapes=[pltpu.VMEM((width // HEAD_DIM * Q_BLOCK, KEY_WIN), F32)],
        compiler_params=pltpu.CompilerParams(dimension_semantics=("arbitrary", "arbitrary")),
        name="nbr_attn",
    )(z3, z3, z3, z3, z3, z3, z3, kc, vc, z3, tiles)
    return out.reshape(B * L, D_ATTN)


def _merge_kernel(yh_ref, ya_ref, mh_ref, ma_ref, x_ref, mod_ref, wbh_ref, wba_ref, wout_ref, lng_ref, lnb_ref,
                  o_ref, *, alpha):
    gate = mod_ref[:, 2 * D_MODEL:3 * D_MODEL]
    for r in range(o_ref.shape[0] // MERGE_CHUNK):
        rows = slice(r * MERGE_CHUNK, (r + 1) * MERGE_CHUNK)
        p_h = _dot(yh_ref[rows, :], wbh_ref[...])
        p_a = _dot(ya_ref[rows, :], wba_ref[...])
        m_h = mh_ref[rows, :].astype(F32)
        m_a = ma_ref[rows, :].astype(F32)
        out = _dot((m_h * p_h + m_a * p_a).astype(BF16), wout_ref[...])
        v = alpha * x_ref[rows, :] + gate * out
        mu = jnp.mean(v, axis=-1, keepdims=True)
        d = v - mu
        var = jnp.mean(d * d, axis=-1, keepdims=True)
        o_ref[rows, :] = d * jax.lax.rsqrt(var + LN_EPS) * lng_ref[...] + lnb_ref[...]


def _merge(yh, ya, z, z_row0, x2d, mod3, mod_index, w_bh, w_ba, w_out, ln_g, ln_b, *, alpha, tm):
    M = x2d.shape[0]
    assert z_row0 % tm == 0
    tok = lambda width, col: pl.BlockSpec((tm, width), lambda i: (i, col))
    ztok = lambda col: pl.BlockSpec((tm, D_MODEL), lambda i: (z_row0 // tm + i, col))
    const = lambda shape: pl.BlockSpec(shape, lambda i: (0, 0))
    return pl.pallas_call(
        functools.partial(_merge_kernel, alpha=alpha),
        out_shape=jax.ShapeDtypeStruct((M, D_MODEL), F32),
        grid=(M // tm,),
        in_specs=[tok(D_HYENA, 0), tok(D_ATTN, 0), ztok(COL_MH), ztok(COL_MA), tok(D_MODEL, 0),
                  pl.BlockSpec((None, 1, 3 * D_MODEL), lambda i: (mod_index(i), 0, 0)),
                  const((D_HYENA, D_MODEL)), const((D_ATTN, D_MODEL)), const((D_MODEL, D_MODEL)),
                  const((1, D_MODEL)), const((1, D_MODEL))],
        out_specs=pl.BlockSpec((tm, D_MODEL), lambda i: (i, 0)),
        compiler_params=pltpu.CompilerParams(
            dimension_semantics=("arbitrary",),
            vmem_limit_bytes=_vmem_limit(
                2 * (4 * _nbytes((tm, D_MODEL), BF16) + 2 * _nbytes((tm, D_MODEL), F32)
                     + 3 * _nbytes((D_MODEL, D_MODEL), BF16)),
                8 * _nbytes((tm, D_MODEL), F32))),
        name="merge",
    )(yh, ya, z, z, x2d, mod3, w_bh, w_ba, w_out, ln_g.reshape(1, -1), ln_b.reshape(1, -1))


def _mixers(x, z, z_row0, mod3, cond_row, ctx_kv, p, filt, tiles, *, alpha):
    B, L, _ = x.shape
    yh = _hyena_branch(z, z_row0, B, L, p["conv_w"], p["conv_b"], p["hyena_d"], *filt,
                       seqs_per_step=max(1, HYENA_ROWS_PER_STEP // L))
    if ctx_kv is None:
        ya = _context_attention(z, z_row0, B, L)
    else:
        ya = _neighbourhood_attention(z, z_row0, B, L, ctx_kv[0], ctx_kv[1], tiles)
    y = _merge(yh, ya, z, z_row0, x.reshape(B * L, D_MODEL), mod3, lambda i: cond_row(i * MERGE_TM // L),
               p["w_bh"], p["w_ba"], p["w_out"], p["ln_g"], p["ln_b"], alpha=alpha, tm=MERGE_TM)
    return y.reshape(B, L, D_MODEL)


def kernel(x_prompt, x_sample, c, cache_k, cache_v, c_ctx, w_ada, b_ada, w_in, conv_w, conv_b, filt_w1, filt_b1,
           filt_w2, filt_b2, filt_w3, filt_freq, hyena_d, rpb, w_bh, w_ba, w_out, ln_g, ln_b):
    depth = w_in.shape[0]
    alpha = (2.0 * depth) ** 0.25
    n_lat, lat_len = x_sample.shape[0], x_sample.shape[1]
    n_ctx, ctx_len = x_prompt.shape[0], x_prompt.shape[1]
    assert ctx_len == IN_TM
    ctx_row = n_lat

    xp, xs = x_prompt, x_sample
    new_k, new_v = [], []
    for l in range(depth):
        mod3 = _modulation(c, c_ctx.reshape(1, -1), w_ada[l], b_ada[l])
        filt = (filt_w1[l], filt_b1[l], filt_w2[l], filt_b2[l], filt_w3[l], filt_freq[l])
        tiles = _rpb_tiles(rpb[l])
        z, k_ctx, v_ctx, w_bh16, w_ba16, w_out16 = _in_projection(
            xs.reshape(-1, D_MODEL), xp.reshape(-1, D_MODEL), mod3, lambda tile: tile * IN_TM // lat_len, ctx_row,
            w_in[l], (w_bh[l], w_ba[l], w_out[l]), tm=IN_TM)
        p = {"conv_w": conv_w[l:l + 1], "conv_b": conv_b[l], "hyena_d": hyena_d[l],
             "w_bh": w_bh16, "w_ba": w_ba16, "w_out": w_out16, "ln_g": ln_g[l], "ln_b": ln_b[l]}
        new_k.append(jnp.transpose(k_ctx.reshape(n_ctx, N_HEADS, HEAD_DIM, ctx_len), (0, 3, 1, 2)))
        new_v.append(jnp.transpose(v_ctx.reshape(n_ctx, N_HEADS, HEAD_DIM, ctx_len), (0, 3, 1, 2)))
        xp = _mixers(xp, z, n_lat * lat_len, mod3, lambda b: ctx_row, None, p, filt, None, alpha=alpha)
        xs = _mixers(xs, z, 0, mod3, lambda b: b, (cache_k[:, l], cache_v[:, l]), p, filt, tiles, alpha=alpha)
    return xp, xs, jnp.stack(new_k, axis=1), jnp.stack(new_v, axis=1)
```

```python
import functools
import math

import jax
import jax.numpy as jnp
import numpy as np
from jax.experimental import pallas as pl
from jax.experimental.pallas import tpu as pltpu

F32 = jnp.float32
BF16 = jnp.bfloat16

D_MODEL = 1024
D_HYENA = 1024
N_HEADS = 16
HEAD_DIM = 64
D_ATTN = N_HEADS * HEAD_DIM
D_IN = 4 * D_HYENA + 4 * D_ATTN + 2 * D_MODEL
GRID_W = 64
WIN_ROWS = 8
WIN_COLS = 16
FILTER_EMB = 33
FILTER_BANDS = (FILTER_EMB - 1) // 2
FILTER_HIDDEN = 64
DECAY_TARGET = 1e-2
MIN_DECAY = math.log(DECAY_TARGET) / 1.5
MAX_DECAY = math.log(DECAY_TARGET) / 0.3
DECAY_SHIFT = 0.05
LN_EPS = 1e-5
NEG_INF = -1e30

COL_VH, COL_X1, COL_X0, COL_GH, COL_Q, COL_K, COL_V, COL_GA, COL_MH, COL_MA = range(10)
PROJ_ORDER = (COL_GH, COL_GA, COL_MH, COL_MA, COL_K, COL_V, COL_Q, COL_VH, COL_X1, COL_X0)

LANES = 128
SUBLANES = 8
CTX_HEAD_TILE = 128
NBR_HEAD_TILE = 128
NBR_BLOCK_LANES = 128
RPB_COPIES = LANES // GRID_W
V7X_VMEM_BYTES = 64 * 1024 * 1024
ATTN_SCALE = HEAD_DIM ** -0.5
LOG2E = math.log2(math.e)

Q_BLOCK = 256
Q_BLOCK_ROWS = Q_BLOCK // GRID_W
KEY_WIN_ROWS = 12
KEY_WIN = KEY_WIN_ROWS * GRID_W

IN_TM = 256
W_STAGES = 2
CAST_ROWS = 64
MERGE_TM = 512
MERGE_CHUNK = 256
HYENA_ROWS_PER_STEP = 2048
HYENA_PAIR_MIN_LEN = 512


def _vmem_limit(block_bytes, temp_bytes):
    need = int(block_bytes + temp_bytes)
    return min(max(need, 16 * 1024 * 1024), V7X_VMEM_BYTES - 8 * 1024 * 1024)


def _nbytes(shape, dtype):
    return int(np.prod(shape)) * jnp.dtype(dtype).itemsize


def _silu(x):
    return x * jax.nn.sigmoid(x)


def _split_bf16(a):
    hi = a.astype(BF16)
    lo = (a - hi.astype(F32)).astype(BF16)
    return hi, lo


def _dot(a, b):
    return jnp.dot(a, b, preferred_element_type=F32)


def _dot_split(a_hi, a_lo, b):
    b_hi, b_lo = _split_bf16(b)
    return _dot(a_hi, b_hi) + (_dot(a_hi, b_lo) + _dot(a_lo, b_hi))


@functools.lru_cache(maxsize=None)
def _dft_constants(L):
    n = 2 * L
    k = np.arange(L, dtype=np.float64)[:, None]
    s = np.arange(L, dtype=np.float64)[None, :]
    ang = 2.0 * np.pi * k * s / n
    nyq = np.cos(np.pi * np.arange(L, dtype=np.float64))
    C = np.cos(ang)
    S = np.sin(ang)
    S[0, :] = nyq
    F = np.concatenate([C, S], axis=0)
    w = np.full((L,), 2.0)
    w[0] = 1.0
    Gc = (C * w[:, None]).T / n
    Ss = 2.0 * np.sin(ang)
    Ss[0, :] = nyq
    Gs = Ss.T / n
    G = np.concatenate([Gc, Gs], axis=1)
    return F.astype(np.float32), G.astype(np.float32)


def _dft_matrices(L):
    F, G = _dft_constants(L)
    return jnp.asarray(F).astype(BF16), jnp.asarray(G).astype(BF16)


@functools.lru_cache(maxsize=None)
def _filter_constants(L):
    t = np.linspace(0.0, 1.0, L, dtype=np.float32)[:, None]
    bands = np.linspace(1e-4, FILTER_BANDS - 1, FILTER_BANDS, dtype=np.float32)[None]
    w = (2.0 * math.pi * np.arange(L, dtype=np.float32)[:, None] / L).astype(np.float32)
    z = np.concatenate([t, np.cos(bands * w), -np.sin(bands * w)], axis=-1).astype(np.float32)
    z_pad = np.zeros((L, LANES), np.float32)
    z_pad[:, :FILTER_EMB] = z
    deltas = np.linspace(MIN_DECAY, MAX_DECAY, D_HYENA, dtype=np.float32)
    decay = (np.exp(-t * np.abs(deltas)) + np.float32(DECAY_SHIFT)).astype(np.float32)
    return z_pad, decay


@functools.lru_cache(maxsize=None)
def _rpb_window_mask():
    c = np.arange(GRID_W)[:, None]
    kc = np.arange(GRID_W)[None, :]
    c0 = np.clip(c - WIN_COLS // 2, 0, GRID_W - WIN_COLS)
    in_win = (kc >= c0) & (kc < c0 + WIN_COLS)
    return np.tile(in_win, (1, RPB_COPIES)).astype(np.float32)


def _mod_kernel(c_ref, cctx_ref, w_ref, b_ref, o_ref, cond_ref):
    n_lat = c_ref.shape[0]
    cond_ref[...] = jnp.zeros(cond_ref.shape, F32)
    cond_ref[0:n_lat, :] = c_ref[...]
    cond_ref[n_lat:n_lat + 1, :] = cctx_ref[...]
    s = _silu(cond_ref[...]).astype(BF16)
    o_ref[:, 0, :] = _dot(s, w_ref[...].astype(BF16)) + b_ref[...]


def _modulation(c, c_ctx, w_ada, b_ada):
    n_lat = c.shape[0]
    assert n_lat + 1 <= SUBLANES
    return pl.pallas_call(
        _mod_kernel,
        out_shape=jax.ShapeDtypeStruct((SUBLANES, 1, 3 * D_MODEL), F32),
        grid=(3,),
        in_specs=[pl.BlockSpec((n_lat, D_MODEL), lambda j: (0, 0)),
                  pl.BlockSpec((1, D_MODEL), lambda j: (0, 0)),
                  pl.BlockSpec((D_MODEL, D_MODEL), lambda j: (0, j)),
                  pl.BlockSpec((1, D_MODEL), lambda j: (0, j))],
        out_specs=pl.BlockSpec((SUBLANES, 1, D_MODEL), lambda j: (0, 0, j)),
        scratch_shapes=[pltpu.VMEM((SUBLANES, D_MODEL), F32)],
        name="mod",
    )(c, c_ctx, w_ada, b_ada.reshape(1, -1))


def _filter_spectra(z_ref, w1_ref, b1_ref, w2_ref, b2_ref, w3f_ref, w3b_ref, freq_ref, decay_ref, d_ref, f_ref,
                    hf_ref, fp_ref, fqa_ref, fpb_ref):
    L = decay_ref.shape[0]

    @pl.when(pl.program_id(0) == 0)
    def _():
        z_hi, z_lo = _split_bf16(z_ref[...])
        hf = jnp.sin(freq_ref[0:1, :] * (_dot_split(z_hi, z_lo, w1_ref[...]) + b1_ref[...]))
        h_hi, h_lo = _split_bf16(hf)
        hf_ref[...] = jnp.sin(freq_ref[1:2, :] * (_dot_split(h_hi, h_lo, w2_ref[...]) + b2_ref[...]))

    h_hi, h_lo = _split_bf16(hf_ref[...])
    decay = decay_ref[...]
    h_fwd = _dot_split(h_hi, h_lo, w3f_ref[...]) * decay
    h_bwd = _dot_split(h_hi, h_lo, w3b_ref[...]) * decay
    row0 = jax.lax.broadcasted_iota(jnp.int32, (L, 1), 0) == 0
    h_bwd = jnp.where(row0, 0.0, h_bwd)
    fsum = (h_fwd + h_bwd).astype(BF16)
    fdif = (h_fwd - h_bwd).astype(BF16)
    skip = d_ref[...]
    fp = _dot(f_ref[0:L, :], fsum) + skip
    fq = _dot(f_ref[L:2 * L, :], fdif)
    nyq = _dot(f_ref[L:L + 2 * SUBLANES, :], fsum)[0:1, :] + skip
    fp_ref[...] = fp
    fqa_ref[...] = jnp.where(row0, 0.0, fq)
    fpb_ref[...] = jnp.where(row0, nyq, fp)


def _rpb_rows(rpb):
    n_dr = 2 * WIN_ROWS - 1
    n_off = 2 * WIN_COLS - 1
    return jnp.tile(jnp.pad(rpb.reshape(N_HEADS * n_dr, n_off), ((0, 0), (0, GRID_W - n_off))), (1, RPB_COPIES))


def _inproj_kernel(xl_ref, xc_ref, mod_ref, w_hbm, wbh_ref, wba_ref, wout_ref,
                   z_ref, kt_ref, vt_ref, wbh16_ref, wba16_ref, wout16_ref, w_ref, stage_ref, sem, *, n_lat_tiles):
    n_col = D_IN // D_MODEL
    n_stage = stage_ref.shape[0]
    epilogue = {COL_GH: _silu, COL_GA: _silu, COL_MH: jax.nn.sigmoid, COL_MA: jax.nn.sigmoid,
                COL_Q: lambda acc: acc * (ATTN_SCALE * LOG2E)}

    def columns(col):
        return slice(col * D_MODEL, (col + 1) * D_MODEL)

    def fetch(n):
        return pltpu.make_async_copy(w_hbm.at[:, columns(PROJ_ORDER[n])], stage_ref.at[n % n_stage], sem.at[n % n_stage])

    def tile(stream_weights):
        if stream_weights:
            for n in range(min(n_stage, n_col)):
                fetch(n).start()
        shift = mod_ref[:, 0:D_MODEL]
        scale = mod_ref[:, D_MODEL:2 * D_MODEL]
        x = jnp.where(pl.program_id(0) < n_lat_tiles, xl_ref[...], xc_ref[...])
        h = (x * (1.0 + scale) + shift).astype(BF16)
        for src_ref, dst_ref in ((wbh_ref, wbh16_ref), (wba_ref, wba16_ref), (wout_ref, wout16_ref)):
            dst_ref[...] = src_ref[...].astype(BF16)
        for n, col in enumerate(PROJ_ORDER):
            if stream_weights:
                fetch(n).wait()
                w_ref[:, columns(col)] = stage_ref[n % n_stage].astype(BF16)
                if n + n_stage < n_col:
                    fetch(n + n_stage).start()
            acc = _dot(h, w_ref[:, columns(col)])
            z_ref[:, columns(col)] = epilogue.get(col, lambda acc: acc)(acc).astype(BF16)
            if col == COL_K:
                kt_ref[...] = acc.T
            if col == COL_V:
                vt_ref[...] = acc.T

    pl.when(pl.program_id(0) == 0)(functools.partial(tile, True))
    pl.when(pl.program_id(0) != 0)(functools.partial(tile, False))


def _in_projection(x_lat, x_ctx, mod3, lat_row, ctx_row, w_in, merge_weights, *, tm):
    n_lat, n_ctx = x_lat.shape[0] // tm, x_ctx.shape[0] // tm
    is_lat = lambda i: i < n_lat
    lat_tile = lambda i: jnp.minimum(i, n_lat - 1)
    ctx_tile = lambda i: jnp.maximum(i - n_lat, 0)
    kv_shape = jax.ShapeDtypeStruct((n_ctx, D_ATTN, tm), F32)
    kv_spec = pl.BlockSpec((None, D_ATTN, tm), lambda i: (ctx_tile(i), 0, 0))
    n_cast = D_MODEL // CAST_ROWS
    assert n_cast <= n_lat + n_ctx and all(w.shape == (D_MODEL, D_MODEL) for w in merge_weights)
    cast_spec = pl.BlockSpec((CAST_ROWS, D_MODEL), lambda i: (jnp.minimum(i, n_cast - 1), 0))
    stage = (W_STAGES, D_MODEL, D_MODEL)
    blocks = (_nbytes((D_MODEL, D_IN), BF16) + _nbytes(stage, F32) + 4 * _nbytes((tm, D_MODEL), F32)
              + 2 * _nbytes((tm, D_IN), BF16) + 4 * _nbytes((tm, D_ATTN), F32)
              + 3 * 2 * (_nbytes((CAST_ROWS, D_MODEL), F32) + _nbytes((CAST_ROWS, D_MODEL), BF16)))
    return pl.pallas_call(
        functools.partial(_inproj_kernel, n_lat_tiles=n_lat),
        out_shape=[jax.ShapeDtypeStruct(((n_lat + n_ctx) * tm, D_IN), BF16), kv_shape, kv_shape]
                  + [jax.ShapeDtypeStruct((D_MODEL, D_MODEL), BF16)] * 3,
        grid=(n_lat + n_ctx,),
        in_specs=[pl.BlockSpec((tm, D_MODEL), lambda i: (lat_tile(i), 0)),
                  pl.BlockSpec((tm, D_MODEL), lambda i: (ctx_tile(i), 0)),
                  pl.BlockSpec((None, 1, 3 * D_MODEL),
                               lambda i: (jnp.where(is_lat(i), lat_row(lat_tile(i)), ctx_row), 0, 0)),
                  pl.BlockSpec(memory_space=pl.ANY)] + [cast_spec] * 3,
        out_specs=[pl.BlockSpec((tm, D_IN), lambda i: (i, 0)), kv_spec, kv_spec] + [cast_spec] * 3,
        scratch_shapes=[pltpu.VMEM((D_MODEL, D_IN), BF16), pltpu.VMEM(stage, F32),
                        pltpu.SemaphoreType.DMA((W_STAGES,))],
        compiler_params=pltpu.CompilerParams(
            dimension_semantics=("arbitrary",),
            vmem_limit_bytes=_vmem_limit(blocks, 4 * _nbytes((tm, D_MODEL), F32))),
        name="inproj",
    )(x_lat, x_ctx, mod3, w_in, *merge_weights)


def _hyena_kernel(vh_ref, x1_ref, x0_ref, gh_ref, cwv_ref, cw1_ref, cw0_ref, cbv_ref, cb1_ref, cb0_ref,
                  f_ref, g_ref, z_ref, w1_ref, b1_ref, w2_ref, b2_ref, w3f_ref, w3b_ref, freq_ref, decay_ref, d_ref,
                  o_ref, hf_ref, fp_ref, fqa_ref, fpb_ref):
    L = fp_ref.shape[0]
    slab = jax.lax.broadcasted_iota(jnp.int32, (SUBLANES, 1), 0)

    @pl.when(pl.program_id(1) == 0)
    def _():
        _filter_spectra(z_ref, w1_ref, b1_ref, w2_ref, b2_ref, w3f_ref, w3b_ref, freq_ref, decay_ref, d_ref, f_ref,
                        hf_ref, fp_ref, fqa_ref, fpb_ref)

    def short_conv(x_ref, rows, w_ref, b_ref):
        x = x_ref[rows, :].astype(F32)
        prev = pltpu.roll(x, 1, 0)
        nxt = pltpu.roll(x, L - 1, 0)
        prev = jnp.concatenate([jnp.where(slab == 0, 0.0, prev[:SUBLANES]), prev[SUBLANES:]], axis=0)
        nxt = jnp.concatenate([nxt[:-SUBLANES], jnp.where(slab == SUBLANES - 1, 0.0, nxt[-SUBLANES:])], axis=0)
        return prev * w_ref[0:1, :] + x * w_ref[1:2, :] + nxt * w_ref[2:3, :] + b_ref[...]

    tc = fp_ref.shape[1]
    n_seq = vh_ref.shape[0] // L
    group = 2 if L >= HYENA_PAIR_MIN_LEN and n_seq % 2 == 0 else 1
    for s in range(0, n_seq, group):
        seqs = [slice((s + i) * L, (s + i + 1) * L) for i in range(group)]
        u = [short_conv(vh_ref, rows, cwv_ref, cbv_ref) * short_conv(x1_ref, rows, cw1_ref, cb1_ref) for rows in seqs]
        t = _dot(f_ref[...], jnp.concatenate([ui.astype(BF16) for ui in u], axis=1))
        spec = []
        for i in range(group):
            p = t[:L, i * tc:(i + 1) * tc]
            q = t[L:, i * tc:(i + 1) * tc]
            yp = p * fp_ref[...] - q * fqa_ref[...]
            yq = p * fqa_ref[...] + q * fpb_ref[...]
            spec.append(jnp.concatenate([yp.astype(BF16), yq.astype(BF16)], axis=0))
        y = _dot(g_ref[...], jnp.concatenate(spec, axis=1))
        for i, rows in enumerate(seqs):
            y_h = y[:, i * tc:(i + 1) * tc] * short_conv(x0_ref, rows, cw0_ref, cb0_ref)
            o_ref[rows, :] = (y_h * gh_ref[rows, :].astype(F32)).astype(BF16)


def _hyena_branch(z, z_row0, B, L, conv_w, conv_b, hyena_d, w1, b1, w2, b2, w3, freq, *, tc=256, seqs_per_step):
    nblk = D_HYENA // tc
    tm = seqs_per_step * L
    assert z_row0 % tm == 0
    f_mat, g_mat = _dft_matrices(L)
    z_emb, decay = _filter_constants(L)
    w1p = jnp.pad(w1, ((0, LANES - FILTER_EMB), (0, 0)))
    zcol = lambda piece: pl.BlockSpec((tm, tc), lambda c, b: (z_row0 // tm + b, piece * nblk + c))
    wcol = lambda rows, piece: pl.BlockSpec((rows, tc), lambda c, b: (0, piece * nblk + c))
    taps = lambda piece: pl.BlockSpec((None, 3, tc), lambda c, b: (0, 0, piece * nblk + c))
    chan = lambda rows: pl.BlockSpec((rows, tc), lambda c, b: (0, c))
    const = lambda shape: pl.BlockSpec(shape, lambda c, b: (0, 0), pipeline_mode=pl.Buffered(1))
    blocks = (2 * _nbytes((2 * L, L), BF16) + 2 * 5 * _nbytes((tm, tc), BF16) + 5 * _nbytes((L, tc), F32)
              + _nbytes((L, LANES), F32) + _nbytes((L, FILTER_HIDDEN), F32))
    return pl.pallas_call(
        _hyena_kernel,
        out_shape=jax.ShapeDtypeStruct((B * L, D_HYENA), BF16),
        grid=(nblk, B // seqs_per_step),
        in_specs=[zcol(COL_VH), zcol(COL_X1), zcol(COL_X0), zcol(COL_GH),
                  taps(0), taps(1), taps(2), wcol(1, 0), wcol(1, 1), wcol(1, 2),
                  const((2 * L, L)), const((L, 2 * L)),
                  const((L, LANES)),
                  const((LANES, FILTER_HIDDEN)), const((1, FILTER_HIDDEN)),
                  const((FILTER_HIDDEN, FILTER_HIDDEN)), const((1, FILTER_HIDDEN)),
                  wcol(FILTER_HIDDEN, 0), wcol(FILTER_HIDDEN, 1),
                  const((2, FILTER_HIDDEN)), chan(L), chan(1)],
        out_specs=pl.BlockSpec((tm, tc), lambda c, b: (b, c)),
        scratch_shapes=[pltpu.VMEM((L, FILTER_HIDDEN), F32)] + [pltpu.VMEM((L, tc), F32)] * 3,
        compiler_params=pltpu.CompilerParams(
            dimension_semantics=("arbitrary", "arbitrary"),
            vmem_limit_bytes=_vmem_limit(blocks, 16 * _nbytes((tm, tc), F32))),
        name=f"hyena_{L}",
    )(z, z, z, z, conv_w, conv_w, conv_w, conv_b.reshape(1, -1), conv_b.reshape(1, -1), conv_b.reshape(1, -1),
      f_mat, g_mat, z_emb, w1p, b1.reshape(1, -1), w2, b2.reshape(1, -1), w3, w3, freq, decay, hyena_d.reshape(1, -1))


def _head_masks(width):
    lane = jax.lax.broadcasted_iota(jnp.int32, (1, width), 1)
    return [(lane >= h * HEAD_DIM) & (lane < (h + 1) * HEAD_DIM) for h in range(width // HEAD_DIM)]


def _qk(q, k):
    return jax.lax.dot_general(q, k, (((1,), (1,)), ((), ())), preferred_element_type=F32)


def _stack_heads(q, masks):
    return jnp.concatenate([jnp.where(msk, q, jnp.zeros_like(q)) for msk in masks], axis=0)


def _unstack_heads(o, masks):
    n = o.shape[0] // len(masks)
    out = o[:n]
    for h in range(1, len(masks)):
        out = jnp.where(masks[h], o[h * n:(h + 1) * n], out)
    return out


def _ctx_attn_kernel(q_ref, k_ref, v_ref, g_ref, o_ref):
    masks = _head_masks(CTX_HEAD_TILE)
    for t in range(D_ATTN // CTX_HEAD_TILE):
        cols = slice(t * CTX_HEAD_TILE, (t + 1) * CTX_HEAD_TILE)
        s = _qk(_stack_heads(q_ref[:, cols], masks), k_ref[:, cols])
        m = jnp.max(s, axis=-1, keepdims=True)
        p = jnp.exp2(s - m)
        l = jnp.sum(p, axis=-1, keepdims=True)
        out = _unstack_heads(_dot(p.astype(BF16), v_ref[:, cols]) / l, masks)
        o_ref[:, cols] = (out * g_ref[:, cols].astype(F32)).astype(BF16)


def _context_attention(z, z_row0, B, L):
    assert z_row0 % L == 0
    zcol = lambda piece: pl.BlockSpec((L, D_ATTN), lambda b: (z_row0 // L + b, piece))
    return pl.pallas_call(
        _ctx_attn_kernel,
        out_shape=jax.ShapeDtypeStruct((B * L, D_ATTN), BF16),
        grid=(B,),
        in_specs=[zcol(COL_Q), zcol(COL_K), zcol(COL_V), zcol(COL_GA)],
        out_specs=pl.BlockSpec((L, D_ATTN), lambda b: (b, 0)),
        compiler_params=pltpu.CompilerParams(dimension_semantics=("arbitrary",)),
        name="ctx_attn",
    )(z, z, z, z)


def _key_window_start(qb):
    rows = 1024 // GRID_W
    r_first = qb * Q_BLOCK_ROWS
    r_last = r_first + Q_BLOCK_ROWS - 1
    lo = min(max(r_first - WIN_ROWS // 2, 0), rows - WIN_ROWS)
    hi = min(max(r_last - WIN_ROWS // 2, 0), rows - WIN_ROWS) + WIN_ROWS
    start = min(lo, rows - KEY_WIN_ROWS)
    start -= start % Q_BLOCK_ROWS
    assert start <= lo and hi <= start + KEY_WIN_ROWS
    return start


def _nbr_attn_kernel(q_ref, k0_ref, k1_ref, k2_ref, v0_ref, v1_ref, v2_ref, kc_ref, vc_ref, g_ref, rpb_ref, mask_ref,
                     o_ref, bias_ref, tiles_ref, *, n_rows):
    masks = _head_masks(NBR_HEAD_TILE)
    n_heads = q_ref.shape[-1] // HEAD_DIM
    chain_tiles = [slice(c * NBR_HEAD_TILE, (c + 1) * NBR_HEAD_TILE) for c in range(q_ref.shape[-1] // NBR_HEAD_TILE)]
    k_refs = (k0_ref, k1_ref, k2_ref)
    v_refs = (v0_ref, v1_ref, v2_ref)
    key_blocks = KEY_WIN // Q_BLOCK

    def window_row_start(r):
        return min(max(r - WIN_ROWS // 2, 0), n_rows - WIN_ROWS)

    n_dr = 2 * WIN_ROWS - 1
    lane_block = pl.program_id(1)

    @pl.when(pl.program_id(0) == 0)
    def _():
        in_cols = mask_ref[...] > 0.0
        for h in range(n_heads):
            for dr in range(n_dr):
                row = rpb_ref[h * n_dr + dr:h * n_dr + dr + 1, :] * LOG2E
                t = pltpu.roll(jnp.broadcast_to(row, (GRID_W, LANES)), LANES - (WIN_COLS - 1), 1,
                               stride=1, stride_axis=0)
                tiles_ref[lane_block, h * n_dr + dr] = jnp.where(in_cols, t, NEG_INF)

    def build_bias(qb):
        k_start = _key_window_start(qb)
        used = set()
        for i in range(Q_BLOCK_ROWS):
            r = qb * Q_BLOCK_ROWS + i
            r0 = window_row_start(r)
            for j in range(KEY_WIN_ROWS):
                kr = k_start + j
                cols = slice(j * GRID_W, (j + 1) * GRID_W)
                lanes = slice((j % RPB_COPIES) * GRID_W, (j % RPB_COPIES + 1) * GRID_W)
                inside = r0 <= kr < r0 + WIN_ROWS
                if inside:
                    used.add(j // Q_BLOCK_ROWS)
                for h in range(n_heads):
                    rows = slice(h * Q_BLOCK + i * GRID_W, h * Q_BLOCK + (i + 1) * GRID_W)
                    if inside:
                        bias_ref[rows, cols] = tiles_ref[lane_block, h * n_dr + kr - r + WIN_ROWS - 1, :, lanes]
                    else:
                        bias_ref[rows, cols] = jnp.full((GRID_W, GRID_W), NEG_INF, F32)
        return sorted(used)

    def attend(blocks):
        for c, lanes in enumerate(chain_tiles):
            stacked = slice(c * len(masks) * Q_BLOCK, (c + 1) * len(masks) * Q_BLOCK)
            for b in range(q_ref.shape[0]):
                qs = _stack_heads(q_ref[b, :, lanes], masks)
                s = [_qk(qs, k_refs[j][b, :, lanes]) + bias_ref[stacked, j * Q_BLOCK:(j + 1) * Q_BLOCK] for j in blocks]
                s.append(_dot(qs, kc_ref[b, lanes, :].astype(BF16)))
                m = jnp.max(functools.reduce(jnp.maximum, s), axis=-1, keepdims=True)
                p = [jnp.exp2(x - m) for x in s]
                l = jnp.sum(functools.reduce(jnp.add, p), axis=-1, keepdims=True)
                p = [x.astype(BF16) for x in p]
                v_win = jnp.concatenate([v_refs[j][b, :, lanes] for j in blocks], axis=0)
                o = _dot(jnp.concatenate(p[:-1], axis=1), v_win) + _qk(p[-1], vc_ref[b, lanes, :].astype(BF16))
                o_ref[b, :, lanes] = (_unstack_heads(o / l, masks) * g_ref[b, :, lanes].astype(F32)).astype(BF16)

    for qb in range(n_rows // Q_BLOCK_ROWS):
        @pl.when(pl.program_id(0) == qb)
        def _(qb=qb):
            blocks = build_bias(qb)
            assert blocks and all(0 <= j < key_blocks for j in blocks)
            attend(blocks)


def _neighbourhood_attention(z, z_row0, B, L, cache_k, cache_v, rpb_rows):
    assert z_row0 % (B * L) == 0
    group = z_row0 // (B * L)
    n_rows = L // GRID_W
    n_qb = L // Q_BLOCK
    assert KEY_WIN == 3 * Q_BLOCK and all(_key_window_start(qb) % Q_BLOCK_ROWS == 0 for qb in range(n_qb))
    z3 = z.reshape(z.shape[0] // L, L, D_IN)
    n_ctx = cache_k.shape[1]
    kc = jnp.transpose(cache_k, (0, 2, 3, 1)).reshape(B, D_ATTN, n_ctx)
    vc = jnp.transpose(cache_v, (0, 2, 3, 1)).reshape(B, D_ATTN, n_ctx)
    width = NBR_BLOCK_LANES
    tiles_per_col = D_ATTN // width
    rows_per_tile = width // HEAD_DIM * (2 * WIN_ROWS - 1)

    def win_block(qb):
        return jnp.where(qb < n_qb // 2, 0, (n_rows - KEY_WIN_ROWS) // Q_BLOCK_ROWS)

    assert [_key_window_start(qb) // Q_BLOCK_ROWS for qb in range(n_qb)] == \
        [0 if qb < n_qb // 2 else (n_rows - KEY_WIN_ROWS) // Q_BLOCK_ROWS for qb in range(n_qb)]
    qspec = lambda piece: pl.BlockSpec((B, Q_BLOCK, width), lambda qb, t: (group, qb, piece * tiles_per_col + t))
    kspec = lambda piece, j: pl.BlockSpec((B, Q_BLOCK, width),
                                          lambda qb, t: (group, win_block(qb) + j, piece * tiles_per_col + t))
    cspec = pl.BlockSpec((B, width, n_ctx), lambda qb, t: (0, t, 0))
    out = pl.pallas_call(
        functools.partial(_nbr_attn_kernel, n_rows=n_rows),
        out_shape=jax.ShapeDtypeStruct((B, L, D_ATTN), BF16),
        grid=(n_qb, tiles_per_col),
        in_specs=[qspec(COL_Q), kspec(COL_K, 0), kspec(COL_K, 1), kspec(COL_K, 2),
                  kspec(COL_V, 0), kspec(COL_V, 1), kspec(COL_V, 2), cspec, cspec, qspec(COL_GA),
                  pl.BlockSpec((None, rows_per_tile, LANES), lambda qb, t: (t, 0, 0)),
                  pl.BlockSpec((GRID_W, LANES), lambda qb, t: (0, 0))],
        out_specs=pl.BlockSpec((B, Q_BLOCK, width), lambda qb, t: (0, qb, t)),
        scratch_shapes=[pltpu.VMEM((width // HEAD_DIM * Q_BLOCK, KEY_WIN), F32),
                        pltpu.VMEM((tiles_per_col, rows_per_tile, GRID_W, LANES), F32)],
        compiler_params=pltpu.CompilerParams(dimension_semantics=("arbitrary", "arbitrary")),
        name="nbr_attn",
    )(z3, z3, z3, z3, z3, z3, z3, kc, vc, z3, rpb_rows.reshape(tiles_per_col, rows_per_tile, LANES),
      jnp.asarray(_rpb_window_mask()))
    return out.reshape(B * L, D_ATTN)


def _merge_kernel(yh_ref, ya_ref, mh_ref, ma_ref, x_ref, mod_ref, wbh_ref, wba_ref, wout_ref, lng_ref, lnb_ref,
                  o_ref, *, alpha):
    gate = mod_ref[:, 2 * D_MODEL:3 * D_MODEL]
    for r in range(o_ref.shape[0] // MERGE_CHUNK):
        rows = slice(r * MERGE_CHUNK, (r + 1) * MERGE_CHUNK)
        p_h = _dot(yh_ref[rows, :], wbh_ref[...])
        p_a = _dot(ya_ref[rows, :], wba_ref[...])
        m_h = mh_ref[rows, :].astype(F32)
        m_a = ma_ref[rows, :].astype(F32)
        out = _dot((m_h * p_h + m_a * p_a).astype(BF16), wout_ref[...])
        v = alpha * x_ref[rows, :] + gate * out
        mu = jnp.mean(v, axis=-1, keepdims=True)
        d = v - mu
        var = jnp.mean(d * d, axis=-1, keepdims=True)
        o_ref[rows, :] = d * jax.lax.rsqrt(var + LN_EPS) * lng_ref[...] + lnb_ref[...]


def _merge(yh, ya, z, z_row0, x2d, mod3, mod_index, w_bh, w_ba, w_out, ln_g, ln_b, *, alpha, tm):
    M = x2d.shape[0]
    assert z_row0 % tm == 0
    tok = lambda width, col: pl.BlockSpec((tm, width), lambda i: (i, col))
    ztok = lambda col: pl.BlockSpec((tm, D_MODEL), lambda i: (z_row0 // tm + i, col))
    const = lambda shape: pl.BlockSpec(shape, lambda i: (0, 0))
    return pl.pallas_call(
        functools.partial(_merge_kernel, alpha=alpha),
        out_shape=jax.ShapeDtypeStruct((M, D_MODEL), F32),
        grid=(M // tm,),
        in_specs=[tok(D_HYENA, 0), tok(D_ATTN, 0), ztok(COL_MH), ztok(COL_MA), tok(D_MODEL, 0),
                  pl.BlockSpec((None, 1, 3 * D_MODEL), lambda i: (mod_index(i), 0, 0)),
                  const((D_HYENA, D_MODEL)), const((D_ATTN, D_MODEL)), const((D_MODEL, D_MODEL)),
                  const((1, D_MODEL)), const((1, D_MODEL))],
        out_specs=pl.BlockSpec((tm, D_MODEL), lambda i: (i, 0)),
        compiler_params=pltpu.CompilerParams(
            dimension_semantics=("arbitrary",),
            vmem_limit_bytes=_vmem_limit(
                2 * (4 * _nbytes((tm, D_MODEL), BF16) + 2 * _nbytes((tm, D_MODEL), F32)
                     + 3 * _nbytes((D_MODEL, D_MODEL), BF16)),
                8 * _nbytes((tm, D_MODEL), F32))),
        name="merge",
    )(yh, ya, z, z, x2d, mod3, w_bh, w_ba, w_out, ln_g.reshape(1, -1), ln_b.reshape(1, -1))


def _mixers(x, z, z_row0, mod3, cond_row, ctx_kv, p, filt, rpb_rows, *, alpha):
    B, L, _ = x.shape
    yh = _hyena_branch(z, z_row0, B, L, p["conv_w"], p["conv_b"], p["hyena_d"], *filt,
                       seqs_per_step=max(1, HYENA_ROWS_PER_STEP // L))
    if ctx_kv is None:
        ya = _context_attention(z, z_row0, B, L)
    else:
        ya = _neighbourhood_attention(z, z_row0, B, L, ctx_kv[0], ctx_kv[1], rpb_rows)
    y = _merge(yh, ya, z, z_row0, x.reshape(B * L, D_MODEL), mod3, lambda i: cond_row(i * MERGE_TM // L),
               p["w_bh"], p["w_ba"], p["w_out"], p["ln_g"], p["ln_b"], alpha=alpha, tm=MERGE_TM)
    return y.reshape(B, L, D_MODEL)


def kernel(x_prompt, x_sample, c, cache_k, cache_v, c_ctx, w_ada, b_ada, w_in, conv_w, conv_b, filt_w1, filt_b1,
           filt_w2, filt_b2, filt_w3, filt_freq, hyena_d, rpb, w_bh, w_ba, w_out, ln_g, ln_b):
    depth = w_in.shape[0]
    alpha = (2.0 * depth) ** 0.25
    n_lat, lat_len = x_sample.shape[0], x_sample.shape[1]
    n_ctx, ctx_len = x_prompt.shape[0], x_prompt.shape[1]
    assert ctx_len == IN_TM
    ctx_row = n_lat

    xp, xs = x_prompt, x_sample
    new_k, new_v = [], []
    for l in range(depth):
        mod3 = _modulation(c, c_ctx.reshape(1, -1), w_ada[l], b_ada[l])
        filt = (filt_w1[l], filt_b1[l], filt_w2[l], filt_b2[l], filt_w3[l], filt_freq[l])
        rpb_rows = _rpb_rows(rpb[l])
        z, k_ctx, v_ctx, w_bh16, w_ba16, w_out16 = _in_projection(
            xs.reshape(-1, D_MODEL), xp.reshape(-1, D_MODEL), mod3, lambda tile: tile * IN_TM // lat_len, ctx_row,
            w_in[l], (w_bh[l], w_ba[l], w_out[l]), tm=IN_TM)
        p = {"conv_w": conv_w[l:l + 1], "conv_b": conv_b[l], "hyena_d": hyena_d[l],
             "w_bh": w_bh16, "w_ba": w_ba16, "w_out": w_out16, "ln_g": ln_g[l], "ln_b": ln_b[l]}
        new_k.append(jnp.transpose(k_ctx.reshape(n_ctx, N_HEADS, HEAD_DIM, ctx_len), (0, 3, 1, 2)))
        new_v.append(jnp.transpose(v_ctx.reshape(n_ctx, N_HEADS, HEAD_DIM, ctx_len), (0, 3, 1, 2)))
        xp = _mixers(xp, z, n_lat * lat_len, mod3, lambda b: ctx_row, None, p, filt, None, alpha=alpha)
        xs = _mixers(xs, z, 0, mod3, lambda b: b, (cache_k[:, l], cache_v[:, l]), p, filt, rpb_rows, alpha=alpha)
    return xp, xs, jnp.stack(new_k, axis=1), jnp.stack(new_v, axis=1)
```

```python
import functools
import math

import jax
import jax.numpy as jnp
import numpy as np
from jax.experimental import pallas as pl
from jax.experimental.pallas import tpu as pltpu

F32 = jnp.float32
BF16 = jnp.bfloat16

D_MODEL = 1024
D_HYENA = 1024
N_HEADS = 16
HEAD_DIM = 64
D_ATTN = N_HEADS * HEAD_DIM
D_IN = 4 * D_HYENA + 4 * D_ATTN + 2 * D_MODEL
GRID_W = 64
WIN_ROWS = 8
WIN_COLS = 16
FILTER_EMB = 33
FILTER_BANDS = (FILTER_EMB - 1) // 2
FILTER_HIDDEN = 64
DECAY_TARGET = 1e-2
MIN_DECAY = math.log(DECAY_TARGET) / 1.5
MAX_DECAY = math.log(DECAY_TARGET) / 0.3
DECAY_SHIFT = 0.05
LN_EPS = 1e-5
NEG_INF = -1e30

COL_VH, COL_X1, COL_X0, COL_GH, COL_Q, COL_K, COL_V, COL_GA, COL_MH, COL_MA = range(10)
PROJ_ORDER = (COL_GH, COL_GA, COL_MH, COL_MA, COL_K, COL_V, COL_Q, COL_VH, COL_X1, COL_X0)

LANES = 128
SUBLANES = 8
CTX_HEAD_TILE = 128
NBR_HEAD_TILE = 128
NBR_BLOCK_LANES = 128
RPB_COPIES = LANES // GRID_W
V7X_VMEM_BYTES = 64 * 1024 * 1024
ATTN_SCALE = HEAD_DIM ** -0.5
LOG2E = math.log2(math.e)

Q_BLOCK = 256
Q_BLOCK_ROWS = Q_BLOCK // GRID_W
KEY_WIN_ROWS = 12
KEY_WIN = KEY_WIN_ROWS * GRID_W

IN_TM = 256
W_STAGES = 2
CAST_ROWS = 64
MERGE_TM = 512
MERGE_CHUNK = 256
HYENA_ROWS_PER_STEP = 2048
HYENA_PAIR_MIN_LEN = 512


def _vmem_limit(block_bytes, temp_bytes):
    need = int(block_bytes + temp_bytes)
    return min(max(need, 16 * 1024 * 1024), V7X_VMEM_BYTES - 8 * 1024 * 1024)


def _nbytes(shape, dtype):
    return int(np.prod(shape)) * jnp.dtype(dtype).itemsize


def _silu(x):
    return x * jax.nn.sigmoid(x)


def _split_bf16(a):
    hi = a.astype(BF16)
    lo = (a - hi.astype(F32)).astype(BF16)
    return hi, lo


def _dot(a, b):
    return jnp.dot(a, b, preferred_element_type=F32)


def _dot_split(a_hi, a_lo, b):
    b_hi, b_lo = _split_bf16(b)
    return _dot(a_hi, b_hi) + (_dot(a_hi, b_lo) + _dot(a_lo, b_hi))


@functools.lru_cache(maxsize=None)
def _dft_constants(L):
    n = 2 * L
    k = np.arange(L, dtype=np.float64)[:, None]
    s = np.arange(L, dtype=np.float64)[None, :]
    ang = 2.0 * np.pi * k * s / n
    nyq = np.cos(np.pi * np.arange(L, dtype=np.float64))
    C = np.cos(ang)
    S = np.sin(ang)
    S[0, :] = nyq
    F = np.concatenate([C, S], axis=0)
    w = np.full((L,), 2.0)
    w[0] = 1.0
    Gc = (C * w[:, None]).T / n
    Ss = 2.0 * np.sin(ang)
    Ss[0, :] = nyq
    Gs = Ss.T / n
    G = np.concatenate([Gc, Gs], axis=1)
    return F.astype(np.float32), G.astype(np.float32)


def _dft_matrices(L):
    F, G = _dft_constants(L)
    return jnp.asarray(F).astype(BF16), jnp.asarray(G).astype(BF16)


@functools.lru_cache(maxsize=None)
def _filter_constants(L):
    t = np.linspace(0.0, 1.0, L, dtype=np.float32)[:, None]
    bands = np.linspace(1e-4, FILTER_BANDS - 1, FILTER_BANDS, dtype=np.float32)[None]
    w = (2.0 * math.pi * np.arange(L, dtype=np.float32)[:, None] / L).astype(np.float32)
    z = np.concatenate([t, np.cos(bands * w), -np.sin(bands * w)], axis=-1).astype(np.float32)
    z_pad = np.zeros((L, LANES), np.float32)
    z_pad[:, :FILTER_EMB] = z
    deltas = np.linspace(MIN_DECAY, MAX_DECAY, D_HYENA, dtype=np.float32)
    decay = (np.exp(-t * np.abs(deltas)) + np.float32(DECAY_SHIFT)).astype(np.float32)
    return z_pad, decay


@functools.lru_cache(maxsize=None)
def _rpb_window_mask():
    c = np.arange(GRID_W)[:, None]
    kc = np.arange(GRID_W)[None, :]
    c0 = np.clip(c - WIN_COLS // 2, 0, GRID_W - WIN_COLS)
    in_win = (kc >= c0) & (kc < c0 + WIN_COLS)
    return np.tile(in_win, (1, RPB_COPIES)).astype(np.float32)


def _mod_kernel(c_ref, cctx_ref, w_ref, b_ref, o_ref, cond_ref):
    n_lat = c_ref.shape[0]
    cond_ref[...] = jnp.zeros(cond_ref.shape, F32)
    cond_ref[0:n_lat, :] = c_ref[...]
    cond_ref[n_lat:n_lat + 1, :] = cctx_ref[...]
    s = _silu(cond_ref[...]).astype(BF16)
    o_ref[:, 0, :] = _dot(s, w_ref[...].astype(BF16)) + b_ref[...]


def _modulation(c, c_ctx, w_ada, b_ada):
    n_lat = c.shape[0]
    assert n_lat + 1 <= SUBLANES
    return pl.pallas_call(
        _mod_kernel,
        out_shape=jax.ShapeDtypeStruct((SUBLANES, 1, 3 * D_MODEL), F32),
        grid=(3,),
        in_specs=[pl.BlockSpec((n_lat, D_MODEL), lambda j: (0, 0)),
                  pl.BlockSpec((1, D_MODEL), lambda j: (0, 0)),
                  pl.BlockSpec((D_MODEL, D_MODEL), lambda j: (0, j)),
                  pl.BlockSpec((1, D_MODEL), lambda j: (0, j))],
        out_specs=pl.BlockSpec((SUBLANES, 1, D_MODEL), lambda j: (0, 0, j)),
        scratch_shapes=[pltpu.VMEM((SUBLANES, D_MODEL), F32)],
        name="mod",
    )(c, c_ctx, w_ada, b_ada.reshape(1, -1))


def _filter_spectra(z_ref, w1_ref, b1_ref, w2_ref, b2_ref, w3f_ref, w3b_ref, freq_ref, decay_ref, d_ref, f_ref,
                    hf_ref, fp_ref, fqa_ref, fpb_ref):
    L = decay_ref.shape[0]

    @pl.when(pl.program_id(0) == 0)
    def _():
        z_hi, z_lo = _split_bf16(z_ref[...])
        hf = jnp.sin(freq_ref[0:1, :] * (_dot_split(z_hi, z_lo, w1_ref[...]) + b1_ref[...]))
        h_hi, h_lo = _split_bf16(hf)
        hf_ref[...] = jnp.sin(freq_ref[1:2, :] * (_dot_split(h_hi, h_lo, w2_ref[...]) + b2_ref[...]))

    h_hi, h_lo = _split_bf16(hf_ref[...])
    decay = decay_ref[...]
    h_fwd = _dot_split(h_hi, h_lo, w3f_ref[...]) * decay
    h_bwd = _dot_split(h_hi, h_lo, w3b_ref[...]) * decay
    row0 = jax.lax.broadcasted_iota(jnp.int32, (L, 1), 0) == 0
    h_bwd = jnp.where(row0, 0.0, h_bwd)
    fsum = (h_fwd + h_bwd).astype(BF16)
    fdif = (h_fwd - h_bwd).astype(BF16)
    skip = d_ref[...]
    fp = _dot(f_ref[0:L, :], fsum) + skip
    fq = _dot(f_ref[L:2 * L, :], fdif)
    nyq = _dot(f_ref[L:L + 2 * SUBLANES, :], fsum)[0:1, :] + skip
    fp_ref[...] = fp
    fqa_ref[...] = jnp.where(row0, 0.0, fq)
    fpb_ref[...] = jnp.where(row0, nyq, fp)


def _rpb_rows(rpb):
    n_dr = 2 * WIN_ROWS - 1
    n_off = 2 * WIN_COLS - 1
    return jnp.tile(jnp.pad(rpb.reshape(N_HEADS * n_dr, n_off), ((0, 0), (0, GRID_W - n_off))), (1, RPB_COPIES))


def _inproj_kernel(xl_ref, xc_ref, mod_ref, w_hbm, wbh_ref, wba_ref, wout_ref,
                   z_ref, kt_ref, vt_ref, wbh16_ref, wba16_ref, wout16_ref, w_ref, stage_ref, sem, *, n_lat_tiles):
    n_col = D_IN // D_MODEL
    n_stage = stage_ref.shape[0]
    epilogue = {COL_GH: _silu, COL_GA: _silu, COL_MH: jax.nn.sigmoid, COL_MA: jax.nn.sigmoid,
                COL_Q: lambda acc: acc * (ATTN_SCALE * LOG2E)}

    def columns(col):
        return slice(col * D_MODEL, (col + 1) * D_MODEL)

    def fetch(n):
        return pltpu.make_async_copy(w_hbm.at[:, columns(PROJ_ORDER[n])], stage_ref.at[n % n_stage], sem.at[n % n_stage])

    def tile(stream_weights):
        if stream_weights:
            for n in range(min(n_stage, n_col)):
                fetch(n).start()
        shift = mod_ref[:, 0:D_MODEL]
        scale = mod_ref[:, D_MODEL:2 * D_MODEL]
        x = jnp.where(pl.program_id(0) < n_lat_tiles, xl_ref[...], xc_ref[...])
        h = (x * (1.0 + scale) + shift).astype(BF16)
        for src_ref, dst_ref in ((wbh_ref, wbh16_ref), (wba_ref, wba16_ref), (wout_ref, wout16_ref)):
            dst_ref[...] = src_ref[...].astype(BF16)
        for n, col in enumerate(PROJ_ORDER):
            if stream_weights:
                fetch(n).wait()
                w_ref[:, columns(col)] = stage_ref[n % n_stage].astype(BF16)
                if n + n_stage < n_col:
                    fetch(n + n_stage).start()
            acc = _dot(h, w_ref[:, columns(col)])
            z_ref[:, columns(col)] = epilogue.get(col, lambda acc: acc)(acc).astype(BF16)
            if col == COL_K:
                kt_ref[...] = acc.T
            if col == COL_V:
                vt_ref[...] = acc.T

    pl.when(pl.program_id(0) == 0)(functools.partial(tile, True))
    pl.when(pl.program_id(0) != 0)(functools.partial(tile, False))


def _in_projection(x_lat, x_ctx, mod3, lat_row, ctx_row, w_in, merge_weights, *, tm):
    n_lat, n_ctx = x_lat.shape[0] // tm, x_ctx.shape[0] // tm
    is_lat = lambda i: i < n_lat
    lat_tile = lambda i: jnp.minimum(i, n_lat - 1)
    ctx_tile = lambda i: jnp.maximum(i - n_lat, 0)
    kv_shape = jax.ShapeDtypeStruct((n_ctx, D_ATTN, tm), F32)
    kv_spec = pl.BlockSpec((None, D_ATTN, tm), lambda i: (ctx_tile(i), 0, 0))
    n_cast = D_MODEL // CAST_ROWS
    assert n_cast <= n_lat + n_ctx and all(w.shape == (D_MODEL, D_MODEL) for w in merge_weights)
    cast_spec = pl.BlockSpec((CAST_ROWS, D_MODEL), lambda i: (jnp.minimum(i, n_cast - 1), 0))
    stage = (W_STAGES, D_MODEL, D_MODEL)
    blocks = (_nbytes((D_MODEL, D_IN), BF16) + _nbytes(stage, F32) + 4 * _nbytes((tm, D_MODEL), F32)
              + 2 * _nbytes((tm, D_IN), BF16) + 4 * _nbytes((tm, D_ATTN), F32)
              + 3 * 2 * (_nbytes((CAST_ROWS, D_MODEL), F32) + _nbytes((CAST_ROWS, D_MODEL), BF16)))
    return pl.pallas_call(
        functools.partial(_inproj_kernel, n_lat_tiles=n_lat),
        out_shape=[jax.ShapeDtypeStruct(((n_lat + n_ctx) * tm, D_IN), BF16), kv_shape, kv_shape]
                  + [jax.ShapeDtypeStruct((D_MODEL, D_MODEL), BF16)] * 3,
        grid=(n_lat + n_ctx,),
        in_specs=[pl.BlockSpec((tm, D_MODEL), lambda i: (lat_tile(i), 0)),
                  pl.BlockSpec((tm, D_MODEL), lambda i: (ctx_tile(i), 0)),
                  pl.BlockSpec((None, 1, 3 * D_MODEL),
                               lambda i: (jnp.where(is_lat(i), lat_row(lat_tile(i)), ctx_row), 0, 0)),
                  pl.BlockSpec(memory_space=pl.ANY)] + [cast_spec] * 3,
        out_specs=[pl.BlockSpec((tm, D_IN), lambda i: (i, 0)), kv_spec, kv_spec] + [cast_spec] * 3,
        scratch_shapes=[pltpu.VMEM((D_MODEL, D_IN), BF16), pltpu.VMEM(stage, F32),
                        pltpu.SemaphoreType.DMA((W_STAGES,))],
        compiler_params=pltpu.CompilerParams(
            dimension_semantics=("arbitrary",),
            vmem_limit_bytes=_vmem_limit(blocks, 4 * _nbytes((tm, D_MODEL), F32))),
        name="inproj",
    )(x_lat, x_ctx, mod3, w_in, *merge_weights)


def _hyena_kernel(vh_ref, x1_ref, x0_ref, gh_ref, cwv_ref, cw1_ref, cw0_ref, cbv_ref, cb1_ref, cb0_ref,
                  f_ref, g_ref, z_ref, w1_ref, b1_ref, w2_ref, b2_ref, w3f_ref, w3b_ref, freq_ref, decay_ref, d_ref,
                  o_ref, hf_ref, fp_ref, fqa_ref, fpb_ref):
    L = fp_ref.shape[0]
    slab = jax.lax.broadcasted_iota(jnp.int32, (SUBLANES, 1), 0)

    @pl.when(pl.program_id(1) == 0)
    def _():
        _filter_spectra(z_ref, w1_ref, b1_ref, w2_ref, b2_ref, w3f_ref, w3b_ref, freq_ref, decay_ref, d_ref, f_ref,
                        hf_ref, fp_ref, fqa_ref, fpb_ref)

    def short_conv(x_ref, rows, w_ref, b_ref):
        x = x_ref[rows, :].astype(F32)
        prev = pltpu.roll(x, 1, 0)
        nxt = pltpu.roll(x, L - 1, 0)
        prev = jnp.concatenate([jnp.where(slab == 0, 0.0, prev[:SUBLANES]), prev[SUBLANES:]], axis=0)
        nxt = jnp.concatenate([nxt[:-SUBLANES], jnp.where(slab == SUBLANES - 1, 0.0, nxt[-SUBLANES:])], axis=0)
        return prev * w_ref[0:1, :] + x * w_ref[1:2, :] + nxt * w_ref[2:3, :] + b_ref[...]

    tc = fp_ref.shape[1]
    n_seq = vh_ref.shape[0] // L
    group = 2 if L >= HYENA_PAIR_MIN_LEN and n_seq % 2 == 0 else 1
    for s in range(0, n_seq, group):
        seqs = [slice((s + i) * L, (s + i + 1) * L) for i in range(group)]
        u = [short_conv(vh_ref, rows, cwv_ref, cbv_ref) * short_conv(x1_ref, rows, cw1_ref, cb1_ref) for rows in seqs]
        t = _dot(f_ref[...], jnp.concatenate([ui.astype(BF16) for ui in u], axis=1))
        spec = []
        for i in range(group):
            p = t[:L, i * tc:(i + 1) * tc]
            q = t[L:, i * tc:(i + 1) * tc]
            yp = p * fp_ref[...] - q * fqa_ref[...]
            yq = p * fqa_ref[...] + q * fpb_ref[...]
            spec.append(jnp.concatenate([yp.astype(BF16), yq.astype(BF16)], axis=0))
        y = _dot(g_ref[...], jnp.concatenate(spec, axis=1))
        for i, rows in enumerate(seqs):
            y_h = y[:, i * tc:(i + 1) * tc] * short_conv(x0_ref, rows, cw0_ref, cb0_ref)
            o_ref[rows, :] = (y_h * gh_ref[rows, :].astype(F32)).astype(BF16)


def _hyena_branch(z, z_row0, B, L, conv_w, conv_b, hyena_d, w1, b1, w2, b2, w3, freq, *, tc=256, seqs_per_step):
    nblk = D_HYENA // tc
    tm = seqs_per_step * L
    assert z_row0 % tm == 0
    f_mat, g_mat = _dft_matrices(L)
    z_emb, decay = _filter_constants(L)
    w1p = jnp.pad(w1, ((0, LANES - FILTER_EMB), (0, 0)))
    zcol = lambda piece: pl.BlockSpec((tm, tc), lambda c, b: (z_row0 // tm + b, piece * nblk + c))
    wcol = lambda rows, piece: pl.BlockSpec((rows, tc), lambda c, b: (0, piece * nblk + c))
    taps = lambda piece: pl.BlockSpec((None, 3, tc), lambda c, b: (0, 0, piece * nblk + c))
    chan = lambda rows: pl.BlockSpec((rows, tc), lambda c, b: (0, c))
    const = lambda shape: pl.BlockSpec(shape, lambda c, b: (0, 0), pipeline_mode=pl.Buffered(1))
    blocks = (2 * _nbytes((2 * L, L), BF16) + 2 * 5 * _nbytes((tm, tc), BF16) + 5 * _nbytes((L, tc), F32)
              + _nbytes((L, LANES), F32) + _nbytes((L, FILTER_HIDDEN), F32))
    return pl.pallas_call(
        _hyena_kernel,
        out_shape=jax.ShapeDtypeStruct((B * L, D_HYENA), BF16),
        grid=(nblk, B // seqs_per_step),
        in_specs=[zcol(COL_VH), zcol(COL_X1), zcol(COL_X0), zcol(COL_GH),
                  taps(0), taps(1), taps(2), wcol(1, 0), wcol(1, 1), wcol(1, 2),
                  const((2 * L, L)), const((L, 2 * L)),
                  const((L, LANES)),
                  const((LANES, FILTER_HIDDEN)), const((1, FILTER_HIDDEN)),
                  const((FILTER_HIDDEN, FILTER_HIDDEN)), const((1, FILTER_HIDDEN)),
                  wcol(FILTER_HIDDEN, 0), wcol(FILTER_HIDDEN, 1),
                  const((2, FILTER_HIDDEN)), chan(L), chan(1)],
        out_specs=pl.BlockSpec((tm, tc), lambda c, b: (b, c)),
        scratch_shapes=[pltpu.VMEM((L, FILTER_HIDDEN), F32)] + [pltpu.VMEM((L, tc), F32)] * 3,
        compiler_params=pltpu.CompilerParams(
            dimension_semantics=("arbitrary", "arbitrary"),
            vmem_limit_bytes=_vmem_limit(blocks, 16 * _nbytes((tm, tc), F32))),
        name=f"hyena_{L}",
    )(z, z, z, z, conv_w, conv_w, conv_w, conv_b.reshape(1, -1), conv_b.reshape(1, -1), conv_b.reshape(1, -1),
      f_mat, g_mat, z_emb, w1p, b1.reshape(1, -1), w2, b2.reshape(1, -1), w3, w3, freq, decay, hyena_d.reshape(1, -1))


def _head_masks(width):
    lane = jax.lax.broadcasted_iota(jnp.int32, (1, width), 1)
    return [(lane >= h * HEAD_DIM) & (lane < (h + 1) * HEAD_DIM) for h in range(width // HEAD_DIM)]


def _qk(q, k):
    return jax.lax.dot_general(q, k, (((1,), (1,)), ((), ())), preferred_element_type=F32)


def _stack_heads(q, masks):
    return jnp.concatenate([jnp.where(msk, q, jnp.zeros_like(q)) for msk in masks], axis=0)


def _unstack_heads(o, masks):
    n = o.shape[0] // len(masks)
    out = o[:n]
    for h in range(1, len(masks)):
        out = jnp.where(masks[h], o[h * n:(h + 1) * n], out)
    return out


def _ctx_attn_kernel(q_ref, k_ref, v_ref, g_ref, o_ref):
    masks = _head_masks(CTX_HEAD_TILE)
    for t in range(D_ATTN // CTX_HEAD_TILE):
        cols = slice(t * CTX_HEAD_TILE, (t + 1) * CTX_HEAD_TILE)
        s = _qk(_stack_heads(q_ref[:, cols], masks), k_ref[:, cols])
        m = jnp.max(s, axis=-1, keepdims=True)
        p = jnp.exp2(s - m)
        l = jnp.sum(p, axis=-1, keepdims=True)
        out = _unstack_heads(_dot(p.astype(BF16), v_ref[:, cols]) / l, masks)
        o_ref[:, cols] = (out * g_ref[:, cols].astype(F32)).astype(BF16)


def _context_attention(z, z_row0, B, L):
    assert z_row0 % L == 0
    zcol = lambda piece: pl.BlockSpec((L, D_ATTN), lambda b: (z_row0 // L + b, piece))
    return pl.pallas_call(
        _ctx_attn_kernel,
        out_shape=jax.ShapeDtypeStruct((B * L, D_ATTN), BF16),
        grid=(B,),
        in_specs=[zcol(COL_Q), zcol(COL_K), zcol(COL_V), zcol(COL_GA)],
        out_specs=pl.BlockSpec((L, D_ATTN), lambda b: (b, 0)),
        compiler_params=pltpu.CompilerParams(dimension_semantics=("arbitrary",)),
        name="ctx_attn",
    )(z, z, z, z)


def _key_window_start(qb):
    rows = 1024 // GRID_W
    r_first = qb * Q_BLOCK_ROWS
    r_last = r_first + Q_BLOCK_ROWS - 1
    lo = min(max(r_first - WIN_ROWS // 2, 0), rows - WIN_ROWS)
    hi = min(max(r_last - WIN_ROWS // 2, 0), rows - WIN_ROWS) + WIN_ROWS
    start = min(lo, rows - KEY_WIN_ROWS)
    start -= start % Q_BLOCK_ROWS
    assert start <= lo and hi <= start + KEY_WIN_ROWS
    return start


def _nbr_attn_kernel(q_ref, k0_ref, k1_ref, k2_ref, v0_ref, v1_ref, v2_ref, kc_ref, vc_ref, g_ref, rpb_ref, mask_ref,
                     o_ref, bias_ref, tiles_ref, *, n_rows):
    masks = _head_masks(NBR_HEAD_TILE)
    n_heads = q_ref.shape[-1] // HEAD_DIM
    chain_tiles = [slice(c * NBR_HEAD_TILE, (c + 1) * NBR_HEAD_TILE) for c in range(q_ref.shape[-1] // NBR_HEAD_TILE)]
    k_refs = (k0_ref, k1_ref, k2_ref)
    v_refs = (v0_ref, v1_ref, v2_ref)
    key_blocks = KEY_WIN // Q_BLOCK

    def window_row_start(r):
        return min(max(r - WIN_ROWS // 2, 0), n_rows - WIN_ROWS)

    n_dr = 2 * WIN_ROWS - 1
    lane_block = pl.program_id(1)

    @pl.when(pl.program_id(0) == 0)
    def _():
        in_cols = mask_ref[...] > 0.0
        for h in range(n_heads):
            for dr in range(n_dr):
                row = rpb_ref[h * n_dr + dr:h * n_dr + dr + 1, :] * LOG2E
                t = pltpu.roll(jnp.broadcast_to(row, (GRID_W, LANES)), LANES - (WIN_COLS - 1), 1,
                               stride=1, stride_axis=0)
                tiles_ref[lane_block, h * n_dr + dr] = jnp.where(in_cols, t, NEG_INF)

    def build_bias(qb):
        k_start = _key_window_start(qb)
        used = set()
        for i in range(Q_BLOCK_ROWS):
            r = qb * Q_BLOCK_ROWS + i
            r0 = window_row_start(r)
            for j in range(KEY_WIN_ROWS):
                kr = k_start + j
                cols = slice(j * GRID_W, (j + 1) * GRID_W)
                lanes = slice((j % RPB_COPIES) * GRID_W, (j % RPB_COPIES + 1) * GRID_W)
                inside = r0 <= kr < r0 + WIN_ROWS
                if inside:
                    used.add(j // Q_BLOCK_ROWS)
                for h in range(n_heads):
                    rows = slice(h * Q_BLOCK + i * GRID_W, h * Q_BLOCK + (i + 1) * GRID_W)
                    if inside:
                        bias_ref[rows, cols] = tiles_ref[lane_block, h * n_dr + kr - r + WIN_ROWS - 1, :, lanes]
                    else:
                        bias_ref[rows, cols] = jnp.full((GRID_W, GRID_W), NEG_INF, F32)
        return sorted(used)

    def attend(blocks):
        for c, lanes in enumerate(chain_tiles):
            stacked = slice(c * len(masks) * Q_BLOCK, (c + 1) * len(masks) * Q_BLOCK)
            for b in range(q_ref.shape[0]):
                qs = _stack_heads(q_ref[b, :, lanes], masks)
                s = [_qk(qs, k_refs[j][b, :, lanes]) + bias_ref[stacked, j * Q_BLOCK:(j + 1) * Q_BLOCK] for j in blocks]
                s.append(_dot(qs, kc_ref[b, lanes, :].astype(BF16)))
                m = jnp.max(functools.reduce(jnp.maximum, s), axis=-1, keepdims=True)
                p = [jnp.exp2(x - m) for x in s]
                l = jnp.sum(functools.reduce(jnp.add, p), axis=-1, keepdims=True)
                p = [x.astype(BF16) for x in p]
                v_win = jnp.concatenate([v_refs[j][b, :, lanes] for j in blocks], axis=0)
                o = _dot(jnp.concatenate(p[:-1], axis=1), v_win) + _qk(p[-1], vc_ref[b, lanes, :].astype(BF16))
                o_ref[b, :, lanes] = (_unstack_heads(o / l, masks) * g_ref[b, :, lanes].astype(F32)).astype(BF16)

    for qb in range(n_rows // Q_BLOCK_ROWS):
        @pl.when(pl.program_id(0) == qb)
        def _(qb=qb):
            blocks = build_bias(qb)
            assert blocks and all(0 <= j < key_blocks for j in blocks)
            attend(blocks)


def _neighbourhood_attention(z, z_row0, B, L, cache_k, cache_v, rpb_rows):
    assert z_row0 % (B * L) == 0
    group = z_row0 // (B * L)
    n_rows = L // GRID_W
    n_qb = L // Q_BLOCK
    assert KEY_WIN == 3 * Q_BLOCK and all(_key_window_start(qb) % Q_BLOCK_ROWS == 0 for qb in range(n_qb))
    z3 = z.reshape(z.shape[0] // L, L, D_IN)
    n_ctx = cache_k.shape[1]
    kc = jnp.transpose(cache_k, (0, 2, 3, 1)).reshape(B, D_ATTN, n_ctx)
    vc = jnp.transpose(cache_v, (0, 2, 3, 1)).reshape(B, D_ATTN, n_ctx)
    width = NBR_BLOCK_LANES
    tiles_per_col = D_ATTN // width
    rows_per_tile = width // HEAD_DIM * (2 * WIN_ROWS - 1)

    def win_block(qb):
        return jnp.where(qb < n_qb // 2, 0, (n_rows - KEY_WIN_ROWS) // Q_BLOCK_ROWS)

    assert [_key_window_start(qb) // Q_BLOCK_ROWS for qb in range(n_qb)] == \
        [0 if qb < n_qb // 2 else (n_rows - KEY_WIN_ROWS) // Q_BLOCK_ROWS for qb in range(n_qb)]
    qspec = lambda piece: pl.BlockSpec((B, Q_BLOCK, width), lambda qb, t: (group, qb, piece * tiles_per_col + t))
    kspec = lambda piece, j: pl.BlockSpec((B, Q_BLOCK, width),
                                          lambda qb, t: (group, win_block(qb) + j, piece * tiles_per_col + t))
    cspec = pl.BlockSpec((B, width, n_ctx), lambda qb, t: (0, t, 0))
    out = pl.pallas_call(
        functools.partial(_nbr_attn_kernel, n_rows=n_rows),
        out_shape=jax.ShapeDtypeStruct((B, L, D_ATTN), BF16),
        grid=(n_qb, tiles_per_col),
        in_specs=[qspec(COL_Q), kspec(COL_K, 0), kspec(COL_K, 1), kspec(COL_K, 2),
                  kspec(COL_V, 0), kspec(COL_V, 1), kspec(COL_V, 2), cspec, cspec, qspec(COL_GA),
                  pl.BlockSpec((None, rows_per_tile, LANES), lambda qb, t: (t, 0, 0)),
                  pl.BlockSpec((GRID_W, LANES), lambda qb, t: (0, 0))],
        out_specs=pl.BlockSpec((B, Q_BLOCK, width), lambda qb, t: (0, qb, t)),
        scratch_shapes=[pltpu.VMEM((width // HEAD_DIM * Q_BLOCK, KEY_WIN), F32),
                        pltpu.VMEM((tiles_per_col, rows_per_tile, GRID_W, LANES), F32)],
        compiler_params=pltpu.CompilerParams(dimension_semantics=("arbitrary", "arbitrary")),
        name="nbr_attn",
    )(z3, z3, z3, z3, z3, z3, z3, kc, vc, z3, rpb_rows.reshape(tiles_per_col, rows_per_tile, LANES),
      jnp.asarray(_rpb_window_mask()))
    return out.reshape(B * L, D_ATTN)


def _merge_kernel(yh_ref, ya_ref, mh_ref, ma_ref, x_ref, mod_ref, wbh_ref, wba_ref, wout_ref, lng_ref, lnb_ref,
                  o_ref, *, alpha):
    gate = mod_ref[:, 2 * D_MODEL:3 * D_MODEL]
    for r in range(o_ref.shape[0] // MERGE_CHUNK):
        rows = slice(r * MERGE_CHUNK, (r + 1) * MERGE_CHUNK)
        p_h = _dot(yh_ref[rows, :], wbh_ref[...])
        p_a = _dot(ya_ref[rows, :], wba_ref[...])
        m_h = mh_ref[rows, :].astype(F32)
        m_a = ma_ref[rows, :].astype(F32)
        out = _dot((m_h * p_h + m_a * p_a).astype(BF16), wout_ref[...])
        v = alpha * x_ref[rows, :] + gate * out
        mu = jnp.mean(v, axis=-1, keepdims=True)
        d = v - mu
        var = jnp.mean(d * d, axis=-1, keepdims=True)
        o_ref[rows, :] = d * jax.lax.rsqrt(var + LN_EPS) * lng_ref[...] + lnb_ref[...]


def _merge(yh, ya, z, z_row0, x2d, mod3, mod_index, w_bh, w_ba, w_out, ln_g, ln_b, *, alpha, tm):
    M = x2d.shape[0]
    assert z_row0 % tm == 0
    tok = lambda width, col: pl.BlockSpec((tm, width), lambda i: (i, col))
    ztok = lambda col: pl.BlockSpec((tm, D_MODEL), lambda i: (z_row0 // tm + i, col))
    const = lambda shape: pl.BlockSpec(shape, lambda i: (0, 0))
    return pl.pallas_call(
        functools.partial(_merge_kernel, alpha=alpha),
        out_shape=jax.ShapeDtypeStruct((M, D_MODEL), F32),
        grid=(M // tm,),
        in_specs=[tok(D_HYENA, 0), tok(D_ATTN, 0), ztok(COL_MH), ztok(COL_MA), tok(D_MODEL, 0),
                  pl.BlockSpec((None, 1, 3 * D_MODEL), lambda i: (mod_index(i), 0, 0)),
                  const((D_HYENA, D_MODEL)), const((D_ATTN, D_MODEL)), const((D_MODEL, D_MODEL)),
                  const((1, D_MODEL)), const((1, D_MODEL))],
        out_specs=pl.BlockSpec((tm, D_MODEL), lambda i: (i, 0)),
        compiler_params=pltpu.CompilerParams(
            dimension_semantics=("arbitrary",),
            vmem_limit_bytes=_vmem_limit(
                2 * (4 * _nbytes((tm, D_MODEL), BF16) + 2 * _nbytes((tm, D_MODEL), F32)
                     + 3 * _nbytes((D_MODEL, D_MODEL), BF16)),
                16 * _nbytes((tm, D_MODEL), F32))),
        name="merge",
    )(yh, ya, z, z, x2d, mod3, w_bh, w_ba, w_out, ln_g.reshape(1, -1), ln_b.reshape(1, -1))


def _mixers(x, z, z_row0, mod3, cond_row, ctx_kv, p, filt, rpb_rows, *, alpha):
    B, L, _ = x.shape
    yh = _hyena_branch(z, z_row0, B, L, p["conv_w"], p["conv_b"], p["hyena_d"], *filt,
                       seqs_per_step=max(1, HYENA_ROWS_PER_STEP // L))
    if ctx_kv is None:
        ya = _context_attention(z, z_row0, B, L)
    else:
        ya = _neighbourhood_attention(z, z_row0, B, L, ctx_kv[0], ctx_kv[1], rpb_rows)
    y = _merge(yh, ya, z, z_row0, x.reshape(B * L, D_MODEL), mod3, lambda i: cond_row(i * MERGE_TM // L),
               p["w_bh"], p["w_ba"], p["w_out"], p["ln_g"], p["ln_b"], alpha=alpha, tm=MERGE_TM)
    return y.reshape(B, L, D_MODEL)


def kernel(x_prompt, x_sample, c, cache_k, cache_v, c_ctx, w_ada, b_ada, w_in, conv_w, conv_b, filt_w1, filt_b1,
           filt_w2, filt_b2, filt_w3, filt_freq, hyena_d, rpb, w_bh, w_ba, w_out, ln_g, ln_b):
    depth = w_in.shape[0]
    alpha = (2.0 * depth) ** 0.25
    n_lat, lat_len = x_sample.shape[0], x_sample.shape[1]
    n_ctx, ctx_len = x_prompt.shape[0], x_prompt.shape[1]
    assert ctx_len == IN_TM
    ctx_row = n_lat

    xp, xs = x_prompt, x_sample
    new_k, new_v = [], []
    for l in range(depth):
        mod3 = _modulation(c, c_ctx.reshape(1, -1), w_ada[l], b_ada[l])
        filt = (filt_w1[l], filt_b1[l], filt_w2[l], filt_b2[l], filt_w3[l], filt_freq[l])
        rpb_rows = _rpb_rows(rpb[l])
        z, k_ctx, v_ctx, w_bh16, w_ba16, w_out16 = _in_projection(
            xs.reshape(-1, D_MODEL), xp.reshape(-1, D_MODEL), mod3, lambda tile: tile * IN_TM // lat_len, ctx_row,
            w_in[l], (w_bh[l], w_ba[l], w_out[l]), tm=IN_TM)
        p = {"conv_w": conv_w[l:l + 1], "conv_b": conv_b[l], "hyena_d": hyena_d[l],
             "w_bh": w_bh16, "w_ba": w_ba16, "w_out": w_out16, "ln_g": ln_g[l], "ln_b": ln_b[l]}
        new_k.append(jnp.transpose(k_ctx.reshape(n_ctx, N_HEADS, HEAD_DIM, ctx_len), (0, 3, 1, 2)))
        new_v.append(jnp.transpose(v_ctx.reshape(n_ctx, N_HEADS, HEAD_DIM, ctx_len), (0, 3, 1, 2)))
        xp = _mixers(xp, z, n_lat * lat_len, mod3, lambda b: ctx_row, None, p, filt, None, alpha=alpha)
        xs = _mixers(xs, z, 0, mod3, lambda b: b, (cache_k[:, l], cache_v[:, l]), p, filt, rpb_rows, alpha=alpha)
    return xp, xs, jnp.stack(new_k, axis=1), jnp.stack(new_v, axis=1)
```
